```python
import math
import jax, jax.numpy as jnp
from jax import lax
import numpy as np

D_MODEL = 1024
BATCH = 32
SEQ = 2048
DEPTH = 2

N_A_LAYERS = DEPTH // 2
N_B_LAYERS = DEPTH - N_A_LAYERS
NORM_EPS = 1e-6

RWKV_HEAD = 64
RWKV_HEADS = D_MODEL // RWKV_HEAD
RWKV_DECAY_LORA = 64
RWKV_AAA_LORA = 64
RWKV_GATE_LORA = 160
RWKV_GN_EPS = 64e-5

NSA_HEADS = 16
NSA_KV_GROUPS = 4
NSA_HPG = NSA_HEADS // NSA_KV_GROUPS
NSA_HEAD_DIM = D_MODEL // NSA_HEADS
NSA_KV_PARTS = 6
CMP_BLOCK = 32
CMP_STRIDE = 16
CMP_HIDDEN = 256
SEL_BLOCK = 64
SEL_TOPK = 8
WINDOW = 512
Q_CHUNK = 32
N_BRANCH = 3
IMP_FORCE = 1e4

REL_BUCKETS = 32
REL_MAX_DIST = 128

MOE_GROUPS = 4
MOE_EXPERTS_PER_GROUP = 8
MOE_EXPERTS = MOE_GROUPS * MOE_EXPERTS_PER_GROUP
MOE_TOPK = 2
MOE_HIDDEN = 512
MOE_BLOCK = 128

kernel_name = 'hybrid_rwkv7_nsa_hmoe_yoco'


def rms_norm(x, g):
    xf = x.astype(jnp.float32)
    y = xf * lax.rsqrt(jnp.mean(xf * xf, axis=-1, keepdims=True) + NORM_EPS)
    return (y * g.astype(jnp.float32)).astype(x.dtype)


def rel_bucket(dist):
    n = jnp.maximum(dist, 0)
    max_exact = REL_BUCKETS // 2
    nf = jnp.maximum(n, 1).astype(jnp.float32)
    large = max_exact + (jnp.log(nf / max_exact) / math.log(REL_MAX_DIST / max_exact)
                         * (REL_BUCKETS - max_exact)).astype(jnp.int32)
    large = jnp.minimum(large, REL_BUCKETS - 1)
    return jnp.where(n < max_exact, n, large)


def masked_softmax(logits, mask):
    logits = jnp.where(mask, logits, -1e30)
    m = jnp.max(logits, axis=-1, keepdims=True)
    p = jnp.exp(logits - m) * mask
    return p / jnp.maximum(jnp.sum(p, axis=-1, keepdims=True), 1e-30)


def rwkv7_time_mix(x, mu, w_r, w_k, w_v, w_o, w0, w1, w2, a0, a1, a2, g1, g2,
                   k_k, k_a, r_k, lnx_w, lnx_b):
    b, t, c = x.shape
    f32 = jnp.float32
    hshape = (b, t, RWKV_HEADS, RWKV_HEAD)
    xx = jnp.pad(x, ((0, 0), (1, 0), (0, 0)))[:, :-1] - x
    xr, xw, xk, xv, xa, xg = (x + xx * mu[i] for i in range(6))
    r = xr @ w_r
    k = xk @ w_k
    v = xv @ w_v
    w = -jax.nn.softplus(-(w0 + jnp.tanh(xw @ w1) @ w2)) - 0.5
    a = jax.nn.sigmoid(a0 + (xa @ a1) @ a2)
    g = jax.nn.sigmoid(xg @ g1) @ g2
    kk = (k * k_k).reshape(hshape).astype(f32)
    kk = kk * lax.rsqrt(jnp.maximum(jnp.sum(kk * kk, -1, keepdims=True), 1e-24))
    k = k * (1.0 + (a - 1.0) * k_a)
    decay = jnp.exp(-jnp.exp(w.astype(f32)))
    r_h = r.reshape(hshape).astype(f32)
    k_h = k.reshape(hshape).astype(f32)
    v_h = v.reshape(hshape).astype(f32)
    a_h = a.reshape(hshape).astype(f32)

    def to_time(z):
        return jnp.moveaxis(z.reshape(hshape).astype(f32), 1, 0)

    xs = (to_time(r_h), to_time(decay), to_time(k_h), to_time(v_h),
          to_time(-kk), to_time(kk * a_h))

    def step(state, inp):
        r_t, w_t, k_t, v_t, a_t, b_t = inp
        sa = jnp.einsum('bhij,bhj->bhi', state, a_t)
        state = (state * w_t[:, :, None, :] + sa[..., None] * b_t[:, :, None, :]
                 + v_t[..., None] * k_t[:, :, None, :])
        return state, jnp.einsum('bhij,bhj->bhi', state, r_t)

    s0 = jnp.zeros((b, RWKV_HEADS, RWKV_HEAD, RWKV_HEAD), f32)
    _, y = lax.scan(step, s0, xs)
    y = jnp.moveaxis(y, 0, 1)
    mean = jnp.mean(y, -1, keepdims=True)
    var = jnp.mean(jnp.square(y - mean), -1, keepdims=True)
    y = ((y - mean) * lax.rsqrt(var + RWKV_GN_EPS)).reshape(b, t, c)
    y = y * lnx_w.astype(f32) + lnx_b.astype(f32)
    bonus = jnp.sum(r_h * k_h * r_k.astype(f32), -1, keepdims=True) * v_h
    y = y + bonus.reshape(b, t, c)
    return (y * g.astype(f32)).astype(x.dtype) @ w_o


def nsa_shared_kv(h, kv_ln, kv_w, cmp_pe_k, cmp_pe_v, cmp_k_w1, cmp_k_w2, cmp_v_w1, cmp_v_w2):
    b, t, _ = h.shape
    G, dh = NSA_KV_GROUPS, NSA_HEAD_DIM
    hn = rms_norm(h, kv_ln)
    kv = (hn @ kv_w).reshape(b, t, NSA_KV_PARTS, G, dh)
    kv = jnp.moveaxis(kv, 1, 3)
    k_cr, v_cr, k_sel, v_sel, k_win, v_win = (kv[:, i] for i in range(NSA_KV_PARTS))
    n_cmp = (t - CMP_BLOCK) // CMP_STRIDE + 1
    blk_idx = jnp.arange(n_cmp)[:, None] * CMP_STRIDE + jnp.arange(CMP_BLOCK)[None, :]

    def compress(z, pe, w1, w2):
        zb = z[:, :, blk_idx] + pe[:, None]
        zb = zb.reshape(b, G, n_cmp, CMP_BLOCK * dh)
        return jax.nn.gelu(zb @ w1) @ w2

    k_cmp = compress(k_cr, cmp_pe_k, cmp_k_w1, cmp_k_w2)
    v_cmp = compress(v_cr, cmp_pe_v, cmp_v_w1, cmp_v_w2)
    n_sel = t // SEL_BLOCK
    k_sel = k_sel.reshape(b, G, n_sel, SEL_BLOCK, dh)
    v_sel = v_sel.reshape(b, G, n_sel, SEL_BLOCK, dh)
    return k_cmp, v_cmp, k_sel, v_sel, k_win, v_win


def nsa_attention(hn, w_qg, w_o, rel_bias, k_cmp, v_cmp, k_sel, v_sel, k_win, v_win):
    b, t, _ = hn.shape
    f32 = jnp.float32
    G, HPG, dh = NSA_KV_GROUPS, NSA_HPG, NSA_HEAD_DIM
    qg = hn @ w_qg
    q = qg[..., :NSA_HEADS * dh].reshape(b, t, G, HPG, dh).transpose(0, 2, 3, 1, 4)
    gates = jax.nn.sigmoid(qg[..., NSA_HEADS * dh:].astype(f32))
    gates = gates.reshape(b, t, G, HPG, N_BRANCH).transpose(0, 2, 3, 1, 4)
    table = rel_bias.astype(f32).reshape(REL_BUCKETS, G, HPG).transpose(1, 2, 0)
    n_cmp = k_cmp.shape[2]
    n_sel = k_sel.shape[2]
    topk = min(SEL_TOPK, n_sel)
    cmp_start = jnp.arange(n_cmp) * CMP_STRIDE
    cmp_end = cmp_start + CMP_BLOCK - 1
    sel_start = jnp.arange(n_sel) * SEL_BLOCK
    overlap = ((cmp_start[:, None] < sel_start[None, :] + SEL_BLOCK)
               & (cmp_start[:, None] + CMP_BLOCK > sel_start[None, :])).astype(f32)
    k_win_p = jnp.pad(k_win, ((0, 0), (0, 0), (WINDOW, 0), (0, 0)))
    v_win_p = jnp.pad(v_win, ((0, 0), (0, 0), (WINDOW, 0), (0, 0)))
    scale = NSA_HEAD_DIM ** -0.5
    g_idx = jnp.arange(G)[None, :, None, None, None]
    h_idx = jnp.arange(HPG)[None, None, :, None, None]
    gather_blocks = jax.vmap(jax.vmap(lambda blocks, ix: blocks[ix]))

    def chunk(c):
        t0 = c * Q_CHUNK
        tq = t0 + jnp.arange(Q_CHUNK)
        qc = lax.dynamic_slice_in_dim(q, t0, Q_CHUNK, axis=3)
        gc = lax.dynamic_slice_in_dim(gates, t0, Q_CHUNK, axis=3)
        lc = (jnp.einsum('bghqd,bgkd->bghqk', qc, k_cmp).astype(f32) * scale
              + table[:, :, rel_bucket(tq[:, None] - cmp_end[None, :])])
        pc = masked_softmax(lc, cmp_end[None, :] <= tq[:, None])
        oc = jnp.einsum('bghqk,bgkd->bghqd', pc.astype(v_cmp.dtype), v_cmp)
        imp = jnp.einsum('bghqk,kj->bgqj', pc, overlap)
        blk_q = (tq // SEL_BLOCK)[:, None]
        jj = jnp.arange(n_sel)[None, :]
        forced = (jj == 0) | (jj == blk_q) | (jj == blk_q - 1)
        score = jnp.where(forced, IMP_FORCE, jnp.where(jj <= blk_q, imp, -1.0))
        _, idx = lax.top_k(score, topk)
        ks = gather_blocks(k_sel, idx).reshape(b, G, Q_CHUNK, topk * SEL_BLOCK, dh)
        vs = gather_blocks(v_sel, idx).reshape(b, G, Q_CHUNK, topk * SEL_BLOCK, dh)
        pos = (idx[..., None] * SEL_BLOCK + jnp.arange(SEL_BLOCK)).reshape(b, G, Q_CHUNK, topk * SEL_BLOCK)
        bias_s = table[g_idx, h_idx, rel_bucket(tq[:, None] - pos)[:, :, None]]
        ls = jnp.einsum('bghqd,bgqkd->bghqk', qc, ks).astype(f32) * scale + bias_s
        ps = masked_softmax(ls, (pos <= tq[:, None])[:, :, None])
        o_sel = jnp.einsum('bghqk,bgqkd->bghqd', ps.astype(vs.dtype), vs)
        kw = lax.dynamic_slice_in_dim(k_win_p, t0, WINDOW + Q_CHUNK, axis=2)
        vw = lax.dynamic_slice_in_dim(v_win_p, t0, WINDOW + Q_CHUNK, axis=2)
        kp = t0 - WINDOW + jnp.arange(WINDOW + Q_CHUNK)
        dist = tq[:, None] - kp[None, :]
        mask_w = (dist >= 0) & (dist < WINDOW) & (kp[None, :] >= 0)
        lw = (jnp.einsum('bghqd,bgkd->bghqk', qc, kw).astype(f32) * scale
              + table[:, :, rel_bucket(dist)])
        pw = masked_softmax(lw, mask_w)
        ow = jnp.einsum('bghqk,bgkd->bghqd', pw.astype(vw.dtype), vw)
        out = gc[..., 0:1] * oc + gc[..., 1:2] * o_sel + gc[..., 2:3] * ow
        return out.astype(hn.dtype)

    o = lax.map(chunk, jnp.arange(t // Q_CHUNK))
    o = o.transpose(1, 0, 4, 2, 3, 5).reshape(b, t, NSA_HEADS * dh)
    return o @ w_o


def hier_moe(h, ln_g, w_gr, b_gr, w_er, b_er, w1, w3, w2):
    bt, t, c = h.shape
    n = bt * t
    f32 = jnp.float32
    xf = rms_norm(h, ln_g).reshape(n, c)
    gp = jax.nn.softmax((xf @ w_gr).astype(f32) + b_gr.astype(f32), axis=-1)
    grp_w, grp = lax.top_k(gp, 1)
    el = ((xf @ w_er).astype(f32) + b_er.astype(f32)).reshape(n, MOE_GROUPS, MOE_EXPERTS_PER_GROUP)
    el = jnp.take_along_axis(el, grp[:, :, None], axis=1)[:, 0]
    ew, ei = lax.top_k(jax.nn.softmax(el, axis=-1), MOE_TOPK)
    weight = grp_w * ew / jnp.sum(ew, -1, keepdims=True)
    expert = grp * MOE_EXPERTS_PER_GROUP + ei
    n_as = n * MOE_TOPK
    flat_e = expert.reshape(n_as)
    flat_t = jnp.repeat(jnp.arange(n), MOE_TOPK)
    flat_w = weight.reshape(n_as)
    order = jnp.argsort(flat_e)
    se, st, sw = flat_e[order], flat_t[order], flat_w[order]
    counts = jnp.bincount(flat_e, length=MOE_EXPERTS)
    starts = jnp.cumsum(counts) - counts
    pcounts = (counts + MOE_BLOCK - 1) // MOE_BLOCK * MOE_BLOCK
    pend = jnp.cumsum(pcounts)
    pstarts = pend - pcounts
    dest = pstarts[se] + jnp.arange(n_as) - starts[se]
    p = n_as + MOE_EXPERTS * MOE_BLOCK
    n_blk = p // MOE_BLOCK
    buf_t = jnp.zeros((p,), jnp.int32).at[dest].set(st)
    buf_w = jnp.zeros((p,), f32).at[dest].set(sw)
    blk_e = jnp.clip(jnp.searchsorted(pend, jnp.arange(n_blk) * MOE_BLOCK, side='right'), 0, MOE_EXPERTS - 1)
    xbuf = xf[buf_t].reshape(n_blk, MOE_BLOCK, c)

    def expert_block(args):
        xb, e = args
        return (jax.nn.silu(xb @ w1[e]) * (xb @ w3[e])) @ w2[e]

    yb = lax.map(expert_block, (xbuf, blk_e)).reshape(p, c)
    y = jnp.zeros((n, c), yb.dtype).at[buf_t].add(yb * buf_w[:, None].astype(yb.dtype))
    return y.reshape(bt, t, c)


def setup_inputs(seed: int = 0) -> dict:
    key = jax.random.key(seed)
    ks = iter(jax.random.split(key, 64))
    f32 = jnp.float32
    D = D_MODEL
    NA, NB = N_A_LAYERS, N_B_LAYERS
    G, dh, H = NSA_KV_GROUPS, NSA_HEAD_DIM, NSA_HEADS

    def nrm(shape, scale):
        return jax.random.normal(next(ks), shape, f32) * scale

    def gain(shape):
        return 1.0 + nrm(shape, 0.05)

    frac = jnp.arange(D, dtype=f32) / (D - 1)
    inp = {}
    inp['x'] = nrm((BATCH, SEQ, D), 1.0)
    inp['a_ln'] = gain((NA, D))
    inp['a_mu'] = jax.random.uniform(next(ks), (NA, 6, D), f32)
    inp['a_wr'] = nrm((NA, D, D), D ** -0.5)
    inp['a_wk'] = nrm((NA, D, D), D ** -0.5)
    inp['a_wv'] = nrm((NA, D, D), D ** -0.5)
    inp['a_wo'] = nrm((NA, D, D), D ** -0.5)
    inp['a_w0'] = (-6.5 + 5.0 * frac ** 0.85)[None] + nrm((NA, D), 0.1)
    inp['a_w1'] = nrm((NA, D, RWKV_DECAY_LORA), D ** -0.5)
    inp['a_w2'] = nrm((NA, RWKV_DECAY_LORA, D), 0.1 * RWKV_DECAY_LORA ** -0.5)
    inp['a_a0'] = nrm((NA, D), 0.1)
    inp['a_a1'] = nrm((NA, D, RWKV_AAA_LORA), D ** -0.5)
    inp['a_a2'] = nrm((NA, RWKV_AAA_LORA, D), 0.5 * RWKV_AAA_LORA ** -0.5)
    inp['a_g1'] = nrm((NA, D, RWKV_GATE_LORA), D ** -0.5)
    inp['a_g2'] = nrm((NA, RWKV_GATE_LORA, D), RWKV_GATE_LORA ** -0.5)
    inp['a_kk'] = 0.85 + nrm((NA, D), 0.05)
    inp['a_ka'] = gain((NA, D))
    inp['a_rk'] = nrm((NA, RWKV_HEADS, RWKV_HEAD), 0.1)
    inp['a_lnx_w'] = gain((NA, D))
    inp['a_lnx_b'] = nrm((NA, D), 0.01)
    inp['kv_ln'] = gain((D,))
    inp['kv_w'] = nrm((D, NSA_KV_PARTS * G * dh), D ** -0.5)
    inp['cmp_pe_k'] = nrm((G, CMP_BLOCK, dh), 0.5)
    inp['cmp_pe_v'] = nrm((G, CMP_BLOCK, dh), 0.5)
    inp['cmp_k_w1'] = nrm((CMP_BLOCK * dh, CMP_HIDDEN), (CMP_BLOCK * dh) ** -0.5)
    inp['cmp_k_w2'] = nrm((CMP_HIDDEN, dh), CMP_HIDDEN ** -0.5)
    inp['cmp_v_w1'] = nrm((CMP_BLOCK * dh, CMP_HIDDEN), (CMP_BLOCK * dh) ** -0.5)
    inp['cmp_v_w2'] = nrm((CMP_HIDDEN, dh), CMP_HIDDEN ** -0.5)
    inp['rel_bias'] = nrm((REL_BUCKETS, H), 0.5)
    inp['b_ln'] = gain((NB, D))
    inp['b_wqg'] = nrm((NB, D, H * dh + N_BRANCH * H), D ** -0.5)
    inp['b_wo'] = nrm((NB, H * dh, D), (H * dh) ** -0.5)
    inp['m_ln'] = gain((DEPTH, D))
    inp['m_wg'] = nrm((DEPTH, D, MOE_GROUPS), D ** -0.5)
    inp['m_bg'] = nrm((DEPTH, MOE_GROUPS), 0.01)
    inp['m_we'] = nrm((DEPTH, D, MOE_EXPERTS), D ** -0.5)
    inp['m_be'] = nrm((DEPTH, MOE_EXPERTS), 0.01)
    inp['m_w1'] = nrm((DEPTH, MOE_EXPERTS, D, MOE_HIDDEN), D ** -0.5)
    inp['m_w3'] = nrm((DEPTH, MOE_EXPERTS, D, MOE_HIDDEN), D ** -0.5)
    inp['m_w2'] = nrm((DEPTH, MOE_EXPERTS, MOE_HIDDEN, D), MOE_HIDDEN ** -0.5)
    inp['final_ln'] = gain((D,))
    return inp


def reference(x, a_ln, a_mu, a_wr, a_wk, a_wv, a_wo, a_w0, a_w1, a_w2, a_a0, a_a1, a_a2,
              a_g1, a_g2, a_kk, a_ka, a_rk, a_lnx_w, a_lnx_b,
              kv_ln, kv_w, cmp_pe_k, cmp_pe_v, cmp_k_w1, cmp_k_w2, cmp_v_w1, cmp_v_w2,
              rel_bias, b_ln, b_wqg, b_wo,
              m_ln, m_wg, m_bg, m_we, m_be, m_w1, m_w3, m_w2, final_ln):
    h = x
    shared = None
    for layer in range(DEPTH):
        if layer < N_A_LAYERS:
            i = layer
            h = h + rwkv7_time_mix(rms_norm(h, a_ln[i]), a_mu[i], a_wr[i], a_wk[i], a_wv[i], a_wo[i],
                                   a_w0[i], a_w1[i], a_w2[i], a_a0[i], a_a1[i], a_a2[i],
                                   a_g1[i], a_g2[i], a_kk[i], a_ka[i], a_rk[i], a_lnx_w[i], a_lnx_b[i])
        else:
            j = layer - N_A_LAYERS
            if j == 0:
                shared = nsa_shared_kv(h, kv_ln, kv_w, cmp_pe_k, cmp_pe_v,
                                       cmp_k_w1, cmp_k_w2, cmp_v_w1, cmp_v_w2)
            k_cmp, v_cmp, k_sel, v_sel, k_win, v_win = shared
            h = h + nsa_attention(rms_norm(h, b_ln[j]), b_wqg[j], b_wo[j], rel_bias,
                                  k_cmp, v_cmp, k_sel, v_sel, k_win, v_win)
        h = h + hier_moe(h, m_ln[layer], m_wg[layer], m_bg[layer], m_we[layer], m_be[layer],
                         m_w1[layer], m_w3[layer], m_w2[layer])
    return rms_norm(h, final_ln)
```

```python
import functools
import math

import numpy as np
import jax
import jax.numpy as jnp
from jax import lax
from jax.experimental import pallas as pl
from jax.experimental.pallas import tpu as pltpu

F32 = jnp.float32
MXU_DTYPE = jnp.bfloat16
ACT_DTYPE = jnp.bfloat16

D_MODEL = 1024
NORM_EPS = 1e-6
RWKV_HEAD = 64
RWKV_HEADS = D_MODEL // RWKV_HEAD
RWKV_GN_EPS = 64e-5
RWKV_CHUNK = 64
LANES = 128
MXU_TILE = 256
VMEM_LIMIT = 56 * 1024 * 1024


def _mx(x):
    return x.astype(MXU_DTYPE)


def _dot(a, b):
    return jnp.dot(_mx(a), _mx(b), preferred_element_type=F32)


def _dot_nt(a, b):
    return lax.dot_general(_mx(a), _mx(b), (((1,), (1,)), ((), ())), preferred_element_type=F32)


def _split2(x):
    hi = x.astype(MXU_DTYPE)
    lo = (x - hi.astype(F32)).astype(MXU_DTYPE)
    return hi, lo


def _dot_hl(x, w):
    hi, lo = _split2(x)
    return jnp.dot(hi, w, preferred_element_type=F32) + jnp.dot(lo, w, preferred_element_type=F32)


def _rms(x, g):
    return x * lax.rsqrt(jnp.mean(x * x, axis=-1, keepdims=True) + NORM_EPS) * g


def _const_spec(shape):
    nd = len(shape)
    return pl.BlockSpec(shape, lambda *_: (0,) * nd)


def _params(sem):
    return pltpu.CompilerParams(dimension_semantics=sem, vmem_limit_bytes=VMEM_LIMIT)


def _rwkv_pre_kernel(x_ref, ln_ref, mu_ref, wr_ref, wk_ref, wv_ref, w1_ref, a1_ref, g1_ref,
                     w2_ref, a2_ref, g2_ref, vec_ref,
                     r_out, k_out, v_out, kk_out, g_out, ld_out, a_out, prev_ref):
    tm = x_ref.shape[1]
    hn = _rms(x_ref[0], ln_ref[...])

    @pl.when(pl.program_id(1) == 0)
    def _():
        prev_ref[...] = jnp.zeros_like(prev_ref)

    rows = lax.broadcasted_iota(jnp.int32, (tm, 1), 0)
    shifted = jnp.where(rows == 0, prev_ref[0:1, :], pltpu.roll(hn, 1, axis=0))
    prev_ref[0:1, :] = hn[tm - 1:tm, :]
    xx = shifted - hn
    xr, xw, xk, xv, xa, xg = (_mx(hn + xx * mu_ref[i:i + 1, :]) for i in range(6))
    w0, a0, kkp, kap = (vec_ref[i:i + 1, :] for i in range(4))
    r = jnp.dot(xr, wr_ref[...], preferred_element_type=F32)
    k = jnp.dot(xk, wk_ref[...], preferred_element_type=F32)
    v = jnp.dot(xv, wv_ref[...], preferred_element_type=F32)
    z = w0 + _dot(jnp.tanh(jnp.dot(xw, w1_ref[...], preferred_element_type=F32)), w2_ref[...])
    a = jax.nn.sigmoid(a0 + _dot(jnp.dot(xa, a1_ref[...], preferred_element_type=F32), a2_ref[...]))
    g = _dot(jax.nn.sigmoid(jnp.dot(xg, g1_ref[...], preferred_element_type=F32)), g2_ref[...])
    ld_out[0] = (-math.exp(-0.5)) * jax.nn.sigmoid(z)
    a_out[0] = a
    r_out[0] = r.astype(r_out.dtype)
    v_out[0] = v.astype(v_out.dtype)
    kk_out[0] = (k * kkp).astype(kk_out.dtype)
    k_out[0] = (k * (1.0 + (a - 1.0) * kap)).astype(k_out.dtype)
    g_out[0] = g.astype(g_out.dtype)


def _rwkv_pre(x, ln, mu, wr, wk, wv, w1, a1, g1, w2, a2, g2, vecs, tm=256):
    b, t, d = x.shape
    act = jax.ShapeDtypeStruct((b, t, d), ACT_DTYPE)
    f32o = jax.ShapeDtypeStruct((b, t, d), F32)
    blk = pl.BlockSpec((1, tm, d), lambda i, j: (i, j, 0))
    consts = [ln, mu, wr, wk, wv, w1, a1, g1, w2, a2, g2, vecs]
    return pl.pallas_call(
        _rwkv_pre_kernel,
        grid=(b, t // tm),
        in_specs=[blk] + [_const_spec(c.shape) for c in consts],
        out_specs=[blk] * 7,
        out_shape=[act, act, act, act, act, f32o, f32o],
        scratch_shapes=[pltpu.VMEM((8, d), F32)],
        compiler_params=_params(("arbitrary", "arbitrary")),
        name="rwkv_pre",
    )(x, *consts)


def _stack4(x):
    return jnp.concatenate([x[:, MXU_TILE * q:MXU_TILE * (q + 1)] for q in range(4)], axis=0)


def _unstack4(y):
    c = y.shape[0] // 4
    return jnp.concatenate([y[c * q:c * (q + 1), :] for q in range(4)], axis=1)


def _rwkv_scan_kernel(r_ref, k_ref, v_ref, kk_ref, g_ref, ld_ref, a_ref,
                      ltri_ref, smask_ref, eye_ref, j_ref, rk_ref, lnw_ref, lnb_ref,
                      y_out, h_ref):
    c = RWKV_CHUNK

    @pl.when(pl.program_id(1) == 0)
    def _():
        h_ref[...] = jnp.zeros_like(h_ref)

    ld = ld_ref[0]
    r = r_ref[0].astype(F32)
    k = k_ref[0].astype(F32)
    v = v_ref[0].astype(F32)
    kk = kk_ref[0].astype(F32)
    jones = j_ref[...]
    ld_hi, ld_lo = _split2(ld)
    cum = (jnp.dot(ltri_ref[...], ld_hi, preferred_element_type=F32)
           + jnp.dot(ltri_ref[...], ld_lo, preferred_element_type=F32))
    cum_last = cum[c - 1:c, :]
    e_l = jnp.exp(cum)
    e_lm = jnp.exp(cum - ld)
    e_nl = jnp.exp(-cum)
    e_cl = jnp.exp(cum_last - cum)
    e_c = jnp.exp(cum_last)
    ss = _unstack4(_dot_hl(_stack4(kk * kk), jones))
    kkn = kk * lax.rsqrt(jnp.maximum(ss, 1e-24))
    a_vec = -kkn
    b_vec = kkn * a_ref[0]
    rt = r * e_l
    at = a_vec * e_lm
    kt = k * e_nl
    bt = b_vec * e_nl
    bh = b_vec * e_cl
    kh = k * e_cl

    lane = lax.broadcasted_iota(jnp.int32, (1, LANES), 1)
    m_l = (lane < RWKV_HEAD).astype(F32)
    m_r = 1.0 - m_l

    def st(xp):
        return jnp.concatenate([xp * m_l, xp * m_r], axis=0)

    def dup(xp):
        return jnp.concatenate([xp, xp], axis=0)

    smask = smask_ref[...] > 0.0
    eye = eye_ref[...]
    pairs = range(D_MODEL // LANES)
    sls = [slice(LANES * p, LANES * (p + 1)) for p in pairs]
    at_s = [st(at[:, sl]) for sl in sls]
    rt_s = [st(rt[:, sl]) for sl in sls]
    v_s = [st(v[:, sl]) for sl in sls]
    s = [jnp.where(smask,
                   _dot_nt(jnp.concatenate([at_s[p], rt_s[p]], axis=0),
                           jnp.concatenate([dup(bt[:, sls[p]]), dup(kt[:, sls[p]])], axis=0)),
                   0.0) for p in pairs]
    a_ab = [x[:2 * c, :2 * c] for x in s]
    a_r = [x[2 * c:, :] for x in s]
    g1 = [_dot(s[p][:2 * c, 2 * c:], v_s[p]) for p in pairs]
    t_m = [eye + x for x in a_ab]
    a_pow = [_dot(x, x) for x in a_ab]
    n = 2
    while n < c:
        if 2 * n < c:
            both = [_dot(a_pow[p], jnp.concatenate([a_pow[p], t_m[p]], axis=1)) for p in pairs]
            a_pow = [x[:, :2 * c] for x in both]
            t_m = [t_m[p] + both[p][:, 2 * c:] for p in pairs]
        else:
            t_m = [t_m[p] + _dot(a_pow[p], t_m[p]) for p in pairs]
        n *= 2
    tw = [_dot(t_m[p], jnp.concatenate([at_s[p], g1[p]], axis=1)) for p in pairs]
    zmat = [jnp.concatenate([tw[p], jnp.concatenate([jnp.zeros_like(v_s[p]), v_s[p]], axis=1)], axis=0)
            for p in pairs]
    upper = [_dot(a_r[p], zmat[p]) for p in pairs]
    bk_t = [jnp.concatenate([st(bh[:, sl]), st(kh[:, sl])], axis=0).T for sl in sls]
    lower = [_dot(bk_t[p], zmat[p]) for p in pairs]
    seq = [_dot(jnp.concatenate([upper[p][:, :LANES] + rt_s[p],
                                 lower[p][:, :LANES] + eye * e_c[:, sls[p]]], axis=0), h_ref[p])
           for p in pairs]
    ys = []
    for p in pairs:
        y_st = seq[p][:2 * c, :] + upper[p][:, LANES:]
        h_ref[p] = seq[p][2 * c:, :] + lower[p][:, LANES:]
        ys.append(y_st[:c, :] + y_st[c:, :])
    y = jnp.concatenate(ys, axis=1)

    inv_n = 1.0 / RWKV_HEAD
    y4 = _stack4(y)
    dlt = y4 - _dot_hl(y4, jones) * inv_n
    var = _dot_hl(dlt * dlt, jones) * inv_n
    yn = dlt * lax.rsqrt(var + RWKV_GN_EPS)
    bonus = _dot_hl(_stack4(r * k) * rk_ref[...], jones) * _stack4(v)
    out = (yn * lnw_ref[...] + lnb_ref[...] + bonus) * _stack4(g_ref[0].astype(F32))
    y_out[0] = _unstack4(out).astype(y_out.dtype)


def _scan_constants():
    c = RWKV_CHUNK
    ltri = np.tril(np.ones((c, c), np.float32))
    rho = np.arange(4 * c)
    r_type, r_head, r_t = rho // (2 * c), (rho % (2 * c)) // c, rho % c
    same = r_head[:, None] == r_head[None, :]
    strict = r_t[None, :] < r_t[:, None]
    incl = r_t[None, :] <= r_t[:, None]
    smask = same & np.where(r_type[:, None] == 0, strict, incl)
    eye = np.eye(LANES, dtype=np.float32)
    head = np.arange(MXU_TILE) // RWKV_HEAD
    jones = (head[:, None] == head[None, :]).astype(np.float32)
    return (jnp.asarray(ltri, MXU_DTYPE), jnp.asarray(smask, F32), jnp.asarray(eye, F32),
            jnp.asarray(jones, MXU_DTYPE))


def _stack_param(p):
    return jnp.repeat(p.reshape(4, MXU_TILE).astype(F32), RWKV_CHUNK, axis=0)


def _rwkv_scan(r, k, v, kk, g, ld, a, rk, lnw, lnb):
    b, t, d = r.shape
    c = RWKV_CHUNK
    blk = pl.BlockSpec((1, c, d), lambda i, j: (i, j, 0))
    consts = list(_scan_constants()) + [_stack_param(rk.reshape(-1)), _stack_param(lnw), _stack_param(lnb)]
    return pl.pallas_call(
        _rwkv_scan_kernel,
        grid=(b, t // c),
        in_specs=[blk] * 7 + [_const_spec(x.shape) for x in consts],
        out_specs=blk,
        out_shape=jax.ShapeDtypeStruct((b, t, d), ACT_DTYPE),
        scratch_shapes=[pltpu.VMEM((d // LANES, LANES, LANES), F32)],
        compiler_params=_params(("arbitrary", "arbitrary")),
        name="rwkv_scan",
    )(r, k, v, kk, g, ld, a, *consts)


def _matmul_res_kernel(a_ref, w_ref, res_ref, o_ref):
    o_ref[...] = res_ref[...] + jnp.dot(a_ref[...], w_ref[...], preferred_element_type=F32)


def _matmul_res(a, w, res, tm=512):
    n, kdim = a.shape
    d = w.shape[1]
    return pl.pallas_call(
        _matmul_res_kernel,
        grid=(n // tm,),
        in_specs=[pl.BlockSpec((tm, kdim), lambda i: (i, 0)), _const_spec(w.shape),
                  pl.BlockSpec((tm, d), lambda i: (i, 0))],
        out_specs=pl.BlockSpec((tm, d), lambda i: (i, 0)),
        out_shape=jax.ShapeDtypeStruct((n, d), F32),
        compiler_params=_params(("arbitrary",)),
        name="matmul_res",
    )(a, w, res)


def _rwkv_block(x, a_ln, a_mu, a_wr, a_wk, a_wv, a_wo, a_w0, a_w1, a_w2, a_a0, a_a1, a_a2,
                a_g1, a_g2, a_kk, a_ka, a_rk, a_lnx_w, a_lnx_b):
    b, t, d = x.shape
    vecs = jnp.stack([a_w0, a_a0, a_kk, a_ka]).astype(F32)
    r, k, v, kk, g, ld, a = _rwkv_pre(
        x, a_ln.reshape(1, d), a_mu, _mx(a_wr), _mx(a_wk), _mx(a_wv), _mx(a_w1), _mx(a_a1), _mx(a_g1),
        _mx(a_w2), _mx(a_a2), _mx(a_g2), vecs)
    y = _rwkv_scan(r, k, v, kk, g, ld, a, a_rk, a_lnx_w, a_lnx_b)
    return _matmul_res(y.reshape(b * t, d), _mx(a_wo), x.reshape(b * t, d)).reshape(b, t, d)


MOE_GROUPS = 4
MOE_EPG = 8
MOE_EXPERTS = MOE_GROUPS * MOE_EPG
MOE_TOPK = 2
MOE_BLOCK = 256
NEG = -1e30


def _route(logits):
    lane = lax.broadcasted_iota(jnp.int32, logits.shape, 1)
    is_grp = lane < MOE_GROUPS
    gl = jnp.where(is_grp, logits, NEG)
    ge = jnp.exp(gl - jnp.max(gl, axis=-1, keepdims=True))
    gp = ge / jnp.sum(ge, axis=-1, keepdims=True)
    grp_w = jnp.max(gp, axis=-1, keepdims=True)
    grp = jnp.min(jnp.where(is_grp & (gp == grp_w), lane, LANES), axis=-1, keepdims=True)
    lo = MOE_GROUPS + MOE_EPG * grp
    in_grp = (lane >= lo) & (lane < lo + MOE_EPG)
    el = jnp.where(in_grp, logits, NEG)
    ee = jnp.exp(el - jnp.max(el, axis=-1, keepdims=True))
    ep = ee / jnp.sum(ee, axis=-1, keepdims=True)
    p1 = jnp.max(jnp.where(in_grp, ep, -1.0), axis=-1, keepdims=True)
    i1 = jnp.min(jnp.where(in_grp & (ep == p1), lane, LANES), axis=-1, keepdims=True)
    rest = in_grp & (lane != i1)
    p2 = jnp.max(jnp.where(rest, ep, -1.0), axis=-1, keepdims=True)
    i2 = jnp.min(jnp.where(rest & (ep == p2), lane, LANES), axis=-1, keepdims=True)
    scale = grp_w / (p1 + p2)
    rec = jnp.where(lane == 0, (i1 - MOE_GROUPS).astype(F32),
                    jnp.where(lane == 1, (i2 - MOE_GROUPS).astype(F32),
                              jnp.where(lane == 2, p1 * scale, jnp.where(lane == 3, p2 * scale, 0.0))))
    return rec


def _router_logits(xf, w_hi_ref, w_lo_ref, bias_ref):
    hi, lo = _split2(xf)
    logits = (jnp.dot(hi, w_hi_ref[...], preferred_element_type=F32)
              + jnp.dot(lo, w_hi_ref[...], preferred_element_type=F32)
              + jnp.dot(hi, w_lo_ref[...], preferred_element_type=F32))
    return logits + bias_ref[...]


def _moe_router_kernel(h_ref, ln_ref, w_hi_ref, w_lo_ref, bias_ref, route_out):
    xf = _rms(h_ref[...], ln_ref[...])
    route_out[...] = _route(_router_logits(xf, w_hi_ref, w_lo_ref, bias_ref))


def _router_weights(w_gr, b_gr, w_er, b_er):
    d = w_gr.shape[0]
    w = jnp.zeros((d, LANES), F32).at[:, :MOE_GROUPS].set(w_gr).at[:, MOE_GROUPS:MOE_GROUPS + MOE_EXPERTS].set(w_er)
    bias = jnp.zeros((1, LANES), F32).at[0, :MOE_GROUPS].set(b_gr).at[0, MOE_GROUPS:MOE_GROUPS + MOE_EXPERTS].set(b_er)
    w_hi = w.astype(MXU_DTYPE)
    w_lo = (w - w_hi.astype(F32)).astype(MXU_DTYPE)
    return w_hi, w_lo, bias


def _moe_router(h, ln, w_hi, w_lo, bias, tm=512):
    n, d = h.shape
    row = pl.BlockSpec((tm, d), lambda i: (i, 0))
    return pl.pallas_call(
        _moe_router_kernel,
        grid=(n // tm,),
        in_specs=[row, _const_spec((1, d)), _const_spec(w_hi.shape), _const_spec(w_lo.shape),
                  _const_spec(bias.shape)],
        out_specs=pl.BlockSpec((tm, LANES), lambda i: (i, 0)),
        out_shape=jax.ShapeDtypeStruct((n, LANES), F32),
        compiler_params=_params(("arbitrary",)),
        name="moe_router",
    )(h, ln.reshape(1, d), w_hi, w_lo, bias)


def _moe_dispatch(route, blk):
    n = route.shape[0]
    n_as = n * MOE_TOPK
    flat_e = route[:, :MOE_TOPK].astype(jnp.int32).reshape(n_as)
    onehot = (flat_e[:, None] == jnp.arange(MOE_EXPERTS, dtype=jnp.int32)[None, :]).astype(jnp.int32)
    csum = jnp.cumsum(onehot, axis=0)
    rank = jnp.take_along_axis(csum, flat_e[:, None], axis=1)[:, 0] - 1
    counts = csum[-1]
    pcounts = (counts + blk - 1) // blk * blk
    pend = jnp.cumsum(pcounts)
    dest = (pend - pcounts)[flat_e] + rank
    n_blk = n_as // blk + MOE_EXPERTS
    p = n_blk * blk
    buf_f = jnp.full((p,), -1, jnp.int32).at[dest].set(jnp.arange(n_as, dtype=jnp.int32))
    blk_start = jnp.arange(n_blk, dtype=jnp.int32) * blk
    blk_e = jnp.clip(jnp.searchsorted(pend, blk_start, side='right'), 0, MOE_EXPERTS - 1).astype(jnp.int32)
    seg_end = (pend - pcounts + counts)[blk_e]
    n_valid = jnp.clip(seg_end - blk_start, 0, blk).astype(jnp.int32)
    n_act = (pend[-1] // blk).astype(jnp.int32).reshape(1)
    tok = jnp.maximum(buf_f, 0) // MOE_TOPK
    return blk_e, n_valid, n_act, tok.reshape(n_blk, 1, blk), jnp.maximum(buf_f, 0).reshape(n_blk, 1, blk)


def _moe_expert_kernel(blk_e_ref, n_valid_ref, n_act_ref,
                       tok_ref, tok_next_ref, dst_ref, xf_hbm, ln_ref, w1_ref, w3_ref, w2_ref, out_hbm,
                       xbuf, obuf, gsem, ssem):
    del blk_e_ref
    i = pl.program_id(0)
    n_act = n_act_ref[0]
    blk = xbuf.shape[1]
    slot = lax.rem(i, 2)

    def gather_copy(tok_row_ref, r, s):
        return pltpu.make_async_copy(xf_hbm.at[pl.ds(tok_row_ref[0, 0, r], 1)], xbuf.at[s, pl.ds(r, 1)], gsem.at[s])

    def scatter_copy(r, s):
        return pltpu.make_async_copy(obuf.at[s, pl.ds(r, 1)], out_hbm.at[pl.ds(dst_ref[0, 0, r], 1)], ssem.at[s])

    def scatter_wait(count, s):
        def body(r, carry):
            pltpu.make_async_copy(obuf.at[s, pl.ds(0, 1)], out_hbm.at[pl.ds(0, 1)], ssem.at[s]).wait()
            return carry
        lax.fori_loop(0, count, body, 0)

    @pl.when(i < n_act)
    def _():
        @pl.when(i == 0)
        def _():
            lax.fori_loop(0, blk, lambda r, c: (gather_copy(tok_ref, r, 0).start(), c)[1], 0)

        @pl.when(i + 1 < n_act)
        def _():
            lax.fori_loop(0, blk, lambda r, c: (gather_copy(tok_next_ref, r, 1 - slot).start(), c)[1], 0)

        lax.fori_loop(0, blk, lambda r, c: (gather_copy(tok_ref, r, slot).wait(), c)[1], 0)
        x = _mx(_rms(xbuf[slot], ln_ref[...]))
        h1 = jnp.dot(x, w1_ref[0], preferred_element_type=F32)
        h3 = jnp.dot(x, w3_ref[0], preferred_element_type=F32)
        y = _dot(h1 * jax.nn.sigmoid(h1) * h3, w2_ref[0])

        @pl.when(i >= 2)
        def _():
            scatter_wait(n_valid_ref[i - 2], slot)

        obuf[slot] = y
        nv = n_valid_ref[i]
        lax.fori_loop(0, nv, lambda r, c: (scatter_copy(r, slot).start(), c)[1], 0)

        @pl.when(i == n_act - 1)
        def _():
            @pl.when(i >= 1)
            def _():
                scatter_wait(n_valid_ref[i - 1], 1 - slot)
            scatter_wait(nv, slot)


def _moe_experts(xf, ln, w1, w3, w2, blk_e, n_valid, n_act, tok, dst):
    n, d = xf.shape
    n_blk, _, blk = tok.shape
    hdim = w1.shape[2]
    smem_row = lambda f: pl.BlockSpec((1, 1, blk), f, memory_space=pltpu.SMEM)
    grid_spec = pltpu.PrefetchScalarGridSpec(
        num_scalar_prefetch=3,
        grid=(n_blk,),
        in_specs=[smem_row(lambda i, be, nv, na: (i, 0, 0)),
                  smem_row(lambda i, be, nv, na: (jnp.minimum(i + 1, n_blk - 1), 0, 0)),
                  smem_row(lambda i, be, nv, na: (i, 0, 0)),
                  pl.BlockSpec(memory_space=pl.ANY),
                  pl.BlockSpec((1, d), lambda i, be, nv, na: (0, 0)),
                  pl.BlockSpec((1, d, hdim), lambda i, be, nv, na: (be[i], 0, 0)),
                  pl.BlockSpec((1, d, hdim), lambda i, be, nv, na: (be[i], 0, 0)),
                  pl.BlockSpec((1, hdim, d), lambda i, be, nv, na: (be[i], 0, 0))],
        out_specs=pl.BlockSpec(memory_space=pl.ANY),
        scratch_shapes=[pltpu.VMEM((2, blk, d), xf.dtype), pltpu.VMEM((2, blk, d), F32),
                        pltpu.SemaphoreType.DMA((2,)), pltpu.SemaphoreType.DMA((2,))],
    )
    return pl.pallas_call(
        _moe_expert_kernel,
        grid_spec=grid_spec,
        out_shape=jax.ShapeDtypeStruct((n * MOE_TOPK, d), F32),
        compiler_params=_params(("arbitrary",)),
        name="moe_experts",
    )(blk_e, n_valid, n_act, tok, tok, dst, xf, ln.reshape(1, d), w1, w3, w2)


def _moe_layer(h, ln, w_gr, b_gr, w_er, b_er, w1, w3, w2):
    n, d = h.shape
    route = _moe_router(h, ln, *_router_weights(w_gr, b_gr, w_er, b_er))
    slots = _moe_experts(h, ln, _mx(w1), _mx(w3), _mx(w2), *_moe_dispatch(route, MOE_BLOCK))
    return slots.reshape(n, MOE_TOPK * d), route


def _moe_combine(h, slots, route):
    d = h.shape[-1]
    return h + route[:, 2:3] * slots[:, :d] + route[:, 3:4] * slots[:, d:]


def _final_kernel(h_ref, slots_ref, route_ref, ln_ref, o_ref):
    o_ref[...] = _rms(_moe_combine(h_ref[...], slots_ref[...], route_ref[...]), ln_ref[...])


def _final(h, slots, route, ln, tm=512):
    n, d = h.shape
    return pl.pallas_call(
        _final_kernel,
        grid=(n // tm,),
        in_specs=[pl.BlockSpec((tm, d), lambda i: (i, 0)), pl.BlockSpec((tm, MOE_TOPK * d), lambda i: (i, 0)),
                  pl.BlockSpec((tm, LANES), lambda i: (i, 0)), _const_spec((1, d))],
        out_specs=pl.BlockSpec((tm, d), lambda i: (i, 0)),
        out_shape=jax.ShapeDtypeStruct((n, d), F32),
        compiler_params=_params(("arbitrary",)),
        name="final_norm",
    )(h, slots, route, ln.reshape(1, d))


NSA_HEADS = 16
NSA_GROUPS = 4
NSA_HPG = NSA_HEADS // NSA_GROUPS
NSA_DH = D_MODEL // NSA_HEADS
NSA_PARTS = 6
CMP_BLOCK = 32
CMP_STRIDE = 16
SEL_BLOCK = 64
SEL_TOPK = 8
WINDOW = 512
N_BRANCH = 3
IMP_FORCE = 1e4
REL_BUCKETS = 32
REL_MAX_DIST = 128
NSA_TQ = 128
NSA_TK = 256
GATE_LANES = LANES


def _nsa_proj_kernel(h_ref, slots_ref, route_ref, kvln_ref, qln_ref, kvw_ref, qw_ref,
                     h_out, q_out, gate_out, kv_out):
    h1 = _moe_combine(h_ref[0], slots_ref[0], route_ref[0])
    h_out[0] = h1
    kv = _dot(_rms(h1, kvln_ref[...]), kvw_ref[...])
    for part in range(NSA_PARTS):
        for g in range(NSA_GROUPS):
            col = (part * NSA_GROUPS + g) * NSA_DH
            kv_out[part, 0, g] = kv[:, col:col + NSA_DH].astype(kv_out.dtype)
    qg = _dot(_rms(h1, qln_ref[...]), qw_ref[...])
    q_out[0] = (qg[:, :D_MODEL] * (NSA_DH ** -0.5)).astype(q_out.dtype)
    gate_out[0] = qg[:, D_MODEL:]


def _nsa_proj(h, slots, route, kv_ln, q_ln, kv_w, q_w, b, t, tm=256):
    d = h.shape[-1]
    blk = lambda w: pl.BlockSpec((1, tm, w), lambda i, j: (i, j, 0))
    gw = NSA_GROUPS * GATE_LANES
    qw = jnp.zeros((d, d + gw), F32).at[:, :d].set(q_w[:, :d])
    for g in range(NSA_GROUPS):
        n_gate = NSA_HPG * N_BRANCH
        qw = qw.at[:, d + g * GATE_LANES:d + g * GATE_LANES + n_gate].set(q_w[:, d + g * n_gate:d + (g + 1) * n_gate])
    kv_shape = (NSA_PARTS, b, NSA_GROUPS, t, NSA_DH)
    return pl.pallas_call(
        _nsa_proj_kernel,
        grid=(b, t // tm),
        in_specs=[blk(d), blk(MOE_TOPK * d), blk(LANES), _const_spec((1, d)), _const_spec((1, d)),
                  _const_spec(kv_w.shape), _const_spec(qw.shape)],
        out_specs=[blk(d), blk(d), blk(gw),
                   pl.BlockSpec((NSA_PARTS, 1, NSA_GROUPS, tm, NSA_DH), lambda i, j: (0, i, 0, j, 0))],
        out_shape=[jax.ShapeDtypeStruct((b, t, d), F32), jax.ShapeDtypeStruct((b, t, d), ACT_DTYPE),
                   jax.ShapeDtypeStruct((b, t, gw), F32), jax.ShapeDtypeStruct(kv_shape, ACT_DTYPE)],
        compiler_params=_params(("arbitrary", "arbitrary")),
        name="nsa_proj",
    )(h.reshape(b, t, d), slots.reshape(b, t, MOE_TOPK * d), route.reshape(b, t, LANES),
      kv_ln.reshape(1, d), q_ln.reshape(1, d), _mx(kv_w), _mx(qw))


def _gelu_tanh(x):
    return 0.5 * x * (1.0 + jnp.tanh(math.sqrt(2.0 / math.pi) * (x + 0.044715 * (x * x * x))))


def _compress_kernel(k_ref, v_ref, pek_ref, pev_ref, kw1_ref, kw2_ref, vw1_ref, vw2_ref, kc_out, vc_out):
    def one(z_ref, pe_ref, w1_ref, w2_ref, out):
        z = z_ref[0, 0, 0].astype(F32)
        half = z.shape[1]
        first = _dot(z + pe_ref[0, 0:1, :], w1_ref[:half, :])
        second = _dot(z + pe_ref[0, 1:2, :], w1_ref[half:, :])
        hid = first + pltpu.roll(second, z.shape[0] - 1, axis=0)
        out[0, 0] = _dot(_gelu_tanh(hid), w2_ref[...]).astype(out.dtype)
    one(k_ref, pek_ref, kw1_ref, kw2_ref, kc_out)
    one(v_ref, pev_ref, vw1_ref, vw2_ref, vc_out)


def _compress(kv, pe_k, pe_v, k_w1, k_w2, v_w1, v_w2):
    _, b, g, t, dh = kv.shape
    n_chunk = t // CMP_STRIDE
    width = CMP_STRIDE * dh
    kv_c = kv.reshape(NSA_PARTS, b, g, n_chunk, width)
    pe = lambda p: p.reshape(g, 2, width).astype(F32)
    part = lambda idx: pl.BlockSpec((1, 1, 1, n_chunk, width), lambda i, j: (idx, i, j, 0, 0))
    pe_spec = pl.BlockSpec((1, 2, width), lambda i, j: (j, 0, 0))
    out_spec = pl.BlockSpec((1, 1, n_chunk, dh), lambda i, j: (i, j, 0, 0))
    out = jax.ShapeDtypeStruct((b, g, n_chunk, dh), ACT_DTYPE)
    return pl.pallas_call(
        _compress_kernel,
        grid=(b, g),
        in_specs=[part(0), part(1), pe_spec, pe_spec, _const_spec(k_w1.shape), _const_spec(k_w2.shape),
                  _const_spec(v_w1.shape), _const_spec(v_w2.shape)],
        out_specs=[out_spec, out_spec],
        out_shape=[out, out],
        compiler_params=_params(("arbitrary", "arbitrary")),
        name="nsa_compress",
    )(kv_c, kv_c, pe(pe_k), pe(pe_v), _mx(k_w1), _mx(k_w2), _mx(v_w1), _mx(v_w2))


def _rel_bucket(dist):
    n = jnp.maximum(dist, 0)
    max_exact = REL_BUCKETS // 2
    nf = jnp.maximum(n, 1).astype(F32)
    large = max_exact + (jnp.log(nf / max_exact) / math.log(REL_MAX_DIST / max_exact)
                         * (REL_BUCKETS - max_exact)).astype(jnp.int32)
    return jnp.where(n < max_exact, n, jnp.minimum(large, REL_BUCKETS - 1))


def _nsa_bias_tiles(rel_bias, t):
    tq, tk = NSA_TQ, NSA_TK
    table = rel_bias.astype(F32).reshape(REL_BUCKETS, NSA_GROUPS, NSA_HPG).transpose(1, 2, 0)
    r = jnp.arange(tq)

    def tile(dist, valid, shift=None):
        b = table[:, :, _rel_bucket(dist)]
        if shift is not None:
            b = b - shift
        b = jnp.where(valid, b, NEG)
        return b.reshape(NSA_GROUPS, NSA_HPG * dist.shape[0], dist.shape[1])

    n_cmp = t // CMP_STRIDE
    cmp_end = jnp.arange(n_cmp) * CMP_STRIDE + CMP_BLOCK - 1
    cmp_ok = jnp.arange(n_cmp) < (t - CMP_BLOCK) // CMP_STRIDE + 1
    cmp_tiles = []
    for i in range(t // tq):
        dist = (i * tq + r)[:, None] - cmp_end[None, :]
        cmp_tiles.append(tile(dist, (dist >= 0) & cmp_ok[None, :]))
    bias_cmp = jnp.stack(cmp_tiles, axis=1)
    wk = WINDOW + tq
    win_tiles = []
    for v in range(WINDOW // tq + 1):
        start = max(v * tq - WINDOW, 0)
        dist = (v * tq + r)[:, None] - (start + jnp.arange(wk))[None, :]
        win_tiles.append(tile(dist, (dist >= 0) & (dist < WINDOW)))
    bias_win = jnp.stack(win_tiles, axis=1)
    far = table[:, :, REL_BUCKETS - 1][:, :, None, None]
    n_near = (REL_MAX_DIST + tk - 1) // tq + 1
    sel_tiles = []
    for e in range(n_near):
        dist = (e * tq + r)[:, None] - jnp.arange(tk)[None, :]
        sel_tiles.append(tile(dist, dist >= 0, far))
    sel_tiles.append(jnp.zeros_like(sel_tiles[0]))
    bias_sel = jnp.stack(sel_tiles, axis=1)
    return bias_cmp, bias_win, bias_sel


def _nsa_constants(t):
    n_sel = t // SEL_BLOCK
    n_cmp = t // CMP_STRIDE
    cmp_start = np.arange(n_cmp) * CMP_STRIDE
    sel_start = np.arange(n_sel) * SEL_BLOCK
    overlap = ((cmp_start[:, None] < sel_start[None, :] + SEL_BLOCK)
               & (cmp_start[:, None] + CMP_BLOCK > sel_start[None, :]))
    overlap[(t - CMP_BLOCK) // CMP_STRIDE + 1:, :] = False
    per = NSA_TK // SEL_BLOCK
    expand = np.zeros((t // NSA_TK, LANES, NSA_TK), np.float32)
    for c in range(t // NSA_TK):
        for s in range(NSA_TK):
            expand[c, per * c + s // SEL_BLOCK, s] = 1.0
    return jnp.asarray(overlap.T, MXU_DTYPE), jnp.asarray(expand, MXU_DTYPE)


def _nsa_attn_kernel(q_ref, gate_ref, kc_ref, vc_ref, ks_ref, vs_ref, kw_ref, vw_ref,
                     bcmp_ref, bwin_ref, bsel_ref, ovl_ref, exp_ref, o_ref):
    tq, tk, dh = NSA_TQ, NSA_TK, NSA_DH
    i = pl.program_id(2)
    t0 = i * tq
    q = q_ref[0]
    qs = jnp.concatenate([q[:, dh * h:dh * (h + 1)] for h in range(NSA_HPG)], axis=0)

    lc = _dot_nt(qs, kc_ref[0, 0]) + bcmp_ref[0, 0]
    mc = jnp.max(lc, axis=-1, keepdims=True)
    pcu = jnp.where(mc > 0.5 * NEG, jnp.exp(lc - mc), 0.0)
    pc = pcu / jnp.maximum(jnp.sum(pcu, axis=-1, keepdims=True), 1e-30)
    oc = _dot(pc, vc_ref[0, 0])

    pcsum = pc[:tq] + pc[tq:2 * tq] + pc[2 * tq:3 * tq] + pc[3 * tq:]
    hi = pcsum.astype(MXU_DTYPE)
    mid = (pcsum - hi.astype(F32))
    lo = (mid - mid.astype(MXU_DTYPE).astype(F32)).astype(MXU_DTYPE)
    ovl = ovl_ref[...]
    imp = (lax.dot_general(ovl, hi, (((1,), (1,)), ((), ())), preferred_element_type=F32)
           + lax.dot_general(ovl, mid.astype(MXU_DTYPE), (((1,), (1,)), ((), ())), preferred_element_type=F32)
           + lax.dot_general(ovl, lo, (((1,), (1,)), ((), ())), preferred_element_type=F32))
    n_sel = imp.shape[0]
    jj = lax.broadcasted_iota(jnp.int32, (n_sel, 1), 0)
    blk_q = lax.shift_right_logical(t0 + lax.broadcasted_iota(jnp.int32, (1, tq), 1),
                                    int(math.log2(SEL_BLOCK)))
    forced = (jj == 0) | (jj == blk_q) | (jj == blk_q - 1)
    score = jnp.where(forced, IMP_FORCE, jnp.where(jj <= blk_q, imp, -1.0))
    rank = jnp.zeros_like(score)
    for j in range(n_sel):
        row = score[j:j + 1, :]
        tie = (jj > j).astype(F32)
        rank = rank + jnp.where(row > score, 1.0, jnp.where(row == score, tie, 0.0))
    sel_t = (rank < SEL_TOPK).astype(F32)
    sel = jnp.concatenate([sel_t, jnp.zeros((LANES - n_sel, tq), F32)], axis=0).T.astype(MXU_DTYPE)

    def sel_step(c, carry):
        m, l, acc = carry
        start = pl.multiple_of(c * tk, tk)
        kc = ks_ref[0, 0, 0, pl.ds(start, tk), :]
        vc = vs_ref[0, 0, 0, pl.ds(start, tk), :]
        keep = jnp.dot(sel, exp_ref[c], preferred_element_type=F32)
        neg = (keep - 1.0) * (-NEG)
        near = jnp.minimum((t0 - start) // tq, bsel_ref.shape[1] - 1)
        lg = _dot_nt(qs, kc) + bsel_ref[0, near] + jnp.concatenate([neg] * NSA_HPG, axis=0)
        m_new = jnp.maximum(m, jnp.max(lg, axis=-1, keepdims=True))
        alpha = jnp.exp(m - m_new)
        p = jnp.exp(lg - m_new)
        return m_new, alpha * l + jnp.sum(p, axis=-1, keepdims=True), alpha * acc + _dot(p, vc)

    init = (jnp.full((NSA_HPG * tq, 1), NEG, F32), jnp.zeros((NSA_HPG * tq, 1), F32),
            jnp.zeros((NSA_HPG * tq, dh), F32))
    _, ls, accs = lax.fori_loop(0, (t0 + tq + tk - 1) // tk, sel_step, init)
    osel = accs / ls

    w0 = pl.multiple_of(jnp.maximum(t0 - WINDOW, 0), tq)
    wk = WINDOW + tq
    lw = _dot_nt(qs, kw_ref[0, 0, 0, pl.ds(w0, wk), :]) + bwin_ref[0, 0]
    pw = jnp.exp(lw - jnp.max(lw, axis=-1, keepdims=True))
    ow = _dot(pw, vw_ref[0, 0, 0, pl.ds(w0, wk), :]) / jnp.sum(pw, axis=-1, keepdims=True)

    gates = jax.nn.sigmoid(gate_ref[0])
    outs = []
    for h in range(NSA_HPG):
        rows = slice(h * tq, (h + 1) * tq)
        g = [gates[:, N_BRANCH * h + br:N_BRANCH * h + br + 1] for br in range(N_BRANCH)]
        outs.append(g[0] * oc[rows] + g[1] * osel[rows] + g[2] * ow[rows])
    o_ref[0] = jnp.concatenate(outs, axis=1).astype(o_ref.dtype)


def _nsa_attention(q, gate_logits, kv, k_cmp, v_cmp, rel_bias):
    b, t, d = q.shape
    tq, tk, dh = NSA_TQ, NSA_TK, NSA_DH
    bias_cmp, bias_win, bias_sel = _nsa_bias_tiles(rel_bias, t)
    ovl, expand = _nsa_constants(t)
    n_cmp = k_cmp.shape[2]
    n_win = bias_win.shape[1]
    rows = NSA_HPG * tq
    part = lambda idx: pl.BlockSpec((1, 1, 1, t, dh), lambda g, bi, i: (idx, bi, g, 0, 0))
    cmp_spec = pl.BlockSpec((1, 1, n_cmp, dh), lambda g, bi, i: (bi, g, 0, 0))
    return pl.pallas_call(
        _nsa_attn_kernel,
        grid=(NSA_GROUPS, b, t // tq),
        in_specs=[pl.BlockSpec((1, tq, NSA_HPG * dh), lambda g, bi, i: (bi, i, g)),
                  pl.BlockSpec((1, tq, GATE_LANES), lambda g, bi, i: (bi, i, g)),
                  cmp_spec, cmp_spec, part(2), part(3), part(4), part(5),
                  pl.BlockSpec((1, 1, rows, n_cmp), lambda g, bi, i: (g, i, 0, 0)),
                  pl.BlockSpec((1, 1, rows, WINDOW + tq), lambda g, bi, i: (g, jnp.minimum(i, n_win - 1), 0, 0)),
                  pl.BlockSpec((1, bias_sel.shape[1], rows, tk), lambda g, bi, i: (g, 0, 0, 0)),
                  _const_spec(ovl.shape), _const_spec(expand.shape)],
        out_specs=pl.BlockSpec((1, tq, NSA_HPG * dh), lambda g, bi, i: (bi, i, g)),
        out_shape=jax.ShapeDtypeStruct((b, t, d), ACT_DTYPE),
        compiler_params=_params(("arbitrary", "arbitrary", "arbitrary")),
        name="nsa_attention",
    )(q, gate_logits, k_cmp, v_cmp, kv, kv, kv, kv, bias_cmp, bias_win, bias_sel, ovl, expand)


def _nsa_block(h, slots, route, kv_ln, kv_w, cmp_pe_k, cmp_pe_v, cmp_k_w1, cmp_k_w2, cmp_v_w1, cmp_v_w2,
               rel_bias, b_ln, b_wqg, b_wo, b, t):
    n, d = h.shape
    h1, q, gate_logits, kv = _nsa_proj(h, slots, route, kv_ln, b_ln, kv_w, b_wqg, b, t)
    k_cmp, v_cmp = _compress(kv, cmp_pe_k, cmp_pe_v, cmp_k_w1, cmp_k_w2, cmp_v_w1, cmp_v_w2)
    o = _nsa_attention(q, gate_logits, kv, k_cmp, v_cmp, rel_bias)
    return _matmul_res(o.reshape(n, d), _mx(b_wo), h1.reshape(n, d))


def kernel(x, a_ln, a_mu, a_wr, a_wk, a_wv, a_wo, a_w0, a_w1, a_w2, a_a0, a_a1, a_a2, a_g1, a_g2, a_kk, a_ka, a_rk, a_lnx_w, a_lnx_b, kv_ln, kv_w, cmp_pe_k, cmp_pe_v, cmp_k_w1, cmp_k_w2, cmp_v_w1, cmp_v_w2, rel_bias, b_ln, b_wqg, b_wo, m_ln, m_wg, m_bg, m_we, m_be, m_w1, m_w3, m_w2, final_ln):
    b, t, d = x.shape
    n = b * t
    h = _rwkv_block(x, a_ln[0], a_mu[0], a_wr[0], a_wk[0], a_wv[0], a_wo[0], a_w0[0], a_w1[0], a_w2[0],
                    a_a0[0], a_a1[0], a_a2[0], a_g1[0], a_g2[0], a_kk[0], a_ka[0], a_rk[0],
                    a_lnx_w[0], a_lnx_b[0]).reshape(n, d)
    moe = lambda hh, l: _moe_layer(hh, m_ln[l], m_wg[l], m_bg[l], m_we[l], m_be[l], m_w1[l], m_w3[l], m_w2[l])
    slots, route = moe(h, 0)
    h = _nsa_block(h, slots, route, kv_ln, kv_w, cmp_pe_k, cmp_pe_v, cmp_k_w1, cmp_k_w2, cmp_v_w1, cmp_v_w2,
                   rel_bias, b_ln[0], b_wqg[0], b_wo[0], b, t)
    slots, route = moe(h, 1)
    return _final(h, slots, route, final_ln).reshape(b, t, d)
```

```python
import functools
import math

import numpy as np
import jax
import jax.numpy as jnp
from jax import lax
from jax.experimental import pallas as pl
from jax.experimental.pallas import tpu as pltpu

F32 = jnp.float32
MXU_DTYPE = jnp.bfloat16
ACT_DTYPE = jnp.bfloat16

D_MODEL = 1024
NORM_EPS = 1e-6
RWKV_HEAD = 64
RWKV_HEADS = D_MODEL // RWKV_HEAD
RWKV_GN_EPS = 64e-5
RWKV_CHUNK = 64
LANES = 128
MXU_TILE = 256
VMEM_LIMIT = 56 * 1024 * 1024


def _mx(x):
    return x.astype(MXU_DTYPE)


def _dot(a, b):
    return jnp.dot(_mx(a), _mx(b), preferred_element_type=F32)


def _dot_nt(a, b):
    return lax.dot_general(_mx(a), _mx(b), (((1,), (1,)), ((), ())), preferred_element_type=F32)


def _split2(x):
    hi = x.astype(MXU_DTYPE)
    lo = (x - hi.astype(F32)).astype(MXU_DTYPE)
    return hi, lo


def _dot_hl(x, w):
    hi, lo = _split2(x)
    return jnp.dot(hi, w, preferred_element_type=F32) + jnp.dot(lo, w, preferred_element_type=F32)


def _rms(x, g):
    return x * lax.rsqrt(jnp.mean(x * x, axis=-1, keepdims=True) + NORM_EPS) * g


def _const_spec(shape):
    nd = len(shape)
    return pl.BlockSpec(shape, lambda *_: (0,) * nd)


def _params(sem):
    return pltpu.CompilerParams(dimension_semantics=sem, vmem_limit_bytes=VMEM_LIMIT)


def _rwkv_pre_kernel(x_ref, ln_ref, mu_ref, wr_ref, wk_ref, wv_ref, w1_ref, a1_ref, g1_ref,
                     w2_ref, a2_ref, g2_ref, vec_ref,
                     r_out, k_out, v_out, kk_out, g_out, ld_out, a_out, prev_ref):
    tm = x_ref.shape[1]
    hn = _rms(x_ref[0], ln_ref[...])

    @pl.when(pl.program_id(1) == 0)
    def _():
        prev_ref[...] = jnp.zeros_like(prev_ref)

    rows = lax.broadcasted_iota(jnp.int32, (tm, 1), 0)
    shifted = jnp.where(rows == 0, prev_ref[0:1, :], pltpu.roll(hn, 1, axis=0))
    prev_ref[0:1, :] = hn[tm - 1:tm, :]
    xx = shifted - hn
    xr, xw, xk, xv, xa, xg = (_mx(hn + xx * mu_ref[i:i + 1, :]) for i in range(6))
    w0, a0, kkp, kap = (vec_ref[i:i + 1, :] for i in range(4))
    r = jnp.dot(xr, wr_ref[...], preferred_element_type=F32)
    k = jnp.dot(xk, wk_ref[...], preferred_element_type=F32)
    v = jnp.dot(xv, wv_ref[...], preferred_element_type=F32)
    z = w0 + _dot(jnp.tanh(jnp.dot(xw, w1_ref[...], preferred_element_type=F32)), w2_ref[...])
    a = jax.nn.sigmoid(a0 + _dot(jnp.dot(xa, a1_ref[...], preferred_element_type=F32), a2_ref[...]))
    g = _dot(jax.nn.sigmoid(jnp.dot(xg, g1_ref[...], preferred_element_type=F32)), g2_ref[...])
    ld_out[0] = (-math.exp(-0.5)) * jax.nn.sigmoid(z)
    a_out[0] = a
    r_out[0] = r.astype(r_out.dtype)
    v_out[0] = v.astype(v_out.dtype)
    kk_out[0] = (k * kkp).astype(kk_out.dtype)
    k_out[0] = (k * (1.0 + (a - 1.0) * kap)).astype(k_out.dtype)
    g_out[0] = g.astype(g_out.dtype)


def _rwkv_pre(x, ln, mu, wr, wk, wv, w1, a1, g1, w2, a2, g2, vecs, tm=256):
    b, t, d = x.shape
    act = jax.ShapeDtypeStruct((b, t, d), ACT_DTYPE)
    f32o = jax.ShapeDtypeStruct((b, t, d), F32)
    blk = pl.BlockSpec((1, tm, d), lambda i, j: (i, j, 0))
    consts = [ln, mu, wr, wk, wv, w1, a1, g1, w2, a2, g2, vecs]
    return pl.pallas_call(
        _rwkv_pre_kernel,
        grid=(b, t // tm),
        in_specs=[blk] + [_const_spec(c.shape) for c in consts],
        out_specs=[blk] * 7,
        out_shape=[act, act, act, act, act, f32o, f32o],
        scratch_shapes=[pltpu.VMEM((8, d), F32)],
        compiler_params=_params(("arbitrary", "arbitrary")),
        name="rwkv_pre",
    )(x, *consts)


def _stack4(x):
    return jnp.concatenate([x[:, MXU_TILE * q:MXU_TILE * (q + 1)] for q in range(4)], axis=0)


def _unstack4(y):
    c = y.shape[0] // 4
    return jnp.concatenate([y[c * q:c * (q + 1), :] for q in range(4)], axis=1)


def _rwkv_scan_kernel(r_ref, k_ref, v_ref, kk_ref, g_ref, ld_ref, a_ref,
                      ltri_ref, smask_ref, eye_ref, j_ref, rk_ref, lnw_ref, lnb_ref,
                      y_out, h_ref):
    c = RWKV_CHUNK

    @pl.when(pl.program_id(1) == 0)
    def _():
        h_ref[...] = jnp.zeros_like(h_ref)

    ld = ld_ref[0]
    r = r_ref[0].astype(F32)
    k = k_ref[0].astype(F32)
    v = v_ref[0].astype(F32)
    kk = kk_ref[0].astype(F32)
    jones = j_ref[...]
    ld_hi, ld_lo = _split2(ld)
    cum = (jnp.dot(ltri_ref[...], ld_hi, preferred_element_type=F32)
           + jnp.dot(ltri_ref[...], ld_lo, preferred_element_type=F32))
    cum_last = cum[c - 1:c, :]
    e_l = jnp.exp(cum)
    e_lm = jnp.exp(cum - ld)
    e_nl = jnp.exp(-cum)
    e_cl = jnp.exp(cum_last - cum)
    e_c = jnp.exp(cum_last)
    ss = _unstack4(_dot_hl(_stack4(kk * kk), jones))
    kkn = kk * lax.rsqrt(jnp.maximum(ss, 1e-24))
    a_vec = -kkn
    b_vec = kkn * a_ref[0]
    rt = r * e_l
    at = a_vec * e_lm
    kt = k * e_nl
    bt = b_vec * e_nl
    bh = b_vec * e_cl
    kh = k * e_cl

    lane = lax.broadcasted_iota(jnp.int32, (1, LANES), 1)
    m_l = (lane < RWKV_HEAD).astype(F32)
    m_r = 1.0 - m_l

    def st(xp):
        return jnp.concatenate([xp * m_l, xp * m_r], axis=0)

    def dup(xp):
        return jnp.concatenate([xp, xp], axis=0)

    smask = smask_ref[...] > 0.0
    eye = eye_ref[...]
    pairs = range(D_MODEL // LANES)
    sls = [slice(LANES * p, LANES * (p + 1)) for p in pairs]
    at_s = [st(at[:, sl]) for sl in sls]
    rt_s = [st(rt[:, sl]) for sl in sls]
    v_s = [st(v[:, sl]) for sl in sls]
    s = [jnp.where(smask,
                   _dot_nt(jnp.concatenate([at_s[p], rt_s[p]], axis=0),
                           jnp.concatenate([dup(bt[:, sls[p]]), dup(kt[:, sls[p]])], axis=0)),
                   0.0) for p in pairs]
    a_ab = [x[:2 * c, :2 * c] for x in s]
    a_r = [x[2 * c:, :] for x in s]
    g1 = [_dot(s[p][:2 * c, 2 * c:], v_s[p]) for p in pairs]
    t_m = [eye + x for x in a_ab]
    a_pow = [_dot(x, x) for x in a_ab]
    n = 2
    while n < c:
        if 2 * n < c:
            both = [_dot(a_pow[p], jnp.concatenate([a_pow[p], t_m[p]], axis=1)) for p in pairs]
            a_pow = [x[:, :2 * c] for x in both]
            t_m = [t_m[p] + both[p][:, 2 * c:] for p in pairs]
        else:
            t_m = [t_m[p] + _dot(a_pow[p], t_m[p]) for p in pairs]
        n *= 2
    tw = [_dot(t_m[p], jnp.concatenate([at_s[p], g1[p]], axis=1)) for p in pairs]
    zmat = [jnp.concatenate([tw[p], jnp.concatenate([jnp.zeros_like(v_s[p]), v_s[p]], axis=1)], axis=0)
            for p in pairs]
    upper = [_dot(a_r[p], zmat[p]) for p in pairs]
    bk_t = [jnp.concatenate([st(bh[:, sl]), st(kh[:, sl])], axis=0).T for sl in sls]
    lower = [_dot(bk_t[p], zmat[p]) for p in pairs]
    seq = [_dot(jnp.concatenate([upper[p][:, :LANES] + rt_s[p],
                                 lower[p][:, :LANES] + eye * e_c[:, sls[p]]], axis=0), h_ref[p])
           for p in pairs]
    ys = []
    for p in pairs:
        y_st = seq[p][:2 * c, :] + upper[p][:, LANES:]
        h_ref[p] = seq[p][2 * c:, :] + lower[p][:, LANES:]
        ys.append(y_st[:c, :] + y_st[c:, :])
    y = jnp.concatenate(ys, axis=1)

    inv_n = 1.0 / RWKV_HEAD
    y4 = _stack4(y)
    dlt = y4 - _dot_hl(y4, jones) * inv_n
    var = _dot_hl(dlt * dlt, jones) * inv_n
    yn = dlt * lax.rsqrt(var + RWKV_GN_EPS)
    bonus = _dot_hl(_stack4(r * k) * rk_ref[...], jones) * _stack4(v)
    out = (yn * lnw_ref[...] + lnb_ref[...] + bonus) * _stack4(g_ref[0].astype(F32))
    y_out[0] = _unstack4(out).astype(y_out.dtype)


def _scan_constants():
    c = RWKV_CHUNK
    ltri = np.tril(np.ones((c, c), np.float32))
    rho = np.arange(4 * c)
    r_type, r_head, r_t = rho // (2 * c), (rho % (2 * c)) // c, rho % c
    same = r_head[:, None] == r_head[None, :]
    strict = r_t[None, :] < r_t[:, None]
    incl = r_t[None, :] <= r_t[:, None]
    smask = same & np.where(r_type[:, None] == 0, strict, incl)
    eye = np.eye(LANES, dtype=np.float32)
    head = np.arange(MXU_TILE) // RWKV_HEAD
    jones = (head[:, None] == head[None, :]).astype(np.float32)
    return (jnp.asarray(ltri, MXU_DTYPE), jnp.asarray(smask, F32), jnp.asarray(eye, F32),
            jnp.asarray(jones, MXU_DTYPE))


def _stack_param(p):
    return jnp.repeat(p.reshape(4, MXU_TILE).astype(F32), RWKV_CHUNK, axis=0)


def _rwkv_scan(r, k, v, kk, g, ld, a, rk, lnw, lnb):
    b, t, d = r.shape
    c = RWKV_CHUNK
    blk = pl.BlockSpec((1, c, d), lambda i, j: (i, j, 0))
    consts = list(_scan_constants()) + [_stack_param(rk.reshape(-1)), _stack_param(lnw), _stack_param(lnb)]
    return pl.pallas_call(
        _rwkv_scan_kernel,
        grid=(b, t // c),
        in_specs=[blk] * 7 + [_const_spec(x.shape) for x in consts],
        out_specs=blk,
        out_shape=jax.ShapeDtypeStruct((b, t, d), ACT_DTYPE),
        scratch_shapes=[pltpu.VMEM((d // LANES, LANES, LANES), F32)],
        compiler_params=_params(("arbitrary", "arbitrary")),
        name="rwkv_scan",
    )(r, k, v, kk, g, ld, a, *consts)


def _matmul_res_kernel(a_ref, w_ref, res_ref, o_ref):
    o_ref[...] = res_ref[...] + jnp.dot(a_ref[...], w_ref[...], preferred_element_type=F32)


def _matmul_res(a, w, res, tm=512):
    n, kdim = a.shape
    d = w.shape[1]
    return pl.pallas_call(
        _matmul_res_kernel,
        grid=(n // tm,),
        in_specs=[pl.BlockSpec((tm, kdim), lambda i: (i, 0)), _const_spec(w.shape),
                  pl.BlockSpec((tm, d), lambda i: (i, 0))],
        out_specs=pl.BlockSpec((tm, d), lambda i: (i, 0)),
        out_shape=jax.ShapeDtypeStruct((n, d), F32),
        compiler_params=_params(("arbitrary",)),
        name="matmul_res",
    )(a, w, res)


def _rwkv_block(x, a_ln, a_mu, a_wr, a_wk, a_wv, a_wo, a_w0, a_w1, a_w2, a_a0, a_a1, a_a2,
                a_g1, a_g2, a_kk, a_ka, a_rk, a_lnx_w, a_lnx_b):
    b, t, d = x.shape
    vecs = jnp.stack([a_w0, a_a0, a_kk, a_ka]).astype(F32)
    r, k, v, kk, g, ld, a = _rwkv_pre(
        x, a_ln.reshape(1, d), a_mu, _mx(a_wr), _mx(a_wk), _mx(a_wv), _mx(a_w1), _mx(a_a1), _mx(a_g1),
        _mx(a_w2), _mx(a_a2), _mx(a_g2), vecs)
    y = _rwkv_scan(r, k, v, kk, g, ld, a, a_rk, a_lnx_w, a_lnx_b)
    return _matmul_res(y.reshape(b * t, d), _mx(a_wo), x.reshape(b * t, d)).reshape(b, t, d)


MOE_GROUPS = 4
MOE_EPG = 8
MOE_EXPERTS = MOE_GROUPS * MOE_EPG
MOE_TOPK = 2
MOE_BLOCK = 256
NEG = -1e30


def _route(logits):
    lane = lax.broadcasted_iota(jnp.int32, logits.shape, 1)
    is_grp = lane < MOE_GROUPS
    gl = jnp.where(is_grp, logits, NEG)
    ge = jnp.exp(gl - jnp.max(gl, axis=-1, keepdims=True))
    gp = ge / jnp.sum(ge, axis=-1, keepdims=True)
    grp_w = jnp.max(gp, axis=-1, keepdims=True)
    grp = jnp.min(jnp.where(is_grp & (gp == grp_w), lane, LANES), axis=-1, keepdims=True)
    lo = MOE_GROUPS + MOE_EPG * grp
    in_grp = (lane >= lo) & (lane < lo + MOE_EPG)
    el = jnp.where(in_grp, logits, NEG)
    ee = jnp.exp(el - jnp.max(el, axis=-1, keepdims=True))
    ep = ee / jnp.sum(ee, axis=-1, keepdims=True)
    p1 = jnp.max(jnp.where(in_grp, ep, -1.0), axis=-1, keepdims=True)
    i1 = jnp.min(jnp.where(in_grp & (ep == p1), lane, LANES), axis=-1, keepdims=True)
    rest = in_grp & (lane != i1)
    p2 = jnp.max(jnp.where(rest, ep, -1.0), axis=-1, keepdims=True)
    i2 = jnp.min(jnp.where(rest & (ep == p2), lane, LANES), axis=-1, keepdims=True)
    scale = grp_w / (p1 + p2)
    return i1 - MOE_GROUPS, i2 - MOE_GROUPS, p1 * scale, p2 * scale


def _router_logits(xf, w_hi_ref, w_lo_ref, bias_ref):
    hi, lo = _split2(xf)
    logits = (jnp.dot(hi, w_hi_ref[...], preferred_element_type=F32)
              + jnp.dot(lo, w_hi_ref[...], preferred_element_type=F32)
              + jnp.dot(hi, w_lo_ref[...], preferred_element_type=F32))
    return logits + bias_ref[...]


def _moe_router_kernel(h_ref, ln_ref, w_hi_ref, w_lo_ref, bias_ref, tri_ref, route_out, count_out, cnt_ref):
    @pl.when(pl.program_id(0) == 0)
    def _():
        cnt_ref[...] = jnp.zeros_like(cnt_ref)

    xf = _rms(h_ref[...], ln_ref[...])
    e1, e2, w1, w2 = _route(_router_logits(xf, w_hi_ref, w_lo_ref, bias_ref))
    lane = lax.broadcasted_iota(jnp.int32, (h_ref.shape[0], LANES), 1)
    hit1, hit2 = lane == e1, lane == e2
    onehot = jnp.where(hit1, 1.0, jnp.where(hit2, 1.0, 0.0))
    before = jnp.dot(tri_ref[...], _mx(onehot), preferred_element_type=F32) + cnt_ref[...]
    r1 = jnp.sum(jnp.where(hit1, before, 0.0), axis=-1, keepdims=True)
    r2 = jnp.sum(jnp.where(hit2, before, 0.0), axis=-1, keepdims=True)
    cnt_ref[...] = cnt_ref[...] + jnp.sum(onehot, axis=0, keepdims=True)
    count_out[...] = cnt_ref[...]
    fields = (e1.astype(F32), e2.astype(F32), w1, w2, r1, r2)
    rec = jnp.zeros(lane.shape, F32)
    for idx, val in enumerate(fields):
        rec = jnp.where(lane == idx, val, rec)
    route_out[...] = rec


def _router_weights(w_gr, b_gr, w_er, b_er):
    d = w_gr.shape[0]
    w = jnp.zeros((d, LANES), F32).at[:, :MOE_GROUPS].set(w_gr).at[:, MOE_GROUPS:MOE_GROUPS + MOE_EXPERTS].set(w_er)
    bias = jnp.zeros((1, LANES), F32).at[0, :MOE_GROUPS].set(b_gr).at[0, MOE_GROUPS:MOE_GROUPS + MOE_EXPERTS].set(b_er)
    w_hi = w.astype(MXU_DTYPE)
    w_lo = (w - w_hi.astype(F32)).astype(MXU_DTYPE)
    return w_hi, w_lo, bias


def _moe_router(h, ln, w_hi, w_lo, bias, tm=512):
    n, d = h.shape
    row = pl.BlockSpec((tm, d), lambda i: (i, 0))
    tri = jnp.asarray(np.tril(np.ones((tm, tm), np.float32), -1), MXU_DTYPE)
    return pl.pallas_call(
        _moe_router_kernel,
        grid=(n // tm,),
        in_specs=[row, _const_spec((1, d)), _const_spec(w_hi.shape), _const_spec(w_lo.shape),
                  _const_spec(bias.shape), _const_spec(tri.shape)],
        out_specs=[pl.BlockSpec((tm, LANES), lambda i: (i, 0)), _const_spec((1, LANES))],
        out_shape=[jax.ShapeDtypeStruct((n, LANES), F32), jax.ShapeDtypeStruct((1, LANES), F32)],
        scratch_shapes=[pltpu.VMEM((1, LANES), F32)],
        compiler_params=_params(("arbitrary",)),
        name="moe_router",
    )(h, ln.reshape(1, d), w_hi, w_lo, bias, tri)


def _moe_plan(route, counts, blk):
    n = route.shape[0]
    counts = counts[0, :MOE_EXPERTS].astype(jnp.int32)
    pcounts = (counts + blk - 1) // blk * blk
    pend = jnp.cumsum(pcounts)
    pstart = (pend - pcounts).astype(jnp.int32)
    n_blk = n * MOE_TOPK // blk + MOE_EXPERTS
    blk_start = jnp.arange(n_blk, dtype=jnp.int32) * blk
    blk_e = jnp.clip(jnp.searchsorted(pend, blk_start, side='right'), 0, MOE_EXPERTS - 1).astype(jnp.int32)
    n_valid = jnp.clip((pstart + counts)[blk_e] - blk_start, 0, blk).astype(jnp.int32)
    n_act = (pend[-1] // blk).astype(jnp.int32).reshape(1)
    expert_rank = jnp.concatenate([route[:, 0:2], route[:, 4:6]], axis=1).astype(jnp.int32)
    return pstart, blk_e, n_valid, n_act, expert_rank


def _moe_permute_kernel(pstart_ref, n_valid_ref, er_ref, src_hbm, dst_hbm, sem, zbuf, zsem, *, to_sorted, n_tok):
    tm = er_ref.shape[2] // 4
    i = pl.program_id(0)
    base = i * tm
    slot = lax.rem(i, 2)

    if to_sorted:
        blk = zbuf.shape[0]

        @pl.when(i == 0)
        def _():
            zbuf[...] = jnp.zeros_like(zbuf)

            def fill(j, wait):
                @pl.when(n_valid_ref[j] < blk)
                def _():
                    start = 0 if wait else pl.multiple_of(j * blk, blk)
                    cp = pltpu.make_async_copy(zbuf, dst_hbm.at[pl.ds(start, blk)], zsem)
                    cp.wait() if wait else cp.start()

            lax.fori_loop(0, n_valid_ref.shape[0], lambda j, c: (fill(j, False), c)[1], 0)
            lax.fori_loop(0, n_valid_ref.shape[0], lambda j, c: (fill(j, True), c)[1], 0)

    def body(t, carry):
        for k in range(MOE_TOPK):
            pos = pstart_ref[er_ref[0, 0, 4 * t + k]] + er_ref[0, 0, 4 * t + 2 + k]
            if to_sorted:
                src, dst = src_hbm.at[pl.ds(base + t, 1)], dst_hbm.at[pl.ds(pos, 1)]
            else:
                src, dst = src_hbm.at[pl.ds(pos, 1)], dst_hbm.at[pl.ds(k * n_tok + base + t, 1)]
            pltpu.make_async_copy(src, dst, sem.at[slot]).start()
        return carry

    lax.fori_loop(0, tm, body, 0, unroll=8)

    def wait_step(s):
        rows = MOE_TOPK * tm
        pltpu.make_async_copy(src_hbm.at[pl.ds(0, rows)], dst_hbm.at[pl.ds(0, rows)], sem.at[s]).wait()

    @pl.when(i >= 1)
    def _():
        wait_step(1 - slot)

    @pl.when(i == pl.num_programs(0) - 1)
    def _():
        wait_step(slot)


def _moe_permute(src, pstart, n_valid, expert_rank, n_out, to_sorted, blk, tm=256):
    n = expert_rank.shape[0]
    d = src.shape[1]
    er = expert_rank.reshape(n // tm, 1, 4 * tm)
    grid_spec = pltpu.PrefetchScalarGridSpec(
        num_scalar_prefetch=2,
        grid=(n // tm,),
        in_specs=[pl.BlockSpec((1, 1, 4 * tm), lambda i, ps, nv: (i, 0, 0), memory_space=pltpu.SMEM),
                  pl.BlockSpec(memory_space=pl.ANY)],
        out_specs=pl.BlockSpec(memory_space=pl.ANY),
        scratch_shapes=[pltpu.SemaphoreType.DMA((2,)), pltpu.VMEM((blk if to_sorted else 8, d), src.dtype),
                        pltpu.SemaphoreType.DMA],
    )
    return pl.pallas_call(
        functools.partial(_moe_permute_kernel, to_sorted=to_sorted, n_tok=n),
        grid_spec=grid_spec,
        out_shape=jax.ShapeDtypeStruct((n_out, d), src.dtype),
        compiler_params=_params(("arbitrary",)),
        name="moe_dispatch" if to_sorted else "moe_collect",
    )(pstart, n_valid, er, src)


def _moe_expert_kernel(blk_e_ref, n_act_ref, x_ref, ln_ref, w1_ref, w3_ref, w2_ref, y_ref):
    del blk_e_ref
    i = pl.program_id(0)

    @pl.when(i < n_act_ref[0])
    def _():
        xn = _mx(_rms(x_ref[...], ln_ref[...]))
        h1 = jnp.dot(xn, w1_ref[0], preferred_element_type=F32)
        h3 = jnp.dot(xn, w3_ref[0], preferred_element_type=F32)
        y_ref[...] = _dot(h1 * jax.nn.sigmoid(h1) * h3, w2_ref[0])

    @pl.when(i >= n_act_ref[0])
    def _():
        y_ref[...] = jnp.zeros_like(y_ref)


def _moe_experts(xs, ln, w1, w3, w2, blk_e, n_act, blk):
    p, d = xs.shape
    hdim = w1.shape[2]
    grid_spec = pltpu.PrefetchScalarGridSpec(
        num_scalar_prefetch=2,
        grid=(p // blk,),
        in_specs=[pl.BlockSpec((blk, d), lambda i, be, na: (i, 0)),
                  pl.BlockSpec((1, d), lambda i, be, na: (0, 0)),
                  pl.BlockSpec((1, d, hdim), lambda i, be, na: (be[i], 0, 0)),
                  pl.BlockSpec((1, d, hdim), lambda i, be, na: (be[i], 0, 0)),
                  pl.BlockSpec((1, hdim, d), lambda i, be, na: (be[i], 0, 0))],
        out_specs=pl.BlockSpec((blk, d), lambda i, be, na: (i, 0)),
    )
    return pl.pallas_call(
        _moe_expert_kernel,
        grid_spec=grid_spec,
        out_shape=jax.ShapeDtypeStruct((p, d), F32),
        compiler_params=_params(("arbitrary",)),
        name="moe_experts",
    )(blk_e, n_act, xs, ln.reshape(1, d), w1, w3, w2)


def _moe_layer(h, ln, w_gr, b_gr, w_er, b_er, w1, w3, w2):
    n, d = h.shape
    blk = MOE_BLOCK
    route, counts = _moe_router(h, ln, *_router_weights(w_gr, b_gr, w_er, b_er))
    pstart, blk_e, n_valid, n_act, expert_rank = _moe_plan(route, counts, blk)
    p = n * MOE_TOPK + MOE_EXPERTS * blk
    xs = _moe_permute(h, pstart, n_valid, expert_rank, p, True, blk)
    ys = _moe_experts(xs, ln, _mx(w1), _mx(w3), _mx(w2), blk_e, n_act, blk)
    return _moe_permute(ys, pstart, n_valid, expert_rank, n * MOE_TOPK, False, blk), route


def _moe_combine(h, slot0, slot1, route):
    return h + route[:, 2:3] * slot0 + route[:, 3:4] * slot1


def _final_kernel(h_ref, s0_ref, s1_ref, route_ref, ln_ref, o_ref):
    o_ref[...] = _rms(_moe_combine(h_ref[...], s0_ref[...], s1_ref[...], route_ref[...]), ln_ref[...])


def _final(h, slots, route, ln, tm=512):
    n, d = h.shape
    row = pl.BlockSpec((tm, d), lambda i: (i, 0))
    return pl.pallas_call(
        _final_kernel,
        grid=(n // tm,),
        in_specs=[row, row, pl.BlockSpec((tm, d), lambda i: (n // tm + i, 0)),
                  pl.BlockSpec((tm, LANES), lambda i: (i, 0)), _const_spec((1, d))],
        out_specs=row,
        out_shape=jax.ShapeDtypeStruct((n, d), F32),
        compiler_params=_params(("arbitrary",)),
        name="final_norm",
    )(h, slots, slots, route, ln.reshape(1, d))


NSA_HEADS = 16
NSA_GROUPS = 4
NSA_HPG = NSA_HEADS // NSA_GROUPS
NSA_DH = D_MODEL // NSA_HEADS
NSA_PARTS = 6
CMP_BLOCK = 32
CMP_STRIDE = 16
SEL_BLOCK = 64
SEL_TOPK = 8
WINDOW = 512
N_BRANCH = 3
IMP_FORCE = 1e4
REL_BUCKETS = 32
REL_MAX_DIST = 128
NSA_TQ = 256
NSA_TK = 256
GATE_LANES = LANES
CMP_PARTS = 2


def _nsa_proj_kernel(h_ref, s0_ref, s1_ref, route_ref, kvln_ref, qln_ref, kvw_ref, qw_ref,
                     h_out, q_out, gate_out, kvc_out, kv_out):
    h1 = _moe_combine(h_ref[0], s0_ref[0, 0], s1_ref[0, 0], route_ref[0])
    h_out[0] = h1
    kv = _dot(_rms(h1, kvln_ref[...]), kvw_ref[...])
    for part in range(NSA_PARTS):
        for g in range(NSA_GROUPS):
            col = (part * NSA_GROUPS + g) * NSA_DH
            piece = kv[:, col:col + NSA_DH]
            if part < CMP_PARTS:
                kvc_out[part, 0, g] = piece
            else:
                kv_out[part - CMP_PARTS, 0, g] = piece.astype(kv_out.dtype)
    qg = _dot(_rms(h1, qln_ref[...]), qw_ref[...])
    q_out[0] = (qg[:, :D_MODEL] * (NSA_DH ** -0.5)).astype(q_out.dtype)
    gate_out[0] = qg[:, D_MODEL:]


def _gate_padded_weight(q_w):
    d = q_w.shape[0]
    n_gate = NSA_HPG * N_BRANCH
    gates = q_w[:, d:].reshape(d, NSA_GROUPS, n_gate)
    gates = jnp.pad(gates, ((0, 0), (0, 0), (0, GATE_LANES - n_gate))).reshape(d, NSA_GROUPS * GATE_LANES)
    return jnp.concatenate([q_w[:, :d], gates], axis=1)


def _nsa_proj(h, slots, route, kv_ln, q_ln, kv_w, q_w, b, t, tm=256):
    d = h.shape[-1]
    blk = lambda w: pl.BlockSpec((1, tm, w), lambda i, j: (i, j, 0))
    slot = lambda k: pl.BlockSpec((1, 1, tm, d), lambda i, j: (k, i, j, 0))
    gw = NSA_GROUPS * GATE_LANES
    qw = _gate_padded_weight(q_w)
    kv_spec = lambda parts: pl.BlockSpec((parts, 1, NSA_GROUPS, tm, NSA_DH), lambda i, j: (0, i, 0, j, 0))
    kv_shape = lambda parts: (parts, b, NSA_GROUPS, t, NSA_DH)
    slots4 = slots.reshape(MOE_TOPK, b, t, d)
    return pl.pallas_call(
        _nsa_proj_kernel,
        grid=(b, t // tm),
        in_specs=[blk(d), slot(0), slot(1), blk(LANES), _const_spec((1, d)), _const_spec((1, d)),
                  _const_spec(kv_w.shape), _const_spec(qw.shape)],
        out_specs=[blk(d), blk(d), blk(gw), kv_spec(CMP_PARTS), kv_spec(NSA_PARTS - CMP_PARTS)],
        out_shape=[jax.ShapeDtypeStruct((b, t, d), F32), jax.ShapeDtypeStruct((b, t, d), ACT_DTYPE),
                   jax.ShapeDtypeStruct((b, t, gw), F32), jax.ShapeDtypeStruct(kv_shape(CMP_PARTS), F32),
                   jax.ShapeDtypeStruct(kv_shape(NSA_PARTS - CMP_PARTS), ACT_DTYPE)],
        compiler_params=_params(("arbitrary", "arbitrary")),
        name="nsa_proj",
    )(h.reshape(b, t, d), slots4, slots4, route.reshape(b, t, LANES),
      kv_ln.reshape(1, d), q_ln.reshape(1, d), _mx(kv_w), _mx(qw))


def _gelu_tanh(x):
    return 0.5 * x * (1.0 + jnp.tanh(math.sqrt(2.0 / math.pi) * (x + 0.044715 * (x * x * x))))


def _compress_kernel(k_ref, v_ref, pek_ref, pev_ref, kw1_ref, kw2_ref, vw1_ref, vw2_ref, kc_out, vc_out):
    dh = k_ref.shape[-1]
    n_chunk = k_ref.shape[3] // CMP_STRIDE

    def one(z_ref, pe_ref, w1_ref, w2_ref, out):
        first = jnp.zeros((n_chunk, w1_ref.shape[1]), F32)
        second = first
        for l in range(CMP_STRIDE):
            z = z_ref[0, 0, 0, pl.ds(l, n_chunk, stride=CMP_STRIDE), :]
            lo, hi = l, CMP_STRIDE + l
            first = first + _dot(z + pe_ref[0, lo:lo + 1, :], w1_ref[dh * lo:dh * (lo + 1), :])
            second = second + _dot(z + pe_ref[0, hi:hi + 1, :], w1_ref[dh * hi:dh * (hi + 1), :])
        hid = first + pltpu.roll(second, n_chunk - 1, axis=0)
        out[0, 0] = _dot(_gelu_tanh(hid), w2_ref[...]).astype(out.dtype)

    one(k_ref, pek_ref, kw1_ref, kw2_ref, kc_out)
    one(v_ref, pev_ref, vw1_ref, vw2_ref, vc_out)


def _compress(kvc, pe_k, pe_v, k_w1, k_w2, v_w1, v_w2):
    _, b, g, t, dh = kvc.shape
    n_chunk = t // CMP_STRIDE
    part = lambda idx: pl.BlockSpec((1, 1, 1, t, dh), lambda i, j: (idx, i, j, 0, 0))
    pe_spec = pl.BlockSpec((1, CMP_BLOCK, dh), lambda i, j: (j, 0, 0))
    out_spec = pl.BlockSpec((1, 1, n_chunk, dh), lambda i, j: (i, j, 0, 0))
    out = jax.ShapeDtypeStruct((b, g, n_chunk, dh), ACT_DTYPE)
    return pl.pallas_call(
        _compress_kernel,
        grid=(b, g),
        in_specs=[part(0), part(1), pe_spec, pe_spec, _const_spec(k_w1.shape), _const_spec(k_w2.shape),
                  _const_spec(v_w1.shape), _const_spec(v_w2.shape)],
        out_specs=[out_spec, out_spec],
        out_shape=[out, out],
        compiler_params=_params(("arbitrary", "arbitrary")),
        name="nsa_compress",
    )(kvc, kvc, pe_k.astype(F32), pe_v.astype(F32), _mx(k_w1), _mx(k_w2), _mx(v_w1), _mx(v_w2))


def _rel_bucket(dist):
    n = jnp.maximum(dist, 0)
    max_exact = REL_BUCKETS // 2
    nf = jnp.maximum(n, 1).astype(F32)
    large = max_exact + (jnp.log(nf / max_exact) / math.log(REL_MAX_DIST / max_exact)
                         * (REL_BUCKETS - max_exact)).astype(jnp.int32)
    return jnp.where(n < max_exact, n, jnp.minimum(large, REL_BUCKETS - 1))


def _toeplitz(vals, rows, cols):
    span = rows + cols
    z = jnp.concatenate([vals, vals[..., :1]], axis=-1)
    flat = jnp.tile(z, rows)[..., :rows * (span - 1)]
    skew = flat.reshape(vals.shape[:-1] + (rows, span - 1))
    return skew[..., rows - 1:rows - 1 + cols]


def _nsa_bias_tiles(rel_bias, t):
    tq, tk = NSA_TQ, NSA_TK
    table = rel_bias.astype(F32).reshape(REL_BUCKETS, NSA_GROUPS, NSA_HPG).transpose(1, 2, 0)
    by_dist = table[:, :, _rel_bucket(jnp.arange(REL_MAX_DIST + 1))]
    far = by_dist[:, :, REL_MAX_DIST:]

    def band(dist, valid, shift=None):
        b = by_dist[:, :, np.clip(dist, 0, REL_MAX_DIST)]
        if shift is not None:
            b = b - shift
        return jnp.where(jnp.asarray(valid), b, NEG)

    u = lambda rows, cols: np.arange(rows + cols - 1) - (rows - 1)
    wk = WINDOW + tq
    win = []
    for v in range(WINDOW // tq + 1):
        dist = v * tq - max(v * tq - WINDOW, 0) - u(tq, wk)
        win.append(_toeplitz(band(dist, (dist >= 0) & (dist < WINDOW)), tq, wk))
    bias_win = jnp.stack(win, axis=1)
    n_near = (REL_MAX_DIST + tk - 1) // tq + 1
    sel = []
    for e in range(n_near):
        dist = e * tq - u(tq, tk)
        sel.append(_toeplitz(band(dist, dist >= 0, far), tq, tk))
    sel.append(jnp.zeros_like(sel[0]))
    bias_sel = jnp.stack(sel, axis=1)
    n_cmp = t // CMP_STRIDE
    per_b = []
    for b in range(CMP_STRIDE):
        dist = b - (CMP_BLOCK - 1) - CMP_STRIDE * u(n_cmp, n_cmp)
        per_b.append(_toeplitz(band(dist, dist >= 0), n_cmp, n_cmp))
    bias_cmp = jnp.stack(per_b, axis=3).reshape(NSA_GROUPS, NSA_HPG, t, n_cmp)
    real = np.arange(n_cmp) < (t - CMP_BLOCK) // CMP_STRIDE + 1
    bias_cmp = jnp.where(jnp.asarray(real), bias_cmp, NEG).reshape(NSA_GROUPS, NSA_HPG, t // tq, tq, n_cmp)
    return bias_cmp, bias_win, bias_sel


def _nsa_constants(t):
    n_sel = t // SEL_BLOCK
    n_cmp = t // CMP_STRIDE
    cmp_start = np.arange(n_cmp) * CMP_STRIDE
    sel_start = np.arange(n_sel) * SEL_BLOCK
    overlap = ((cmp_start[:, None] < sel_start[None, :] + SEL_BLOCK)
               & (cmp_start[:, None] + CMP_BLOCK > sel_start[None, :]))
    overlap[(t - CMP_BLOCK) // CMP_STRIDE + 1:, :] = False
    per = NSA_TK // SEL_BLOCK
    expand = np.zeros((t // NSA_TK, LANES, NSA_TK), np.float32)
    for c in range(t // NSA_TK):
        for s in range(NSA_TK):
            expand[c, per * c + s // SEL_BLOCK, s] = 1.0
    return jnp.asarray(overlap.T, MXU_DTYPE), jnp.asarray(expand, MXU_DTYPE)


def _nsa_attn_kernel(q_ref, gate_ref, kc_ref, vc_ref, ks_ref, vs_ref, kw_ref, vw_ref,
                     bcmp_ref, bwin_ref, bsel_ref, ovl_ref, exp_ref, o_ref):
    tq, tk, dh = NSA_TQ, NSA_TK, NSA_DH
    i = pl.program_id(2)
    t0 = i * tq
    q = q_ref[0]
    heads = range(NSA_HPG)
    qh = [q[:, dh * h:dh * (h + 1)] for h in heads]

    k_cmp, v_cmp = kc_ref[0, 0], vc_ref[0, 0]
    lc = [_dot_nt(qh[h], k_cmp) + bcmp_ref[0, h, 0] for h in heads]
    mc = [jnp.max(x, axis=-1, keepdims=True) for x in lc]
    pcu = [jnp.where(mc[h] > 0.5 * NEG, jnp.exp(lc[h] - mc[h]), 0.0) for h in heads]
    pc = [x / jnp.maximum(jnp.sum(x, axis=-1, keepdims=True), 1e-30) for x in pcu]
    oc = [_dot(x, v_cmp) for x in pc]

    pcsum = (pc[0] + pc[1]) + (pc[2] + pc[3])
    hi = pcsum.astype(MXU_DTYPE)
    mid = (pcsum - hi.astype(F32))
    lo = (mid - mid.astype(MXU_DTYPE).astype(F32)).astype(MXU_DTYPE)
    ovl = ovl_ref[...]
    imp = (lax.dot_general(ovl, hi, (((1,), (1,)), ((), ())), preferred_element_type=F32)
           + lax.dot_general(ovl, mid.astype(MXU_DTYPE), (((1,), (1,)), ((), ())), preferred_element_type=F32)
           + lax.dot_general(ovl, lo, (((1,), (1,)), ((), ())), preferred_element_type=F32))
    n_sel = imp.shape[0]
    jj = lax.broadcasted_iota(jnp.int32, (n_sel, 1), 0)
    blk_q = lax.shift_right_logical(t0 + lax.broadcasted_iota(jnp.int32, (1, tq), 1),
                                    int(math.log2(SEL_BLOCK)))
    forced = (jj == 0) | (jj == blk_q) | (jj == blk_q - 1)
    score = jnp.where(forced, IMP_FORCE, jnp.where(jj <= blk_q, imp, -1.0))
    rank = jnp.zeros_like(score)
    for j in range(n_sel):
        row = score[j:j + 1, :]
        tie = (jj > j).astype(F32)
        rank = rank + jnp.where(row > score, 1.0, jnp.where(row == score, tie, 0.0))
    sel_t = (rank < SEL_TOPK).astype(F32)
    sel = jnp.concatenate([sel_t, jnp.zeros((LANES - n_sel, tq), F32)], axis=0).T.astype(MXU_DTYPE)

    def sel_step(c, carry):
        ms, ls, accs = carry
        start = pl.multiple_of(c * tk, tk)
        k_c = ks_ref[0, 0, 0, pl.ds(start, tk), :]
        v_c = vs_ref[0, 0, 0, pl.ds(start, tk), :]
        keep = jnp.dot(sel, exp_ref[c], preferred_element_type=F32)
        neg = (keep - 1.0) * (-NEG)
        near = jnp.minimum((t0 - start) // tq, bsel_ref.shape[1] - 1)
        lg = [_dot_nt(qh[h], k_c) + bsel_ref[0, near, h] + neg for h in heads]
        m_new = [jnp.maximum(ms[h], jnp.max(lg[h], axis=-1, keepdims=True)) for h in heads]
        alpha = [jnp.exp(ms[h] - m_new[h]) for h in heads]
        p = [jnp.exp(lg[h] - m_new[h]) for h in heads]
        l_new = [alpha[h] * ls[h] + jnp.sum(p[h], axis=-1, keepdims=True) for h in heads]
        acc_new = [alpha[h] * accs[h] + _dot(p[h], v_c) for h in heads]
        return tuple(m_new), tuple(l_new), tuple(acc_new)

    init = (tuple(jnp.full((tq, 1), NEG, F32) for _ in heads), tuple(jnp.zeros((tq, 1), F32) for _ in heads),
            tuple(jnp.zeros((tq, dh), F32) for _ in heads))
    _, l_sel, acc_sel = lax.fori_loop(0, (t0 + tq + tk - 1) // tk, sel_step, init)

    w0 = pl.multiple_of(jnp.maximum(t0 - WINDOW, 0), tq)
    wk = WINDOW + tq
    k_w = kw_ref[0, 0, 0, pl.ds(w0, wk), :]
    v_w = vw_ref[0, 0, 0, pl.ds(w0, wk), :]
    lw = [_dot_nt(qh[h], k_w) + bwin_ref[0, 0, h] for h in heads]
    pw = [jnp.exp(x - jnp.max(x, axis=-1, keepdims=True)) for x in lw]
    ow = [_dot(x, v_w) / jnp.sum(x, axis=-1, keepdims=True) for x in pw]

    gates = jax.nn.sigmoid(gate_ref[0])
    outs = []
    for h in heads:
        g = [gates[:, N_BRANCH * h + br:N_BRANCH * h + br + 1] for br in range(N_BRANCH)]
        outs.append(g[0] * oc[h] + g[1] * (acc_sel[h] / l_sel[h]) + g[2] * ow[h])
    o_ref[0] = jnp.concatenate(outs, axis=1).astype(o_ref.dtype)


def _nsa_attention(q, gate_logits, kv, k_cmp, v_cmp, rel_bias):
    b, t, d = q.shape
    tq, tk, dh = NSA_TQ, NSA_TK, NSA_DH
    bias_cmp, bias_win, bias_sel = _nsa_bias_tiles(rel_bias, t)
    ovl, expand = _nsa_constants(t)
    n_cmp = k_cmp.shape[2]
    n_win = bias_win.shape[1]
    part = lambda idx: pl.BlockSpec((1, 1, 1, t, dh), lambda g, bi, i: (idx, bi, g, 0, 0))
    cmp_spec = pl.BlockSpec((1, 1, n_cmp, dh), lambda g, bi, i: (bi, g, 0, 0))
    return pl.pallas_call(
        _nsa_attn_kernel,
        grid=(NSA_GROUPS, b, t // tq),
        in_specs=[pl.BlockSpec((1, tq, NSA_HPG * dh), lambda g, bi, i: (bi, i, g)),
                  pl.BlockSpec((1, tq, GATE_LANES), lambda g, bi, i: (bi, i, g)),
                  cmp_spec, cmp_spec, part(0), part(1), part(2), part(3),
                  pl.BlockSpec((1, NSA_HPG, 1, tq, n_cmp), lambda g, bi, i: (g, 0, i, 0, 0)),
                  pl.BlockSpec((1, 1, NSA_HPG, tq, WINDOW + tq),
                               lambda g, bi, i: (g, jnp.minimum(i, n_win - 1), 0, 0, 0)),
                  pl.BlockSpec((1, bias_sel.shape[1], NSA_HPG, tq, tk), lambda g, bi, i: (g, 0, 0, 0, 0)),
                  _const_spec(ovl.shape), _const_spec(expand.shape)],
        out_specs=pl.BlockSpec((1, tq, NSA_HPG * dh), lambda g, bi, i: (bi, i, g)),
        out_shape=jax.ShapeDtypeStruct((b, t, d), ACT_DTYPE),
        compiler_params=_params(("arbitrary", "arbitrary", "arbitrary")),
        name="nsa_attention",
    )(q, gate_logits, k_cmp, v_cmp, kv, kv, kv, kv, bias_cmp, bias_win, bias_sel, ovl, expand)


def _nsa_block(h, slots, route, kv_ln, kv_w, cmp_pe_k, cmp_pe_v, cmp_k_w1, cmp_k_w2, cmp_v_w1, cmp_v_w2,
               rel_bias, b_ln, b_wqg, b_wo, b, t):
    n, d = h.shape
    h1, q, gate_logits, kvc, kv = _nsa_proj(h, slots, route, kv_ln, b_ln, kv_w, b_wqg, b, t)
    k_cmp, v_cmp = _compress(kvc, cmp_pe_k, cmp_pe_v, cmp_k_w1, cmp_k_w2, cmp_v_w1, cmp_v_w2)
    o = _nsa_attention(q, gate_logits, kv, k_cmp, v_cmp, rel_bias)
    return _matmul_res(o.reshape(n, d), _mx(b_wo), h1.reshape(n, d))


def kernel(x, a_ln, a_mu, a_wr, a_wk, a_wv, a_wo, a_w0, a_w1, a_w2, a_a0, a_a1, a_a2, a_g1, a_g2, a_kk, a_ka, a_rk, a_lnx_w, a_lnx_b, kv_ln, kv_w, cmp_pe_k, cmp_pe_v, cmp_k_w1, cmp_k_w2, cmp_v_w1, cmp_v_w2, rel_bias, b_ln, b_wqg, b_wo, m_ln, m_wg, m_bg, m_we, m_be, m_w1, m_w3, m_w2, final_ln):
    b, t, d = x.shape
    n = b * t
    h = _rwkv_block(x, a_ln[0], a_mu[0], a_wr[0], a_wk[0], a_wv[0], a_wo[0], a_w0[0], a_w1[0], a_w2[0],
                    a_a0[0], a_a1[0], a_a2[0], a_g1[0], a_g2[0], a_kk[0], a_ka[0], a_rk[0],
                    a_lnx_w[0], a_lnx_b[0]).reshape(n, d)
    moe = lambda hh, l: _moe_layer(hh, m_ln[l], m_wg[l], m_bg[l], m_we[l], m_be[l], m_w1[l], m_w3[l], m_w2[l])
    slots, route = moe(h, 0)
    h = _nsa_block(h, slots, route, kv_ln, kv_w, cmp_pe_k, cmp_pe_v, cmp_k_w1, cmp_k_w2, cmp_v_w1, cmp_v_w2,
                   rel_bias, b_ln[0], b_wqg[0], b_wo[0], b, t)
    slots, route = moe(h, 1)
    return _final(h, slots, route, final_ln).reshape(b, t, d)
```

```python
import functools
import math

import numpy as np
import jax
import jax.numpy as jnp
from jax import lax
from jax.experimental import pallas as pl
from jax.experimental.pallas import tpu as pltpu

F32 = jnp.float32
MXU_DTYPE = jnp.bfloat16
ACT_DTYPE = jnp.bfloat16

D_MODEL = 1024
NORM_EPS = 1e-6
RWKV_HEAD = 64
RWKV_HEADS = D_MODEL // RWKV_HEAD
RWKV_GN_EPS = 64e-5
RWKV_CHUNK = 64
LANES = 128
MXU_TILE = 256
VMEM_LIMIT = 56 * 1024 * 1024


def _mx(x):
    return x.astype(MXU_DTYPE)


def _dot(a, b):
    return jnp.dot(_mx(a), _mx(b), preferred_element_type=F32)


def _dot_nt(a, b):
    return lax.dot_general(_mx(a), _mx(b), (((1,), (1,)), ((), ())), preferred_element_type=F32)


def _split2(x):
    hi = x.astype(MXU_DTYPE)
    lo = (x - hi.astype(F32)).astype(MXU_DTYPE)
    return hi, lo


def _dot_hl(x, w):
    hi, lo = _split2(x)
    return jnp.dot(hi, w, preferred_element_type=F32) + jnp.dot(lo, w, preferred_element_type=F32)


def _rms(x, g):
    return x * lax.rsqrt(jnp.mean(x * x, axis=-1, keepdims=True) + NORM_EPS) * g


TOK_ROWS = 8


def _tok_load(ref):
    tm = ref.shape[0] // TOK_ROWS
    return jnp.concatenate([ref[pl.ds(s, tm, stride=TOK_ROWS), :] for s in range(TOK_ROWS)], axis=1)


def _tok_store(ref, val):
    tm = val.shape[0]
    for s in range(TOK_ROWS):
        ref[pl.ds(s, tm, stride=TOK_ROWS), :] = val[:, LANES * s:LANES * (s + 1)]


def _tok_spec(tm, index_map):
    return pl.BlockSpec((TOK_ROWS * tm, LANES), index_map)


def _const_spec(shape):
    nd = len(shape)
    return pl.BlockSpec(shape, lambda *_: (0,) * nd)


def _params(sem):
    return pltpu.CompilerParams(dimension_semantics=sem, vmem_limit_bytes=VMEM_LIMIT)


def _rwkv_pre_kernel(x_ref, ln_ref, mu_ref, wr_ref, wk_ref, wv_ref, w1_ref, a1_ref, g1_ref,
                     w2_ref, a2_ref, g2_ref, vec_ref,
                     r_out, k_out, v_out, kk_out, g_out, ld_out, a_out, prev_ref):
    tm = x_ref.shape[1]
    hn = _rms(x_ref[0], ln_ref[...])

    @pl.when(pl.program_id(1) == 0)
    def _():
        prev_ref[...] = jnp.zeros_like(prev_ref)

    rows = lax.broadcasted_iota(jnp.int32, (tm, 1), 0)
    shifted = jnp.where(rows == 0, prev_ref[0:1, :], pltpu.roll(hn, 1, axis=0))
    prev_ref[0:1, :] = hn[tm - 1:tm, :]
    xx = shifted - hn
    xr, xw, xk, xv, xa, xg = (_mx(hn + xx * mu_ref[i:i + 1, :]) for i in range(6))
    w0, a0, kkp, kap = (vec_ref[i:i + 1, :] for i in range(4))
    r = jnp.dot(xr, wr_ref[...], preferred_element_type=F32)
    k = jnp.dot(xk, wk_ref[...], preferred_element_type=F32)
    v = jnp.dot(xv, wv_ref[...], preferred_element_type=F32)
    z = w0 + _dot(jnp.tanh(jnp.dot(xw, w1_ref[...], preferred_element_type=F32)), w2_ref[...])
    a = jax.nn.sigmoid(a0 + _dot(jnp.dot(xa, a1_ref[...], preferred_element_type=F32), a2_ref[...]))
    g = _dot(jax.nn.sigmoid(jnp.dot(xg, g1_ref[...], preferred_element_type=F32)), g2_ref[...])
    ld_out[0] = (-math.exp(-0.5)) * jax.nn.sigmoid(z)
    a_out[0] = a
    r_out[0] = r.astype(r_out.dtype)
    v_out[0] = v.astype(v_out.dtype)
    kk_out[0] = (k * kkp).astype(kk_out.dtype)
    k_out[0] = (k * (1.0 + (a - 1.0) * kap)).astype(k_out.dtype)
    g_out[0] = g.astype(g_out.dtype)


def _rwkv_pre(x, ln, mu, wr, wk, wv, w1, a1, g1, w2, a2, g2, vecs, tm=256):
    b, t, d = x.shape
    act = jax.ShapeDtypeStruct((b, t, d), ACT_DTYPE)
    f32o = jax.ShapeDtypeStruct((b, t, d), F32)
    blk = pl.BlockSpec((1, tm, d), lambda i, j: (i, j, 0))
    consts = [ln, mu, wr, wk, wv, w1, a1, g1, w2, a2, g2, vecs]
    return pl.pallas_call(
        _rwkv_pre_kernel,
        grid=(b, t // tm),
        in_specs=[blk] + [_const_spec(c.shape) for c in consts],
        out_specs=[blk] * 7,
        out_shape=[act, act, act, act, act, f32o, f32o],
        scratch_shapes=[pltpu.VMEM((8, d), F32)],
        compiler_params=_params(("arbitrary", "arbitrary")),
        name="rwkv_pre",
    )(x, *consts)


def _stack4(x):
    return jnp.concatenate([x[:, MXU_TILE * q:MXU_TILE * (q + 1)] for q in range(4)], axis=0)


def _unstack4(y):
    c = y.shape[0] // 4
    return jnp.concatenate([y[c * q:c * (q + 1), :] for q in range(4)], axis=1)


def _rwkv_scan_kernel(r_ref, k_ref, v_ref, kk_ref, g_ref, ld_ref, a_ref,
                      ltri_ref, smask_ref, eye_ref, j_ref, rk_ref, lnw_ref, lnb_ref,
                      y_out, h_ref):
    c = RWKV_CHUNK

    @pl.when(pl.program_id(1) == 0)
    def _():
        h_ref[...] = jnp.zeros_like(h_ref)

    ld = ld_ref[0]
    r = r_ref[0].astype(F32)
    k = k_ref[0].astype(F32)
    v = v_ref[0].astype(F32)
    kk = kk_ref[0].astype(F32)
    jones = j_ref[...]
    ld_hi, ld_lo = _split2(ld)
    cum = (jnp.dot(ltri_ref[...], ld_hi, preferred_element_type=F32)
           + jnp.dot(ltri_ref[...], ld_lo, preferred_element_type=F32))
    cum_last = cum[c - 1:c, :]
    e_l = jnp.exp(cum)
    e_lm = jnp.exp(cum - ld)
    e_nl = jnp.exp(-cum)
    e_cl = jnp.exp(cum_last - cum)
    e_c = jnp.exp(cum_last)
    ss = _unstack4(_dot_hl(_stack4(kk * kk), jones))
    kkn = kk * lax.rsqrt(jnp.maximum(ss, 1e-24))
    a_vec = -kkn
    b_vec = kkn * a_ref[0]
    rt = r * e_l
    at = a_vec * e_lm
    kt = k * e_nl
    bt = b_vec * e_nl
    bh = b_vec * e_cl
    kh = k * e_cl

    lane = lax.broadcasted_iota(jnp.int32, (1, LANES), 1)
    m_l = (lane < RWKV_HEAD).astype(F32)
    m_r = 1.0 - m_l

    def st(xp):
        return jnp.concatenate([xp * m_l, xp * m_r], axis=0)

    def dup(xp):
        return jnp.concatenate([xp, xp], axis=0)

    smask = smask_ref[...] > 0.0
    eye = eye_ref[...]
    pairs = range(D_MODEL // LANES)
    sls = [slice(LANES * p, LANES * (p + 1)) for p in pairs]
    at_s = [st(at[:, sl]) for sl in sls]
    rt_s = [st(rt[:, sl]) for sl in sls]
    v_s = [st(v[:, sl]) for sl in sls]
    s = [jnp.where(smask,
                   _dot_nt(jnp.concatenate([at_s[p], rt_s[p]], axis=0),
                           jnp.concatenate([dup(bt[:, sls[p]]), dup(kt[:, sls[p]])], axis=0)),
                   0.0) for p in pairs]
    a_ab = [x[:2 * c, :2 * c] for x in s]
    a_r = [x[2 * c:, :] for x in s]
    g1 = [_dot(s[p][:2 * c, 2 * c:], v_s[p]) for p in pairs]
    t_m = [eye + x for x in a_ab]
    a_pow = [_dot(x, x) for x in a_ab]
    n = 2
    while n < c:
        if 2 * n < c:
            both = [_dot(a_pow[p], jnp.concatenate([a_pow[p], t_m[p]], axis=1)) for p in pairs]
            a_pow = [x[:, :2 * c] for x in both]
            t_m = [t_m[p] + both[p][:, 2 * c:] for p in pairs]
        else:
            t_m = [t_m[p] + _dot(a_pow[p], t_m[p]) for p in pairs]
        n *= 2
    tw = [_dot(t_m[p], jnp.concatenate([at_s[p], g1[p]], axis=1)) for p in pairs]
    zmat = [jnp.concatenate([tw[p], jnp.concatenate([jnp.zeros_like(v_s[p]), v_s[p]], axis=1)], axis=0)
            for p in pairs]
    upper = [_dot(a_r[p], zmat[p]) for p in pairs]
    bk_t = [jnp.concatenate([st(bh[:, sl]), st(kh[:, sl])], axis=0).T for sl in sls]
    lower = [_dot(bk_t[p], zmat[p]) for p in pairs]
    seq = [_dot(jnp.concatenate([upper[p][:, :LANES] + rt_s[p],
                                 lower[p][:, :LANES] + eye * e_c[:, sls[p]]], axis=0), h_ref[p])
           for p in pairs]
    ys = []
    for p in pairs:
        y_st = seq[p][:2 * c, :] + upper[p][:, LANES:]
        h_ref[p] = seq[p][2 * c:, :] + lower[p][:, LANES:]
        ys.append(y_st[:c, :] + y_st[c:, :])
    y = jnp.concatenate(ys, axis=1)

    inv_n = 1.0 / RWKV_HEAD
    y4 = _stack4(y)
    dlt = y4 - _dot_hl(y4, jones) * inv_n
    var = _dot_hl(dlt * dlt, jones) * inv_n
    yn = dlt * lax.rsqrt(var + RWKV_GN_EPS)
    bonus = _dot_hl(_stack4(r * k) * rk_ref[...], jones) * _stack4(v)
    out = (yn * lnw_ref[...] + lnb_ref[...] + bonus) * _stack4(g_ref[0].astype(F32))
    y_out[0] = _unstack4(out).astype(y_out.dtype)


def _scan_constants():
    c = RWKV_CHUNK
    ltri = np.tril(np.ones((c, c), np.float32))
    rho = np.arange(4 * c)
    r_type, r_head, r_t = rho // (2 * c), (rho % (2 * c)) // c, rho % c
    same = r_head[:, None] == r_head[None, :]
    strict = r_t[None, :] < r_t[:, None]
    incl = r_t[None, :] <= r_t[:, None]
    smask = same & np.where(r_type[:, None] == 0, strict, incl)
    eye = np.eye(LANES, dtype=np.float32)
    head = np.arange(MXU_TILE) // RWKV_HEAD
    jones = (head[:, None] == head[None, :]).astype(np.float32)
    return (jnp.asarray(ltri, MXU_DTYPE), jnp.asarray(smask, F32), jnp.asarray(eye, F32),
            jnp.asarray(jones, MXU_DTYPE))


def _stack_param(p):
    return jnp.repeat(p.reshape(4, MXU_TILE).astype(F32), RWKV_CHUNK, axis=0)


def _rwkv_scan(r, k, v, kk, g, ld, a, rk, lnw, lnb):
    b, t, d = r.shape
    c = RWKV_CHUNK
    blk = pl.BlockSpec((1, c, d), lambda i, j: (i, j, 0))
    consts = list(_scan_constants()) + [_stack_param(rk.reshape(-1)), _stack_param(lnw), _stack_param(lnb)]
    return pl.pallas_call(
        _rwkv_scan_kernel,
        grid=(b, t // c),
        in_specs=[blk] * 7 + [_const_spec(x.shape) for x in consts],
        out_specs=blk,
        out_shape=jax.ShapeDtypeStruct((b, t, d), ACT_DTYPE),
        scratch_shapes=[pltpu.VMEM((d // LANES, LANES, LANES), F32)],
        compiler_params=_params(("arbitrary", "arbitrary")),
        name="rwkv_scan",
    )(r, k, v, kk, g, ld, a, *consts)


def _matmul_res_kernel(a_ref, w_ref, res_ref, o_ref):
    _tok_store(o_ref, res_ref[...] + jnp.dot(a_ref[...], w_ref[...], preferred_element_type=F32))


def _matmul_res(a, w, res, tm=512):
    n, kdim = a.shape
    d = w.shape[1]
    return pl.pallas_call(
        _matmul_res_kernel,
        grid=(n // tm,),
        in_specs=[pl.BlockSpec((tm, kdim), lambda i: (i, 0)), _const_spec(w.shape),
                  pl.BlockSpec((tm, d), lambda i: (i, 0))],
        out_specs=_tok_spec(tm, lambda i: (i, 0)),
        out_shape=jax.ShapeDtypeStruct((TOK_ROWS * n, LANES), F32),
        compiler_params=_params(("arbitrary",)),
        name="matmul_res",
    )(a, w, res)


def _rwkv_block(x, a_ln, a_mu, a_wr, a_wk, a_wv, a_wo, a_w0, a_w1, a_w2, a_a0, a_a1, a_a2,
                a_g1, a_g2, a_kk, a_ka, a_rk, a_lnx_w, a_lnx_b):
    b, t, d = x.shape
    vecs = jnp.stack([a_w0, a_a0, a_kk, a_ka]).astype(F32)
    r, k, v, kk, g, ld, a = _rwkv_pre(
        x, a_ln.reshape(1, d), a_mu, _mx(a_wr), _mx(a_wk), _mx(a_wv), _mx(a_w1), _mx(a_a1), _mx(a_g1),
        _mx(a_w2), _mx(a_a2), _mx(a_g2), vecs)
    y = _rwkv_scan(r, k, v, kk, g, ld, a, a_rk, a_lnx_w, a_lnx_b)
    return _matmul_res(y.reshape(b * t, d), _mx(a_wo), x.reshape(b * t, d))


MOE_GROUPS = 4
MOE_EPG = 8
MOE_EXPERTS = MOE_GROUPS * MOE_EPG
MOE_TOPK = 2
MOE_BLOCK = 256
NEG = -1e30


def _route(logits):
    lane = lax.broadcasted_iota(jnp.int32, logits.shape, 1)
    is_grp = lane < MOE_GROUPS
    gl = jnp.where(is_grp, logits, NEG)
    ge = jnp.exp(gl - jnp.max(gl, axis=-1, keepdims=True))
    gp = ge / jnp.sum(ge, axis=-1, keepdims=True)
    grp_w = jnp.max(gp, axis=-1, keepdims=True)
    grp = jnp.min(jnp.where(is_grp & (gp == grp_w), lane, LANES), axis=-1, keepdims=True)
    lo = MOE_GROUPS + MOE_EPG * grp
    in_grp = (lane >= lo) & (lane < lo + MOE_EPG)
    el = jnp.where(in_grp, logits, NEG)
    ee = jnp.exp(el - jnp.max(el, axis=-1, keepdims=True))
    ep = ee / jnp.sum(ee, axis=-1, keepdims=True)
    p1 = jnp.max(jnp.where(in_grp, ep, -1.0), axis=-1, keepdims=True)
    i1 = jnp.min(jnp.where(in_grp & (ep == p1), lane, LANES), axis=-1, keepdims=True)
    rest = in_grp & (lane != i1)
    p2 = jnp.max(jnp.where(rest, ep, -1.0), axis=-1, keepdims=True)
    i2 = jnp.min(jnp.where(rest & (ep == p2), lane, LANES), axis=-1, keepdims=True)
    scale = grp_w / (p1 + p2)
    return i1 - MOE_GROUPS, i2 - MOE_GROUPS, p1 * scale, p2 * scale


def _router_logits(xf, w_hi_ref, w_lo_ref, bias_ref):
    hi, lo = _split2(xf)
    logits = (jnp.dot(hi, w_hi_ref[...], preferred_element_type=F32)
              + jnp.dot(lo, w_hi_ref[...], preferred_element_type=F32)
              + jnp.dot(hi, w_lo_ref[...], preferred_element_type=F32))
    return logits + bias_ref[...]


def _moe_router_kernel(h_ref, ln_ref, w_hi_ref, w_lo_ref, bias_ref, tri_ref, route_out, count_out, cnt_ref):
    @pl.when(pl.program_id(0) == 0)
    def _():
        cnt_ref[...] = jnp.zeros_like(cnt_ref)

    xf = _rms(_tok_load(h_ref), ln_ref[...])
    e1, e2, w1, w2 = _route(_router_logits(xf, w_hi_ref, w_lo_ref, bias_ref))
    lane = lax.broadcasted_iota(jnp.int32, (xf.shape[0], LANES), 1)
    hit1, hit2 = lane == e1, lane == e2
    onehot = jnp.where(hit1, 1.0, jnp.where(hit2, 1.0, 0.0))
    before = jnp.dot(tri_ref[...], _mx(onehot), preferred_element_type=F32) + cnt_ref[...]
    r1 = jnp.sum(jnp.where(hit1, before, 0.0), axis=-1, keepdims=True)
    r2 = jnp.sum(jnp.where(hit2, before, 0.0), axis=-1, keepdims=True)
    cnt_ref[...] = cnt_ref[...] + jnp.sum(onehot, axis=0, keepdims=True)
    count_out[...] = cnt_ref[...]
    fields = (e1.astype(F32), e2.astype(F32), w1, w2, r1, r2)
    rec = jnp.zeros(lane.shape, F32)
    for idx, val in enumerate(fields):
        rec = jnp.where(lane == idx, val, rec)
    route_out[...] = rec


def _router_weights(w_gr, b_gr, w_er, b_er):
    d = w_gr.shape[0]
    w = jnp.zeros((d, LANES), F32).at[:, :MOE_GROUPS].set(w_gr).at[:, MOE_GROUPS:MOE_GROUPS + MOE_EXPERTS].set(w_er)
    bias = jnp.zeros((1, LANES), F32).at[0, :MOE_GROUPS].set(b_gr).at[0, MOE_GROUPS:MOE_GROUPS + MOE_EXPERTS].set(b_er)
    w_hi = w.astype(MXU_DTYPE)
    w_lo = (w - w_hi.astype(F32)).astype(MXU_DTYPE)
    return w_hi, w_lo, bias


def _moe_router(h, ln, w_hi, w_lo, bias, tm=512):
    n, d = h.shape[0] // TOK_ROWS, ln.shape[0]
    tri = jnp.asarray(np.tril(np.ones((tm, tm), np.float32), -1), MXU_DTYPE)
    return pl.pallas_call(
        _moe_router_kernel,
        grid=(n // tm,),
        in_specs=[_tok_spec(tm, lambda i: (i, 0)), _const_spec((1, d)), _const_spec(w_hi.shape),
                  _const_spec(w_lo.shape),
                  _const_spec(bias.shape), _const_spec(tri.shape)],
        out_specs=[pl.BlockSpec((tm, LANES), lambda i: (i, 0)), _const_spec((1, LANES))],
        out_shape=[jax.ShapeDtypeStruct((n, LANES), F32), jax.ShapeDtypeStruct((1, LANES), F32)],
        scratch_shapes=[pltpu.VMEM((1, LANES), F32)],
        compiler_params=_params(("arbitrary",)),
        name="moe_router",
    )(h, ln.reshape(1, d), w_hi, w_lo, bias, tri)


def _moe_plan(route, counts, blk):
    n = route.shape[0]
    counts = counts[0, :MOE_EXPERTS].astype(jnp.int32)
    pcounts = (counts + blk - 1) // blk * blk
    pend = jnp.cumsum(pcounts)
    pstart = (pend - pcounts).astype(jnp.int32)
    n_blk = n * MOE_TOPK // blk + MOE_EXPERTS
    blk_start = jnp.arange(n_blk, dtype=jnp.int32) * blk
    blk_e = jnp.sum((blk_start[:, None] >= pend[None, :]).astype(jnp.int32), axis=1)
    blk_e = jnp.minimum(blk_e, MOE_EXPERTS - 1)
    n_valid = jnp.clip((pstart + counts)[blk_e] - blk_start, 0, blk).astype(jnp.int32)
    n_act = (pend[-1] // blk).astype(jnp.int32).reshape(1)
    expert = route[:, 0:MOE_TOPK].astype(jnp.int32)
    onehot = expert[:, :, None] == jnp.arange(MOE_EXPERTS, dtype=jnp.int32)
    seg_start = jnp.sum(jnp.where(onehot, pstart, 0), axis=-1)
    sorted_pos = seg_start + route[:, 4:4 + MOE_TOPK].astype(jnp.int32)
    return blk_e, n_valid, n_act, sorted_pos


def _moe_permute_kernel(n_valid_ref, pos_ref, src_hbm, dst_hbm, sem, zbuf, zsem, *, to_sorted, n_tok):
    tm = pos_ref.shape[2] // MOE_TOPK
    i = pl.program_id(0)
    base = i * tm
    slot = lax.rem(i, 2)

    def tile(ref, tok):
        return ref.at[pl.ds(pl.multiple_of(tok * TOK_ROWS, TOK_ROWS), TOK_ROWS)]

    if to_sorted:
        rows = zbuf.shape[0]
        blk = rows // TOK_ROWS

        @pl.when(i == 0)
        def _():
            zbuf[...] = jnp.zeros_like(zbuf)

            def fill(j, wait):
                @pl.when(n_valid_ref[j] < blk)
                def _():
                    start = 0 if wait else pl.multiple_of(j * rows, rows)
                    cp = pltpu.make_async_copy(zbuf, dst_hbm.at[pl.ds(start, rows)], zsem)
                    cp.wait() if wait else cp.start()

            lax.fori_loop(0, n_valid_ref.shape[0], lambda j, c: (fill(j, False), c)[1], 0)
            lax.fori_loop(0, n_valid_ref.shape[0], lambda j, c: (fill(j, True), c)[1], 0)

    group = 8

    def body(g, carry):
        t0 = g * group
        pos = [[pos_ref[0, 0, MOE_TOPK * (t0 + j) + k] for k in range(MOE_TOPK)] for j in range(group)]
        for j in range(group):
            for k in range(MOE_TOPK):
                if to_sorted:
                    src, dst = tile(src_hbm, base + t0 + j), tile(dst_hbm, pos[j][k])
                else:
                    src, dst = tile(src_hbm, pos[j][k]), tile(dst_hbm, k * n_tok + base + t0 + j)
                pltpu.make_async_copy(src, dst, sem.at[slot]).start()
        return carry

    lax.fori_loop(0, tm // group, body, 0)

    def wait_step(s):
        rows = MOE_TOPK * tm * TOK_ROWS
        pltpu.make_async_copy(src_hbm.at[pl.ds(0, rows)], dst_hbm.at[pl.ds(0, rows)], sem.at[s]).wait()

    @pl.when(i >= 1)
    def _():
        wait_step(1 - slot)

    @pl.when(i == pl.num_programs(0) - 1)
    def _():
        wait_step(slot)


def _moe_permute(src, n_valid, sorted_pos, n_out, to_sorted, blk, tm=256):
    n = sorted_pos.shape[0]
    pos = sorted_pos.reshape(n // tm, 1, MOE_TOPK * tm)
    grid_spec = pltpu.PrefetchScalarGridSpec(
        num_scalar_prefetch=1,
        grid=(n // tm,),
        in_specs=[pl.BlockSpec((1, 1, MOE_TOPK * tm), lambda i, nv: (i, 0, 0), memory_space=pltpu.SMEM),
                  pl.BlockSpec(memory_space=pl.ANY)],
        out_specs=pl.BlockSpec(memory_space=pl.ANY),
        scratch_shapes=[pltpu.SemaphoreType.DMA((2,)),
                        pltpu.VMEM((TOK_ROWS * (blk if to_sorted else 1), LANES), src.dtype),
                        pltpu.SemaphoreType.DMA],
    )
    return pl.pallas_call(
        functools.partial(_moe_permute_kernel, to_sorted=to_sorted, n_tok=n),
        grid_spec=grid_spec,
        out_shape=jax.ShapeDtypeStruct((TOK_ROWS * n_out, LANES), src.dtype),
        compiler_params=_params(("arbitrary",)),
        name="moe_dispatch" if to_sorted else "moe_collect",
    )(n_valid, pos, src)


def _moe_expert_kernel(blk_e_ref, n_act_ref, x_ref, ln_ref, w1_ref, w3_ref, w2_ref, y_ref):
    del blk_e_ref
    i = pl.program_id(0)

    @pl.when(i < n_act_ref[0])
    def _():
        xn = _mx(_rms(_tok_load(x_ref), ln_ref[...]))
        h1 = jnp.dot(xn, w1_ref[0], preferred_element_type=F32)
        h3 = jnp.dot(xn, w3_ref[0], preferred_element_type=F32)
        _tok_store(y_ref, _dot(h1 * jax.nn.sigmoid(h1) * h3, w2_ref[0]))

    @pl.when(i >= n_act_ref[0])
    def _():
        y_ref[...] = jnp.zeros_like(y_ref)


def _moe_experts(xs, ln, w1, w3, w2, blk_e, n_act, blk):
    p = xs.shape[0] // TOK_ROWS
    d, hdim = w1.shape[1], w1.shape[2]
    grid_spec = pltpu.PrefetchScalarGridSpec(
        num_scalar_prefetch=2,
        grid=(p // blk,),
        in_specs=[_tok_spec(blk, lambda i, be, na: (i, 0)),
                  pl.BlockSpec((1, d), lambda i, be, na: (0, 0)),
                  pl.BlockSpec((1, d, hdim), lambda i, be, na: (be[i], 0, 0)),
                  pl.BlockSpec((1, d, hdim), lambda i, be, na: (be[i], 0, 0)),
                  pl.BlockSpec((1, hdim, d), lambda i, be, na: (be[i], 0, 0))],
        out_specs=_tok_spec(blk, lambda i, be, na: (i, 0)),
    )
    return pl.pallas_call(
        _moe_expert_kernel,
        grid_spec=grid_spec,
        out_shape=jax.ShapeDtypeStruct(xs.shape, F32),
        compiler_params=_params(("arbitrary",)),
        name="moe_experts",
    )(blk_e, n_act, xs, ln.reshape(1, d), w1, w3, w2)


def _moe_layer(h, ln, w_gr, b_gr, w_er, b_er, w1, w3, w2):
    n = h.shape[0] // TOK_ROWS
    blk = MOE_BLOCK
    route, counts = _moe_router(h, ln, *_router_weights(w_gr, b_gr, w_er, b_er))
    blk_e, n_valid, n_act, sorted_pos = _moe_plan(route, counts, blk)
    p = n * MOE_TOPK + MOE_EXPERTS * blk
    xs = _moe_permute(h, n_valid, sorted_pos, p, True, blk)
    ys = _moe_experts(xs, ln, _mx(w1), _mx(w3), _mx(w2), blk_e, n_act, blk)
    return _moe_permute(ys, n_valid, sorted_pos, n * MOE_TOPK, False, blk), route


def _moe_combine(h_ref, s0_ref, s1_ref, route):
    return _tok_load(h_ref) + route[:, 2:3] * _tok_load(s0_ref) + route[:, 3:4] * _tok_load(s1_ref)


def _final_kernel(h_ref, s0_ref, s1_ref, route_ref, ln_ref, o_ref):
    o_ref[...] = _rms(_moe_combine(h_ref, s0_ref, s1_ref, route_ref[...]), ln_ref[...])


def _final(h, slots, route, ln, tm=512):
    n, d = route.shape[0], ln.shape[0]
    return pl.pallas_call(
        _final_kernel,
        grid=(n // tm,),
        in_specs=[_tok_spec(tm, lambda i: (i, 0)), _tok_spec(tm, lambda i: (i, 0)),
                  _tok_spec(tm, lambda i: (n // tm + i, 0)),
                  pl.BlockSpec((tm, LANES), lambda i: (i, 0)), _const_spec((1, d))],
        out_specs=pl.BlockSpec((tm, d), lambda i: (i, 0)),
        out_shape=jax.ShapeDtypeStruct((n, d), F32),
        compiler_params=_params(("arbitrary",)),
        name="final_norm",
    )(h, slots, slots, route, ln.reshape(1, d))


NSA_HEADS = 16
NSA_GROUPS = 4
NSA_HPG = NSA_HEADS // NSA_GROUPS
NSA_DH = D_MODEL // NSA_HEADS
NSA_PARTS = 6
CMP_BLOCK = 32
CMP_STRIDE = 16
SEL_BLOCK = 64
SEL_TOPK = 8
WINDOW = 512
N_BRANCH = 3
IMP_FORCE = 1e4
REL_BUCKETS = 32
REL_MAX_DIST = 128
NSA_TQ = 256
NSA_TK = 256
GATE_LANES = LANES
CMP_PARTS = 2


def _nsa_proj_kernel(h_ref, s0_ref, s1_ref, route_ref, kvln_ref, qln_ref, kvw_ref, qw_ref,
                     h_out, q_out, gate_out, kvc_out, kv_out):
    h1 = _moe_combine(h_ref, s0_ref, s1_ref, route_ref[0])
    h_out[0] = h1
    kv = _dot(_rms(h1, kvln_ref[...]), kvw_ref[...])
    for part in range(NSA_PARTS):
        for g in range(NSA_GROUPS):
            col = (part * NSA_GROUPS + g) * NSA_DH
            piece = kv[:, col:col + NSA_DH]
            if part < CMP_PARTS:
                kvc_out[part, 0, g] = piece
            else:
                kv_out[part - CMP_PARTS, 0, g] = piece.astype(kv_out.dtype)
    qg = _dot(_rms(h1, qln_ref[...]), qw_ref[...])
    q_out[0] = (qg[:, :D_MODEL] * (NSA_DH ** -0.5)).astype(q_out.dtype)
    gate_out[0] = qg[:, D_MODEL:]


def _gate_padded_weight(q_w):
    d = q_w.shape[0]
    n_gate = NSA_HPG * N_BRANCH
    gates = q_w[:, d:].reshape(d, NSA_GROUPS, n_gate)
    gates = jnp.pad(gates, ((0, 0), (0, 0), (0, GATE_LANES - n_gate))).reshape(d, NSA_GROUPS * GATE_LANES)
    return jnp.concatenate([q_w[:, :d], gates], axis=1)


def _nsa_proj(h, slots, route, kv_ln, q_ln, kv_w, q_w, b, t, tm=256):
    d = kv_ln.shape[0]
    blk = lambda w: pl.BlockSpec((1, tm, w), lambda i, j: (i, j, 0))
    per_b = t // tm
    tok = lambda first: _tok_spec(tm, lambda i, j: (first + i * per_b + j, 0))
    gw = NSA_GROUPS * GATE_LANES
    qw = _gate_padded_weight(q_w)
    kv_spec = lambda parts: pl.BlockSpec((parts, 1, NSA_GROUPS, tm, NSA_DH), lambda i, j: (0, i, 0, j, 0))
    kv_shape = lambda parts: (parts, b, NSA_GROUPS, t, NSA_DH)
    return pl.pallas_call(
        _nsa_proj_kernel,
        grid=(b, per_b),
        in_specs=[tok(0), tok(0), tok(b * per_b), blk(LANES), _const_spec((1, d)), _const_spec((1, d)),
                  _const_spec(kv_w.shape), _const_spec(qw.shape)],
        out_specs=[blk(d), blk(d), blk(gw), kv_spec(CMP_PARTS), kv_spec(NSA_PARTS - CMP_PARTS)],
        out_shape=[jax.ShapeDtypeStruct((b, t, d), F32), jax.ShapeDtypeStruct((b, t, d), ACT_DTYPE),
                   jax.ShapeDtypeStruct((b, t, gw), F32), jax.ShapeDtypeStruct(kv_shape(CMP_PARTS), F32),
                   jax.ShapeDtypeStruct(kv_shape(NSA_PARTS - CMP_PARTS), ACT_DTYPE)],
        compiler_params=_params(("arbitrary", "arbitrary")),
        name="nsa_proj",
    )(h, slots, slots, route.reshape(b, t, LANES),
      kv_ln.reshape(1, d), q_ln.reshape(1, d), _mx(kv_w), _mx(qw))


def _gelu_tanh(x):
    return 0.5 * x * (1.0 + jnp.tanh(math.sqrt(2.0 / math.pi) * (x + 0.044715 * (x * x * x))))


def _compress_kernel(k_ref, v_ref, pek_ref, pev_ref, kw1_ref, kw2_ref, vw1_ref, vw2_ref, kc_out, vc_out):
    dh = k_ref.shape[-1]
    n_chunk = k_ref.shape[3] // CMP_STRIDE

    def one(z_ref, pe_ref, w1_ref, w2_ref, out):
        first = jnp.zeros((n_chunk, w1_ref.shape[1]), F32)
        second = first
        for l in range(CMP_STRIDE):
            z = z_ref[0, 0, 0, pl.ds(l, n_chunk, stride=CMP_STRIDE), :]
            lo, hi = l, CMP_STRIDE + l
            first = first + _dot(z + pe_ref[0, lo:lo + 1, :], w1_ref[dh * lo:dh * (lo + 1), :])
            second = second + _dot(z + pe_ref[0, hi:hi + 1, :], w1_ref[dh * hi:dh * (hi + 1), :])
        hid = first + pltpu.roll(second, n_chunk - 1, axis=0)
        out[0, 0] = _dot(_gelu_tanh(hid), w2_ref[...]).astype(out.dtype)

    one(k_ref, pek_ref, kw1_ref, kw2_ref, kc_out)
    one(v_ref, pev_ref, vw1_ref, vw2_ref, vc_out)


def _compress(kvc, pe_k, pe_v, k_w1, k_w2, v_w1, v_w2):
    _, b, g, t, dh = kvc.shape
    n_chunk = t // CMP_STRIDE
    part = lambda idx: pl.BlockSpec((1, 1, 1, t, dh), lambda i, j: (idx, i, j, 0, 0))
    pe_spec = pl.BlockSpec((1, CMP_BLOCK, dh), lambda i, j: (j, 0, 0))
    out_spec = pl.BlockSpec((1, 1, n_chunk, dh), lambda i, j: (i, j, 0, 0))
    out = jax.ShapeDtypeStruct((b, g, n_chunk, dh), ACT_DTYPE)
    return pl.pallas_call(
        _compress_kernel,
        grid=(b, g),
        in_specs=[part(0), part(1), pe_spec, pe_spec, _const_spec(k_w1.shape), _const_spec(k_w2.shape),
                  _const_spec(v_w1.shape), _const_spec(v_w2.shape)],
        out_specs=[out_spec, out_spec],
        out_shape=[out, out],
        compiler_params=_params(("arbitrary", "arbitrary")),
        name="nsa_compress",
    )(kvc, kvc, pe_k.astype(F32), pe_v.astype(F32), _mx(k_w1), _mx(k_w2), _mx(v_w1), _mx(v_w2))


def _rel_bucket(dist):
    n = jnp.maximum(dist, 0)
    max_exact = REL_BUCKETS // 2
    nf = jnp.maximum(n, 1).astype(F32)
    large = max_exact + (jnp.log(nf / max_exact) / math.log(REL_MAX_DIST / max_exact)
                         * (REL_BUCKETS - max_exact)).astype(jnp.int32)
    return jnp.where(n < max_exact, n, jnp.minimum(large, REL_BUCKETS - 1))


def _toeplitz(vals, rows, cols):
    span = rows + cols
    z = jnp.concatenate([vals, vals[..., :1]], axis=-1)
    flat = jnp.tile(z, rows)[..., :rows * (span - 1)]
    skew = flat.reshape(vals.shape[:-1] + (rows, span - 1))
    return skew[..., rows - 1:rows - 1 + cols]


def _nsa_bias_tiles(rel_bias, t):
    tq, tk = NSA_TQ, NSA_TK
    table = rel_bias.astype(F32).reshape(REL_BUCKETS, NSA_GROUPS, NSA_HPG).transpose(1, 2, 0)
    by_dist = table[:, :, _rel_bucket(jnp.arange(REL_MAX_DIST + 1))]
    far = by_dist[:, :, REL_MAX_DIST:]

    def band(dist, valid, shift=None):
        b = by_dist[:, :, np.clip(dist, 0, REL_MAX_DIST)]
        if shift is not None:
            b = b - shift
        return jnp.where(jnp.asarray(valid), b, NEG)

    u = lambda rows, cols: np.arange(rows + cols - 1) - (rows - 1)
    wk = WINDOW + tq
    win = []
    for v in range(WINDOW // tq + 1):
        dist = v * tq - max(v * tq - WINDOW, 0) - u(tq, wk)
        win.append(_toeplitz(band(dist, (dist >= 0) & (dist < WINDOW)), tq, wk))
    bias_win = jnp.stack(win, axis=1)
    n_near = (REL_MAX_DIST + tk - 1) // tq + 1
    sel = []
    for e in range(n_near):
        dist = e * tq - u(tq, tk)
        sel.append(_toeplitz(band(dist, dist >= 0, far), tq, tk))
    sel.append(jnp.zeros_like(sel[0]))
    bias_sel = jnp.stack(sel, axis=1)
    n_cmp = t // CMP_STRIDE
    per_b = []
    for b in range(CMP_STRIDE):
        dist = b - (CMP_BLOCK - 1) - CMP_STRIDE * u(n_cmp, n_cmp)
        per_b.append(_toeplitz(band(dist, dist >= 0), n_cmp, n_cmp))
    bias_cmp = jnp.stack(per_b, axis=3).reshape(NSA_GROUPS, NSA_HPG, t, n_cmp)
    real = np.arange(n_cmp) < (t - CMP_BLOCK) // CMP_STRIDE + 1
    bias_cmp = jnp.where(jnp.asarray(real), bias_cmp, NEG).reshape(NSA_GROUPS, NSA_HPG, t // tq, tq, n_cmp)
    return bias_cmp, bias_win, bias_sel


def _nsa_constants(t):
    n_sel = t // SEL_BLOCK
    n_cmp = t // CMP_STRIDE
    cmp_start = np.arange(n_cmp) * CMP_STRIDE
    sel_start = np.arange(n_sel) * SEL_BLOCK
    overlap = ((cmp_start[:, None] < sel_start[None, :] + SEL_BLOCK)
               & (cmp_start[:, None] + CMP_BLOCK > sel_start[None, :]))
    overlap[(t - CMP_BLOCK) // CMP_STRIDE + 1:, :] = False
    per = NSA_TK // SEL_BLOCK
    expand = np.zeros((t // NSA_TK, LANES, NSA_TK), np.float32)
    for c in range(t // NSA_TK):
        for s in range(NSA_TK):
            expand[c, per * c + s // SEL_BLOCK, s] = 1.0
    return jnp.asarray(overlap.T, MXU_DTYPE), jnp.asarray(expand, MXU_DTYPE)


def _nsa_attn_kernel(q_ref, gate_ref, kc_ref, vc_ref, ks_ref, vs_ref, kw_ref, vw_ref,
                     bcmp_ref, bwin_ref, bsel_ref, ovl_ref, exp_ref, o_ref):
    tq, tk, dh = NSA_TQ, NSA_TK, NSA_DH
    i = pl.program_id(2)
    t0 = i * tq
    q = q_ref[0]
    heads = range(NSA_HPG)
    qh = [q[:, dh * h:dh * (h + 1)] for h in heads]

    k_cmp, v_cmp = kc_ref[0, 0], vc_ref[0, 0]
    lc = [_dot_nt(qh[h], k_cmp) + bcmp_ref[0, h, 0] for h in heads]
    mc = [jnp.max(x, axis=-1, keepdims=True) for x in lc]
    pcu = [jnp.where(mc[h] > 0.5 * NEG, jnp.exp(lc[h] - mc[h]), 0.0) for h in heads]
    pc = [x / jnp.maximum(jnp.sum(x, axis=-1, keepdims=True), 1e-30) for x in pcu]
    oc = [_dot(x, v_cmp) for x in pc]

    pcsum = (pc[0] + pc[1]) + (pc[2] + pc[3])
    hi = pcsum.astype(MXU_DTYPE)
    mid = (pcsum - hi.astype(F32))
    lo = (mid - mid.astype(MXU_DTYPE).astype(F32)).astype(MXU_DTYPE)
    ovl = ovl_ref[...]
    imp = (lax.dot_general(ovl, hi, (((1,), (1,)), ((), ())), preferred_element_type=F32)
           + lax.dot_general(ovl, mid.astype(MXU_DTYPE), (((1,), (1,)), ((), ())), preferred_element_type=F32)
           + lax.dot_general(ovl, lo, (((1,), (1,)), ((), ())), preferred_element_type=F32))
    n_sel = imp.shape[0]
    jj = lax.broadcasted_iota(jnp.int32, (n_sel, 1), 0)
    blk_q = lax.shift_right_logical(t0 + lax.broadcasted_iota(jnp.int32, (1, tq), 1),
                                    int(math.log2(SEL_BLOCK)))
    forced = (jj == 0) | (jj == blk_q) | (jj == blk_q - 1)
    score = jnp.where(forced, IMP_FORCE, jnp.where(jj <= blk_q, imp, -1.0))
    rank = jnp.zeros_like(score)
    for j in range(n_sel):
        row = score[j:j + 1, :]
        tie = (jj > j).astype(F32)
        rank = rank + jnp.where(row > score, 1.0, jnp.where(row == score, tie, 0.0))
    sel_t = (rank < SEL_TOPK).astype(F32)
    sel = jnp.concatenate([sel_t, jnp.zeros((LANES - n_sel, tq), F32)], axis=0).T.astype(MXU_DTYPE)

    def sel_step(c, carry):
        ms, ls, accs = carry
        start = pl.multiple_of(c * tk, tk)
        k_c = ks_ref[0, 0, 0, pl.ds(start, tk), :]
        v_c = vs_ref[0, 0, 0, pl.ds(start, tk), :]
        keep = jnp.dot(sel, exp_ref[c], preferred_element_type=F32)
        neg = (keep - 1.0) * (-NEG)
        near = jnp.minimum((t0 - start) // tq, bsel_ref.shape[1] - 1)
        lg = [_dot_nt(qh[h], k_c) + bsel_ref[0, near, h] + neg for h in heads]
        m_new = [jnp.maximum(ms[h], jnp.max(lg[h], axis=-1, keepdims=True)) for h in heads]
        alpha = [jnp.exp(ms[h] - m_new[h]) for h in heads]
        p = [jnp.exp(lg[h] - m_new[h]) for h in heads]
        l_new = [alpha[h] * ls[h] + jnp.sum(p[h], axis=-1, keepdims=True) for h in heads]
        acc_new = [alpha[h] * accs[h] + _dot(p[h], v_c) for h in heads]
        return tuple(m_new), tuple(l_new), tuple(acc_new)

    init = (tuple(jnp.full((tq, 1), NEG, F32) for _ in heads), tuple(jnp.zeros((tq, 1), F32) for _ in heads),
            tuple(jnp.zeros((tq, dh), F32) for _ in heads))
    _, l_sel, acc_sel = lax.fori_loop(0, (t0 + tq + tk - 1) // tk, sel_step, init)

    w0 = pl.multiple_of(jnp.maximum(t0 - WINDOW, 0), tq)
    wk = WINDOW + tq
    k_w = kw_ref[0, 0, 0, pl.ds(w0, wk), :]
    v_w = vw_ref[0, 0, 0, pl.ds(w0, wk), :]
    lw = [_dot_nt(qh[h], k_w) + bwin_ref[0, 0, h] for h in heads]
    pw = [jnp.exp(x - jnp.max(x, axis=-1, keepdims=True)) for x in lw]
    ow = [_dot(x, v_w) / jnp.sum(x, axis=-1, keepdims=True) for x in pw]

    gates = jax.nn.sigmoid(gate_ref[0])
    outs = []
    for h in heads:
        g = [gates[:, N_BRANCH * h + br:N_BRANCH * h + br + 1] for br in range(N_BRANCH)]
        outs.append(g[0] * oc[h] + g[1] * (acc_sel[h] / l_sel[h]) + g[2] * ow[h])
    o_ref[0] = jnp.concatenate(outs, axis=1).astype(o_ref.dtype)


def _nsa_attention(q, gate_logits, kv, k_cmp, v_cmp, rel_bias):
    b, t, d = q.shape
    tq, tk, dh = NSA_TQ, NSA_TK, NSA_DH
    bias_cmp, bias_win, bias_sel = _nsa_bias_tiles(rel_bias, t)
    ovl, expand = _nsa_constants(t)
    n_cmp = k_cmp.shape[2]
    n_win = bias_win.shape[1]
    part = lambda idx: pl.BlockSpec((1, 1, 1, t, dh), lambda g, bi, i: (idx, bi, g, 0, 0))
    cmp_spec = pl.BlockSpec((1, 1, n_cmp, dh), lambda g, bi, i: (bi, g, 0, 0))
    return pl.pallas_call(
        _nsa_attn_kernel,
        grid=(NSA_GROUPS, b, t // tq),
        in_specs=[pl.BlockSpec((1, tq, NSA_HPG * dh), lambda g, bi, i: (bi, i, g)),
                  pl.BlockSpec((1, tq, GATE_LANES), lambda g, bi, i: (bi, i, g)),
                  cmp_spec, cmp_spec, part(0), part(1), part(2), part(3),
                  pl.BlockSpec((1, NSA_HPG, 1, tq, n_cmp), lambda g, bi, i: (g, 0, i, 0, 0)),
                  pl.BlockSpec((1, 1, NSA_HPG, tq, WINDOW + tq),
                               lambda g, bi, i: (g, jnp.minimum(i, n_win - 1), 0, 0, 0)),
                  pl.BlockSpec((1, bias_sel.shape[1], NSA_HPG, tq, tk), lambda g, bi, i: (g, 0, 0, 0, 0)),
                  _const_spec(ovl.shape), _const_spec(expand.shape)],
        out_specs=pl.BlockSpec((1, tq, NSA_HPG * dh), lambda g, bi, i: (bi, i, g)),
        out_shape=jax.ShapeDtypeStruct((b, t, d), ACT_DTYPE),
        compiler_params=_params(("arbitrary", "arbitrary", "arbitrary")),
        name="nsa_attention",
    )(q, gate_logits, k_cmp, v_cmp, kv, kv, kv, kv, bias_cmp, bias_win, bias_sel, ovl, expand)


def _nsa_block(h, slots, route, kv_ln, kv_w, cmp_pe_k, cmp_pe_v, cmp_k_w1, cmp_k_w2, cmp_v_w1, cmp_v_w2,
               rel_bias, b_ln, b_wqg, b_wo, b, t):
    n, d = b * t, kv_ln.shape[0]
    h1, q, gate_logits, kvc, kv = _nsa_proj(h, slots, route, kv_ln, b_ln, kv_w, b_wqg, b, t)
    k_cmp, v_cmp = _compress(kvc, cmp_pe_k, cmp_pe_v, cmp_k_w1, cmp_k_w2, cmp_v_w1, cmp_v_w2)
    o = _nsa_attention(q, gate_logits, kv, k_cmp, v_cmp, rel_bias)
    return _matmul_res(o.reshape(n, d), _mx(b_wo), h1.reshape(n, d))


def kernel(x, a_ln, a_mu, a_wr, a_wk, a_wv, a_wo, a_w0, a_w1, a_w2, a_a0, a_a1, a_a2, a_g1, a_g2, a_kk, a_ka, a_rk, a_lnx_w, a_lnx_b, kv_ln, kv_w, cmp_pe_k, cmp_pe_v, cmp_k_w1, cmp_k_w2, cmp_v_w1, cmp_v_w2, rel_bias, b_ln, b_wqg, b_wo, m_ln, m_wg, m_bg, m_we, m_be, m_w1, m_w3, m_w2, final_ln):
    b, t, d = x.shape
    h = _rwkv_block(x, a_ln[0], a_mu[0], a_wr[0], a_wk[0], a_wv[0], a_wo[0], a_w0[0], a_w1[0], a_w2[0],
                    a_a0[0], a_a1[0], a_a2[0], a_g1[0], a_g2[0], a_kk[0], a_ka[0], a_rk[0],
                    a_lnx_w[0], a_lnx_b[0])
    moe = lambda hh, l: _moe_layer(hh, m_ln[l], m_wg[l], m_bg[l], m_we[l], m_be[l], m_w1[l], m_w3[l], m_w2[l])
    slots, route = moe(h, 0)
    h = _nsa_block(h, slots, route, kv_ln, kv_w, cmp_pe_k, cmp_pe_v, cmp_k_w1, cmp_k_w2, cmp_v_w1, cmp_v_w2,
                   rel_bias, b_ln[0], b_wqg[0], b_wo[0], b, t)
    slots, route = moe(h, 1)
    return _final(h, slots, route, final_ln).reshape(b, t, d)
```

```python
import functools
import math

import numpy as np
import jax
import jax.numpy as jnp
from jax import lax
from jax.experimental import pallas as pl
from jax.experimental.pallas import tpu as pltpu

F32 = jnp.float32
MXU_DTYPE = jnp.bfloat16
ACT_DTYPE = jnp.bfloat16

D_MODEL = 1024
NORM_EPS = 1e-6
RWKV_HEAD = 64
RWKV_HEADS = D_MODEL // RWKV_HEAD
RWKV_GN_EPS = 64e-5
RWKV_CHUNK = 64
LANES = 128
MXU_TILE = 256
VMEM_LIMIT = 56 * 1024 * 1024


def _mx(x):
    return x.astype(MXU_DTYPE)


def _dot(a, b):
    return jnp.dot(_mx(a), _mx(b), preferred_element_type=F32)


def _dot_nt(a, b):
    return lax.dot_general(_mx(a), _mx(b), (((1,), (1,)), ((), ())), preferred_element_type=F32)


def _split2(x):
    hi = x.astype(MXU_DTYPE)
    lo = (x - hi.astype(F32)).astype(MXU_DTYPE)
    return hi, lo


def _dot_hl(x, w):
    hi, lo = _split2(x)
    return jnp.dot(hi, w, preferred_element_type=F32) + jnp.dot(lo, w, preferred_element_type=F32)


def _rms(x, g):
    return x * lax.rsqrt(jnp.mean(x * x, axis=-1, keepdims=True) + NORM_EPS) * g


TOK_ROWS = 8


def _tok_load(ref):
    tm = ref.shape[0] // TOK_ROWS
    return jnp.concatenate([ref[pl.ds(s, tm, stride=TOK_ROWS), :] for s in range(TOK_ROWS)], axis=1)


def _tok_store(ref, val):
    tm = val.shape[0]
    for s in range(TOK_ROWS):
        ref[pl.ds(s, tm, stride=TOK_ROWS), :] = val[:, LANES * s:LANES * (s + 1)]


def _tok_spec(tm, index_map):
    return pl.BlockSpec((TOK_ROWS * tm, LANES), index_map)


def _slot_spec(tm, k, block_index):
    return pl.BlockSpec((1, TOK_ROWS * tm, LANES), lambda *idx: (k, block_index(*idx), 0))


def _const_spec(shape):
    nd = len(shape)
    return pl.BlockSpec(shape, lambda *_: (0,) * nd)


def _params(sem):
    return pltpu.CompilerParams(dimension_semantics=sem, vmem_limit_bytes=VMEM_LIMIT)


def _rwkv_pre_kernel(x_ref, ln_ref, mu_ref, wr_ref, wk_ref, wv_ref, w1_ref, a1_ref, g1_ref,
                     w2_ref, a2_ref, g2_ref, vec_ref,
                     r_out, k_out, v_out, kk_out, g_out, ld_out, a_out, prev_ref):
    tm = x_ref.shape[1]
    hn = _rms(x_ref[0], ln_ref[...])

    @pl.when(pl.program_id(1) == 0)
    def _():
        prev_ref[...] = jnp.zeros_like(prev_ref)

    rows = lax.broadcasted_iota(jnp.int32, (tm, 1), 0)
    shifted = jnp.where(rows == 0, prev_ref[0:1, :], pltpu.roll(hn, 1, axis=0))
    prev_ref[0:1, :] = hn[tm - 1:tm, :]
    xx = shifted - hn
    xr, xw, xk, xv, xa, xg = (_mx(hn + xx * mu_ref[i:i + 1, :]) for i in range(6))
    w0, a0, kkp, kap = (vec_ref[i:i + 1, :] for i in range(4))
    r = jnp.dot(xr, wr_ref[...], preferred_element_type=F32)
    k = jnp.dot(xk, wk_ref[...], preferred_element_type=F32)
    v = jnp.dot(xv, wv_ref[...], preferred_element_type=F32)
    z = w0 + _dot(jnp.tanh(jnp.dot(xw, w1_ref[...], preferred_element_type=F32)), w2_ref[...])
    a = jax.nn.sigmoid(a0 + _dot(jnp.dot(xa, a1_ref[...], preferred_element_type=F32), a2_ref[...]))
    g = _dot(jax.nn.sigmoid(jnp.dot(xg, g1_ref[...], preferred_element_type=F32)), g2_ref[...])
    ld_out[0] = (-math.exp(-0.5)) * jax.nn.sigmoid(z)
    a_out[0] = a
    r_out[0] = r.astype(r_out.dtype)
    v_out[0] = v.astype(v_out.dtype)
    kk_out[0] = (k * kkp).astype(kk_out.dtype)
    k_out[0] = (k * (1.0 + (a - 1.0) * kap)).astype(k_out.dtype)
    g_out[0] = g.astype(g_out.dtype)


def _rwkv_pre(x, ln, mu, wr, wk, wv, w1, a1, g1, w2, a2, g2, vecs, tm=256):
    b, t, d = x.shape
    act = jax.ShapeDtypeStruct((b, t, d), ACT_DTYPE)
    f32o = jax.ShapeDtypeStruct((b, t, d), F32)
    blk = pl.BlockSpec((1, tm, d), lambda i, j: (i, j, 0))
    consts = [ln, mu, wr, wk, wv, w1, a1, g1, w2, a2, g2, vecs]
    return pl.pallas_call(
        _rwkv_pre_kernel,
        grid=(b, t // tm),
        in_specs=[blk] + [_const_spec(c.shape) for c in consts],
        out_specs=[blk] * 7,
        out_shape=[act, act, act, act, act, f32o, f32o],
        scratch_shapes=[pltpu.VMEM((8, d), F32)],
        compiler_params=_params(("arbitrary", "arbitrary")),
        name="rwkv_pre",
    )(x, *consts)


def _stack4(x):
    return jnp.concatenate([x[:, MXU_TILE * q:MXU_TILE * (q + 1)] for q in range(4)], axis=0)


def _unstack4(y):
    c = y.shape[0] // 4
    return jnp.concatenate([y[c * q:c * (q + 1), :] for q in range(4)], axis=1)


def _rwkv_scan_kernel(r_ref, k_ref, v_ref, kk_ref, g_ref, ld_ref, a_ref,
                      ltri_ref, smask_ref, eye_ref, j_ref, rk_ref, lnw_ref, lnb_ref,
                      y_out, h_ref):
    c = RWKV_CHUNK

    @pl.when(pl.program_id(1) == 0)
    def _():
        h_ref[...] = jnp.zeros_like(h_ref)

    ld = ld_ref[0]
    r = r_ref[0].astype(F32)
    k = k_ref[0].astype(F32)
    v = v_ref[0].astype(F32)
    kk = kk_ref[0].astype(F32)
    jones = j_ref[...]
    ld_hi, ld_lo = _split2(ld)
    cum = (jnp.dot(ltri_ref[...], ld_hi, preferred_element_type=F32)
           + jnp.dot(ltri_ref[...], ld_lo, preferred_element_type=F32))
    cum_last = cum[c - 1:c, :]
    e_l = jnp.exp(cum)
    e_lm = jnp.exp(cum - ld)
    e_nl = jnp.exp(-cum)
    e_cl = jnp.exp(cum_last - cum)
    e_c = jnp.exp(cum_last)
    ss = _unstack4(_dot_hl(_stack4(kk * kk), jones))
    kkn = kk * lax.rsqrt(jnp.maximum(ss, 1e-24))
    a_vec = -kkn
    b_vec = kkn * a_ref[0]
    rt = r * e_l
    at = a_vec * e_lm
    kt = k * e_nl
    bt = b_vec * e_nl
    bh = b_vec * e_cl
    kh = k * e_cl

    lane = lax.broadcasted_iota(jnp.int32, (1, LANES), 1)
    m_l = (lane < RWKV_HEAD).astype(F32)
    m_r = 1.0 - m_l

    def st(xp):
        return jnp.concatenate([xp * m_l, xp * m_r], axis=0)

    def dup(xp):
        return jnp.concatenate([xp, xp], axis=0)

    smask = smask_ref[...] > 0.0
    eye = eye_ref[...]
    pairs = range(D_MODEL // LANES)
    sls = [slice(LANES * p, LANES * (p + 1)) for p in pairs]
    at_s = [st(at[:, sl]) for sl in sls]
    rt_s = [st(rt[:, sl]) for sl in sls]
    v_s = [st(v[:, sl]) for sl in sls]
    s = [jnp.where(smask,
                   _dot_nt(jnp.concatenate([at_s[p], rt_s[p]], axis=0),
                           jnp.concatenate([dup(bt[:, sls[p]]), dup(kt[:, sls[p]])], axis=0)),
                   0.0) for p in pairs]
    a_ab = [x[:2 * c, :2 * c] for x in s]
    a_r = [x[2 * c:, :] for x in s]
    g1 = [_dot(s[p][:2 * c, 2 * c:], v_s[p]) for p in pairs]
    t_m = [eye + x for x in a_ab]
    a_pow = [_dot(x, x) for x in a_ab]
    n = 2
    while n < c:
        if 2 * n < c:
            both = [_dot(a_pow[p], jnp.concatenate([a_pow[p], t_m[p]], axis=1)) for p in pairs]
            a_pow = [x[:, :2 * c] for x in both]
            t_m = [t_m[p] + both[p][:, 2 * c:] for p in pairs]
        else:
            t_m = [t_m[p] + _dot(a_pow[p], t_m[p]) for p in pairs]
        n *= 2
    tw = [_dot(t_m[p], jnp.concatenate([at_s[p], g1[p]], axis=1)) for p in pairs]
    zmat = [jnp.concatenate([tw[p], jnp.concatenate([jnp.zeros_like(v_s[p]), v_s[p]], axis=1)], axis=0)
            for p in pairs]
    upper = [_dot(a_r[p], zmat[p]) for p in pairs]
    bk_t = [jnp.concatenate([st(bh[:, sl]), st(kh[:, sl])], axis=0).T for sl in sls]
    lower = [_dot(bk_t[p], zmat[p]) for p in pairs]
    seq = [_dot(jnp.concatenate([upper[p][:, :LANES] + rt_s[p],
                                 lower[p][:, :LANES] + eye * e_c[:, sls[p]]], axis=0), h_ref[p])
           for p in pairs]
    ys = []
    for p in pairs:
        y_st = seq[p][:2 * c, :] + upper[p][:, LANES:]
        h_ref[p] = seq[p][2 * c:, :] + lower[p][:, LANES:]
        ys.append(y_st[:c, :] + y_st[c:, :])
    y = jnp.concatenate(ys, axis=1)

    inv_n = 1.0 / RWKV_HEAD
    y4 = _stack4(y)
    dlt = y4 - _dot_hl(y4, jones) * inv_n
    var = _dot_hl(dlt * dlt, jones) * inv_n
    yn = dlt * lax.rsqrt(var + RWKV_GN_EPS)
    bonus = _dot_hl(_stack4(r * k) * rk_ref[...], jones) * _stack4(v)
    out = (yn * lnw_ref[...] + lnb_ref[...] + bonus) * _stack4(g_ref[0].astype(F32))
    y_out[0] = _unstack4(out).astype(y_out.dtype)


def _scan_constants():
    c = RWKV_CHUNK
    ltri = np.tril(np.ones((c, c), np.float32))
    rho = np.arange(4 * c)
    r_type, r_head, r_t = rho // (2 * c), (rho % (2 * c)) // c, rho % c
    same = r_head[:, None] == r_head[None, :]
    strict = r_t[None, :] < r_t[:, None]
    incl = r_t[None, :] <= r_t[:, None]
    smask = same & np.where(r_type[:, None] == 0, strict, incl)
    eye = np.eye(LANES, dtype=np.float32)
    head = np.arange(MXU_TILE) // RWKV_HEAD
    jones = (head[:, None] == head[None, :]).astype(np.float32)
    return (jnp.asarray(ltri, MXU_DTYPE), jnp.asarray(smask, F32), jnp.asarray(eye, F32),
            jnp.asarray(jones, MXU_DTYPE))


def _stack_param(p):
    return jnp.repeat(p.reshape(4, MXU_TILE).astype(F32), RWKV_CHUNK, axis=0)


def _rwkv_scan(r, k, v, kk, g, ld, a, rk, lnw, lnb):
    b, t, d = r.shape
    c = RWKV_CHUNK
    blk = pl.BlockSpec((1, c, d), lambda i, j: (i, j, 0))
    consts = list(_scan_constants()) + [_stack_param(rk.reshape(-1)), _stack_param(lnw), _stack_param(lnb)]
    return pl.pallas_call(
        _rwkv_scan_kernel,
        grid=(b, t // c),
        in_specs=[blk] * 7 + [_const_spec(x.shape) for x in consts],
        out_specs=blk,
        out_shape=jax.ShapeDtypeStruct((b, t, d), ACT_DTYPE),
        scratch_shapes=[pltpu.VMEM((d // LANES, LANES, LANES), F32)],
        compiler_params=_params(("arbitrary", "arbitrary")),
        name="rwkv_scan",
    )(r, k, v, kk, g, ld, a, *consts)


def _matmul_res_kernel(a_ref, w_ref, res_ref, o_ref):
    _tok_store(o_ref, res_ref[...] + jnp.dot(a_ref[...], w_ref[...], preferred_element_type=F32))


def _matmul_res(a, w, res, tm=512):
    n, kdim = a.shape
    d = w.shape[1]
    return pl.pallas_call(
        _matmul_res_kernel,
        grid=(n // tm,),
        in_specs=[pl.BlockSpec((tm, kdim), lambda i: (i, 0)), _const_spec(w.shape),
                  pl.BlockSpec((tm, d), lambda i: (i, 0))],
        out_specs=_tok_spec(tm, lambda i: (i, 0)),
        out_shape=jax.ShapeDtypeStruct((TOK_ROWS * n, LANES), F32),
        compiler_params=_params(("arbitrary",)),
        name="matmul_res",
    )(a, w, res)


def _rwkv_block(x, a_ln, a_mu, a_wr, a_wk, a_wv, a_wo, a_w0, a_w1, a_w2, a_a0, a_a1, a_a2,
                a_g1, a_g2, a_kk, a_ka, a_rk, a_lnx_w, a_lnx_b):
    b, t, d = x.shape
    vecs = jnp.stack([a_w0, a_a0, a_kk, a_ka]).astype(F32)
    r, k, v, kk, g, ld, a = _rwkv_pre(
        x, a_ln.reshape(1, d), a_mu, _mx(a_wr), _mx(a_wk), _mx(a_wv), _mx(a_w1), _mx(a_a1), _mx(a_g1),
        _mx(a_w2), _mx(a_a2), _mx(a_g2), vecs)
    y = _rwkv_scan(r, k, v, kk, g, ld, a, a_rk, a_lnx_w, a_lnx_b)
    return _matmul_res(y.reshape(b * t, d), _mx(a_wo), x.reshape(b * t, d))


MOE_GROUPS = 4
MOE_EPG = 8
MOE_EXPERTS = MOE_GROUPS * MOE_EPG
MOE_TOPK = 2
MOE_BLOCK = 256
NEG = -1e30


def _route(logits):
    lane = lax.broadcasted_iota(jnp.int32, logits.shape, 1)
    is_grp = lane < MOE_GROUPS
    gl = jnp.where(is_grp, logits, NEG)
    ge = jnp.exp(gl - jnp.max(gl, axis=-1, keepdims=True))
    gp = ge / jnp.sum(ge, axis=-1, keepdims=True)
    grp_w = jnp.max(gp, axis=-1, keepdims=True)
    grp = jnp.min(jnp.where(is_grp & (gp == grp_w), lane, LANES), axis=-1, keepdims=True)
    lo = MOE_GROUPS + MOE_EPG * grp
    in_grp = (lane >= lo) & (lane < lo + MOE_EPG)
    el = jnp.where(in_grp, logits, NEG)
    ee = jnp.exp(el - jnp.max(el, axis=-1, keepdims=True))
    ep = ee / jnp.sum(ee, axis=-1, keepdims=True)
    p1 = jnp.max(jnp.where(in_grp, ep, -1.0), axis=-1, keepdims=True)
    i1 = jnp.min(jnp.where(in_grp & (ep == p1), lane, LANES), axis=-1, keepdims=True)
    rest = in_grp & (lane != i1)
    p2 = jnp.max(jnp.where(rest, ep, -1.0), axis=-1, keepdims=True)
    i2 = jnp.min(jnp.where(rest & (ep == p2), lane, LANES), axis=-1, keepdims=True)
    scale = grp_w / (p1 + p2)
    return i1 - MOE_GROUPS, i2 - MOE_GROUPS, p1 * scale, p2 * scale


def _router_logits(xf, w_hi_ref, w_lo_ref, bias_ref):
    hi, lo = _split2(xf)
    logits = (jnp.dot(hi, w_hi_ref[...], preferred_element_type=F32)
              + jnp.dot(lo, w_hi_ref[...], preferred_element_type=F32)
              + jnp.dot(hi, w_lo_ref[...], preferred_element_type=F32))
    return logits + bias_ref[...]


def _moe_router_kernel(h_ref, ln_ref, w_hi_ref, w_lo_ref, bias_ref, tri_ref, route_out, count_out, cnt_ref):
    @pl.when(pl.program_id(0) == 0)
    def _():
        cnt_ref[...] = jnp.zeros_like(cnt_ref)

    xf = _rms(_tok_load(h_ref), ln_ref[...])
    e1, e2, w1, w2 = _route(_router_logits(xf, w_hi_ref, w_lo_ref, bias_ref))
    lane = lax.broadcasted_iota(jnp.int32, (xf.shape[0], LANES), 1)
    hit1, hit2 = lane == e1, lane == e2
    onehot = jnp.where(hit1, 1.0, jnp.where(hit2, 1.0, 0.0))
    before = jnp.dot(tri_ref[...], _mx(onehot), preferred_element_type=F32) + cnt_ref[...]
    r1 = jnp.sum(jnp.where(hit1, before, 0.0), axis=-1, keepdims=True)
    r2 = jnp.sum(jnp.where(hit2, before, 0.0), axis=-1, keepdims=True)
    cnt_ref[...] = cnt_ref[...] + jnp.sum(onehot, axis=0, keepdims=True)
    count_out[...] = cnt_ref[...]
    fields = (e1.astype(F32), e2.astype(F32), w1, w2, r1, r2)
    rec = jnp.zeros(lane.shape, F32)
    for idx, val in enumerate(fields):
        rec = jnp.where(lane == idx, val, rec)
    route_out[...] = rec


def _router_weights(w_gr, b_gr, w_er, b_er):
    d = w_gr.shape[0]
    w = jnp.zeros((d, LANES), F32).at[:, :MOE_GROUPS].set(w_gr).at[:, MOE_GROUPS:MOE_GROUPS + MOE_EXPERTS].set(w_er)
    bias = jnp.zeros((1, LANES), F32).at[0, :MOE_GROUPS].set(b_gr).at[0, MOE_GROUPS:MOE_GROUPS + MOE_EXPERTS].set(b_er)
    w_hi = w.astype(MXU_DTYPE)
    w_lo = (w - w_hi.astype(F32)).astype(MXU_DTYPE)
    return w_hi, w_lo, bias


def _moe_router(h, ln, w_hi, w_lo, bias, tm=512):
    n, d = h.shape[0] // TOK_ROWS, ln.shape[0]
    tri = jnp.asarray(np.tril(np.ones((tm, tm), np.float32), -1), MXU_DTYPE)
    return pl.pallas_call(
        _moe_router_kernel,
        grid=(n // tm,),
        in_specs=[_tok_spec(tm, lambda i: (i, 0)), _const_spec((1, d)), _const_spec(w_hi.shape),
                  _const_spec(w_lo.shape),
                  _const_spec(bias.shape), _const_spec(tri.shape)],
        out_specs=[pl.BlockSpec((tm, LANES), lambda i: (i, 0)), _const_spec((1, LANES))],
        out_shape=[jax.ShapeDtypeStruct((n, LANES), F32), jax.ShapeDtypeStruct((1, LANES), F32)],
        scratch_shapes=[pltpu.VMEM((1, LANES), F32)],
        compiler_params=_params(("arbitrary",)),
        name="moe_router",
    )(h, ln.reshape(1, d), w_hi, w_lo, bias, tri)


def _moe_plan(route, counts, blk):
    n = route.shape[0]
    counts = counts[0, :MOE_EXPERTS].astype(jnp.int32)
    pcounts = (counts + blk - 1) // blk * blk
    pend = jnp.cumsum(pcounts)
    pstart = (pend - pcounts).astype(jnp.int32)
    n_blk = n * MOE_TOPK // blk + MOE_EXPERTS
    blk_start = jnp.arange(n_blk, dtype=jnp.int32) * blk
    blk_e = jnp.sum((blk_start[:, None] >= pend[None, :]).astype(jnp.int32), axis=1)
    blk_e = jnp.minimum(blk_e, MOE_EXPERTS - 1)
    n_valid = jnp.clip((pstart + counts)[blk_e] - blk_start, 0, blk).astype(jnp.int32)
    n_act = (pend[-1] // blk).astype(jnp.int32).reshape(1)
    expert = route[:, 0:MOE_TOPK].astype(jnp.int32)
    onehot = expert[:, :, None] == jnp.arange(MOE_EXPERTS, dtype=jnp.int32)
    seg_start = jnp.sum(jnp.where(onehot, pstart, 0), axis=-1)
    sorted_pos = seg_start + route[:, 4:4 + MOE_TOPK].astype(jnp.int32)
    return blk_e, n_valid, n_act, sorted_pos


def _token_copies(pos_ref, tm, make_copy):
    group = 8

    def body(g, carry):
        t0 = g * group
        pos = [[pos_ref[0, 0, MOE_TOPK * (t0 + j) + k] for k in range(MOE_TOPK)] for j in range(group)]
        for j in range(group):
            for k in range(MOE_TOPK):
                make_copy(t0 + j, k, pos[j][k]).start()
        return carry

    lax.fori_loop(0, tm // group, body, 0)


def _tile(ref, tok):
    return ref.at[pl.ds(pl.multiple_of(tok * TOK_ROWS, TOK_ROWS), TOK_ROWS)]


def _moe_dispatch_kernel(n_valid_ref, pos_ref, src_ref, dst_hbm, sem, zbuf, zsem):
    tm = src_ref.shape[0] // TOK_ROWS

    @pl.when(pl.program_id(0) == 0)
    def _():
        rows = zbuf.shape[0]
        zbuf[...] = jnp.zeros_like(zbuf)

        def fill(j, wait):
            @pl.when(n_valid_ref[j] < rows // TOK_ROWS)
            def _():
                start = 0 if wait else pl.multiple_of(j * rows, rows)
                cp = pltpu.make_async_copy(zbuf, dst_hbm.at[pl.ds(start, rows)], zsem)
                cp.wait() if wait else cp.start()

        lax.fori_loop(0, n_valid_ref.shape[0], lambda j, c: (fill(j, False), c)[1], 0)
        lax.fori_loop(0, n_valid_ref.shape[0], lambda j, c: (fill(j, True), c)[1], 0)

    _token_copies(pos_ref, tm, lambda t, k, pos: pltpu.make_async_copy(_tile(src_ref, t), _tile(dst_hbm, pos), sem))
    for k in range(MOE_TOPK):
        pltpu.make_async_copy(src_ref, dst_hbm.at[pl.ds(0, src_ref.shape[0])], sem).wait()


def _moe_collect_kernel(pos_ref, src_hbm, dst_ref, sem):
    tm = dst_ref.shape[1] // TOK_ROWS
    _token_copies(pos_ref, tm,
                  lambda t, k, pos: pltpu.make_async_copy(_tile(src_hbm, pos), _tile(dst_ref.at[k], t), sem))
    for k in range(MOE_TOPK):
        pltpu.make_async_copy(src_hbm.at[pl.ds(0, dst_ref.shape[1])], dst_ref.at[k], sem).wait()


def _moe_dispatch(h, n_valid, sorted_pos, p, blk, tm=512):
    n = sorted_pos.shape[0]
    pos = sorted_pos.reshape(n // tm, 1, MOE_TOPK * tm)
    grid_spec = pltpu.PrefetchScalarGridSpec(
        num_scalar_prefetch=1,
        grid=(n // tm,),
        in_specs=[pl.BlockSpec((1, 1, MOE_TOPK * tm), lambda i, nv: (i, 0, 0), memory_space=pltpu.SMEM),
                  _tok_spec(tm, lambda i, nv: (i, 0))],
        out_specs=pl.BlockSpec(memory_space=pl.ANY),
        scratch_shapes=[pltpu.SemaphoreType.DMA, pltpu.VMEM((TOK_ROWS * blk, LANES), h.dtype),
                        pltpu.SemaphoreType.DMA],
    )
    return pl.pallas_call(
        _moe_dispatch_kernel,
        grid_spec=grid_spec,
        out_shape=jax.ShapeDtypeStruct((TOK_ROWS * p, LANES), h.dtype),
        compiler_params=_params(("arbitrary",)),
        name="moe_dispatch",
    )(n_valid, pos, h)


def _moe_collect(ys, sorted_pos, tm=512):
    n = sorted_pos.shape[0]
    pos = sorted_pos.reshape(n // tm, 1, MOE_TOPK * tm)
    return pl.pallas_call(
        _moe_collect_kernel,
        grid=(n // tm,),
        in_specs=[pl.BlockSpec((1, 1, MOE_TOPK * tm), lambda i: (i, 0, 0), memory_space=pltpu.SMEM),
                  pl.BlockSpec(memory_space=pl.ANY)],
        out_specs=pl.BlockSpec((MOE_TOPK, TOK_ROWS * tm, LANES), lambda i: (0, i, 0)),
        out_shape=jax.ShapeDtypeStruct((MOE_TOPK, TOK_ROWS * n, LANES), ys.dtype),
        scratch_shapes=[pltpu.SemaphoreType.DMA],
        compiler_params=_params(("arbitrary",)),
        name="moe_collect",
    )(pos, ys)


def _moe_expert_kernel(blk_e_ref, n_act_ref, x_ref, ln_ref, w1_ref, w3_ref, w2_ref, y_ref):
    del blk_e_ref
    i = pl.program_id(0)

    @pl.when(i < n_act_ref[0])
    def _():
        xn = _mx(_rms(_tok_load(x_ref), ln_ref[...]))
        h1 = jnp.dot(xn, w1_ref[0], preferred_element_type=F32)
        h3 = jnp.dot(xn, w3_ref[0], preferred_element_type=F32)
        _tok_store(y_ref, _dot(h1 * jax.nn.sigmoid(h1) * h3, w2_ref[0]))

    @pl.when(i >= n_act_ref[0])
    def _():
        y_ref[...] = jnp.zeros_like(y_ref)


def _moe_experts(xs, ln, w1, w3, w2, blk_e, n_act, blk):
    p = xs.shape[0] // TOK_ROWS
    d, hdim = w1.shape[1], w1.shape[2]
    grid_spec = pltpu.PrefetchScalarGridSpec(
        num_scalar_prefetch=2,
        grid=(p // blk,),
        in_specs=[_tok_spec(blk, lambda i, be, na: (i, 0)),
                  pl.BlockSpec((1, d), lambda i, be, na: (0, 0)),
                  pl.BlockSpec((1, d, hdim), lambda i, be, na: (be[i], 0, 0)),
                  pl.BlockSpec((1, d, hdim), lambda i, be, na: (be[i], 0, 0)),
                  pl.BlockSpec((1, hdim, d), lambda i, be, na: (be[i], 0, 0))],
        out_specs=_tok_spec(blk, lambda i, be, na: (i, 0)),
    )
    return pl.pallas_call(
        _moe_expert_kernel,
        grid_spec=grid_spec,
        out_shape=jax.ShapeDtypeStruct(xs.shape, F32),
        compiler_params=_params(("arbitrary",)),
        name="moe_experts",
    )(blk_e, n_act, xs, ln.reshape(1, d), w1, w3, w2)


def _moe_layer(h, ln, w_gr, b_gr, w_er, b_er, w1, w3, w2):
    n = h.shape[0] // TOK_ROWS
    blk = MOE_BLOCK
    route, counts = _moe_router(h, ln, *_router_weights(w_gr, b_gr, w_er, b_er))
    blk_e, n_valid, n_act, sorted_pos = _moe_plan(route, counts, blk)
    p = n * MOE_TOPK + MOE_EXPERTS * blk
    xs = _moe_dispatch(h, n_valid, sorted_pos, p, blk)
    ys = _moe_experts(xs, ln, _mx(w1), _mx(w3), _mx(w2), blk_e, n_act, blk)
    return _moe_collect(ys, sorted_pos), route


def _moe_combine(h_ref, s0_ref, s1_ref, route):
    return _tok_load(h_ref) + route[:, 2:3] * _tok_load(s0_ref.at[0]) + route[:, 3:4] * _tok_load(s1_ref.at[0])


def _final_kernel(h_ref, s0_ref, s1_ref, route_ref, ln_ref, o_ref):
    o_ref[...] = _rms(_moe_combine(h_ref, s0_ref, s1_ref, route_ref[...]), ln_ref[...])


def _final(h, slots, route, ln, tm=512):
    n, d = route.shape[0], ln.shape[0]
    return pl.pallas_call(
        _final_kernel,
        grid=(n // tm,),
        in_specs=[_tok_spec(tm, lambda i: (i, 0)), _slot_spec(tm, 0, lambda i: i), _slot_spec(tm, 1, lambda i: i),
                  pl.BlockSpec((tm, LANES), lambda i: (i, 0)), _const_spec((1, d))],
        out_specs=pl.BlockSpec((tm, d), lambda i: (i, 0)),
        out_shape=jax.ShapeDtypeStruct((n, d), F32),
        compiler_params=_params(("arbitrary",)),
        name="final_norm",
    )(h, slots, slots, route, ln.reshape(1, d))


NSA_HEADS = 16
NSA_GROUPS = 4
NSA_HPG = NSA_HEADS // NSA_GROUPS
NSA_DH = D_MODEL // NSA_HEADS
NSA_PARTS = 6
CMP_BLOCK = 32
CMP_STRIDE = 16
SEL_BLOCK = 64
SEL_TOPK = 8
WINDOW = 512
N_BRANCH = 3
IMP_FORCE = 1e4
REL_BUCKETS = 32
REL_MAX_DIST = 128
NSA_TQ = 256
NSA_TK = 256
GATE_LANES = LANES
CMP_PARTS = 2


def _nsa_proj_kernel(h_ref, s0_ref, s1_ref, route_ref, kvln_ref, qln_ref, kvw_ref, qw_ref,
                     h_out, q_out, gate_out, kvc_out, kv_out):
    h1 = _moe_combine(h_ref, s0_ref, s1_ref, route_ref[0])
    h_out[0] = h1
    kv = _dot(_rms(h1, kvln_ref[...]), kvw_ref[...])
    for part in range(NSA_PARTS):
        for g in range(NSA_GROUPS):
            col = (part * NSA_GROUPS + g) * NSA_DH
            piece = kv[:, col:col + NSA_DH]
            if part < CMP_PARTS:
                kvc_out[part, 0, g] = piece
            else:
                kv_out[part - CMP_PARTS, 0, g] = piece.astype(kv_out.dtype)
    qg = _dot(_rms(h1, qln_ref[...]), qw_ref[...])
    q_out[0] = (qg[:, :D_MODEL] * (NSA_DH ** -0.5)).astype(q_out.dtype)
    gate_out[0] = qg[:, D_MODEL:]


def _gate_padded_weight(q_w):
    d = q_w.shape[0]
    n_gate = NSA_HPG * N_BRANCH
    gates = q_w[:, d:].reshape(d, NSA_GROUPS, n_gate)
    gates = jnp.pad(gates, ((0, 0), (0, 0), (0, GATE_LANES - n_gate))).reshape(d, NSA_GROUPS * GATE_LANES)
    return jnp.concatenate([q_w[:, :d], gates], axis=1)


def _nsa_proj(h, slots, route, kv_ln, q_ln, kv_w, q_w, b, t, tm=256):
    d = kv_ln.shape[0]
    blk = lambda w: pl.BlockSpec((1, tm, w), lambda i, j: (i, j, 0))
    per_b = t // tm
    tok_block = lambda i, j: i * per_b + j
    gw = NSA_GROUPS * GATE_LANES
    qw = _gate_padded_weight(q_w)
    kv_spec = lambda parts: pl.BlockSpec((parts, 1, NSA_GROUPS, tm, NSA_DH), lambda i, j: (0, i, 0, j, 0))
    kv_shape = lambda parts: (parts, b, NSA_GROUPS, t, NSA_DH)
    return pl.pallas_call(
        _nsa_proj_kernel,
        grid=(b, per_b),
        in_specs=[_tok_spec(tm, lambda i, j: (tok_block(i, j), 0)), _slot_spec(tm, 0, tok_block),
                  _slot_spec(tm, 1, tok_block), blk(LANES), _const_spec((1, d)), _const_spec((1, d)),
                  _const_spec(kv_w.shape), _const_spec(qw.shape)],
        out_specs=[blk(d), blk(d), blk(gw), kv_spec(CMP_PARTS), kv_spec(NSA_PARTS - CMP_PARTS)],
        out_shape=[jax.ShapeDtypeStruct((b, t, d), F32), jax.ShapeDtypeStruct((b, t, d), ACT_DTYPE),
                   jax.ShapeDtypeStruct((b, t, gw), F32), jax.ShapeDtypeStruct(kv_shape(CMP_PARTS), F32),
                   jax.ShapeDtypeStruct(kv_shape(NSA_PARTS - CMP_PARTS), ACT_DTYPE)],
        compiler_params=_params(("arbitrary", "arbitrary")),
        name="nsa_proj",
    )(h, slots, slots, route.reshape(b, t, LANES),
      kv_ln.reshape(1, d), q_ln.reshape(1, d), _mx(kv_w), _mx(qw))


def _gelu_tanh(x):
    return 0.5 * x * (1.0 + jnp.tanh(math.sqrt(2.0 / math.pi) * (x + 0.044715 * (x * x * x))))


def _compress_kernel(k_ref, v_ref, pek_ref, pev_ref, kw1_ref, kw2_ref, vw1_ref, vw2_ref, kc_out, vc_out):
    dh = k_ref.shape[-1]
    n_chunk = k_ref.shape[3] // CMP_STRIDE

    def one(z_ref, pe_ref, w1_ref, w2_ref, out):
        first = jnp.zeros((n_chunk, w1_ref.shape[1]), F32)
        second = first
        for l in range(CMP_STRIDE):
            z = z_ref[0, 0, 0, pl.ds(l, n_chunk, stride=CMP_STRIDE), :]
            lo, hi = l, CMP_STRIDE + l
            first = first + _dot(z + pe_ref[0, lo:lo + 1, :], w1_ref[dh * lo:dh * (lo + 1), :])
            second = second + _dot(z + pe_ref[0, hi:hi + 1, :], w1_ref[dh * hi:dh * (hi + 1), :])
        hid = first + pltpu.roll(second, n_chunk - 1, axis=0)
        out[0, 0] = _dot(_gelu_tanh(hid), w2_ref[...]).astype(out.dtype)

    one(k_ref, pek_ref, kw1_ref, kw2_ref, kc_out)
    one(v_ref, pev_ref, vw1_ref, vw2_ref, vc_out)


def _compress(kvc, pe_k, pe_v, k_w1, k_w2, v_w1, v_w2):
    _, b, g, t, dh = kvc.shape
    n_chunk = t // CMP_STRIDE
    part = lambda idx: pl.BlockSpec((1, 1, 1, t, dh), lambda i, j: (idx, i, j, 0, 0))
    pe_spec = pl.BlockSpec((1, CMP_BLOCK, dh), lambda i, j: (j, 0, 0))
    out_spec = pl.BlockSpec((1, 1, n_chunk, dh), lambda i, j: (i, j, 0, 0))
    out = jax.ShapeDtypeStruct((b, g, n_chunk, dh), ACT_DTYPE)
    return pl.pallas_call(
        _compress_kernel,
        grid=(b, g),
        in_specs=[part(0), part(1), pe_spec, pe_spec, _const_spec(k_w1.shape), _const_spec(k_w2.shape),
                  _const_spec(v_w1.shape), _const_spec(v_w2.shape)],
        out_specs=[out_spec, out_spec],
        out_shape=[out, out],
        compiler_params=_params(("arbitrary", "arbitrary")),
        name="nsa_compress",
    )(kvc, kvc, pe_k.astype(F32), pe_v.astype(F32), _mx(k_w1), _mx(k_w2), _mx(v_w1), _mx(v_w2))


def _rel_bucket(dist):
    n = jnp.maximum(dist, 0)
    max_exact = REL_BUCKETS // 2
    nf = jnp.maximum(n, 1).astype(F32)
    large = max_exact + (jnp.log(nf / max_exact) / math.log(REL_MAX_DIST / max_exact)
                         * (REL_BUCKETS - max_exact)).astype(jnp.int32)
    return jnp.where(n < max_exact, n, jnp.minimum(large, REL_BUCKETS - 1))


def _toeplitz(vals, rows, cols):
    span = rows + cols
    z = jnp.concatenate([vals, vals[..., :1]], axis=-1)
    flat = jnp.tile(z, rows)[..., :rows * (span - 1)]
    skew = flat.reshape(vals.shape[:-1] + (rows, span - 1))
    return skew[..., rows - 1:rows - 1 + cols]


def _nsa_bias_tiles(rel_bias, t):
    tq, tk = NSA_TQ, NSA_TK
    table = rel_bias.astype(F32).reshape(REL_BUCKETS, NSA_GROUPS, NSA_HPG).transpose(1, 2, 0)
    by_dist = table[:, :, _rel_bucket(jnp.arange(REL_MAX_DIST + 1))]
    far = by_dist[:, :, REL_MAX_DIST:]

    def band(dist, valid, shift=None):
        b = by_dist[:, :, np.clip(dist, 0, REL_MAX_DIST)]
        if shift is not None:
            b = b - shift
        return jnp.where(jnp.asarray(valid), b, NEG)

    u = lambda rows, cols: np.arange(rows + cols - 1) - (rows - 1)
    wk = WINDOW + tq
    win = []
    for v in range(WINDOW // tq + 1):
        dist = v * tq - max(v * tq - WINDOW, 0) - u(tq, wk)
        win.append(_toeplitz(band(dist, (dist >= 0) & (dist < WINDOW)), tq, wk))
    bias_win = jnp.stack(win, axis=1)
    n_near = (REL_MAX_DIST + tk - 1) // tq + 1
    sel = []
    for e in range(n_near):
        dist = e * tq - u(tq, tk)
        sel.append(_toeplitz(band(dist, dist >= 0, far), tq, tk))
    sel.append(jnp.zeros_like(sel[0]))
    bias_sel = jnp.stack(sel, axis=1)
    n_cmp = t // CMP_STRIDE
    per_b = []
    for b in range(CMP_STRIDE):
        dist = b - (CMP_BLOCK - 1) - CMP_STRIDE * u(n_cmp, n_cmp)
        per_b.append(_toeplitz(band(dist, dist >= 0), n_cmp, n_cmp))
    bias_cmp = jnp.stack(per_b, axis=3).reshape(NSA_GROUPS, NSA_HPG, t, n_cmp)
    real = np.arange(n_cmp) < (t - CMP_BLOCK) // CMP_STRIDE + 1
    bias_cmp = jnp.where(jnp.asarray(real), bias_cmp, NEG).reshape(NSA_GROUPS, NSA_HPG, t // tq, tq, n_cmp)
    return bias_cmp, bias_win, bias_sel


def _nsa_constants(t):
    n_sel = t // SEL_BLOCK
    n_cmp = t // CMP_STRIDE
    cmp_start = np.arange(n_cmp) * CMP_STRIDE
    sel_start = np.arange(n_sel) * SEL_BLOCK
    overlap = ((cmp_start[:, None] < sel_start[None, :] + SEL_BLOCK)
               & (cmp_start[:, None] + CMP_BLOCK > sel_start[None, :]))
    overlap[(t - CMP_BLOCK) // CMP_STRIDE + 1:, :] = False
    per = NSA_TK // SEL_BLOCK
    expand = np.zeros((t // NSA_TK, LANES, NSA_TK), np.float32)
    for c in range(t // NSA_TK):
        for s in range(NSA_TK):
            expand[c, per * c + s // SEL_BLOCK, s] = 1.0
    return jnp.asarray(overlap.T, MXU_DTYPE), jnp.asarray(expand, MXU_DTYPE)


def _nsa_attn_kernel(q_ref, gate_ref, kc_ref, vc_ref, ks_ref, vs_ref, kw_ref, vw_ref,
                     bcmp_ref, bwin_ref, bsel_ref, ovl_ref, exp_ref, o_ref):
    tq, tk, dh = NSA_TQ, NSA_TK, NSA_DH
    i = pl.program_id(2)
    t0 = i * tq
    q = q_ref[0]
    heads = range(NSA_HPG)
    qh = [q[:, dh * h:dh * (h + 1)] for h in heads]

    k_cmp, v_cmp = kc_ref[0, 0], vc_ref[0, 0]
    lc = [_dot_nt(qh[h], k_cmp) + bcmp_ref[0, h, 0] for h in heads]
    mc = [jnp.max(x, axis=-1, keepdims=True) for x in lc]
    pcu = [jnp.where(mc[h] > 0.5 * NEG, jnp.exp(lc[h] - mc[h]), 0.0) for h in heads]
    pc = [x / jnp.maximum(jnp.sum(x, axis=-1, keepdims=True), 1e-30) for x in pcu]
    oc = [_dot(x, v_cmp) for x in pc]

    pcsum = (pc[0] + pc[1]) + (pc[2] + pc[3])
    hi = pcsum.astype(MXU_DTYPE)
    mid = (pcsum - hi.astype(F32))
    lo = (mid - mid.astype(MXU_DTYPE).astype(F32)).astype(MXU_DTYPE)
    ovl = ovl_ref[...]
    imp = (lax.dot_general(ovl, hi, (((1,), (1,)), ((), ())), preferred_element_type=F32)
           + lax.dot_general(ovl, mid.astype(MXU_DTYPE), (((1,), (1,)), ((), ())), preferred_element_type=F32)
           + lax.dot_general(ovl, lo, (((1,), (1,)), ((), ())), preferred_element_type=F32))
    n_sel = imp.shape[0]
    jj = lax.broadcasted_iota(jnp.int32, (n_sel, 1), 0)
    blk_q = lax.shift_right_logical(t0 + lax.broadcasted_iota(jnp.int32, (1, tq), 1),
                                    int(math.log2(SEL_BLOCK)))
    forced = (jj == 0) | (jj == blk_q) | (jj == blk_q - 1)
    score = jnp.where(forced, IMP_FORCE, jnp.where(jj <= blk_q, imp, -1.0))
    rank = jnp.zeros_like(score)
    for j in range(n_sel):
        row = score[j:j + 1, :]
        tie = (jj > j).astype(F32)
        rank = rank + jnp.where(row > score, 1.0, jnp.where(row == score, tie, 0.0))
    sel_t = (rank < SEL_TOPK).astype(F32)
    sel = jnp.concatenate([sel_t, jnp.zeros((LANES - n_sel, tq), F32)], axis=0).T.astype(MXU_DTYPE)

    def sel_step(c, carry):
        ms, ls, accs = carry
        start = pl.multiple_of(c * tk, tk)
        k_c = ks_ref[0, 0, 0, pl.ds(start, tk), :]
        v_c = vs_ref[0, 0, 0, pl.ds(start, tk), :]
        keep = jnp.dot(sel, exp_ref[c], preferred_element_type=F32)
        neg = (keep - 1.0) * (-NEG)
        near = jnp.minimum((t0 - start) // tq, bsel_ref.shape[1] - 1)
        lg = [_dot_nt(qh[h], k_c) + bsel_ref[0, near, h] + neg for h in heads]
        m_new = [jnp.maximum(ms[h], jnp.max(lg[h], axis=-1, keepdims=True)) for h in heads]
        alpha = [jnp.exp(ms[h] - m_new[h]) for h in heads]
        p = [jnp.exp(lg[h] - m_new[h]) for h in heads]
        l_new = [alpha[h] * ls[h] + jnp.sum(p[h], axis=-1, keepdims=True) for h in heads]
        acc_new = [alpha[h] * accs[h] + _dot(p[h], v_c) for h in heads]
        return tuple(m_new), tuple(l_new), tuple(acc_new)

    init = (tuple(jnp.full((tq, 1), NEG, F32) for _ in heads), tuple(jnp.zeros((tq, 1), F32) for _ in heads),
            tuple(jnp.zeros((tq, dh), F32) for _ in heads))
    _, l_sel, acc_sel = lax.fori_loop(0, (t0 + tq + tk - 1) // tk, sel_step, init)

    w0 = pl.multiple_of(jnp.maximum(t0 - WINDOW, 0), tq)
    wk = WINDOW + tq
    k_w = kw_ref[0, 0, 0, pl.ds(w0, wk), :]
    v_w = vw_ref[0, 0, 0, pl.ds(w0, wk), :]
    lw = [_dot_nt(qh[h], k_w) + bwin_ref[0, 0, h] for h in heads]
    pw = [jnp.exp(x - jnp.max(x, axis=-1, keepdims=True)) for x in lw]
    ow = [_dot(x, v_w) / jnp.sum(x, axis=-1, keepdims=True) for x in pw]

    gates = jax.nn.sigmoid(gate_ref[0])
    outs = []
    for h in heads:
        g = [gates[:, N_BRANCH * h + br:N_BRANCH * h + br + 1] for br in range(N_BRANCH)]
        outs.append(g[0] * oc[h] + g[1] * (acc_sel[h] / l_sel[h]) + g[2] * ow[h])
    o_ref[0] = jnp.concatenate(outs, axis=1).astype(o_ref.dtype)


def _nsa_attention(q, gate_logits, kv, k_cmp, v_cmp, rel_bias):
    b, t, d = q.shape
    tq, tk, dh = NSA_TQ, NSA_TK, NSA_DH
    bias_cmp, bias_win, bias_sel = _nsa_bias_tiles(rel_bias, t)
    ovl, expand = _nsa_constants(t)
    n_cmp = k_cmp.shape[2]
    n_win = bias_win.shape[1]
    part = lambda idx: pl.BlockSpec((1, 1, 1, t, dh), lambda g, bi, i: (idx, bi, g, 0, 0))
    cmp_spec = pl.BlockSpec((1, 1, n_cmp, dh), lambda g, bi, i: (bi, g, 0, 0))
    return pl.pallas_call(
        _nsa_attn_kernel,
        grid=(NSA_GROUPS, b, t // tq),
        in_specs=[pl.BlockSpec((1, tq, NSA_HPG * dh), lambda g, bi, i: (bi, i, g)),
                  pl.BlockSpec((1, tq, GATE_LANES), lambda g, bi, i: (bi, i, g)),
                  cmp_spec, cmp_spec, part(0), part(1), part(2), part(3),
                  pl.BlockSpec((1, NSA_HPG, 1, tq, n_cmp), lambda g, bi, i: (g, 0, i, 0, 0)),
                  pl.BlockSpec((1, 1, NSA_HPG, tq, WINDOW + tq),
                               lambda g, bi, i: (g, jnp.minimum(i, n_win - 1), 0, 0, 0)),
                  pl.BlockSpec((1, bias_sel.shape[1], NSA_HPG, tq, tk), lambda g, bi, i: (g, 0, 0, 0, 0)),
                  _const_spec(ovl.shape), _const_spec(expand.shape)],
        out_specs=pl.BlockSpec((1, tq, NSA_HPG * dh), lambda g, bi, i: (bi, i, g)),
        out_shape=jax.ShapeDtypeStruct((b, t, d), ACT_DTYPE),
        compiler_params=_params(("arbitrary", "arbitrary", "arbitrary")),
        name="nsa_attention",
    )(q, gate_logits, k_cmp, v_cmp, kv, kv, kv, kv, bias_cmp, bias_win, bias_sel, ovl, expand)


def _nsa_block(h, slots, route, kv_ln, kv_w, cmp_pe_k, cmp_pe_v, cmp_k_w1, cmp_k_w2, cmp_v_w1, cmp_v_w2,
               rel_bias, b_ln, b_wqg, b_wo, b, t):
    n, d = b * t, kv_ln.shape[0]
    h1, q, gate_logits, kvc, kv = _nsa_proj(h, slots, route, kv_ln, b_ln, kv_w, b_wqg, b, t)
    k_cmp, v_cmp = _compress(kvc, cmp_pe_k, cmp_pe_v, cmp_k_w1, cmp_k_w2, cmp_v_w1, cmp_v_w2)
    o = _nsa_attention(q, gate_logits, kv, k_cmp, v_cmp, rel_bias)
    return _matmul_res(o.reshape(n, d), _mx(b_wo), h1.reshape(n, d))


def kernel(x, a_ln, a_mu, a_wr, a_wk, a_wv, a_wo, a_w0, a_w1, a_w2, a_a0, a_a1, a_a2, a_g1, a_g2, a_kk, a_ka, a_rk, a_lnx_w, a_lnx_b, kv_ln, kv_w, cmp_pe_k, cmp_pe_v, cmp_k_w1, cmp_k_w2, cmp_v_w1, cmp_v_w2, rel_bias, b_ln, b_wqg, b_wo, m_ln, m_wg, m_bg, m_we, m_be, m_w1, m_w3, m_w2, final_ln):
    b, t, d = x.shape
    h = _rwkv_block(x, a_ln[0], a_mu[0], a_wr[0], a_wk[0], a_wv[0], a_wo[0], a_w0[0], a_w1[0], a_w2[0],
                    a_a0[0], a_a1[0], a_a2[0], a_g1[0], a_g2[0], a_kk[0], a_ka[0], a_rk[0],
                    a_lnx_w[0], a_lnx_b[0])
    moe = lambda hh, l: _moe_layer(hh, m_ln[l], m_wg[l], m_bg[l], m_we[l], m_be[l], m_w1[l], m_w3[l], m_w2[l])
    slots, route = moe(h, 0)
    h = _nsa_block(h, slots, route, kv_ln, kv_w, cmp_pe_k, cmp_pe_v, cmp_k_w1, cmp_k_w2, cmp_v_w1, cmp_v_w2,
                   rel_bias, b_ln[0], b_wqg[0], b_wo[0], b, t)
    slots, route = moe(h, 1)
    return _final(h, slots, route, final_ln).reshape(b, t, d)
```

```python
import functools
import math

import numpy as np
import jax
import jax.numpy as jnp
from jax import lax
from jax.experimental import pallas as pl
from jax.experimental.pallas import tpu as pltpu

F32 = jnp.float32
MXU_DTYPE = jnp.bfloat16
ACT_DTYPE = jnp.bfloat16

D_MODEL = 1024
NORM_EPS = 1e-6
RWKV_HEAD = 64
RWKV_HEADS = D_MODEL // RWKV_HEAD
RWKV_GN_EPS = 64e-5
RWKV_CHUNK = 64
LANES = 128
MXU_TILE = 256
VMEM_LIMIT = 56 * 1024 * 1024


def _mx(x):
    return x.astype(MXU_DTYPE)


def _dot(a, b):
    return jnp.dot(_mx(a), _mx(b), preferred_element_type=F32)


def _dot_nt(a, b):
    return lax.dot_general(_mx(a), _mx(b), (((1,), (1,)), ((), ())), preferred_element_type=F32)


def _split2(x):
    hi = x.astype(MXU_DTYPE)
    lo = (x - hi.astype(F32)).astype(MXU_DTYPE)
    return hi, lo


def _dot_hl(x, w):
    hi, lo = _split2(x)
    return jnp.dot(hi, w, preferred_element_type=F32) + jnp.dot(lo, w, preferred_element_type=F32)


def _rms(x, g):
    return x * lax.rsqrt(jnp.mean(x * x, axis=-1, keepdims=True) + NORM_EPS) * g


TOK_ROWS = 8


def _tok_load(ref):
    tm = ref.shape[0] // TOK_ROWS
    return jnp.concatenate([ref[pl.ds(s, tm, stride=TOK_ROWS), :] for s in range(TOK_ROWS)], axis=1)


def _tok_store(ref, val):
    tm = val.shape[0]
    for s in range(TOK_ROWS):
        ref[pl.ds(s, tm, stride=TOK_ROWS), :] = val[:, LANES * s:LANES * (s + 1)]


def _tok_spec(tm, index_map):
    return pl.BlockSpec((TOK_ROWS * tm, LANES), index_map)


def _slot_spec(tm, k, block_index):
    return pl.BlockSpec((1, TOK_ROWS * tm, LANES), lambda *idx: (k, block_index(*idx), 0))


def _const_spec(shape):
    nd = len(shape)
    return pl.BlockSpec(shape, lambda *_: (0,) * nd)


def _params(sem):
    return pltpu.CompilerParams(dimension_semantics=sem, vmem_limit_bytes=VMEM_LIMIT)


def _rwkv_pre_kernel(x_ref, ln_ref, mu_ref, wr_ref, wk_ref, wv_ref, w1_ref, a1_ref, g1_ref,
                     w2_ref, a2_ref, g2_ref, vec_ref,
                     r_out, k_out, v_out, kk_out, g_out, ld_out, a_out, prev_ref):
    tm = x_ref.shape[1]
    hn = _rms(x_ref[0], ln_ref[...])

    @pl.when(pl.program_id(1) == 0)
    def _():
        prev_ref[...] = jnp.zeros_like(prev_ref)

    rows = lax.broadcasted_iota(jnp.int32, (tm, 1), 0)
    shifted = jnp.where(rows == 0, prev_ref[0:1, :], pltpu.roll(hn, 1, axis=0))
    prev_ref[0:1, :] = hn[tm - 1:tm, :]
    xx = shifted - hn
    xr, xw, xk, xv, xa, xg = (_mx(hn + xx * mu_ref[i:i + 1, :]) for i in range(6))
    w0, a0, kkp, kap = (vec_ref[i:i + 1, :] for i in range(4))
    r = jnp.dot(xr, wr_ref[...], preferred_element_type=F32)
    k = jnp.dot(xk, wk_ref[...], preferred_element_type=F32)
    v = jnp.dot(xv, wv_ref[...], preferred_element_type=F32)
    z = w0 + _dot(jnp.tanh(jnp.dot(xw, w1_ref[...], preferred_element_type=F32)), w2_ref[...])
    a = jax.nn.sigmoid(a0 + _dot(jnp.dot(xa, a1_ref[...], preferred_element_type=F32), a2_ref[...]))
    g = _dot(jax.nn.sigmoid(jnp.dot(xg, g1_ref[...], preferred_element_type=F32)), g2_ref[...])
    ld_out[0] = (-math.exp(-0.5)) * jax.nn.sigmoid(z)
    a_out[0] = a
    r_out[0] = r.astype(r_out.dtype)
    v_out[0] = v.astype(v_out.dtype)
    kk_out[0] = (k * kkp).astype(kk_out.dtype)
    k_out[0] = (k * (1.0 + (a - 1.0) * kap)).astype(k_out.dtype)
    g_out[0] = g.astype(g_out.dtype)


def _rwkv_pre(x, ln, mu, wr, wk, wv, w1, a1, g1, w2, a2, g2, vecs, tm=256):
    b, t, d = x.shape
    act = jax.ShapeDtypeStruct((b, t, d), ACT_DTYPE)
    f32o = jax.ShapeDtypeStruct((b, t, d), F32)
    blk = pl.BlockSpec((1, tm, d), lambda i, j: (i, j, 0))
    consts = [ln, mu, wr, wk, wv, w1, a1, g1, w2, a2, g2, vecs]
    return pl.pallas_call(
        _rwkv_pre_kernel,
        grid=(b, t // tm),
        in_specs=[blk] + [_const_spec(c.shape) for c in consts],
        out_specs=[blk] * 7,
        out_shape=[act, act, act, act, act, f32o, f32o],
        scratch_shapes=[pltpu.VMEM((8, d), F32)],
        compiler_params=_params(("arbitrary", "arbitrary")),
        name="rwkv_pre",
    )(x, *consts)


def _stack4(x):
    return jnp.concatenate([x[:, MXU_TILE * q:MXU_TILE * (q + 1)] for q in range(4)], axis=0)


def _unstack4(y):
    c = y.shape[0] // 4
    return jnp.concatenate([y[c * q:c * (q + 1), :] for q in range(4)], axis=1)


def _rwkv_scan_kernel(r_ref, k_ref, v_ref, kk_ref, g_ref, ld_ref, a_ref,
                      ltri_ref, smask_ref, eye_ref, j_ref, rk_ref, lnw_ref, lnb_ref,
                      y_out, h_ref):
    c = RWKV_CHUNK

    @pl.when(pl.program_id(1) == 0)
    def _():
        h_ref[...] = jnp.zeros_like(h_ref)

    ld = ld_ref[0]
    r = r_ref[0].astype(F32)
    k = k_ref[0].astype(F32)
    v = v_ref[0].astype(F32)
    kk = kk_ref[0].astype(F32)
    jones = j_ref[...]
    ld_hi, ld_lo = _split2(ld)
    cum = (jnp.dot(ltri_ref[...], ld_hi, preferred_element_type=F32)
           + jnp.dot(ltri_ref[...], ld_lo, preferred_element_type=F32))
    cum_last = cum[c - 1:c, :]
    e_l = jnp.exp(cum)
    e_lm = jnp.exp(cum - ld)
    e_nl = jnp.exp(-cum)
    e_cl = jnp.exp(cum_last - cum)
    e_c = jnp.exp(cum_last)
    ss = _unstack4(_dot_hl(_stack4(kk * kk), jones))
    kkn = kk * lax.rsqrt(jnp.maximum(ss, 1e-24))
    a_vec = -kkn
    b_vec = kkn * a_ref[0]
    rt = r * e_l
    at = a_vec * e_lm
    kt = k * e_nl
    bt = b_vec * e_nl
    bh = b_vec * e_cl
    kh = k * e_cl

    lane = lax.broadcasted_iota(jnp.int32, (1, LANES), 1)
    m_l = (lane < RWKV_HEAD).astype(F32)
    m_r = 1.0 - m_l

    def st(xp):
        return jnp.concatenate([xp * m_l, xp * m_r], axis=0)

    def dup(xp):
        return jnp.concatenate([xp, xp], axis=0)

    smask = smask_ref[...] > 0.0
    eye = eye_ref[...]
    pairs = range(D_MODEL // LANES)
    sls = [slice(LANES * p, LANES * (p + 1)) for p in pairs]
    at_s = [st(at[:, sl]) for sl in sls]
    rt_s = [st(rt[:, sl]) for sl in sls]
    v_s = [st(v[:, sl]) for sl in sls]
    s = [jnp.where(smask,
                   _dot_nt(jnp.concatenate([at_s[p], rt_s[p]], axis=0),
                           jnp.concatenate([dup(bt[:, sls[p]]), dup(kt[:, sls[p]])], axis=0)),
                   0.0) for p in pairs]
    a_ab = [x[:2 * c, :2 * c] for x in s]
    a_r = [x[2 * c:, :] for x in s]
    g1 = [_dot(s[p][:2 * c, 2 * c:], v_s[p]) for p in pairs]
    t_m = [eye + x for x in a_ab]
    a_pow = [_dot(x, x) for x in a_ab]
    n = 2
    while n < c:
        if 2 * n < c:
            both = [_dot(a_pow[p], jnp.concatenate([a_pow[p], t_m[p]], axis=1)) for p in pairs]
            a_pow = [x[:, :2 * c] for x in both]
            t_m = [t_m[p] + both[p][:, 2 * c:] for p in pairs]
        else:
            t_m = [t_m[p] + _dot(a_pow[p], t_m[p]) for p in pairs]
        n *= 2
    tw = [_dot(t_m[p], jnp.concatenate([at_s[p], g1[p]], axis=1)) for p in pairs]
    zmat = [jnp.concatenate([tw[p], jnp.concatenate([jnp.zeros_like(v_s[p]), v_s[p]], axis=1)], axis=0)
            for p in pairs]
    upper = [_dot(a_r[p], zmat[p]) for p in pairs]
    bk_t = [jnp.concatenate([st(bh[:, sl]), st(kh[:, sl])], axis=0).T for sl in sls]
    lower = [_dot(bk_t[p], zmat[p]) for p in pairs]
    seq = [_dot(jnp.concatenate([upper[p][:, :LANES] + rt_s[p],
                                 lower[p][:, :LANES] + eye * e_c[:, sls[p]]], axis=0), h_ref[p])
           for p in pairs]
    ys = []
    for p in pairs:
        y_st = seq[p][:2 * c, :] + upper[p][:, LANES:]
        h_ref[p] = seq[p][2 * c:, :] + lower[p][:, LANES:]
        ys.append(y_st[:c, :] + y_st[c:, :])
    y = jnp.concatenate(ys, axis=1)

    inv_n = 1.0 / RWKV_HEAD
    y4 = _stack4(y)
    dlt = y4 - _dot_hl(y4, jones) * inv_n
    var = _dot_hl(dlt * dlt, jones) * inv_n
    yn = dlt * lax.rsqrt(var + RWKV_GN_EPS)
    bonus = _dot_hl(_stack4(r * k) * rk_ref[...], jones) * _stack4(v)
    out = (yn * lnw_ref[...] + lnb_ref[...] + bonus) * _stack4(g_ref[0].astype(F32))
    y_out[0] = _unstack4(out).astype(y_out.dtype)


def _scan_constants():
    c = RWKV_CHUNK
    ltri = np.tril(np.ones((c, c), np.float32))
    rho = np.arange(4 * c)
    r_type, r_head, r_t = rho // (2 * c), (rho % (2 * c)) // c, rho % c
    same = r_head[:, None] == r_head[None, :]
    strict = r_t[None, :] < r_t[:, None]
    incl = r_t[None, :] <= r_t[:, None]
    smask = same & np.where(r_type[:, None] == 0, strict, incl)
    eye = np.eye(LANES, dtype=np.float32)
    head = np.arange(MXU_TILE) // RWKV_HEAD
    jones = (head[:, None] == head[None, :]).astype(np.float32)
    return (jnp.asarray(ltri, MXU_DTYPE), jnp.asarray(smask, F32), jnp.asarray(eye, F32),
            jnp.asarray(jones, MXU_DTYPE))


def _stack_param(p):
    return jnp.repeat(p.reshape(4, MXU_TILE).astype(F32), RWKV_CHUNK, axis=0)


def _rwkv_scan(r, k, v, kk, g, ld, a, rk, lnw, lnb):
    b, t, d = r.shape
    c = RWKV_CHUNK
    blk = pl.BlockSpec((1, c, d), lambda i, j: (i, j, 0))
    consts = list(_scan_constants()) + [_stack_param(rk.reshape(-1)), _stack_param(lnw), _stack_param(lnb)]
    return pl.pallas_call(
        _rwkv_scan_kernel,
        grid=(b, t // c),
        in_specs=[blk] * 7 + [_const_spec(x.shape) for x in consts],
        out_specs=blk,
        out_shape=jax.ShapeDtypeStruct((b, t, d), ACT_DTYPE),
        scratch_shapes=[pltpu.VMEM((d // LANES, LANES, LANES), F32)],
        compiler_params=_params(("arbitrary", "arbitrary")),
        name="rwkv_scan",
    )(r, k, v, kk, g, ld, a, *consts)


def _matmul_res_kernel(a_ref, w_ref, res_ref, o_ref):
    _tok_store(o_ref, res_ref[...] + jnp.dot(a_ref[...], w_ref[...], preferred_element_type=F32))


def _matmul_res(a, w, res, tm=512):
    n, kdim = a.shape
    d = w.shape[1]
    return pl.pallas_call(
        _matmul_res_kernel,
        grid=(n // tm,),
        in_specs=[pl.BlockSpec((tm, kdim), lambda i: (i, 0)), _const_spec(w.shape),
                  pl.BlockSpec((tm, d), lambda i: (i, 0))],
        out_specs=_tok_spec(tm, lambda i: (i, 0)),
        out_shape=jax.ShapeDtypeStruct((TOK_ROWS * n, LANES), F32),
        compiler_params=_params(("arbitrary",)),
        name="matmul_res",
    )(a, w, res)


def _rwkv_block(x, a_ln, a_mu, a_wr, a_wk, a_wv, a_wo, a_w0, a_w1, a_w2, a_a0, a_a1, a_a2,
                a_g1, a_g2, a_kk, a_ka, a_rk, a_lnx_w, a_lnx_b):
    b, t, d = x.shape
    vecs = jnp.stack([a_w0, a_a0, a_kk, a_ka]).astype(F32)
    r, k, v, kk, g, ld, a = _rwkv_pre(
        x, a_ln.reshape(1, d), a_mu, _mx(a_wr), _mx(a_wk), _mx(a_wv), _mx(a_w1), _mx(a_a1), _mx(a_g1),
        _mx(a_w2), _mx(a_a2), _mx(a_g2), vecs)
    y = _rwkv_scan(r, k, v, kk, g, ld, a, a_rk, a_lnx_w, a_lnx_b)
    return _matmul_res(y.reshape(b * t, d), _mx(a_wo), x.reshape(b * t, d))


MOE_GROUPS = 4
MOE_EPG = 8
MOE_EXPERTS = MOE_GROUPS * MOE_EPG
MOE_TOPK = 2
MOE_BLOCK = 256
NEG = -1e30


def _route(logits):
    lane = lax.broadcasted_iota(jnp.int32, logits.shape, 1)
    is_grp = lane < MOE_GROUPS
    gl = jnp.where(is_grp, logits, NEG)
    ge = jnp.exp(gl - jnp.max(gl, axis=-1, keepdims=True))
    gp = ge / jnp.sum(ge, axis=-1, keepdims=True)
    grp_w = jnp.max(gp, axis=-1, keepdims=True)
    grp = jnp.min(jnp.where(is_grp & (gp == grp_w), lane, LANES), axis=-1, keepdims=True)
    lo = MOE_GROUPS + MOE_EPG * grp
    in_grp = (lane >= lo) & (lane < lo + MOE_EPG)
    el = jnp.where(in_grp, logits, NEG)
    ee = jnp.exp(el - jnp.max(el, axis=-1, keepdims=True))
    ep = ee / jnp.sum(ee, axis=-1, keepdims=True)
    p1 = jnp.max(jnp.where(in_grp, ep, -1.0), axis=-1, keepdims=True)
    i1 = jnp.min(jnp.where(in_grp & (ep == p1), lane, LANES), axis=-1, keepdims=True)
    rest = in_grp & (lane != i1)
    p2 = jnp.max(jnp.where(rest, ep, -1.0), axis=-1, keepdims=True)
    i2 = jnp.min(jnp.where(rest & (ep == p2), lane, LANES), axis=-1, keepdims=True)
    scale = grp_w / (p1 + p2)
    return i1 - MOE_GROUPS, i2 - MOE_GROUPS, p1 * scale, p2 * scale


def _router_logits(xf, w_hi_ref, w_lo_ref, bias_ref):
    hi, lo = _split2(xf)
    logits = (jnp.dot(hi, w_hi_ref[...], preferred_element_type=F32)
              + jnp.dot(lo, w_hi_ref[...], preferred_element_type=F32)
              + jnp.dot(hi, w_lo_ref[...], preferred_element_type=F32))
    return logits + bias_ref[...]


def _moe_router_kernel(h_ref, ln_ref, w_hi_ref, w_lo_ref, bias_ref, tri_ref, route_out, count_out, cnt_ref):
    @pl.when(pl.program_id(0) == 0)
    def _():
        cnt_ref[...] = jnp.zeros_like(cnt_ref)

    xf = _rms(_tok_load(h_ref), ln_ref[...])
    e1, e2, w1, w2 = _route(_router_logits(xf, w_hi_ref, w_lo_ref, bias_ref))
    lane = lax.broadcasted_iota(jnp.int32, (xf.shape[0], LANES), 1)
    hit1, hit2 = lane == e1, lane == e2
    onehot = jnp.where(hit1, 1.0, jnp.where(hit2, 1.0, 0.0))
    before = jnp.dot(tri_ref[...], _mx(onehot), preferred_element_type=F32) + cnt_ref[...]
    r1 = jnp.sum(jnp.where(hit1, before, 0.0), axis=-1, keepdims=True)
    r2 = jnp.sum(jnp.where(hit2, before, 0.0), axis=-1, keepdims=True)
    cnt_ref[...] = cnt_ref[...] + jnp.sum(onehot, axis=0, keepdims=True)
    count_out[...] = cnt_ref[...]
    fields = (e1.astype(F32), e2.astype(F32), w1, w2, r1, r2)
    rec = jnp.zeros(lane.shape, F32)
    for idx, val in enumerate(fields):
        rec = jnp.where(lane == idx, val, rec)
    route_out[...] = rec


def _router_weights(w_gr, b_gr, w_er, b_er):
    d = w_gr.shape[0]
    w = jnp.zeros((d, LANES), F32).at[:, :MOE_GROUPS].set(w_gr).at[:, MOE_GROUPS:MOE_GROUPS + MOE_EXPERTS].set(w_er)
    bias = jnp.zeros((1, LANES), F32).at[0, :MOE_GROUPS].set(b_gr).at[0, MOE_GROUPS:MOE_GROUPS + MOE_EXPERTS].set(b_er)
    w_hi = w.astype(MXU_DTYPE)
    w_lo = (w - w_hi.astype(F32)).astype(MXU_DTYPE)
    return w_hi, w_lo, bias


def _moe_router(h, ln, w_hi, w_lo, bias, tm=512):
    n, d = h.shape[0] // TOK_ROWS, ln.shape[0]
    tri = jnp.asarray(np.tril(np.ones((tm, tm), np.float32), -1), MXU_DTYPE)
    return pl.pallas_call(
        _moe_router_kernel,
        grid=(n // tm,),
        in_specs=[_tok_spec(tm, lambda i: (i, 0)), _const_spec((1, d)), _const_spec(w_hi.shape),
                  _const_spec(w_lo.shape),
                  _const_spec(bias.shape), _const_spec(tri.shape)],
        out_specs=[pl.BlockSpec((tm, LANES), lambda i: (i, 0)), _const_spec((1, LANES))],
        out_shape=[jax.ShapeDtypeStruct((n, LANES), F32), jax.ShapeDtypeStruct((1, LANES), F32)],
        scratch_shapes=[pltpu.VMEM((1, LANES), F32)],
        compiler_params=_params(("arbitrary",)),
        name="moe_router",
    )(h, ln.reshape(1, d), w_hi, w_lo, bias, tri)


def _moe_plan(route, counts, blk):
    n = route.shape[0]
    counts = counts[0, :MOE_EXPERTS].astype(jnp.int32)
    pcounts = (counts + blk - 1) // blk * blk
    pend = jnp.cumsum(pcounts)
    pstart = (pend - pcounts).astype(jnp.int32)
    n_blk = n * MOE_TOPK // blk + MOE_EXPERTS
    blk_start = jnp.arange(n_blk, dtype=jnp.int32) * blk
    blk_e = jnp.sum((blk_start[:, None] >= pend[None, :]).astype(jnp.int32), axis=1)
    blk_e = jnp.minimum(blk_e, MOE_EXPERTS - 1)
    n_valid = jnp.clip((pstart + counts)[blk_e] - blk_start, 0, blk).astype(jnp.int32)
    n_act = (pend[-1] // blk).astype(jnp.int32).reshape(1)
    expert = route[:, 0:MOE_TOPK].astype(jnp.int32)
    onehot = expert[:, :, None] == jnp.arange(MOE_EXPERTS, dtype=jnp.int32)
    seg_start = jnp.sum(jnp.where(onehot, pstart, 0), axis=-1)
    sorted_pos = seg_start + route[:, 4:4 + MOE_TOPK].astype(jnp.int32)
    return blk_e, n_valid, n_act, sorted_pos


def _token_copies(pos_ref, tm, make_copy):
    group = 8

    def body(g, carry):
        t0 = g * group
        pos = [[pos_ref[0, 0, MOE_TOPK * (t0 + j) + k] for k in range(MOE_TOPK)] for j in range(group)]
        for j in range(group):
            for k in range(MOE_TOPK):
                make_copy(t0 + j, k, pos[j][k]).start()
        return carry

    lax.fori_loop(0, tm // group, body, 0)


def _tile(ref, tok):
    return ref.at[pl.ds(pl.multiple_of(tok * TOK_ROWS, TOK_ROWS), TOK_ROWS)]


def _moe_dispatch_kernel(n_valid_ref, pos_ref, src_ref, dst_hbm, sem, zbuf, zsem):
    tm = src_ref.shape[0] // TOK_ROWS

    @pl.when(pl.program_id(0) == 0)
    def _():
        rows = zbuf.shape[0]
        zbuf[...] = jnp.zeros_like(zbuf)

        def fill(j, wait):
            @pl.when(n_valid_ref[j] < rows // TOK_ROWS)
            def _():
                start = 0 if wait else pl.multiple_of(j * rows, rows)
                cp = pltpu.make_async_copy(zbuf, dst_hbm.at[pl.ds(start, rows)], zsem)
                cp.wait() if wait else cp.start()

        lax.fori_loop(0, n_valid_ref.shape[0], lambda j, c: (fill(j, False), c)[1], 0)
        lax.fori_loop(0, n_valid_ref.shape[0], lambda j, c: (fill(j, True), c)[1], 0)

    _token_copies(pos_ref, tm, lambda t, k, pos: pltpu.make_async_copy(_tile(src_ref, t), _tile(dst_hbm, pos), sem))
    for k in range(MOE_TOPK):
        pltpu.make_async_copy(src_ref, dst_hbm.at[pl.ds(0, src_ref.shape[0])], sem).wait()


def _moe_collect_kernel(pos_ref, src_hbm, dst_ref, sem):
    tm = dst_ref.shape[1] // TOK_ROWS
    _token_copies(pos_ref, tm,
                  lambda t, k, pos: pltpu.make_async_copy(_tile(src_hbm, pos), _tile(dst_ref.at[k], t), sem))
    for k in range(MOE_TOPK):
        pltpu.make_async_copy(src_hbm.at[pl.ds(0, dst_ref.shape[1])], dst_ref.at[k], sem).wait()


def _moe_dispatch(h, n_valid, sorted_pos, p, blk, tm=512):
    n = sorted_pos.shape[0]
    pos = sorted_pos.reshape(n // tm, 1, MOE_TOPK * tm)
    grid_spec = pltpu.PrefetchScalarGridSpec(
        num_scalar_prefetch=1,
        grid=(n // tm,),
        in_specs=[pl.BlockSpec((1, 1, MOE_TOPK * tm), lambda i, nv: (i, 0, 0), memory_space=pltpu.SMEM),
                  _tok_spec(tm, lambda i, nv: (i, 0))],
        out_specs=pl.BlockSpec(memory_space=pl.ANY),
        scratch_shapes=[pltpu.SemaphoreType.DMA, pltpu.VMEM((TOK_ROWS * blk, LANES), h.dtype),
                        pltpu.SemaphoreType.DMA],
    )
    return pl.pallas_call(
        _moe_dispatch_kernel,
        grid_spec=grid_spec,
        out_shape=jax.ShapeDtypeStruct((TOK_ROWS * p, LANES), h.dtype),
        compiler_params=_params(("arbitrary",)),
        name="moe_dispatch",
    )(n_valid, pos, h)


def _moe_collect(ys, sorted_pos, tm=512):
    n = sorted_pos.shape[0]
    pos = sorted_pos.reshape(n // tm, 1, MOE_TOPK * tm)
    return pl.pallas_call(
        _moe_collect_kernel,
        grid=(n // tm,),
        in_specs=[pl.BlockSpec((1, 1, MOE_TOPK * tm), lambda i: (i, 0, 0), memory_space=pltpu.SMEM),
                  pl.BlockSpec(memory_space=pl.ANY)],
        out_specs=pl.BlockSpec((MOE_TOPK, TOK_ROWS * tm, LANES), lambda i: (0, i, 0)),
        out_shape=jax.ShapeDtypeStruct((MOE_TOPK, TOK_ROWS * n, LANES), ys.dtype),
        scratch_shapes=[pltpu.SemaphoreType.DMA],
        compiler_params=_params(("arbitrary",)),
        name="moe_collect",
    )(pos, ys)


def _moe_expert_kernel(blk_e_ref, n_act_ref, x_ref, ln_ref, w1_ref, w3_ref, w2_ref, y_ref):
    del blk_e_ref
    i = pl.program_id(0)

    @pl.when(i < n_act_ref[0])
    def _():
        xn = _mx(_rms(_tok_load(x_ref), ln_ref[...]))
        h1 = jnp.dot(xn, w1_ref[0], preferred_element_type=F32)
        h3 = jnp.dot(xn, w3_ref[0], preferred_element_type=F32)
        _tok_store(y_ref, _dot(h1 * jax.nn.sigmoid(h1) * h3, w2_ref[0]))

    @pl.when(i >= n_act_ref[0])
    def _():
        y_ref[...] = jnp.zeros_like(y_ref)


def _moe_experts(xs, ln, w1, w3, w2, blk_e, n_act, blk):
    p = xs.shape[0] // TOK_ROWS
    d, hdim = w1.shape[1], w1.shape[2]
    grid_spec = pltpu.PrefetchScalarGridSpec(
        num_scalar_prefetch=2,
        grid=(p // blk,),
        in_specs=[_tok_spec(blk, lambda i, be, na: (i, 0)),
                  pl.BlockSpec((1, d), lambda i, be, na: (0, 0)),
                  pl.BlockSpec((1, d, hdim), lambda i, be, na: (be[i], 0, 0)),
                  pl.BlockSpec((1, d, hdim), lambda i, be, na: (be[i], 0, 0)),
                  pl.BlockSpec((1, hdim, d), lambda i, be, na: (be[i], 0, 0))],
        out_specs=_tok_spec(blk, lambda i, be, na: (i, 0)),
    )
    return pl.pallas_call(
        _moe_expert_kernel,
        grid_spec=grid_spec,
        out_shape=jax.ShapeDtypeStruct(xs.shape, F32),
        compiler_params=_params(("arbitrary",)),
        name="moe_experts",
    )(blk_e, n_act, xs, ln.reshape(1, d), w1, w3, w2)


def _moe_layer(h, ln, w_gr, b_gr, w_er, b_er, w1, w3, w2):
    n = h.shape[0] // TOK_ROWS
    blk = MOE_BLOCK
    route, counts = _moe_router(h, ln, *_router_weights(w_gr, b_gr, w_er, b_er))
    blk_e, n_valid, n_act, sorted_pos = _moe_plan(route, counts, blk)
    p = n * MOE_TOPK + MOE_EXPERTS * blk
    xs = _moe_dispatch(h, n_valid, sorted_pos, p, blk)
    ys = _moe_experts(xs, ln, _mx(w1), _mx(w3), _mx(w2), blk_e, n_act, blk)
    return _moe_collect(ys, sorted_pos), route


def _moe_combine(h_ref, s0_ref, s1_ref, route):
    return _tok_load(h_ref) + route[:, 2:3] * _tok_load(s0_ref.at[0]) + route[:, 3:4] * _tok_load(s1_ref.at[0])


def _final_kernel(h_ref, s0_ref, s1_ref, route_ref, ln_ref, o_ref):
    o_ref[...] = _rms(_moe_combine(h_ref, s0_ref, s1_ref, route_ref[...]), ln_ref[...])


def _final(h, slots, route, ln, tm=512):
    n, d = route.shape[0], ln.shape[0]
    return pl.pallas_call(
        _final_kernel,
        grid=(n // tm,),
        in_specs=[_tok_spec(tm, lambda i: (i, 0)), _slot_spec(tm, 0, lambda i: i), _slot_spec(tm, 1, lambda i: i),
                  pl.BlockSpec((tm, LANES), lambda i: (i, 0)), _const_spec((1, d))],
        out_specs=pl.BlockSpec((tm, d), lambda i: (i, 0)),
        out_shape=jax.ShapeDtypeStruct((n, d), F32),
        compiler_params=_params(("arbitrary",)),
        name="final_norm",
    )(h, slots, slots, route, ln.reshape(1, d))


NSA_HEADS = 16
NSA_GROUPS = 4
NSA_HPG = NSA_HEADS // NSA_GROUPS
NSA_DH = D_MODEL // NSA_HEADS
NSA_PARTS = 6
CMP_BLOCK = 32
CMP_STRIDE = 16
SEL_BLOCK = 64
SEL_TOPK = 8
WINDOW = 512
N_BRANCH = 3
IMP_FORCE = 1e4
REL_BUCKETS = 32
REL_MAX_DIST = 128
LOG2E = math.log2(math.e)
NSA_TQ = 256
NSA_TK = 256
NSA_ROW_BLOCK = 32
GATE_LANES = LANES
CMP_PARTS = 2


def _nsa_proj_kernel(h_ref, s0_ref, s1_ref, route_ref, kvln_ref, qln_ref, kvw_ref, qw_ref,
                     h_out, q_out, gate_out, kvc_out, kv_out):
    h1 = _moe_combine(h_ref, s0_ref, s1_ref, route_ref[0])
    h_out[0] = h1
    kv = _dot(_rms(h1, kvln_ref[...]), kvw_ref[...])
    for part in range(NSA_PARTS):
        for g in range(NSA_GROUPS):
            col = (part * NSA_GROUPS + g) * NSA_DH
            piece = kv[:, col:col + NSA_DH]
            if part < CMP_PARTS:
                kvc_out[part, 0, g] = piece
            else:
                kv_out[part - CMP_PARTS, 0, g] = piece.astype(kv_out.dtype)
    qg = _dot(_rms(h1, qln_ref[...]), qw_ref[...])
    q_out[0] = (qg[:, :D_MODEL] * (NSA_DH ** -0.5 * LOG2E)).astype(q_out.dtype)
    gate_out[0] = qg[:, D_MODEL:]


def _gate_padded_weight(q_w):
    d = q_w.shape[0]
    n_gate = NSA_HPG * N_BRANCH
    gates = q_w[:, d:].reshape(d, NSA_GROUPS, n_gate)
    gates = jnp.pad(gates, ((0, 0), (0, 0), (0, GATE_LANES - n_gate))).reshape(d, NSA_GROUPS * GATE_LANES)
    return jnp.concatenate([q_w[:, :d], gates], axis=1)


def _nsa_proj(h, slots, route, kv_ln, q_ln, kv_w, q_w, b, t, tm=256):
    d = kv_ln.shape[0]
    blk = lambda w: pl.BlockSpec((1, tm, w), lambda i, j: (i, j, 0))
    per_b = t // tm
    tok_block = lambda i, j: i * per_b + j
    gw = NSA_GROUPS * GATE_LANES
    qw = _gate_padded_weight(q_w)
    kv_spec = lambda parts: pl.BlockSpec((parts, 1, NSA_GROUPS, tm, NSA_DH), lambda i, j: (0, i, 0, j, 0))
    kv_shape = lambda parts: (parts, b, NSA_GROUPS, t, NSA_DH)
    return pl.pallas_call(
        _nsa_proj_kernel,
        grid=(b, per_b),
        in_specs=[_tok_spec(tm, lambda i, j: (tok_block(i, j), 0)), _slot_spec(tm, 0, tok_block),
                  _slot_spec(tm, 1, tok_block), blk(LANES), _const_spec((1, d)), _const_spec((1, d)),
                  _const_spec(kv_w.shape), _const_spec(qw.shape)],
        out_specs=[blk(d), blk(d), blk(gw), kv_spec(CMP_PARTS), kv_spec(NSA_PARTS - CMP_PARTS)],
        out_shape=[jax.ShapeDtypeStruct((b, t, d), F32), jax.ShapeDtypeStruct((b, t, d), ACT_DTYPE),
                   jax.ShapeDtypeStruct((b, t, gw), F32), jax.ShapeDtypeStruct(kv_shape(CMP_PARTS), F32),
                   jax.ShapeDtypeStruct(kv_shape(NSA_PARTS - CMP_PARTS), ACT_DTYPE)],
        compiler_params=_params(("arbitrary", "arbitrary")),
        name="nsa_proj",
    )(h, slots, slots, route.reshape(b, t, LANES),
      kv_ln.reshape(1, d), q_ln.reshape(1, d), _mx(kv_w), _mx(qw))


def _gelu_tanh(x):
    return 0.5 * x * (1.0 + jnp.tanh(math.sqrt(2.0 / math.pi) * (x + 0.044715 * (x * x * x))))


def _compress_kernel(k_ref, v_ref, pek_ref, pev_ref, kw1_ref, kw2_ref, vw1_ref, vw2_ref, kc_out, vc_out):
    dh = k_ref.shape[-1]
    n_chunk = k_ref.shape[3] // CMP_STRIDE

    def one(z_ref, pe_ref, w1_ref, w2_ref, out):
        first = jnp.zeros((n_chunk, w1_ref.shape[1]), F32)
        second = first
        for l in range(CMP_STRIDE):
            z = z_ref[0, 0, 0, pl.ds(l, n_chunk, stride=CMP_STRIDE), :]
            lo, hi = l, CMP_STRIDE + l
            first = first + _dot(z + pe_ref[0, lo:lo + 1, :], w1_ref[dh * lo:dh * (lo + 1), :])
            second = second + _dot(z + pe_ref[0, hi:hi + 1, :], w1_ref[dh * hi:dh * (hi + 1), :])
        hid = first + pltpu.roll(second, n_chunk - 1, axis=0)
        out[0, 0] = _dot(_gelu_tanh(hid), w2_ref[...]).astype(out.dtype)

    one(k_ref, pek_ref, kw1_ref, kw2_ref, kc_out)
    one(v_ref, pev_ref, vw1_ref, vw2_ref, vc_out)


def _compress(kvc, pe_k, pe_v, k_w1, k_w2, v_w1, v_w2):
    _, b, g, t, dh = kvc.shape
    n_chunk = t // CMP_STRIDE
    part = lambda idx: pl.BlockSpec((1, 1, 1, t, dh), lambda i, j: (idx, i, j, 0, 0))
    pe_spec = pl.BlockSpec((1, CMP_BLOCK, dh), lambda i, j: (j, 0, 0))
    out_spec = pl.BlockSpec((1, 1, n_chunk, dh), lambda i, j: (i, j, 0, 0))
    out = jax.ShapeDtypeStruct((b, g, n_chunk, dh), ACT_DTYPE)
    return pl.pallas_call(
        _compress_kernel,
        grid=(b, g),
        in_specs=[part(0), part(1), pe_spec, pe_spec, _const_spec(k_w1.shape), _const_spec(k_w2.shape),
                  _const_spec(v_w1.shape), _const_spec(v_w2.shape)],
        out_specs=[out_spec, out_spec],
        out_shape=[out, out],
        compiler_params=_params(("arbitrary", "arbitrary")),
        name="nsa_compress",
    )(kvc, kvc, pe_k.astype(F32), pe_v.astype(F32), _mx(k_w1), _mx(k_w2), _mx(v_w1), _mx(v_w2))


def _rel_bucket(dist):
    n = jnp.maximum(dist, 0)
    max_exact = REL_BUCKETS // 2
    nf = jnp.maximum(n, 1).astype(F32)
    large = max_exact + (jnp.log(nf / max_exact) / math.log(REL_MAX_DIST / max_exact)
                         * (REL_BUCKETS - max_exact)).astype(jnp.int32)
    return jnp.where(n < max_exact, n, jnp.minimum(large, REL_BUCKETS - 1))


def _toeplitz(vals, rows, cols):
    span = rows + cols
    z = jnp.concatenate([vals, vals[..., :1]], axis=-1)
    flat = jnp.tile(z, rows)[..., :rows * (span - 1)]
    skew = flat.reshape(vals.shape[:-1] + (rows, span - 1))
    return skew[..., rows - 1:rows - 1 + cols]


def _nsa_bias_tiles(rel_bias, t):
    tq, tk = NSA_TQ, NSA_TK
    table = rel_bias.astype(F32).reshape(REL_BUCKETS, NSA_GROUPS, NSA_HPG).transpose(1, 2, 0)
    by_dist = table[:, :, _rel_bucket(jnp.arange(REL_MAX_DIST + 1))]
    far = by_dist[:, :, REL_MAX_DIST:]

    def band(dist, valid, shift=None):
        b = by_dist[:, :, np.clip(dist, 0, REL_MAX_DIST)]
        if shift is not None:
            b = b - shift
        return jnp.where(jnp.asarray(valid), b * LOG2E, NEG)

    u = lambda rows, cols: np.arange(rows + cols - 1) - (rows - 1)
    wk = WINDOW + tq
    win = []
    for v in range(WINDOW // tq + 1):
        dist = v * tq - max(v * tq - WINDOW, 0) - u(tq, wk)
        win.append(_toeplitz(band(dist, (dist >= 0) & (dist < WINDOW)), tq, wk))
    bias_win = jnp.stack(win, axis=1)
    n_near = (REL_MAX_DIST + tk - 1) // tq + 1
    sel = []
    for e in range(n_near):
        dist = e * tq - u(tq, tk)
        sel.append(_toeplitz(band(dist, dist >= 0, far), tq, tk))
    sel.append(jnp.zeros_like(sel[0]))
    bias_sel = jnp.stack(sel, axis=1)
    n_cmp = t // CMP_STRIDE
    per_b = []
    for b in range(CMP_STRIDE):
        dist = b - (CMP_BLOCK - 1) - CMP_STRIDE * u(n_cmp, n_cmp)
        per_b.append(_toeplitz(band(dist, dist >= 0), n_cmp, n_cmp))
    bias_cmp = jnp.stack(per_b, axis=3).reshape(NSA_GROUPS, NSA_HPG, t, n_cmp)
    real = np.arange(n_cmp) < (t - CMP_BLOCK) // CMP_STRIDE + 1
    bias_cmp = jnp.where(jnp.asarray(real), bias_cmp, NEG).reshape(NSA_GROUPS, NSA_HPG, t // tq, tq, n_cmp)
    return bias_cmp, bias_win, bias_sel


def _nsa_constants(t):
    n_sel = t // SEL_BLOCK
    n_cmp = t // CMP_STRIDE
    cmp_start = np.arange(n_cmp) * CMP_STRIDE
    sel_start = np.arange(n_sel) * SEL_BLOCK
    overlap = ((cmp_start[:, None] < sel_start[None, :] + SEL_BLOCK)
               & (cmp_start[:, None] + CMP_BLOCK > sel_start[None, :]))
    overlap[(t - CMP_BLOCK) // CMP_STRIDE + 1:, :] = False
    per = NSA_TK // SEL_BLOCK
    expand = np.zeros((t // NSA_TK, LANES, NSA_TK), np.float32)
    for c in range(t // NSA_TK):
        for s in range(NSA_TK):
            expand[c, per * c + s // SEL_BLOCK, s] = 1.0
    return jnp.asarray(overlap.T, MXU_DTYPE), jnp.asarray(expand, MXU_DTYPE)


def _nsa_attn_kernel(q_ref, gate_ref, kc_ref, vc_ref, ks_ref, vs_ref, kw_ref, vw_ref,
                     bcmp_ref, bwin_ref, bsel_ref, ovl_ref, exp_ref, o_ref,
                     s_ref, p_ref, neg_ref, m_ref, alpha_ref, acc_ref):
    tq, tk, dh = NSA_TQ, NSA_TK, NSA_DH
    i = pl.program_id(2)
    t0 = i * tq
    q = q_ref[0]
    heads = range(NSA_HPG)
    qh = [q[:, dh * h:dh * (h + 1)] for h in heads]

    k_cmp, v_cmp = kc_ref[0, 0], vc_ref[0, 0]
    lc = [_dot_nt(qh[h], k_cmp) + bcmp_ref[0, h, 0] for h in heads]
    mc = [jnp.max(x, axis=-1, keepdims=True) for x in lc]
    pcu = [jnp.where(mc[h] > 0.5 * NEG, jnp.exp2(lc[h] - mc[h]), 0.0) for h in heads]
    pc = [x / jnp.maximum(jnp.sum(x, axis=-1, keepdims=True), 1e-30) for x in pcu]
    oc = [_dot(x, v_cmp) for x in pc]

    pcsum = (pc[0] + pc[1]) + (pc[2] + pc[3])
    hi = pcsum.astype(MXU_DTYPE)
    mid = (pcsum - hi.astype(F32))
    lo = (mid - mid.astype(MXU_DTYPE).astype(F32)).astype(MXU_DTYPE)
    ovl = ovl_ref[...]
    imp = (lax.dot_general(ovl, hi, (((1,), (1,)), ((), ())), preferred_element_type=F32)
           + lax.dot_general(ovl, mid.astype(MXU_DTYPE), (((1,), (1,)), ((), ())), preferred_element_type=F32)
           + lax.dot_general(ovl, lo, (((1,), (1,)), ((), ())), preferred_element_type=F32))
    n_sel = imp.shape[0]
    jj = lax.broadcasted_iota(jnp.int32, (n_sel, 1), 0)
    blk_q = lax.shift_right_logical(t0 + lax.broadcasted_iota(jnp.int32, (1, tq), 1),
                                    int(math.log2(SEL_BLOCK)))
    forced = (jj == 0) | (jj == blk_q) | (jj == blk_q - 1)
    score = jnp.where(forced, IMP_FORCE, jnp.where(jj <= blk_q, imp, -1.0))
    rank = jnp.zeros_like(score)
    for j in range(n_sel):
        row = score[j:j + 1, :]
        tie = (jj > j).astype(F32)
        rank = rank + jnp.where(row > score, 1.0, jnp.where(row == score, tie, 0.0))
    sel_t = (rank < SEL_TOPK).astype(F32)
    sel = jnp.concatenate([sel_t, jnp.zeros((LANES - n_sel, tq), F32)], axis=0).T.astype(MXU_DTYPE)

    def with_ones(v):
        ones_col = (lax.broadcasted_iota(jnp.int32, v.shape, 1) == 0).astype(v.dtype)
        return jnp.concatenate([v, ones_col], axis=1)

    def normalised(acc):
        return acc[:, :dh] / acc[:, dh:dh + 1]

    rb = NSA_ROW_BLOCK
    n_rb = tq // rb

    def row_blocks(fn):
        for j in range(n_rb):
            for h in heads:
                fn(h, pl.ds(j * rb, rb))

    m_ref[...] = jnp.full(m_ref.shape, NEG, F32)
    acc_ref[...] = jnp.zeros_like(acc_ref)

    def sel_step(c, carry):
        start = pl.multiple_of(c * tk, tk)
        k_c = ks_ref[0, 0, 0, pl.ds(start, tk), :]
        v_c = with_ones(vs_ref[0, 0, 0, pl.ds(start, tk), :])
        keep = jnp.dot(sel, exp_ref[c], preferred_element_type=F32)
        neg_ref[...] = (keep - 1.0) * (-NEG)
        near = jnp.minimum((t0 - start) // tq, bsel_ref.shape[1] - 1)
        for h in heads:
            s_ref[h, :, :tk] = _dot_nt(qh[h], k_c)

        def numerators(h, rows):
            x = s_ref[h, rows, :tk] + bsel_ref[0, near, h, rows, :] + neg_ref[rows, :]
            m_old = m_ref[h, rows, :]
            m_new = jnp.maximum(m_old, jnp.max(x, axis=-1, keepdims=True))
            m_ref[h, rows, :] = m_new
            alpha_ref[h, rows, :] = jnp.exp2(m_old - m_new)
            p_ref[h, rows, :tk] = jnp.exp2(x - jnp.concatenate([m_new] * (tk // LANES), axis=1)).astype(p_ref.dtype)

        row_blocks(numerators)
        for h in heads:
            acc_ref[h] = alpha_ref[h] * acc_ref[h] + jnp.dot(p_ref[h, :, :tk], v_c, preferred_element_type=F32)
        return carry

    lax.fori_loop(0, (t0 + tq + tk - 1) // tk, sel_step, 0)

    w0 = pl.multiple_of(jnp.maximum(t0 - WINDOW, 0), tq)
    wk = WINDOW + tq
    k_w = kw_ref[0, 0, 0, pl.ds(w0, wk), :]
    v_w = with_ones(vw_ref[0, 0, 0, pl.ds(w0, wk), :])
    for h in heads:
        s_ref[h] = _dot_nt(qh[h], k_w)

    def win_numerators(h, rows):
        x = s_ref[h, rows, :] + bwin_ref[0, 0, h, rows, :]
        p_ref[h, rows, :] = jnp.exp2(x - jnp.max(x, axis=-1, keepdims=True)).astype(p_ref.dtype)

    row_blocks(win_numerators)
    ow = [jnp.dot(p_ref[h], v_w, preferred_element_type=F32) for h in heads]

    gates = jax.nn.sigmoid(gate_ref[0])
    outs = []
    for h in heads:
        g = [gates[:, N_BRANCH * h + br:N_BRANCH * h + br + 1] for br in range(N_BRANCH)]
        outs.append(g[0] * oc[h] + g[1] * normalised(acc_ref[h]) + g[2] * normalised(ow[h]))
    o_ref[0] = jnp.concatenate(outs, axis=1).astype(o_ref.dtype)


def _nsa_attention(q, gate_logits, kv, k_cmp, v_cmp, rel_bias):
    b, t, d = q.shape
    tq, tk, dh = NSA_TQ, NSA_TK, NSA_DH
    bias_cmp, bias_win, bias_sel = _nsa_bias_tiles(rel_bias, t)
    ovl, expand = _nsa_constants(t)
    n_cmp = k_cmp.shape[2]
    n_win = bias_win.shape[1]
    part = lambda idx: pl.BlockSpec((1, 1, 1, t, dh), lambda g, bi, i: (idx, bi, g, 0, 0))
    cmp_spec = pl.BlockSpec((1, 1, n_cmp, dh), lambda g, bi, i: (bi, g, 0, 0))
    return pl.pallas_call(
        _nsa_attn_kernel,
        grid=(NSA_GROUPS, b, t // tq),
        in_specs=[pl.BlockSpec((1, tq, NSA_HPG * dh), lambda g, bi, i: (bi, i, g)),
                  pl.BlockSpec((1, tq, GATE_LANES), lambda g, bi, i: (bi, i, g)),
                  cmp_spec, cmp_spec, part(0), part(1), part(2), part(3),
                  pl.BlockSpec((1, NSA_HPG, 1, tq, n_cmp), lambda g, bi, i: (g, 0, i, 0, 0)),
                  pl.BlockSpec((1, 1, NSA_HPG, tq, WINDOW + tq),
                               lambda g, bi, i: (g, jnp.minimum(i, n_win - 1), 0, 0, 0)),
                  pl.BlockSpec((1, bias_sel.shape[1], NSA_HPG, tq, tk), lambda g, bi, i: (g, 0, 0, 0, 0)),
                  _const_spec(ovl.shape), _const_spec(expand.shape)],
        out_specs=pl.BlockSpec((1, tq, NSA_HPG * dh), lambda g, bi, i: (bi, i, g)),
        out_shape=jax.ShapeDtypeStruct((b, t, d), ACT_DTYPE),
        scratch_shapes=[pltpu.VMEM((NSA_HPG, tq, WINDOW + tq), F32),
                        pltpu.VMEM((NSA_HPG, tq, WINDOW + tq), MXU_DTYPE),
                        pltpu.VMEM((tq, tk), F32),
                        pltpu.VMEM((NSA_HPG, tq, LANES), F32),
                        pltpu.VMEM((NSA_HPG, tq, LANES), F32),
                        pltpu.VMEM((NSA_HPG, tq, 2 * dh), F32)],
        compiler_params=_params(("arbitrary", "arbitrary", "arbitrary")),
        name="nsa_attention",
    )(q, gate_logits, k_cmp, v_cmp, kv, kv, kv, kv, bias_cmp, bias_win, bias_sel, ovl, expand)


def _nsa_block(h, slots, route, kv_ln, kv_w, cmp_pe_k, cmp_pe_v, cmp_k_w1, cmp_k_w2, cmp_v_w1, cmp_v_w2,
               rel_bias, b_ln, b_wqg, b_wo, b, t):
    n, d = b * t, kv_ln.shape[0]
    h1, q, gate_logits, kvc, kv = _nsa_proj(h, slots, route, kv_ln, b_ln, kv_w, b_wqg, b, t)
    k_cmp, v_cmp = _compress(kvc, cmp_pe_k, cmp_pe_v, cmp_k_w1, cmp_k_w2, cmp_v_w1, cmp_v_w2)
    o = _nsa_attention(q, gate_logits, kv, k_cmp, v_cmp, rel_bias)
    return _matmul_res(o.reshape(n, d), _mx(b_wo), h1.reshape(n, d))


def kernel(x, a_ln, a_mu, a_wr, a_wk, a_wv, a_wo, a_w0, a_w1, a_w2, a_a0, a_a1, a_a2, a_g1, a_g2, a_kk, a_ka, a_rk, a_lnx_w, a_lnx_b, kv_ln, kv_w, cmp_pe_k, cmp_pe_v, cmp_k_w1, cmp_k_w2, cmp_v_w1, cmp_v_w2, rel_bias, b_ln, b_wqg, b_wo, m_ln, m_wg, m_bg, m_we, m_be, m_w1, m_w3, m_w2, final_ln):
    b, t, d = x.shape
    h = _rwkv_block(x, a_ln[0], a_mu[0], a_wr[0], a_wk[0], a_wv[0], a_wo[0], a_w0[0], a_w1[0], a_w2[0],
                    a_a0[0], a_a1[0], a_a2[0], a_g1[0], a_g2[0], a_kk[0], a_ka[0], a_rk[0],
                    a_lnx_w[0], a_lnx_b[0])
    moe = lambda hh, l: _moe_layer(hh, m_ln[l], m_wg[l], m_bg[l], m_we[l], m_be[l], m_w1[l], m_w3[l], m_w2[l])
    slots, route = moe(h, 0)
    h = _nsa_block(h, slots, route, kv_ln, kv_w, cmp_pe_k, cmp_pe_v, cmp_k_w1, cmp_k_w2, cmp_v_w1, cmp_v_w2,
                   rel_bias, b_ln[0], b_wqg[0], b_wo[0], b, t)
    slots, route = moe(h, 1)
    return _final(h, slots, route, final_ln).reshape(b, t, d)
```

```python
import functools
import math

import numpy as np
import jax
import jax.numpy as jnp
from jax import lax
from jax.experimental import pallas as pl
from jax.experimental.pallas import tpu as pltpu

F32 = jnp.float32
MXU_DTYPE = jnp.bfloat16
ACT_DTYPE = jnp.bfloat16

D_MODEL = 1024
NORM_EPS = 1e-6
RWKV_HEAD = 64
RWKV_HEADS = D_MODEL // RWKV_HEAD
RWKV_GN_EPS = 64e-5
RWKV_CHUNK = 64
LANES = 128
MXU_TILE = 256
VMEM_LIMIT = 56 * 1024 * 1024


def _mx(x):
    return x.astype(MXU_DTYPE)


def _dot(a, b):
    return jnp.dot(_mx(a), _mx(b), preferred_element_type=F32)


def _dot_nt(a, b):
    return lax.dot_general(_mx(a), _mx(b), (((1,), (1,)), ((), ())), preferred_element_type=F32)


def _split2(x):
    hi = x.astype(MXU_DTYPE)
    lo = (x - hi.astype(F32)).astype(MXU_DTYPE)
    return hi, lo


def _dot_hl(x, w):
    hi, lo = _split2(x)
    return jnp.dot(hi, w, preferred_element_type=F32) + jnp.dot(lo, w, preferred_element_type=F32)


def _rms(x, g):
    return x * lax.rsqrt(jnp.mean(x * x, axis=-1, keepdims=True) + NORM_EPS) * g


TOK_ROWS = 8


def _tok_load(ref):
    tm = ref.shape[0] // TOK_ROWS
    return jnp.concatenate([ref[pl.ds(s, tm, stride=TOK_ROWS), :] for s in range(TOK_ROWS)], axis=1)


def _tok_store(ref, val):
    tm = val.shape[0]
    for s in range(TOK_ROWS):
        ref[pl.ds(s, tm, stride=TOK_ROWS), :] = val[:, LANES * s:LANES * (s + 1)]


def _tok_spec(tm, index_map):
    return pl.BlockSpec((TOK_ROWS * tm, LANES), index_map)


def _slot_spec(tm, k, block_index):
    return pl.BlockSpec((1, TOK_ROWS * tm, LANES), lambda *idx: (k, block_index(*idx), 0))


def _const_spec(shape):
    nd = len(shape)
    return pl.BlockSpec(shape, lambda *_: (0,) * nd)


def _params(sem):
    return pltpu.CompilerParams(dimension_semantics=sem, vmem_limit_bytes=VMEM_LIMIT)


def _rwkv_pre_kernel(x_ref, ln_ref, mu_ref, wr_ref, wk_ref, wv_ref, w1_ref, a1_ref, g1_ref,
                     w2_ref, a2_ref, g2_ref, vec_ref,
                     r_out, k_out, v_out, kk_out, g_out, ld_out, a_out, prev_ref):
    tm = x_ref.shape[1]
    hn = _rms(x_ref[0], ln_ref[...])

    @pl.when(pl.program_id(1) == 0)
    def _():
        prev_ref[...] = jnp.zeros_like(prev_ref)

    rows = lax.broadcasted_iota(jnp.int32, (tm, 1), 0)
    shifted = jnp.where(rows == 0, prev_ref[0:1, :], pltpu.roll(hn, 1, axis=0))
    prev_ref[0:1, :] = hn[tm - 1:tm, :]
    xx = shifted - hn
    xr, xw, xk, xv, xa, xg = (_mx(hn + xx * mu_ref[i:i + 1, :]) for i in range(6))
    w0, a0, kkp, kap = (vec_ref[i:i + 1, :] for i in range(4))
    r = jnp.dot(xr, wr_ref[...], preferred_element_type=F32)
    k = jnp.dot(xk, wk_ref[...], preferred_element_type=F32)
    v = jnp.dot(xv, wv_ref[...], preferred_element_type=F32)
    z = w0 + _dot(jnp.tanh(jnp.dot(xw, w1_ref[...], preferred_element_type=F32)), w2_ref[...])
    a = jax.nn.sigmoid(a0 + _dot(jnp.dot(xa, a1_ref[...], preferred_element_type=F32), a2_ref[...]))
    g = _dot(jax.nn.sigmoid(jnp.dot(xg, g1_ref[...], preferred_element_type=F32)), g2_ref[...])
    ld_out[0] = (-math.exp(-0.5)) * jax.nn.sigmoid(z)
    a_out[0] = a
    r_out[0] = r.astype(r_out.dtype)
    v_out[0] = v.astype(v_out.dtype)
    kk_out[0] = (k * kkp).astype(kk_out.dtype)
    k_out[0] = (k * (1.0 + (a - 1.0) * kap)).astype(k_out.dtype)
    g_out[0] = g.astype(g_out.dtype)


def _rwkv_pre(x, ln, mu, wr, wk, wv, w1, a1, g1, w2, a2, g2, vecs, tm=256):
    b, t, d = x.shape
    act = jax.ShapeDtypeStruct((b, t, d), ACT_DTYPE)
    f32o = jax.ShapeDtypeStruct((b, t, d), F32)
    blk = pl.BlockSpec((1, tm, d), lambda i, j: (i, j, 0))
    consts = [ln, mu, wr, wk, wv, w1, a1, g1, w2, a2, g2, vecs]
    return pl.pallas_call(
        _rwkv_pre_kernel,
        grid=(b, t // tm),
        in_specs=[blk] + [_const_spec(c.shape) for c in consts],
        out_specs=[blk] * 7,
        out_shape=[act, act, act, act, act, f32o, f32o],
        scratch_shapes=[pltpu.VMEM((8, d), F32)],
        compiler_params=_params(("arbitrary", "arbitrary")),
        name="rwkv_pre",
    )(x, *consts)


def _stack4(x):
    return jnp.concatenate([x[:, MXU_TILE * q:MXU_TILE * (q + 1)] for q in range(4)], axis=0)


def _unstack4(y):
    c = y.shape[0] // 4
    return jnp.concatenate([y[c * q:c * (q + 1), :] for q in range(4)], axis=1)


def _rwkv_scan_kernel(r_ref, k_ref, v_ref, kk_ref, g_ref, ld_ref, a_ref,
                      ltri_ref, smask_ref, eye_ref, j_ref, rk_ref, lnw_ref, lnb_ref,
                      y_out, h_ref):
    c = RWKV_CHUNK

    @pl.when(pl.program_id(1) == 0)
    def _():
        h_ref[...] = jnp.zeros_like(h_ref)

    ld = ld_ref[0]
    r = r_ref[0].astype(F32)
    k = k_ref[0].astype(F32)
    v = v_ref[0].astype(F32)
    kk = kk_ref[0].astype(F32)
    jones = j_ref[...]
    ld_hi, ld_lo = _split2(ld)
    cum = (jnp.dot(ltri_ref[...], ld_hi, preferred_element_type=F32)
           + jnp.dot(ltri_ref[...], ld_lo, preferred_element_type=F32))
    cum_last = cum[c - 1:c, :]
    e_l = jnp.exp(cum)
    e_lm = jnp.exp(cum - ld)
    e_nl = jnp.exp(-cum)
    e_cl = jnp.exp(cum_last - cum)
    e_c = jnp.exp(cum_last)
    ss = _unstack4(_dot_hl(_stack4(kk * kk), jones))
    kkn = kk * lax.rsqrt(jnp.maximum(ss, 1e-24))
    a_vec = -kkn
    b_vec = kkn * a_ref[0]
    rt = r * e_l
    at = a_vec * e_lm
    kt = k * e_nl
    bt = b_vec * e_nl
    bh = b_vec * e_cl
    kh = k * e_cl

    lane = lax.broadcasted_iota(jnp.int32, (1, LANES), 1)
    m_l = (lane < RWKV_HEAD).astype(F32)
    m_r = 1.0 - m_l

    def st(xp):
        return jnp.concatenate([xp * m_l, xp * m_r], axis=0)

    def dup(xp):
        return jnp.concatenate([xp, xp], axis=0)

    smask = smask_ref[...] > 0.0
    eye = eye_ref[...]
    pairs = range(D_MODEL // LANES)
    sls = [slice(LANES * p, LANES * (p + 1)) for p in pairs]
    at_s = [st(at[:, sl]) for sl in sls]
    rt_s = [st(rt[:, sl]) for sl in sls]
    v_s = [st(v[:, sl]) for sl in sls]
    s = [jnp.where(smask,
                   _dot_nt(jnp.concatenate([at_s[p], rt_s[p]], axis=0),
                           jnp.concatenate([dup(bt[:, sls[p]]), dup(kt[:, sls[p]])], axis=0)),
                   0.0) for p in pairs]
    a_ab = [x[:2 * c, :2 * c] for x in s]
    a_r = [x[2 * c:, :] for x in s]
    g1 = [_dot(s[p][:2 * c, 2 * c:], v_s[p]) for p in pairs]
    t_m = [eye + x for x in a_ab]
    a_pow = [_dot(x, x) for x in a_ab]
    n = 2
    while n < c:
        if 2 * n < c:
            both = [_dot(a_pow[p], jnp.concatenate([a_pow[p], t_m[p]], axis=1)) for p in pairs]
            a_pow = [x[:, :2 * c] for x in both]
            t_m = [t_m[p] + both[p][:, 2 * c:] for p in pairs]
        else:
            t_m = [t_m[p] + _dot(a_pow[p], t_m[p]) for p in pairs]
        n *= 2
    tw = [_dot(t_m[p], jnp.concatenate([at_s[p], g1[p]], axis=1)) for p in pairs]
    zmat = [jnp.concatenate([tw[p], jnp.concatenate([jnp.zeros_like(v_s[p]), v_s[p]], axis=1)], axis=0)
            for p in pairs]
    upper = [_dot(a_r[p], zmat[p]) for p in pairs]
    bk_t = [jnp.concatenate([st(bh[:, sl]), st(kh[:, sl])], axis=0).T for sl in sls]
    lower = [_dot(bk_t[p], zmat[p]) for p in pairs]
    seq = [_dot(jnp.concatenate([upper[p][:, :LANES] + rt_s[p],
                                 lower[p][:, :LANES] + eye * e_c[:, sls[p]]], axis=0), h_ref[p])
           for p in pairs]
    ys = []
    for p in pairs:
        y_st = seq[p][:2 * c, :] + upper[p][:, LANES:]
        h_ref[p] = seq[p][2 * c:, :] + lower[p][:, LANES:]
        ys.append(y_st[:c, :] + y_st[c:, :])
    y = jnp.concatenate(ys, axis=1)

    inv_n = 1.0 / RWKV_HEAD
    y4 = _stack4(y)
    dlt = y4 - _dot_hl(y4, jones) * inv_n
    var = _dot_hl(dlt * dlt, jones) * inv_n
    yn = dlt * lax.rsqrt(var + RWKV_GN_EPS)
    bonus = _dot_hl(_stack4(r * k) * rk_ref[...], jones) * _stack4(v)
    out = (yn * lnw_ref[...] + lnb_ref[...] + bonus) * _stack4(g_ref[0].astype(F32))
    y_out[0] = _unstack4(out).astype(y_out.dtype)


def _scan_constants():
    c = RWKV_CHUNK
    ltri = np.tril(np.ones((c, c), np.float32))
    rho = np.arange(4 * c)
    r_type, r_head, r_t = rho // (2 * c), (rho % (2 * c)) // c, rho % c
    same = r_head[:, None] == r_head[None, :]
    strict = r_t[None, :] < r_t[:, None]
    incl = r_t[None, :] <= r_t[:, None]
    smask = same & np.where(r_type[:, None] == 0, strict, incl)
    eye = np.eye(LANES, dtype=np.float32)
    head = np.arange(MXU_TILE) // RWKV_HEAD
    jones = (head[:, None] == head[None, :]).astype(np.float32)
    return (jnp.asarray(ltri, MXU_DTYPE), jnp.asarray(smask, F32), jnp.asarray(eye, F32),
            jnp.asarray(jones, MXU_DTYPE))


def _stack_param(p):
    return jnp.repeat(p.reshape(4, MXU_TILE).astype(F32), RWKV_CHUNK, axis=0)


def _rwkv_scan(r, k, v, kk, g, ld, a, rk, lnw, lnb):
    b, t, d = r.shape
    c = RWKV_CHUNK
    blk = pl.BlockSpec((1, c, d), lambda i, j: (i, j, 0))
    consts = list(_scan_constants()) + [_stack_param(rk.reshape(-1)), _stack_param(lnw), _stack_param(lnb)]
    return pl.pallas_call(
        _rwkv_scan_kernel,
        grid=(b, t // c),
        in_specs=[blk] * 7 + [_const_spec(x.shape) for x in consts],
        out_specs=blk,
        out_shape=jax.ShapeDtypeStruct((b, t, d), ACT_DTYPE),
        scratch_shapes=[pltpu.VMEM((d // LANES, LANES, LANES), F32)],
        compiler_params=_params(("arbitrary", "arbitrary")),
        name="rwkv_scan",
    )(r, k, v, kk, g, ld, a, *consts)


def _matmul_res_kernel(a_ref, w_ref, res_ref, o_ref):
    _tok_store(o_ref, res_ref[...] + jnp.dot(a_ref[...], w_ref[...], preferred_element_type=F32))


def _matmul_res(a, w, res, tm=512):
    n, kdim = a.shape
    d = w.shape[1]
    return pl.pallas_call(
        _matmul_res_kernel,
        grid=(n // tm,),
        in_specs=[pl.BlockSpec((tm, kdim), lambda i: (i, 0)), _const_spec(w.shape),
                  pl.BlockSpec((tm, d), lambda i: (i, 0))],
        out_specs=_tok_spec(tm, lambda i: (i, 0)),
        out_shape=jax.ShapeDtypeStruct((TOK_ROWS * n, LANES), F32),
        compiler_params=_params(("arbitrary",)),
        name="matmul_res",
    )(a, w, res)


def _rwkv_block(x, a_ln, a_mu, a_wr, a_wk, a_wv, a_wo, a_w0, a_w1, a_w2, a_a0, a_a1, a_a2,
                a_g1, a_g2, a_kk, a_ka, a_rk, a_lnx_w, a_lnx_b):
    b, t, d = x.shape
    vecs = jnp.stack([a_w0, a_a0, a_kk, a_ka]).astype(F32)
    r, k, v, kk, g, ld, a = _rwkv_pre(
        x, a_ln.reshape(1, d), a_mu, _mx(a_wr), _mx(a_wk), _mx(a_wv), _mx(a_w1), _mx(a_a1), _mx(a_g1),
        _mx(a_w2), _mx(a_a2), _mx(a_g2), vecs)
    y = _rwkv_scan(r, k, v, kk, g, ld, a, a_rk, a_lnx_w, a_lnx_b)
    return _matmul_res(y.reshape(b * t, d), _mx(a_wo), x.reshape(b * t, d))


MOE_GROUPS = 4
MOE_EPG = 8
MOE_EXPERTS = MOE_GROUPS * MOE_EPG
MOE_TOPK = 2
MOE_BLOCK = 256
NEG = -1e30


def _route(logits):
    lane = lax.broadcasted_iota(jnp.int32, logits.shape, 1)
    is_grp = lane < MOE_GROUPS
    gl = jnp.where(is_grp, logits, NEG)
    ge = jnp.exp(gl - jnp.max(gl, axis=-1, keepdims=True))
    gp = ge / jnp.sum(ge, axis=-1, keepdims=True)
    grp_w = jnp.max(gp, axis=-1, keepdims=True)
    grp = jnp.min(jnp.where(is_grp & (gp == grp_w), lane, LANES), axis=-1, keepdims=True)
    lo = MOE_GROUPS + MOE_EPG * grp
    in_grp = (lane >= lo) & (lane < lo + MOE_EPG)
    el = jnp.where(in_grp, logits, NEG)
    ee = jnp.exp(el - jnp.max(el, axis=-1, keepdims=True))
    ep = ee / jnp.sum(ee, axis=-1, keepdims=True)
    p1 = jnp.max(jnp.where(in_grp, ep, -1.0), axis=-1, keepdims=True)
    i1 = jnp.min(jnp.where(in_grp & (ep == p1), lane, LANES), axis=-1, keepdims=True)
    rest = in_grp & (lane != i1)
    p2 = jnp.max(jnp.where(rest, ep, -1.0), axis=-1, keepdims=True)
    i2 = jnp.min(jnp.where(rest & (ep == p2), lane, LANES), axis=-1, keepdims=True)
    scale = grp_w / (p1 + p2)
    return i1 - MOE_GROUPS, i2 - MOE_GROUPS, p1 * scale, p2 * scale


def _router_logits(xf, w_hi_ref, w_lo_ref, bias_ref):
    hi, lo = _split2(xf)
    logits = (jnp.dot(hi, w_hi_ref[...], preferred_element_type=F32)
              + jnp.dot(lo, w_hi_ref[...], preferred_element_type=F32)
              + jnp.dot(hi, w_lo_ref[...], preferred_element_type=F32))
    return logits + bias_ref[...]


def _moe_router_kernel(h_ref, ln_ref, w_hi_ref, w_lo_ref, bias_ref, tri_ref, route_out, count_out, cnt_ref):
    @pl.when(pl.program_id(0) == 0)
    def _():
        cnt_ref[...] = jnp.zeros_like(cnt_ref)

    xf = _rms(_tok_load(h_ref), ln_ref[...])
    e1, e2, w1, w2 = _route(_router_logits(xf, w_hi_ref, w_lo_ref, bias_ref))
    lane = lax.broadcasted_iota(jnp.int32, (xf.shape[0], LANES), 1)
    hit1, hit2 = lane == e1, lane == e2
    onehot = jnp.where(hit1, 1.0, jnp.where(hit2, 1.0, 0.0))
    before = jnp.dot(tri_ref[...], _mx(onehot), preferred_element_type=F32) + cnt_ref[...]
    r1 = jnp.sum(jnp.where(hit1, before, 0.0), axis=-1, keepdims=True)
    r2 = jnp.sum(jnp.where(hit2, before, 0.0), axis=-1, keepdims=True)
    cnt_ref[...] = cnt_ref[...] + jnp.sum(onehot, axis=0, keepdims=True)
    count_out[...] = cnt_ref[...]
    fields = (e1.astype(F32), e2.astype(F32), w1, w2, r1, r2)
    rec = jnp.zeros(lane.shape, F32)
    for idx, val in enumerate(fields):
        rec = jnp.where(lane == idx, val, rec)
    route_out[...] = rec


def _router_weights(w_gr, b_gr, w_er, b_er):
    d = w_gr.shape[0]
    w = jnp.zeros((d, LANES), F32).at[:, :MOE_GROUPS].set(w_gr).at[:, MOE_GROUPS:MOE_GROUPS + MOE_EXPERTS].set(w_er)
    bias = jnp.zeros((1, LANES), F32).at[0, :MOE_GROUPS].set(b_gr).at[0, MOE_GROUPS:MOE_GROUPS + MOE_EXPERTS].set(b_er)
    w_hi = w.astype(MXU_DTYPE)
    w_lo = (w - w_hi.astype(F32)).astype(MXU_DTYPE)
    return w_hi, w_lo, bias


def _moe_router(h, ln, w_hi, w_lo, bias, tm=512):
    n, d = h.shape[0] // TOK_ROWS, ln.shape[0]
    tri = jnp.asarray(np.tril(np.ones((tm, tm), np.float32), -1), MXU_DTYPE)
    return pl.pallas_call(
        _moe_router_kernel,
        grid=(n // tm,),
        in_specs=[_tok_spec(tm, lambda i: (i, 0)), _const_spec((1, d)), _const_spec(w_hi.shape),
                  _const_spec(w_lo.shape),
                  _const_spec(bias.shape), _const_spec(tri.shape)],
        out_specs=[pl.BlockSpec((tm, LANES), lambda i: (i, 0)), _const_spec((1, LANES))],
        out_shape=[jax.ShapeDtypeStruct((n, LANES), F32), jax.ShapeDtypeStruct((1, LANES), F32)],
        scratch_shapes=[pltpu.VMEM((1, LANES), F32)],
        compiler_params=_params(("arbitrary",)),
        name="moe_router",
    )(h, ln.reshape(1, d), w_hi, w_lo, bias, tri)


def _moe_plan(route, counts, blk):
    n = route.shape[0]
    counts = counts[0, :MOE_EXPERTS].astype(jnp.int32)
    pcounts = (counts + blk - 1) // blk * blk
    pend = jnp.cumsum(pcounts)
    pstart = (pend - pcounts).astype(jnp.int32)
    n_blk = n * MOE_TOPK // blk + MOE_EXPERTS
    blk_start = jnp.arange(n_blk, dtype=jnp.int32) * blk
    blk_e = jnp.sum((blk_start[:, None] >= pend[None, :]).astype(jnp.int32), axis=1)
    blk_e = jnp.minimum(blk_e, MOE_EXPERTS - 1)
    n_valid = jnp.clip((pstart + counts)[blk_e] - blk_start, 0, blk).astype(jnp.int32)
    n_act = (pend[-1] // blk).astype(jnp.int32).reshape(1)
    expert = route[:, 0:MOE_TOPK].astype(jnp.int32)
    onehot = expert[:, :, None] == jnp.arange(MOE_EXPERTS, dtype=jnp.int32)
    seg_start = jnp.sum(jnp.where(onehot, pstart, 0), axis=-1)
    sorted_pos = seg_start + route[:, 4:4 + MOE_TOPK].astype(jnp.int32)
    return blk_e, n_valid, n_act, sorted_pos


def _token_copies(pos_ref, tm, make_copy):
    group = 8

    def body(g, carry):
        t0 = g * group
        pos = [[pos_ref[0, 0, MOE_TOPK * (t0 + j) + k] for k in range(MOE_TOPK)] for j in range(group)]
        for j in range(group):
            for k in range(MOE_TOPK):
                make_copy(t0 + j, k, pos[j][k]).start()
        return carry

    lax.fori_loop(0, tm // group, body, 0)


def _tile(ref, tok):
    return ref.at[pl.ds(pl.multiple_of(tok * TOK_ROWS, TOK_ROWS), TOK_ROWS)]


def _moe_dispatch_kernel(n_valid_ref, pos_ref, src_ref, dst_hbm, sem, zbuf, zsem):
    tm = src_ref.shape[0] // TOK_ROWS

    @pl.when(pl.program_id(0) == 0)
    def _():
        rows = zbuf.shape[0]
        zbuf[...] = jnp.zeros_like(zbuf)

        def fill(j, wait):
            @pl.when(n_valid_ref[j] < rows // TOK_ROWS)
            def _():
                start = 0 if wait else pl.multiple_of(j * rows, rows)
                cp = pltpu.make_async_copy(zbuf, dst_hbm.at[pl.ds(start, rows)], zsem)
                cp.wait() if wait else cp.start()

        lax.fori_loop(0, n_valid_ref.shape[0], lambda j, c: (fill(j, False), c)[1], 0)
        lax.fori_loop(0, n_valid_ref.shape[0], lambda j, c: (fill(j, True), c)[1], 0)

    _token_copies(pos_ref, tm, lambda t, k, pos: pltpu.make_async_copy(_tile(src_ref, t), _tile(dst_hbm, pos), sem))
    for k in range(MOE_TOPK):
        pltpu.make_async_copy(src_ref, dst_hbm.at[pl.ds(0, src_ref.shape[0])], sem).wait()


def _moe_collect_kernel(pos_ref, src_hbm, dst_ref, sem):
    tm = dst_ref.shape[1] // TOK_ROWS
    _token_copies(pos_ref, tm,
                  lambda t, k, pos: pltpu.make_async_copy(_tile(src_hbm, pos), _tile(dst_ref.at[k], t), sem))
    for k in range(MOE_TOPK):
        pltpu.make_async_copy(src_hbm.at[pl.ds(0, dst_ref.shape[1])], dst_ref.at[k], sem).wait()


def _moe_dispatch(h, n_valid, sorted_pos, p, blk, tm=512):
    n = sorted_pos.shape[0]
    pos = sorted_pos.reshape(n // tm, 1, MOE_TOPK * tm)
    grid_spec = pltpu.PrefetchScalarGridSpec(
        num_scalar_prefetch=1,
        grid=(n // tm,),
        in_specs=[pl.BlockSpec((1, 1, MOE_TOPK * tm), lambda i, nv: (i, 0, 0), memory_space=pltpu.SMEM),
                  _tok_spec(tm, lambda i, nv: (i, 0))],
        out_specs=pl.BlockSpec(memory_space=pl.ANY),
        scratch_shapes=[pltpu.SemaphoreType.DMA, pltpu.VMEM((TOK_ROWS * blk, LANES), h.dtype),
                        pltpu.SemaphoreType.DMA],
    )
    return pl.pallas_call(
        _moe_dispatch_kernel,
        grid_spec=grid_spec,
        out_shape=jax.ShapeDtypeStruct((TOK_ROWS * p, LANES), h.dtype),
        compiler_params=_params(("arbitrary",)),
        name="moe_dispatch",
    )(n_valid, pos, h)


def _moe_collect(ys, sorted_pos, tm=512):
    n = sorted_pos.shape[0]
    pos = sorted_pos.reshape(n // tm, 1, MOE_TOPK * tm)
    return pl.pallas_call(
        _moe_collect_kernel,
        grid=(n // tm,),
        in_specs=[pl.BlockSpec((1, 1, MOE_TOPK * tm), lambda i: (i, 0, 0), memory_space=pltpu.SMEM),
                  pl.BlockSpec(memory_space=pl.ANY)],
        out_specs=pl.BlockSpec((MOE_TOPK, TOK_ROWS * tm, LANES), lambda i: (0, i, 0)),
        out_shape=jax.ShapeDtypeStruct((MOE_TOPK, TOK_ROWS * n, LANES), ys.dtype),
        scratch_shapes=[pltpu.SemaphoreType.DMA],
        compiler_params=_params(("arbitrary",)),
        name="moe_collect",
    )(pos, ys)


def _moe_expert_kernel(blk_e_ref, n_act_ref, x_ref, ln_ref, w1_ref, w3_ref, w2_ref, y_ref):
    del blk_e_ref
    i = pl.program_id(0)

    @pl.when(i < n_act_ref[0])
    def _():
        xn = _mx(_rms(_tok_load(x_ref), ln_ref[...]))
        h1 = jnp.dot(xn, w1_ref[0], preferred_element_type=F32)
        h3 = jnp.dot(xn, w3_ref[0], preferred_element_type=F32)
        _tok_store(y_ref, _dot(h1 * jax.nn.sigmoid(h1) * h3, w2_ref[0]))

    @pl.when(i >= n_act_ref[0])
    def _():
        y_ref[...] = jnp.zeros_like(y_ref)


def _moe_experts(xs, ln, w1, w3, w2, blk_e, n_act, blk):
    p = xs.shape[0] // TOK_ROWS
    d, hdim = w1.shape[1], w1.shape[2]
    grid_spec = pltpu.PrefetchScalarGridSpec(
        num_scalar_prefetch=2,
        grid=(p // blk,),
        in_specs=[_tok_spec(blk, lambda i, be, na: (i, 0)),
                  pl.BlockSpec((1, d), lambda i, be, na: (0, 0)),
                  pl.BlockSpec((1, d, hdim), lambda i, be, na: (be[i], 0, 0)),
                  pl.BlockSpec((1, d, hdim), lambda i, be, na: (be[i], 0, 0)),
                  pl.BlockSpec((1, hdim, d), lambda i, be, na: (be[i], 0, 0))],
        out_specs=_tok_spec(blk, lambda i, be, na: (i, 0)),
    )
    return pl.pallas_call(
        _moe_expert_kernel,
        grid_spec=grid_spec,
        out_shape=jax.ShapeDtypeStruct(xs.shape, F32),
        compiler_params=_params(("arbitrary",)),
        name="moe_experts",
    )(blk_e, n_act, xs, ln.reshape(1, d), w1, w3, w2)


def _moe_layer(h, ln, w_gr, b_gr, w_er, b_er, w1, w3, w2):
    n = h.shape[0] // TOK_ROWS
    blk = MOE_BLOCK
    route, counts = _moe_router(h, ln, *_router_weights(w_gr, b_gr, w_er, b_er))
    blk_e, n_valid, n_act, sorted_pos = _moe_plan(route, counts, blk)
    p = n * MOE_TOPK + MOE_EXPERTS * blk
    xs = _moe_dispatch(h, n_valid, sorted_pos, p, blk)
    ys = _moe_experts(xs, ln, _mx(w1), _mx(w3), _mx(w2), blk_e, n_act, blk)
    return _moe_collect(ys, sorted_pos), route


def _moe_combine(h_ref, s0_ref, s1_ref, route):
    return _tok_load(h_ref) + route[:, 2:3] * _tok_load(s0_ref.at[0]) + route[:, 3:4] * _tok_load(s1_ref.at[0])


def _final_kernel(h_ref, s0_ref, s1_ref, route_ref, ln_ref, o_ref):
    o_ref[...] = _rms(_moe_combine(h_ref, s0_ref, s1_ref, route_ref[...]), ln_ref[...])


def _final(h, slots, route, ln, tm=512):
    n, d = route.shape[0], ln.shape[0]
    return pl.pallas_call(
        _final_kernel,
        grid=(n // tm,),
        in_specs=[_tok_spec(tm, lambda i: (i, 0)), _slot_spec(tm, 0, lambda i: i), _slot_spec(tm, 1, lambda i: i),
                  pl.BlockSpec((tm, LANES), lambda i: (i, 0)), _const_spec((1, d))],
        out_specs=pl.BlockSpec((tm, d), lambda i: (i, 0)),
        out_shape=jax.ShapeDtypeStruct((n, d), F32),
        compiler_params=_params(("arbitrary",)),
        name="final_norm",
    )(h, slots, slots, route, ln.reshape(1, d))


NSA_HEADS = 16
NSA_GROUPS = 4
NSA_HPG = NSA_HEADS // NSA_GROUPS
NSA_DH = D_MODEL // NSA_HEADS
NSA_PARTS = 6
CMP_BLOCK = 32
CMP_STRIDE = 16
SEL_BLOCK = 64
SEL_TOPK = 8
WINDOW = 512
N_BRANCH = 3
IMP_FORCE = 1e4
REL_BUCKETS = 32
REL_MAX_DIST = 128
LOG2E = math.log2(math.e)
NSA_TQ = 256
NSA_TK = 256
NSA_ROW_BLOCK = 32
GATE_LANES = LANES
CMP_PARTS = 2


def _nsa_proj_kernel(h_ref, s0_ref, s1_ref, route_ref, kvln_ref, qln_ref, kvw_ref, qw_ref,
                     h_out, q_out, gate_out, kvc_out, kv_out):
    h1 = _moe_combine(h_ref, s0_ref, s1_ref, route_ref[0])
    h_out[0] = h1
    kv = _dot(_rms(h1, kvln_ref[...]), kvw_ref[...])
    for part in range(NSA_PARTS):
        for g in range(NSA_GROUPS):
            col = (part * NSA_GROUPS + g) * NSA_DH
            piece = kv[:, col:col + NSA_DH]
            if part < CMP_PARTS:
                kvc_out[part, 0, g] = piece
            else:
                kv_out[part - CMP_PARTS, 0, g] = piece.astype(kv_out.dtype)
    qg = _dot(_rms(h1, qln_ref[...]), qw_ref[...])
    q_out[0] = (qg[:, :D_MODEL] * (NSA_DH ** -0.5 * LOG2E)).astype(q_out.dtype)
    gate_out[0] = qg[:, D_MODEL:]


def _gate_padded_weight(q_w):
    d = q_w.shape[0]
    n_gate = NSA_HPG * N_BRANCH
    gates = q_w[:, d:].reshape(d, NSA_GROUPS, n_gate)
    gates = jnp.pad(gates, ((0, 0), (0, 0), (0, GATE_LANES - n_gate))).reshape(d, NSA_GROUPS * GATE_LANES)
    return jnp.concatenate([q_w[:, :d], gates], axis=1)


def _nsa_proj(h, slots, route, kv_ln, q_ln, kv_w, q_w, b, t, tm=256):
    d = kv_ln.shape[0]
    blk = lambda w: pl.BlockSpec((1, tm, w), lambda i, j: (i, j, 0))
    per_b = t // tm
    tok_block = lambda i, j: i * per_b + j
    gw = NSA_GROUPS * GATE_LANES
    qw = _gate_padded_weight(q_w)
    kv_spec = lambda parts: pl.BlockSpec((parts, 1, NSA_GROUPS, tm, NSA_DH), lambda i, j: (0, i, 0, j, 0))
    kv_shape = lambda parts: (parts, b, NSA_GROUPS, t, NSA_DH)
    return pl.pallas_call(
        _nsa_proj_kernel,
        grid=(b, per_b),
        in_specs=[_tok_spec(tm, lambda i, j: (tok_block(i, j), 0)), _slot_spec(tm, 0, tok_block),
                  _slot_spec(tm, 1, tok_block), blk(LANES), _const_spec((1, d)), _const_spec((1, d)),
                  _const_spec(kv_w.shape), _const_spec(qw.shape)],
        out_specs=[blk(d), blk(d), blk(gw), kv_spec(CMP_PARTS), kv_spec(NSA_PARTS - CMP_PARTS)],
        out_shape=[jax.ShapeDtypeStruct((b, t, d), F32), jax.ShapeDtypeStruct((b, t, d), ACT_DTYPE),
                   jax.ShapeDtypeStruct((b, t, gw), F32), jax.ShapeDtypeStruct(kv_shape(CMP_PARTS), F32),
                   jax.ShapeDtypeStruct(kv_shape(NSA_PARTS - CMP_PARTS), ACT_DTYPE)],
        compiler_params=_params(("arbitrary", "arbitrary")),
        name="nsa_proj",
    )(h, slots, slots, route.reshape(b, t, LANES),
      kv_ln.reshape(1, d), q_ln.reshape(1, d), _mx(kv_w), _mx(qw))


def _gelu_tanh(x):
    return 0.5 * x * (1.0 + jnp.tanh(math.sqrt(2.0 / math.pi) * (x + 0.044715 * (x * x * x))))


def _compress_kernel(k_ref, v_ref, pek_ref, pev_ref, kw1_ref, kw2_ref, vw1_ref, vw2_ref, kc_out, vc_out):
    dh = k_ref.shape[-1]
    n_chunk = k_ref.shape[3] // CMP_STRIDE

    def one(z_ref, pe_ref, w1_ref, w2_ref, out):
        first = jnp.zeros((n_chunk, w1_ref.shape[1]), F32)
        second = first
        for l in range(CMP_STRIDE):
            z = z_ref[0, 0, 0, pl.ds(l, n_chunk, stride=CMP_STRIDE), :]
            lo, hi = l, CMP_STRIDE + l
            first = first + _dot(z + pe_ref[0, lo:lo + 1, :], w1_ref[dh * lo:dh * (lo + 1), :])
            second = second + _dot(z + pe_ref[0, hi:hi + 1, :], w1_ref[dh * hi:dh * (hi + 1), :])
        hid = first + pltpu.roll(second, n_chunk - 1, axis=0)
        out[0, 0] = _dot(_gelu_tanh(hid), w2_ref[...]).astype(out.dtype)

    one(k_ref, pek_ref, kw1_ref, kw2_ref, kc_out)
    one(v_ref, pev_ref, vw1_ref, vw2_ref, vc_out)


def _compress(kvc, pe_k, pe_v, k_w1, k_w2, v_w1, v_w2):
    _, b, g, t, dh = kvc.shape
    n_chunk = t // CMP_STRIDE
    part = lambda idx: pl.BlockSpec((1, 1, 1, t, dh), lambda i, j: (idx, i, j, 0, 0))
    pe_spec = pl.BlockSpec((1, CMP_BLOCK, dh), lambda i, j: (j, 0, 0))
    out_spec = pl.BlockSpec((1, 1, n_chunk, dh), lambda i, j: (i, j, 0, 0))
    out = jax.ShapeDtypeStruct((b, g, n_chunk, dh), ACT_DTYPE)
    return pl.pallas_call(
        _compress_kernel,
        grid=(b, g),
        in_specs=[part(0), part(1), pe_spec, pe_spec, _const_spec(k_w1.shape), _const_spec(k_w2.shape),
                  _const_spec(v_w1.shape), _const_spec(v_w2.shape)],
        out_specs=[out_spec, out_spec],
        out_shape=[out, out],
        compiler_params=_params(("arbitrary", "arbitrary")),
        name="nsa_compress",
    )(kvc, kvc, pe_k.astype(F32), pe_v.astype(F32), _mx(k_w1), _mx(k_w2), _mx(v_w1), _mx(v_w2))


def _rel_bucket(dist):
    n = jnp.maximum(dist, 0)
    max_exact = REL_BUCKETS // 2
    nf = jnp.maximum(n, 1).astype(F32)
    large = max_exact + (jnp.log(nf / max_exact) / math.log(REL_MAX_DIST / max_exact)
                         * (REL_BUCKETS - max_exact)).astype(jnp.int32)
    return jnp.where(n < max_exact, n, jnp.minimum(large, REL_BUCKETS - 1))


def _toeplitz(vals, rows, cols):
    span = rows + cols
    z = jnp.concatenate([vals, vals[..., :1]], axis=-1)
    flat = jnp.tile(z, rows)[..., :rows * (span - 1)]
    skew = flat.reshape(vals.shape[:-1] + (rows, span - 1))
    return skew[..., rows - 1:rows - 1 + cols]


def _nsa_bias_tiles(rel_bias, t):
    tq, tk = NSA_TQ, NSA_TK
    table = rel_bias.astype(F32).reshape(REL_BUCKETS, NSA_GROUPS, NSA_HPG).transpose(1, 2, 0)
    by_dist = table[:, :, _rel_bucket(jnp.arange(REL_MAX_DIST + 1))]
    far = by_dist[:, :, REL_MAX_DIST:]

    def band(dist, valid, shift=None):
        b = by_dist[:, :, np.clip(dist, 0, REL_MAX_DIST)]
        if shift is not None:
            b = b - shift
        return jnp.where(jnp.asarray(valid), b * LOG2E, NEG)

    u = lambda rows, cols: np.arange(rows + cols - 1) - (rows - 1)
    wk = WINDOW + tq
    win = []
    for v in range(WINDOW // tq + 1):
        dist = v * tq - max(v * tq - WINDOW, 0) + u(wk, tq)
        win.append(_toeplitz(band(dist, (dist >= 0) & (dist < WINDOW)), wk, tq))
    bias_win = jnp.stack(win, axis=1)
    n_near = (REL_MAX_DIST + tk - 1) // tq + 1
    sel = []
    for e in range(n_near):
        dist = e * tq + u(tk, tq)
        sel.append(_toeplitz(band(dist, dist >= 0, far), tk, tq))
    sel.append(jnp.zeros_like(sel[0]))
    bias_sel = jnp.stack(sel, axis=1)
    n_cmp = t // CMP_STRIDE
    per_b = []
    for b in range(CMP_STRIDE):
        dist = b - (CMP_BLOCK - 1) + CMP_STRIDE * u(n_cmp, n_cmp)
        per_b.append(_toeplitz(band(dist, dist >= 0), n_cmp, n_cmp))
    bias_cmp = jnp.stack(per_b, axis=4).reshape(NSA_GROUPS, NSA_HPG, n_cmp, t)
    real = np.arange(n_cmp) < (t - CMP_BLOCK) // CMP_STRIDE + 1
    bias_cmp = jnp.where(jnp.asarray(real)[:, None], bias_cmp, NEG)
    return bias_cmp, bias_win, bias_sel


def _nsa_constants(t):
    n_sel = t // SEL_BLOCK
    n_cmp = t // CMP_STRIDE
    cmp_start = np.arange(n_cmp) * CMP_STRIDE
    sel_start = np.arange(n_sel) * SEL_BLOCK
    overlap = ((cmp_start[:, None] < sel_start[None, :] + SEL_BLOCK)
               & (cmp_start[:, None] + CMP_BLOCK > sel_start[None, :]))
    overlap[(t - CMP_BLOCK) // CMP_STRIDE + 1:, :] = False
    return jnp.asarray(overlap.T, MXU_DTYPE)


def _nsa_attn_kernel(q_ref, gate_ref, kc_ref, vc_ref, ks_ref, vs_ref, kw_ref, vw_ref,
                     bcmp_ref, bwin_ref, bsel_ref, ovl_ref, o_ref, neg_ref):
    tq, tk, dh = NSA_TQ, NSA_TK, NSA_DH
    i = pl.program_id(2)
    t0 = i * tq
    heads = range(NSA_HPG)
    q_t = q_ref[0].astype(F32).T
    qh = [_mx(q_t[dh * h:dh * (h + 1), :]) for h in heads]

    def values_t(v):
        ones_col = (lax.broadcasted_iota(jnp.int32, v.shape, 1) == 0).astype(F32)
        return _mx(jnp.concatenate([v.astype(F32), ones_col], axis=1).T)

    k_cmp = kc_ref[0, 0]
    lc = [jnp.dot(k_cmp, qh[h], preferred_element_type=F32) + bcmp_ref[0, h] for h in heads]
    mc = [jnp.max(x, axis=0, keepdims=True) for x in lc]
    pcu = [jnp.where(mc[h] > 0.5 * NEG, jnp.exp2(lc[h] - mc[h]), 0.0) for h in heads]
    pc = [x / jnp.maximum(jnp.sum(x, axis=0, keepdims=True), 1e-30) for x in pcu]
    v_cmp_t = values_t(vc_ref[0, 0])[:dh]
    oc = [_dot(v_cmp_t, x) for x in pc]

    pcsum = (pc[0] + pc[1]) + (pc[2] + pc[3])
    hi = pcsum.astype(MXU_DTYPE)
    mid = (pcsum - hi.astype(F32))
    lo = (mid - mid.astype(MXU_DTYPE).astype(F32)).astype(MXU_DTYPE)
    ovl = ovl_ref[...]
    imp = (jnp.dot(ovl, hi, preferred_element_type=F32) + jnp.dot(ovl, mid.astype(MXU_DTYPE), preferred_element_type=F32)
           + jnp.dot(ovl, lo, preferred_element_type=F32))
    n_sel = imp.shape[0]
    jj = lax.broadcasted_iota(jnp.int32, (n_sel, 1), 0)
    blk_q = lax.shift_right_logical(t0 + lax.broadcasted_iota(jnp.int32, (1, tq), 1),
                                    int(math.log2(SEL_BLOCK)))
    forced = (jj == 0) | (jj == blk_q) | (jj == blk_q - 1)
    score = jnp.where(forced, IMP_FORCE, jnp.where(jj <= blk_q, imp, -1.0))
    rank = jnp.zeros_like(score)
    for j in range(n_sel):
        row = score[j:j + 1, :]
        tie = (jj > j).astype(F32)
        rank = rank + jnp.where(row > score, 1.0, jnp.where(row == score, tie, 0.0))
    neg_ref[...] = jnp.where(rank < SEL_TOPK, 0.0, NEG)

    def normalised(acc):
        return acc[:dh] / acc[dh:dh + 1]

    per_step = tk // SEL_BLOCK

    def sel_step(c, carry):
        ms, accs = carry
        start = pl.multiple_of(c * tk, tk)
        k_c = ks_ref[0, 0, 0, pl.ds(start, tk), :]
        v_c = values_t(vs_ref[0, 0, 0, pl.ds(start, tk), :])
        neg = jnp.concatenate([jnp.broadcast_to(neg_ref[pl.ds(per_step * c + r, 1), :], (SEL_BLOCK, tq))
                               for r in range(per_step)], axis=0)
        near = jnp.minimum((t0 - start) // tq, bsel_ref.shape[1] - 1)
        x = [jnp.dot(k_c, qh[h], preferred_element_type=F32) + bsel_ref[0, near, h] + neg for h in heads]
        m_new = [jnp.maximum(ms[h], jnp.max(x[h], axis=0, keepdims=True)) for h in heads]
        acc_new = [jnp.exp2(ms[h] - m_new[h]) * accs[h] + _dot(v_c, jnp.exp2(x[h] - m_new[h])) for h in heads]
        return tuple(m_new), tuple(acc_new)

    init = (tuple(jnp.full((1, tq), NEG, F32) for _ in heads), tuple(jnp.zeros((2 * dh, tq), F32) for _ in heads))
    _, acc_sel = lax.fori_loop(0, (t0 + tq + tk - 1) // tk, sel_step, init)

    w0 = pl.multiple_of(jnp.maximum(t0 - WINDOW, 0), tq)
    wk = WINDOW + tq
    k_w = kw_ref[0, 0, 0, pl.ds(w0, wk), :]
    v_w = values_t(vw_ref[0, 0, 0, pl.ds(w0, wk), :])
    xw = [jnp.dot(k_w, qh[h], preferred_element_type=F32) + bwin_ref[0, 0, h] for h in heads]
    ow = [_dot(v_w, jnp.exp2(x - jnp.max(x, axis=0, keepdims=True))) for x in xw]

    gates = jax.nn.sigmoid(gate_ref[0]).T
    outs = []
    for h in heads:
        g = [gates[N_BRANCH * h + br:N_BRANCH * h + br + 1, :] for br in range(N_BRANCH)]
        outs.append(g[0] * oc[h] + g[1] * normalised(acc_sel[h]) + g[2] * normalised(ow[h]))
    o_ref[0] = jnp.concatenate(outs, axis=0).T.astype(o_ref.dtype)


def _nsa_attention(q, gate_logits, kv, k_cmp, v_cmp, rel_bias):
    b, t, d = q.shape
    tq, tk, dh = NSA_TQ, NSA_TK, NSA_DH
    bias_cmp, bias_win, bias_sel = _nsa_bias_tiles(rel_bias, t)
    ovl = _nsa_constants(t)
    n_cmp = k_cmp.shape[2]
    n_win = bias_win.shape[1]
    part = lambda idx: pl.BlockSpec((1, 1, 1, t, dh), lambda g, bi, i: (idx, bi, g, 0, 0))
    cmp_spec = pl.BlockSpec((1, 1, n_cmp, dh), lambda g, bi, i: (bi, g, 0, 0))
    return pl.pallas_call(
        _nsa_attn_kernel,
        grid=(NSA_GROUPS, b, t // tq),
        in_specs=[pl.BlockSpec((1, tq, NSA_HPG * dh), lambda g, bi, i: (bi, i, g)),
                  pl.BlockSpec((1, tq, GATE_LANES), lambda g, bi, i: (bi, i, g)),
                  cmp_spec, cmp_spec, part(0), part(1), part(2), part(3),
                  pl.BlockSpec((1, NSA_HPG, n_cmp, tq), lambda g, bi, i: (g, 0, 0, i)),
                  pl.BlockSpec((1, 1, NSA_HPG, WINDOW + tq, tq),
                               lambda g, bi, i: (g, jnp.minimum(i, n_win - 1), 0, 0, 0)),
                  pl.BlockSpec((1, bias_sel.shape[1], NSA_HPG, tk, tq), lambda g, bi, i: (g, 0, 0, 0, 0)),
                  _const_spec(ovl.shape)],
        out_specs=pl.BlockSpec((1, tq, NSA_HPG * dh), lambda g, bi, i: (bi, i, g)),
        out_shape=jax.ShapeDtypeStruct((b, t, d), ACT_DTYPE),
        scratch_shapes=[pltpu.VMEM((t // SEL_BLOCK, tq), F32)],
        compiler_params=_params(("arbitrary", "arbitrary", "arbitrary")),
        name="nsa_attention",
    )(q, gate_logits, k_cmp, v_cmp, kv, kv, kv, kv, bias_cmp, bias_win, bias_sel, ovl)


def _nsa_block(h, slots, route, kv_ln, kv_w, cmp_pe_k, cmp_pe_v, cmp_k_w1, cmp_k_w2, cmp_v_w1, cmp_v_w2,
               rel_bias, b_ln, b_wqg, b_wo, b, t):
    n, d = b * t, kv_ln.shape[0]
    h1, q, gate_logits, kvc, kv = _nsa_proj(h, slots, route, kv_ln, b_ln, kv_w, b_wqg, b, t)
    k_cmp, v_cmp = _compress(kvc, cmp_pe_k, cmp_pe_v, cmp_k_w1, cmp_k_w2, cmp_v_w1, cmp_v_w2)
    o = _nsa_attention(q, gate_logits, kv, k_cmp, v_cmp, rel_bias)
    return _matmul_res(o.reshape(n, d), _mx(b_wo), h1.reshape(n, d))


def kernel(x, a_ln, a_mu, a_wr, a_wk, a_wv, a_wo, a_w0, a_w1, a_w2, a_a0, a_a1, a_a2, a_g1, a_g2, a_kk, a_ka, a_rk, a_lnx_w, a_lnx_b, kv_ln, kv_w, cmp_pe_k, cmp_pe_v, cmp_k_w1, cmp_k_w2, cmp_v_w1, cmp_v_w2, rel_bias, b_ln, b_wqg, b_wo, m_ln, m_wg, m_bg, m_we, m_be, m_w1, m_w3, m_w2, final_ln):
    b, t, d = x.shape
    h = _rwkv_block(x, a_ln[0], a_mu[0], a_wr[0], a_wk[0], a_wv[0], a_wo[0], a_w0[0], a_w1[0], a_w2[0],
                    a_a0[0], a_a1[0], a_a2[0], a_g1[0], a_g2[0], a_kk[0], a_ka[0], a_rk[0],
                    a_lnx_w[0], a_lnx_b[0])
    moe = lambda hh, l: _moe_layer(hh, m_ln[l], m_wg[l], m_bg[l], m_we[l], m_be[l], m_w1[l], m_w3[l], m_w2[l])
    slots, route = moe(h, 0)
    h = _nsa_block(h, slots, route, kv_ln, kv_w, cmp_pe_k, cmp_pe_v, cmp_k_w1, cmp_k_w2, cmp_v_w1, cmp_v_w2,
                   rel_bias, b_ln[0], b_wqg[0], b_wo[0], b, t)
    slots, route = moe(h, 1)
    return _final(h, slots, route, final_ln).reshape(b, t, d)
```

```python
import functools
import math

import numpy as np
import jax
import jax.numpy as jnp
from jax import lax
from jax.experimental import pallas as pl
from jax.experimental.pallas import tpu as pltpu

F32 = jnp.float32
MXU_DTYPE = jnp.bfloat16
ACT_DTYPE = jnp.bfloat16

D_MODEL = 1024
NORM_EPS = 1e-6
RWKV_HEAD = 64
RWKV_HEADS = D_MODEL // RWKV_HEAD
RWKV_GN_EPS = 64e-5
RWKV_CHUNK = 64
LANES = 128
MXU_TILE = 256
VMEM_LIMIT = 56 * 1024 * 1024


def _mx(x):
    return x.astype(MXU_DTYPE)


def _dot(a, b):
    return jnp.dot(_mx(a), _mx(b), preferred_element_type=F32)


def _dot_nt(a, b):
    return lax.dot_general(_mx(a), _mx(b), (((1,), (1,)), ((), ())), preferred_element_type=F32)


def _split2(x):
    hi = x.astype(MXU_DTYPE)
    lo = (x - hi.astype(F32)).astype(MXU_DTYPE)
    return hi, lo


def _dot_hl(x, w):
    hi, lo = _split2(x)
    return jnp.dot(hi, w, preferred_element_type=F32) + jnp.dot(lo, w, preferred_element_type=F32)


def _rms(x, g):
    return x * lax.rsqrt(jnp.mean(x * x, axis=-1, keepdims=True) + NORM_EPS) * g


TOK_ROWS = 8


def _tok_load(ref):
    tm = ref.shape[0] // TOK_ROWS
    return jnp.concatenate([ref[pl.ds(s, tm, stride=TOK_ROWS), :] for s in range(TOK_ROWS)], axis=1)


def _tok_store(ref, val):
    tm = val.shape[0]
    for s in range(TOK_ROWS):
        ref[pl.ds(s, tm, stride=TOK_ROWS), :] = val[:, LANES * s:LANES * (s + 1)]


def _tok_spec(tm, index_map):
    return pl.BlockSpec((TOK_ROWS * tm, LANES), index_map)


def _const_spec(shape):
    nd = len(shape)
    return pl.BlockSpec(shape, lambda *_: (0,) * nd)


def _params(sem):
    return pltpu.CompilerParams(dimension_semantics=sem, vmem_limit_bytes=VMEM_LIMIT)


def _rwkv_pre_kernel(x_ref, ln_ref, mu_ref, wr_ref, wk_ref, wv_ref, w1_ref, a1_ref, g1_ref,
                     w2_ref, a2_ref, g2_ref, vec_ref,
                     r_out, k_out, v_out, kk_out, g_out, ld_out, a_out, prev_ref):
    tm = x_ref.shape[1]
    hn = _rms(x_ref[0], ln_ref[...])

    @pl.when(pl.program_id(1) == 0)
    def _():
        prev_ref[...] = jnp.zeros_like(prev_ref)

    rows = lax.broadcasted_iota(jnp.int32, (tm, 1), 0)
    shifted = jnp.where(rows == 0, prev_ref[0:1, :], pltpu.roll(hn, 1, axis=0))
    prev_ref[0:1, :] = hn[tm - 1:tm, :]
    xx = shifted - hn
    xr, xw, xk, xv, xa, xg = (_mx(hn + xx * mu_ref[i:i + 1, :]) for i in range(6))
    w0, a0, kkp, kap = (vec_ref[i:i + 1, :] for i in range(4))
    r = jnp.dot(xr, wr_ref[...], preferred_element_type=F32)
    k = jnp.dot(xk, wk_ref[...], preferred_element_type=F32)
    v = jnp.dot(xv, wv_ref[...], preferred_element_type=F32)
    z = w0 + _dot(jnp.tanh(jnp.dot(xw, w1_ref[...], preferred_element_type=F32)), w2_ref[...])
    a = jax.nn.sigmoid(a0 + _dot(jnp.dot(xa, a1_ref[...], preferred_element_type=F32), a2_ref[...]))
    g = _dot(jax.nn.sigmoid(jnp.dot(xg, g1_ref[...], preferred_element_type=F32)), g2_ref[...])
    ld_out[0] = (-math.exp(-0.5)) * jax.nn.sigmoid(z)
    a_out[0] = a
    r_out[0] = r.astype(r_out.dtype)
    v_out[0] = v.astype(v_out.dtype)
    kk_out[0] = (k * kkp).astype(kk_out.dtype)
    k_out[0] = (k * (1.0 + (a - 1.0) * kap)).astype(k_out.dtype)
    g_out[0] = g.astype(g_out.dtype)


def _rwkv_pre(x, ln, mu, wr, wk, wv, w1, a1, g1, w2, a2, g2, vecs, tm=256):
    b, t, d = x.shape
    act = jax.ShapeDtypeStruct((b, t, d), ACT_DTYPE)
    f32o = jax.ShapeDtypeStruct((b, t, d), F32)
    blk = pl.BlockSpec((1, tm, d), lambda i, j: (i, j, 0))
    consts = [ln, mu, wr, wk, wv, w1, a1, g1, w2, a2, g2, vecs]
    return pl.pallas_call(
        _rwkv_pre_kernel,
        grid=(b, t // tm),
        in_specs=[blk] + [_const_spec(c.shape) for c in consts],
        out_specs=[blk] * 7,
        out_shape=[act, act, act, act, act, f32o, f32o],
        scratch_shapes=[pltpu.VMEM((8, d), F32)],
        compiler_params=_params(("arbitrary", "arbitrary")),
        name="rwkv_pre",
    )(x, *consts)


def _stack4(x):
    return jnp.concatenate([x[:, MXU_TILE * q:MXU_TILE * (q + 1)] for q in range(4)], axis=0)


def _unstack4(y):
    c = y.shape[0] // 4
    return jnp.concatenate([y[c * q:c * (q + 1), :] for q in range(4)], axis=1)


def _rwkv_scan_kernel(r_ref, k_ref, v_ref, kk_ref, g_ref, ld_ref, a_ref,
                      ltri_ref, smask_ref, eye_ref, j_ref, rk_ref, lnw_ref, lnb_ref,
                      y_out, h_ref):
    c = RWKV_CHUNK

    @pl.when(pl.program_id(1) == 0)
    def _():
        h_ref[...] = jnp.zeros_like(h_ref)

    ld = ld_ref[0]
    r = r_ref[0].astype(F32)
    k = k_ref[0].astype(F32)
    v = v_ref[0].astype(F32)
    kk = kk_ref[0].astype(F32)
    jones = j_ref[...]
    ld_hi, ld_lo = _split2(ld)
    cum = (jnp.dot(ltri_ref[...], ld_hi, preferred_element_type=F32)
           + jnp.dot(ltri_ref[...], ld_lo, preferred_element_type=F32))
    cum_last = cum[c - 1:c, :]
    e_l = jnp.exp(cum)
    e_lm = jnp.exp(cum - ld)
    e_nl = jnp.exp(-cum)
    e_cl = jnp.exp(cum_last - cum)
    e_c = jnp.exp(cum_last)
    ss = _unstack4(_dot_hl(_stack4(kk * kk), jones))
    kkn = kk * lax.rsqrt(jnp.maximum(ss, 1e-24))
    a_vec = -kkn
    b_vec = kkn * a_ref[0]
    rt = r * e_l
    at = a_vec * e_lm
    kt = k * e_nl
    bt = b_vec * e_nl
    bh = b_vec * e_cl
    kh = k * e_cl

    lane = lax.broadcasted_iota(jnp.int32, (1, LANES), 1)
    m_l = (lane < RWKV_HEAD).astype(F32)
    m_r = 1.0 - m_l

    def st(xp):
        return jnp.concatenate([xp * m_l, xp * m_r], axis=0)

    def dup(xp):
        return jnp.concatenate([xp, xp], axis=0)

    smask = smask_ref[...] > 0.0
    eye = eye_ref[...]
    pairs = range(D_MODEL // LANES)
    sls = [slice(LANES * p, LANES * (p + 1)) for p in pairs]
    at_s = [st(at[:, sl]) for sl in sls]
    rt_s = [st(rt[:, sl]) for sl in sls]
    v_s = [st(v[:, sl]) for sl in sls]
    s = [jnp.where(smask,
                   _dot_nt(jnp.concatenate([at_s[p], rt_s[p]], axis=0),
                           jnp.concatenate([dup(bt[:, sls[p]]), dup(kt[:, sls[p]])], axis=0)),
                   0.0) for p in pairs]
    a_ab = [x[:2 * c, :2 * c] for x in s]
    a_r = [x[2 * c:, :] for x in s]
    g1 = [_dot(s[p][:2 * c, 2 * c:], v_s[p]) for p in pairs]
    t_m = [eye + x for x in a_ab]
    a_pow = [_dot(x, x) for x in a_ab]
    n = 2
    while n < c:
        if 2 * n < c:
            both = [_dot(a_pow[p], jnp.concatenate([a_pow[p], t_m[p]], axis=1)) for p in pairs]
            a_pow = [x[:, :2 * c] for x in both]
            t_m = [t_m[p] + both[p][:, 2 * c:] for p in pairs]
        else:
            t_m = [t_m[p] + _dot(a_pow[p], t_m[p]) for p in pairs]
        n *= 2
    tw = [_dot(t_m[p], jnp.concatenate([at_s[p], g1[p]], axis=1)) for p in pairs]
    zmat = [jnp.concatenate([tw[p], jnp.concatenate([jnp.zeros_like(v_s[p]), v_s[p]], axis=1)], axis=0)
            for p in pairs]
    upper = [_dot(a_r[p], zmat[p]) for p in pairs]
    bk_t = [jnp.concatenate([st(bh[:, sl]), st(kh[:, sl])], axis=0).T for sl in sls]
    lower = [_dot(bk_t[p], zmat[p]) for p in pairs]
    seq = [_dot(jnp.concatenate([upper[p][:, :LANES] + rt_s[p],
                                 lower[p][:, :LANES] + eye * e_c[:, sls[p]]], axis=0), h_ref[p])
           for p in pairs]
    ys = []
    for p in pairs:
        y_st = seq[p][:2 * c, :] + upper[p][:, LANES:]
        h_ref[p] = seq[p][2 * c:, :] + lower[p][:, LANES:]
        ys.append(y_st[:c, :] + y_st[c:, :])
    y = jnp.concatenate(ys, axis=1)

    inv_n = 1.0 / RWKV_HEAD
    y4 = _stack4(y)
    dlt = y4 - _dot_hl(y4, jones) * inv_n
    var = _dot_hl(dlt * dlt, jones) * inv_n
    yn = dlt * lax.rsqrt(var + RWKV_GN_EPS)
    bonus = _dot_hl(_stack4(r * k) * rk_ref[...], jones) * _stack4(v)
    out = (yn * lnw_ref[...] + lnb_ref[...] + bonus) * _stack4(g_ref[0].astype(F32))
    y_out[0] = _unstack4(out).astype(y_out.dtype)


def _scan_constants():
    c = RWKV_CHUNK
    ltri = np.tril(np.ones((c, c), np.float32))
    rho = np.arange(4 * c)
    r_type, r_head, r_t = rho // (2 * c), (rho % (2 * c)) // c, rho % c
    same = r_head[:, None] == r_head[None, :]
    strict = r_t[None, :] < r_t[:, None]
    incl = r_t[None, :] <= r_t[:, None]
    smask = same & np.where(r_type[:, None] == 0, strict, incl)
    eye = np.eye(LANES, dtype=np.float32)
    head = np.arange(MXU_TILE) // RWKV_HEAD
    jones = (head[:, None] == head[None, :]).astype(np.float32)
    return (jnp.asarray(ltri, MXU_DTYPE), jnp.asarray(smask, F32), jnp.asarray(eye, F32),
            jnp.asarray(jones, MXU_DTYPE))


def _stack_param(p):
    return jnp.repeat(p.reshape(4, MXU_TILE).astype(F32), RWKV_CHUNK, axis=0)


def _rwkv_scan(r, k, v, kk, g, ld, a, rk, lnw, lnb):
    b, t, d = r.shape
    c = RWKV_CHUNK
    blk = pl.BlockSpec((1, c, d), lambda i, j: (i, j, 0))
    consts = list(_scan_constants()) + [_stack_param(rk.reshape(-1)), _stack_param(lnw), _stack_param(lnb)]
    return pl.pallas_call(
        _rwkv_scan_kernel,
        grid=(b, t // c),
        in_specs=[blk] * 7 + [_const_spec(x.shape) for x in consts],
        out_specs=blk,
        out_shape=jax.ShapeDtypeStruct((b, t, d), ACT_DTYPE),
        scratch_shapes=[pltpu.VMEM((d // LANES, LANES, LANES), F32)],
        compiler_params=_params(("arbitrary", "arbitrary")),
        name="rwkv_scan",
    )(r, k, v, kk, g, ld, a, *consts)


def _matmul_res_kernel(a_ref, w_ref, res_ref, o_ref):
    _tok_store(o_ref, res_ref[...] + jnp.dot(a_ref[...], w_ref[...], preferred_element_type=F32))


def _matmul_res(a, w, res, tm=512):
    n, kdim = a.shape
    d = w.shape[1]
    return pl.pallas_call(
        _matmul_res_kernel,
        grid=(n // tm,),
        in_specs=[pl.BlockSpec((tm, kdim), lambda i: (i, 0)), _const_spec(w.shape),
                  pl.BlockSpec((tm, d), lambda i: (i, 0))],
        out_specs=_tok_spec(tm, lambda i: (i, 0)),
        out_shape=jax.ShapeDtypeStruct((TOK_ROWS * n, LANES), F32),
        compiler_params=_params(("arbitrary",)),
        name="matmul_res",
    )(a, w, res)


def _rwkv_block(x, a_ln, a_mu, a_wr, a_wk, a_wv, a_wo, a_w0, a_w1, a_w2, a_a0, a_a1, a_a2,
                a_g1, a_g2, a_kk, a_ka, a_rk, a_lnx_w, a_lnx_b):
    b, t, d = x.shape
    vecs = jnp.stack([a_w0, a_a0, a_kk, a_ka]).astype(F32)
    r, k, v, kk, g, ld, a = _rwkv_pre(
        x, a_ln.reshape(1, d), a_mu, _mx(a_wr), _mx(a_wk), _mx(a_wv), _mx(a_w1), _mx(a_a1), _mx(a_g1),
        _mx(a_w2), _mx(a_a2), _mx(a_g2), vecs)
    y = _rwkv_scan(r, k, v, kk, g, ld, a, a_rk, a_lnx_w, a_lnx_b)
    return _matmul_res(y.reshape(b * t, d), _mx(a_wo), x.reshape(b * t, d))


MOE_GROUPS = 4
MOE_EPG = 8
MOE_EXPERTS = MOE_GROUPS * MOE_EPG
MOE_TOPK = 2
MOE_BLOCK = 256
NEG = -1e30


def _route(logits):
    lane = lax.broadcasted_iota(jnp.int32, logits.shape, 1)
    is_grp = lane < MOE_GROUPS
    gl = jnp.where(is_grp, logits, NEG)
    ge = jnp.exp(gl - jnp.max(gl, axis=-1, keepdims=True))
    gp = ge / jnp.sum(ge, axis=-1, keepdims=True)
    grp_w = jnp.max(gp, axis=-1, keepdims=True)
    grp = jnp.min(jnp.where(is_grp & (gp == grp_w), lane, LANES), axis=-1, keepdims=True)
    lo = MOE_GROUPS + MOE_EPG * grp
    in_grp = (lane >= lo) & (lane < lo + MOE_EPG)
    el = jnp.where(in_grp, logits, NEG)
    ee = jnp.exp(el - jnp.max(el, axis=-1, keepdims=True))
    ep = ee / jnp.sum(ee, axis=-1, keepdims=True)
    p1 = jnp.max(jnp.where(in_grp, ep, -1.0), axis=-1, keepdims=True)
    i1 = jnp.min(jnp.where(in_grp & (ep == p1), lane, LANES), axis=-1, keepdims=True)
    rest = in_grp & (lane != i1)
    p2 = jnp.max(jnp.where(rest, ep, -1.0), axis=-1, keepdims=True)
    i2 = jnp.min(jnp.where(rest & (ep == p2), lane, LANES), axis=-1, keepdims=True)
    scale = grp_w / (p1 + p2)
    return i1 - MOE_GROUPS, i2 - MOE_GROUPS, p1 * scale, p2 * scale


def _router_logits(xf, w_hi_ref, w_lo_ref, bias_ref):
    hi, lo = _split2(xf)
    logits = (jnp.dot(hi, w_hi_ref[...], preferred_element_type=F32)
              + jnp.dot(lo, w_hi_ref[...], preferred_element_type=F32)
              + jnp.dot(hi, w_lo_ref[...], preferred_element_type=F32))
    return logits + bias_ref[...]


def _moe_router_kernel(h_ref, ln_ref, w_hi_ref, w_lo_ref, bias_ref, tri_ref, route_out, count_out, cnt_ref):
    @pl.when(pl.program_id(0) == 0)
    def _():
        cnt_ref[...] = jnp.zeros_like(cnt_ref)

    xf = _rms(_tok_load(h_ref), ln_ref[...])
    e1, e2, w1, w2 = _route(_router_logits(xf, w_hi_ref, w_lo_ref, bias_ref))
    lane = lax.broadcasted_iota(jnp.int32, (xf.shape[0], LANES), 1)
    hit1, hit2 = lane == e1, lane == e2
    onehot = jnp.where(hit1, 1.0, jnp.where(hit2, 1.0, 0.0))
    before = jnp.dot(tri_ref[...], _mx(onehot), preferred_element_type=F32) + cnt_ref[...]
    r1 = jnp.sum(jnp.where(hit1, before, 0.0), axis=-1, keepdims=True)
    r2 = jnp.sum(jnp.where(hit2, before, 0.0), axis=-1, keepdims=True)
    cnt_ref[...] = cnt_ref[...] + jnp.sum(onehot, axis=0, keepdims=True)
    count_out[...] = cnt_ref[...]
    fields = (e1.astype(F32), e2.astype(F32), w1, w2, r1, r2)
    rec = jnp.zeros(lane.shape, F32)
    for idx, val in enumerate(fields):
        rec = jnp.where(lane == idx, val, rec)
    route_out[...] = rec


def _router_weights(w_gr, b_gr, w_er, b_er):
    d = w_gr.shape[0]
    w = jnp.zeros((d, LANES), F32).at[:, :MOE_GROUPS].set(w_gr).at[:, MOE_GROUPS:MOE_GROUPS + MOE_EXPERTS].set(w_er)
    bias = jnp.zeros((1, LANES), F32).at[0, :MOE_GROUPS].set(b_gr).at[0, MOE_GROUPS:MOE_GROUPS + MOE_EXPERTS].set(b_er)
    w_hi = w.astype(MXU_DTYPE)
    w_lo = (w - w_hi.astype(F32)).astype(MXU_DTYPE)
    return w_hi, w_lo, bias


def _moe_router(h, ln, w_hi, w_lo, bias, tm=512):
    n, d = h.shape[0] // TOK_ROWS, ln.shape[0]
    tri = jnp.asarray(np.tril(np.ones((tm, tm), np.float32), -1), MXU_DTYPE)
    return pl.pallas_call(
        _moe_router_kernel,
        grid=(n // tm,),
        in_specs=[_tok_spec(tm, lambda i: (i, 0)), _const_spec((1, d)), _const_spec(w_hi.shape),
                  _const_spec(w_lo.shape),
                  _const_spec(bias.shape), _const_spec(tri.shape)],
        out_specs=[pl.BlockSpec((tm, LANES), lambda i: (i, 0)), _const_spec((1, LANES))],
        out_shape=[jax.ShapeDtypeStruct((n, LANES), F32), jax.ShapeDtypeStruct((1, LANES), F32)],
        scratch_shapes=[pltpu.VMEM((1, LANES), F32)],
        compiler_params=_params(("arbitrary",)),
        name="moe_router",
    )(h, ln.reshape(1, d), w_hi, w_lo, bias, tri)


def _moe_plan(route, counts, blk):
    n = route.shape[0]
    counts = counts[0, :MOE_EXPERTS].astype(jnp.int32)
    pcounts = (counts + blk - 1) // blk * blk
    pend = jnp.cumsum(pcounts)
    pstart = (pend - pcounts).astype(jnp.int32)
    n_blk = n * MOE_TOPK // blk + MOE_EXPERTS
    blk_start = jnp.arange(n_blk, dtype=jnp.int32) * blk
    blk_e = jnp.sum((blk_start[:, None] >= pend[None, :]).astype(jnp.int32), axis=1)
    blk_e = jnp.minimum(blk_e, MOE_EXPERTS - 1)
    n_valid = jnp.clip((pstart + counts)[blk_e] - blk_start, 0, blk).astype(jnp.int32)
    n_act = (pend[-1] // blk).astype(jnp.int32).reshape(1)
    expert = route[:, 0:MOE_TOPK].astype(jnp.int32)
    onehot = expert[:, :, None] == jnp.arange(MOE_EXPERTS, dtype=jnp.int32)
    seg_start = jnp.sum(jnp.where(onehot, pstart, 0), axis=-1)
    sorted_pos = seg_start + route[:, 4:4 + MOE_TOPK].astype(jnp.int32)
    return blk_e, n_valid, n_act, sorted_pos


def _token_copies(pos_ref, tm, make_copy):
    group = 8

    def body(g, carry):
        t0 = g * group
        pos = [[pos_ref[0, 0, MOE_TOPK * (t0 + j) + k] for k in range(MOE_TOPK)] for j in range(group)]
        for j in range(group):
            for k in range(MOE_TOPK):
                make_copy(t0 + j, k, pos[j][k]).start()
        return carry

    lax.fori_loop(0, tm // group, body, 0)


def _tile(ref, tok):
    return ref.at[pl.ds(pl.multiple_of(tok * TOK_ROWS, TOK_ROWS), TOK_ROWS)]


def _moe_dispatch_kernel(n_valid_ref, pos_ref, src_ref, dst_hbm, sem, zbuf, zsem):
    tm = src_ref.shape[0] // TOK_ROWS

    @pl.when(pl.program_id(0) == 0)
    def _():
        rows = zbuf.shape[0]
        zbuf[...] = jnp.zeros_like(zbuf)

        def fill(j, wait):
            @pl.when(n_valid_ref[j] < rows // TOK_ROWS)
            def _():
                start = 0 if wait else pl.multiple_of(j * rows, rows)
                cp = pltpu.make_async_copy(zbuf, dst_hbm.at[pl.ds(start, rows)], zsem)
                cp.wait() if wait else cp.start()

        lax.fori_loop(0, n_valid_ref.shape[0], lambda j, c: (fill(j, False), c)[1], 0)
        lax.fori_loop(0, n_valid_ref.shape[0], lambda j, c: (fill(j, True), c)[1], 0)

    _token_copies(pos_ref, tm, lambda t, k, pos: pltpu.make_async_copy(_tile(src_ref, t), _tile(dst_hbm, pos), sem))
    for k in range(MOE_TOPK):
        pltpu.make_async_copy(src_ref, dst_hbm.at[pl.ds(0, src_ref.shape[0])], sem).wait()


def _moe_gather_combine(step, n_steps, pos_ref, pos_next_ref, ys_hbm, h_ref, route, gbuf, gsem):
    tm = gbuf.shape[2] // TOK_ROWS
    slot = lax.rem(step, 2)

    def start(p_ref, s):
        _token_copies(p_ref, tm, lambda t, k, pos: pltpu.make_async_copy(
            _tile(ys_hbm, pos), _tile(gbuf.at[s, k], t), gsem.at[s]))

    @pl.when(step == 0)
    def _():
        start(pos_ref, 0)

    @pl.when(step + 1 < n_steps)
    def _():
        start(pos_next_ref, 1 - slot)

    for k in range(MOE_TOPK):
        pltpu.make_async_copy(ys_hbm.at[pl.ds(0, gbuf.shape[2])], gbuf.at[slot, k], gsem.at[slot]).wait()
    return (_tok_load(h_ref) + route[:, 2:3] * _tok_load(gbuf.at[slot, 0])
            + route[:, 3:4] * _tok_load(gbuf.at[slot, 1]))


def _gather_specs(sorted_pos, tm, step_index):
    n = sorted_pos.shape[0]
    pos = sorted_pos.reshape(n // tm, 1, MOE_TOPK * tm)
    last = n // tm - 1
    cur = pl.BlockSpec((1, 1, MOE_TOPK * tm), lambda *g: (step_index(*g), 0, 0), memory_space=pltpu.SMEM)
    nxt = pl.BlockSpec((1, 1, MOE_TOPK * tm), lambda *g: (jnp.minimum(step_index(*g) + 1, last), 0, 0),
                       memory_space=pltpu.SMEM)
    scratch = [pltpu.VMEM((2, MOE_TOPK, TOK_ROWS * tm, LANES), F32), pltpu.SemaphoreType.DMA((2,))]
    return pos, [cur, nxt, pl.BlockSpec(memory_space=pl.ANY)], scratch


def _moe_dispatch(h, n_valid, sorted_pos, p, blk, tm=512):
    n = sorted_pos.shape[0]
    pos = sorted_pos.reshape(n // tm, 1, MOE_TOPK * tm)
    grid_spec = pltpu.PrefetchScalarGridSpec(
        num_scalar_prefetch=1,
        grid=(n // tm,),
        in_specs=[pl.BlockSpec((1, 1, MOE_TOPK * tm), lambda i, nv: (i, 0, 0), memory_space=pltpu.SMEM),
                  _tok_spec(tm, lambda i, nv: (i, 0))],
        out_specs=pl.BlockSpec(memory_space=pl.ANY),
        scratch_shapes=[pltpu.SemaphoreType.DMA, pltpu.VMEM((TOK_ROWS * blk, LANES), h.dtype),
                        pltpu.SemaphoreType.DMA],
    )
    return pl.pallas_call(
        _moe_dispatch_kernel,
        grid_spec=grid_spec,
        out_shape=jax.ShapeDtypeStruct((TOK_ROWS * p, LANES), h.dtype),
        compiler_params=_params(("arbitrary",)),
        name="moe_dispatch",
    )(n_valid, pos, h)


def _moe_expert_kernel(blk_e_ref, n_act_ref, x_ref, ln_ref, w1_ref, w3_ref, w2_ref, y_ref):
    del blk_e_ref
    i = pl.program_id(0)

    @pl.when(i < n_act_ref[0])
    def _():
        xn = _mx(_rms(_tok_load(x_ref), ln_ref[...]))
        h1 = jnp.dot(xn, w1_ref[0], preferred_element_type=F32)
        h3 = jnp.dot(xn, w3_ref[0], preferred_element_type=F32)
        _tok_store(y_ref, _dot(h1 * jax.nn.sigmoid(h1) * h3, w2_ref[0]))

    @pl.when(i >= n_act_ref[0])
    def _():
        y_ref[...] = jnp.zeros_like(y_ref)


def _moe_experts(xs, ln, w1, w3, w2, blk_e, n_act, blk):
    p = xs.shape[0] // TOK_ROWS
    d, hdim = w1.shape[1], w1.shape[2]
    grid_spec = pltpu.PrefetchScalarGridSpec(
        num_scalar_prefetch=2,
        grid=(p // blk,),
        in_specs=[_tok_spec(blk, lambda i, be, na: (i, 0)),
                  pl.BlockSpec((1, d), lambda i, be, na: (0, 0)),
                  pl.BlockSpec((1, d, hdim), lambda i, be, na: (be[i], 0, 0)),
                  pl.BlockSpec((1, d, hdim), lambda i, be, na: (be[i], 0, 0)),
                  pl.BlockSpec((1, hdim, d), lambda i, be, na: (be[i], 0, 0))],
        out_specs=_tok_spec(blk, lambda i, be, na: (i, 0)),
    )
    return pl.pallas_call(
        _moe_expert_kernel,
        grid_spec=grid_spec,
        out_shape=jax.ShapeDtypeStruct(xs.shape, F32),
        compiler_params=_params(("arbitrary",)),
        name="moe_experts",
    )(blk_e, n_act, xs, ln.reshape(1, d), w1, w3, w2)


def _moe_layer(h, ln, w_gr, b_gr, w_er, b_er, w1, w3, w2):
    n = h.shape[0] // TOK_ROWS
    blk = MOE_BLOCK
    route, counts = _moe_router(h, ln, *_router_weights(w_gr, b_gr, w_er, b_er))
    blk_e, n_valid, n_act, sorted_pos = _moe_plan(route, counts, blk)
    p = n * MOE_TOPK + MOE_EXPERTS * blk
    xs = _moe_dispatch(h, n_valid, sorted_pos, p, blk)
    ys = _moe_experts(xs, ln, _mx(w1), _mx(w3), _mx(w2), blk_e, n_act, blk)
    return ys, sorted_pos, route


def _final_kernel(pos_ref, pos_next_ref, ys_hbm, h_ref, route_ref, ln_ref, o_ref, gbuf, gsem):
    h = _moe_gather_combine(pl.program_id(0), pl.num_programs(0), pos_ref, pos_next_ref, ys_hbm, h_ref,
                            route_ref[...], gbuf, gsem)
    o_ref[...] = _rms(h, ln_ref[...])


def _final(h, moe_out, ln, tm=512):
    ys, sorted_pos, route = moe_out
    n, d = route.shape[0], ln.shape[0]
    pos, gather_specs, scratch = _gather_specs(sorted_pos, tm, lambda i: i)
    return pl.pallas_call(
        _final_kernel,
        grid=(n // tm,),
        in_specs=gather_specs + [_tok_spec(tm, lambda i: (i, 0)), pl.BlockSpec((tm, LANES), lambda i: (i, 0)),
                                 _const_spec((1, d))],
        out_specs=pl.BlockSpec((tm, d), lambda i: (i, 0)),
        out_shape=jax.ShapeDtypeStruct((n, d), F32),
        scratch_shapes=scratch,
        compiler_params=_params(("arbitrary",)),
        name="final_norm",
    )(pos, pos, ys, h, route, ln.reshape(1, d))


NSA_HEADS = 16
NSA_GROUPS = 4
NSA_HPG = NSA_HEADS // NSA_GROUPS
NSA_DH = D_MODEL // NSA_HEADS
NSA_PARTS = 6
CMP_BLOCK = 32
CMP_STRIDE = 16
SEL_BLOCK = 64
SEL_TOPK = 8
WINDOW = 512
N_BRANCH = 3
IMP_FORCE = 1e4
REL_BUCKETS = 32
REL_MAX_DIST = 128
LOG2E = math.log2(math.e)
NSA_TQ = 256
NSA_TK = 512
NSA_ROW_BLOCK = 32
GATE_LANES = LANES
CMP_PARTS = 2


def _nsa_proj_kernel(pos_ref, pos_next_ref, ys_hbm, h_ref, route_ref, kvln_ref, qln_ref, kvw_ref, qw_ref,
                     h_out, q_out, gate_out, kvc_out, kv_out, gbuf, gsem):
    step = pl.program_id(0) * pl.num_programs(1) + pl.program_id(1)
    h1 = _moe_gather_combine(step, pl.num_programs(0) * pl.num_programs(1), pos_ref, pos_next_ref, ys_hbm, h_ref,
                             route_ref[0], gbuf, gsem)
    h_out[0] = h1
    kv = _dot(_rms(h1, kvln_ref[...]), kvw_ref[...])
    for part in range(NSA_PARTS):
        for g in range(NSA_GROUPS):
            col = (part * NSA_GROUPS + g) * NSA_DH
            piece = kv[:, col:col + NSA_DH]
            if part < CMP_PARTS:
                kvc_out[part, 0, g] = piece
            else:
                kv_out[part - CMP_PARTS, 0, g] = piece.astype(kv_out.dtype)
    qg = _dot(_rms(h1, qln_ref[...]), qw_ref[...])
    q_out[0] = (qg[:, :D_MODEL] * (NSA_DH ** -0.5 * LOG2E)).astype(q_out.dtype)
    gate_out[0] = qg[:, D_MODEL:]


def _gate_padded_weight(q_w):
    d = q_w.shape[0]
    n_gate = NSA_HPG * N_BRANCH
    gates = q_w[:, d:].reshape(d, NSA_GROUPS, n_gate)
    gates = jnp.pad(gates, ((0, 0), (0, 0), (0, GATE_LANES - n_gate))).reshape(d, NSA_GROUPS * GATE_LANES)
    return jnp.concatenate([q_w[:, :d], gates], axis=1)


def _nsa_proj(h, moe_out, kv_ln, q_ln, kv_w, q_w, b, t, tm=256):
    ys, sorted_pos, route = moe_out
    d = kv_ln.shape[0]
    blk = lambda w: pl.BlockSpec((1, tm, w), lambda i, j: (i, j, 0))
    per_b = t // tm
    tok_block = lambda i, j: i * per_b + j
    pos, gather_specs, scratch = _gather_specs(sorted_pos, tm, tok_block)
    gw = NSA_GROUPS * GATE_LANES
    qw = _gate_padded_weight(q_w)
    kv_spec = lambda parts: pl.BlockSpec((parts, 1, NSA_GROUPS, tm, NSA_DH), lambda i, j: (0, i, 0, j, 0))
    kv_shape = lambda parts: (parts, b, NSA_GROUPS, t, NSA_DH)
    return pl.pallas_call(
        _nsa_proj_kernel,
        grid=(b, per_b),
        in_specs=gather_specs + [_tok_spec(tm, lambda i, j: (tok_block(i, j), 0)), blk(LANES),
                                 _const_spec((1, d)), _const_spec((1, d)),
                                 _const_spec(kv_w.shape), _const_spec(qw.shape)],
        out_specs=[blk(d), blk(d), blk(gw), kv_spec(CMP_PARTS), kv_spec(NSA_PARTS - CMP_PARTS)],
        out_shape=[jax.ShapeDtypeStruct((b, t, d), F32), jax.ShapeDtypeStruct((b, t, d), ACT_DTYPE),
                   jax.ShapeDtypeStruct((b, t, gw), F32), jax.ShapeDtypeStruct(kv_shape(CMP_PARTS), F32),
                   jax.ShapeDtypeStruct(kv_shape(NSA_PARTS - CMP_PARTS), ACT_DTYPE)],
        scratch_shapes=scratch,
        compiler_params=_params(("arbitrary", "arbitrary")),
        name="nsa_proj",
    )(pos, pos, ys, h, route.reshape(b, t, LANES),
      kv_ln.reshape(1, d), q_ln.reshape(1, d), _mx(kv_w), _mx(qw))


def _gelu_tanh(x):
    return 0.5 * x * (1.0 + jnp.tanh(math.sqrt(2.0 / math.pi) * (x + 0.044715 * (x * x * x))))


def _compress_kernel(k_ref, v_ref, pek_ref, pev_ref, kw1_ref, kw2_ref, vw1_ref, vw2_ref, kc_out, vc_out):
    dh = k_ref.shape[-1]
    n_chunk = k_ref.shape[3] // CMP_STRIDE

    def one(z_ref, pe_ref, w1_ref, w2_ref, out):
        first = jnp.zeros((n_chunk, w1_ref.shape[1]), F32)
        second = first
        for l in range(CMP_STRIDE):
            z = z_ref[0, 0, 0, pl.ds(l, n_chunk, stride=CMP_STRIDE), :]
            lo, hi = l, CMP_STRIDE + l
            first = first + _dot(z + pe_ref[0, lo:lo + 1, :], w1_ref[dh * lo:dh * (lo + 1), :])
            second = second + _dot(z + pe_ref[0, hi:hi + 1, :], w1_ref[dh * hi:dh * (hi + 1), :])
        hid = first + pltpu.roll(second, n_chunk - 1, axis=0)
        out[0, 0] = _dot(_gelu_tanh(hid), w2_ref[...]).astype(out.dtype)

    one(k_ref, pek_ref, kw1_ref, kw2_ref, kc_out)
    one(v_ref, pev_ref, vw1_ref, vw2_ref, vc_out)


def _compress(kvc, pe_k, pe_v, k_w1, k_w2, v_w1, v_w2):
    _, b, g, t, dh = kvc.shape
    n_chunk = t // CMP_STRIDE
    part = lambda idx: pl.BlockSpec((1, 1, 1, t, dh), lambda i, j: (idx, i, j, 0, 0))
    pe_spec = pl.BlockSpec((1, CMP_BLOCK, dh), lambda i, j: (j, 0, 0))
    out_spec = pl.BlockSpec((1, 1, n_chunk, dh), lambda i, j: (i, j, 0, 0))
    out = jax.ShapeDtypeStruct((b, g, n_chunk, dh), ACT_DTYPE)
    return pl.pallas_call(
        _compress_kernel,
        grid=(b, g),
        in_specs=[part(0), part(1), pe_spec, pe_spec, _const_spec(k_w1.shape), _const_spec(k_w2.shape),
                  _const_spec(v_w1.shape), _const_spec(v_w2.shape)],
        out_specs=[out_spec, out_spec],
        out_shape=[out, out],
        compiler_params=_params(("arbitrary", "arbitrary")),
        name="nsa_compress",
    )(kvc, kvc, pe_k.astype(F32), pe_v.astype(F32), _mx(k_w1), _mx(k_w2), _mx(v_w1), _mx(v_w2))


def _rel_bucket(dist):
    n = jnp.maximum(dist, 0)
    max_exact = REL_BUCKETS // 2
    nf = jnp.maximum(n, 1).astype(F32)
    large = max_exact + (jnp.log(nf / max_exact) / math.log(REL_MAX_DIST / max_exact)
                         * (REL_BUCKETS - max_exact)).astype(jnp.int32)
    return jnp.where(n < max_exact, n, jnp.minimum(large, REL_BUCKETS - 1))


def _toeplitz(vals, rows, cols):
    span = rows + cols
    z = jnp.concatenate([vals, vals[..., :1]], axis=-1)
    flat = jnp.tile(z, rows)[..., :rows * (span - 1)]
    skew = flat.reshape(vals.shape[:-1] + (rows, span - 1))
    return skew[..., rows - 1:rows - 1 + cols]


def _nsa_bias_tiles(rel_bias, t):
    tq, tk = NSA_TQ, NSA_TK
    table = rel_bias.astype(F32).reshape(REL_BUCKETS, NSA_GROUPS, NSA_HPG).transpose(1, 2, 0)
    by_dist = table[:, :, _rel_bucket(jnp.arange(REL_MAX_DIST + 1))]
    far = by_dist[:, :, REL_MAX_DIST:]

    def band(dist, valid, shift=None):
        b = by_dist[:, :, np.clip(dist, 0, REL_MAX_DIST)]
        if shift is not None:
            b = b - shift
        return jnp.where(jnp.asarray(valid), b * LOG2E, NEG)

    u = lambda rows, cols: np.arange(rows + cols - 1) - (rows - 1)
    wk = WINDOW + tq
    dist = np.stack([v * tq - max(v * tq - WINDOW, 0) + u(wk, tq) for v in range(WINDOW // tq + 1)])
    bias_win = _toeplitz(band(dist, (dist >= 0) & (dist < WINDOW)), wk, tq)
    n_near = (REL_MAX_DIST + tk - 1) // tq + 1
    dist = np.stack([e * tq + u(tk, tq) for e in range(n_near)])
    near = _toeplitz(band(dist, dist >= 0, far[:, :, None]), tk, tq)
    bias_sel = jnp.concatenate([near, jnp.zeros_like(near[:, :, :1])], axis=2)
    n_cmp = t // CMP_STRIDE
    dist = np.stack([b - (CMP_BLOCK - 1) + CMP_STRIDE * u(n_cmp, n_cmp) for b in range(CMP_STRIDE)])
    per_b = _toeplitz(band(dist, dist >= 0), n_cmp, n_cmp)
    bias_cmp = jnp.moveaxis(per_b, 2, 4).reshape(NSA_GROUPS, NSA_HPG, n_cmp, t)
    real = np.arange(n_cmp) < (t - CMP_BLOCK) // CMP_STRIDE + 1
    bias_cmp = jnp.where(jnp.asarray(real)[:, None], bias_cmp, NEG)
    return bias_cmp, bias_win, bias_sel


def _nsa_constants(t):
    n_sel = t // SEL_BLOCK
    n_cmp = t // CMP_STRIDE
    cmp_start = np.arange(n_cmp) * CMP_STRIDE
    sel_start = np.arange(n_sel) * SEL_BLOCK
    overlap = ((cmp_start[:, None] < sel_start[None, :] + SEL_BLOCK)
               & (cmp_start[:, None] + CMP_BLOCK > sel_start[None, :]))
    overlap[(t - CMP_BLOCK) // CMP_STRIDE + 1:, :] = False
    return jnp.asarray(overlap.T, MXU_DTYPE)


def _nsa_attn_kernel(q_ref, gate_ref, kc_ref, vc_ref, ks_ref, vs_ref, kw_ref, vw_ref,
                     bcmp_ref, bwin_ref, bsel_ref, ovl_ref, o_ref, neg_ref):
    tq, tk, dh = NSA_TQ, NSA_TK, NSA_DH
    i = pl.program_id(2)
    t0 = i * tq
    heads = range(NSA_HPG)
    q_t = q_ref[0].astype(F32).T
    qh = [_mx(q_t[dh * h:dh * (h + 1), :]) for h in heads]

    def values_t(v):
        ones_col = (lax.broadcasted_iota(jnp.int32, v.shape, 1) == 0).astype(F32)
        return _mx(jnp.concatenate([v.astype(F32), ones_col], axis=1).T)

    k_cmp = kc_ref[0, 0]
    lc = [jnp.dot(k_cmp, qh[h], preferred_element_type=F32) + bcmp_ref[0, h] for h in heads]
    mc = [jnp.max(x, axis=0, keepdims=True) for x in lc]
    pcu = [jnp.where(mc[h] > 0.5 * NEG, jnp.exp2(lc[h] - mc[h]), 0.0) for h in heads]
    pc = [x / jnp.maximum(jnp.sum(x, axis=0, keepdims=True), 1e-30) for x in pcu]
    v_cmp_t = values_t(vc_ref[0, 0])[:dh]
    oc = [_dot(v_cmp_t, x) for x in pc]

    pcsum = (pc[0] + pc[1]) + (pc[2] + pc[3])
    hi = pcsum.astype(MXU_DTYPE)
    mid = (pcsum - hi.astype(F32))
    lo = (mid - mid.astype(MXU_DTYPE).astype(F32)).astype(MXU_DTYPE)
    ovl = ovl_ref[...]
    imp = (jnp.dot(ovl, hi, preferred_element_type=F32) + jnp.dot(ovl, mid.astype(MXU_DTYPE), preferred_element_type=F32)
           + jnp.dot(ovl, lo, preferred_element_type=F32))
    n_sel = imp.shape[0]
    jj = lax.broadcasted_iota(jnp.int32, (n_sel, 1), 0)
    blk_q = lax.shift_right_logical(t0 + lax.broadcasted_iota(jnp.int32, (1, tq), 1),
                                    int(math.log2(SEL_BLOCK)))
    forced = (jj == 0) | (jj == blk_q) | (jj == blk_q - 1)
    score = jnp.where(forced, IMP_FORCE, jnp.where(jj <= blk_q, imp, -1.0))
    rank = jnp.zeros_like(score)
    for j in range(n_sel):
        row = score[j:j + 1, :]
        tie = (jj > j).astype(F32)
        rank = rank + jnp.where(row > score, 1.0, jnp.where(row == score, tie, 0.0))
    neg_ref[...] = jnp.where(rank < SEL_TOPK, 0.0, NEG)

    def normalised(acc):
        return acc[:dh] / acc[dh:dh + 1]

    per_step = tk // SEL_BLOCK

    def sel_step(c, carry):
        ms, accs = carry
        start = pl.multiple_of(c * tk, tk)
        k_c = ks_ref[0, 0, 0, pl.ds(start, tk), :]
        v_c = values_t(vs_ref[0, 0, 0, pl.ds(start, tk), :])
        neg = jnp.concatenate([jnp.broadcast_to(neg_ref[pl.ds(per_step * c + r, 1), :], (SEL_BLOCK, tq))
                               for r in range(per_step)], axis=0)
        near = jnp.minimum((t0 - start) // tq, bsel_ref.shape[2] - 1)
        x = [jnp.dot(k_c, qh[h], preferred_element_type=F32) + bsel_ref[0, h, near] + neg for h in heads]
        m_new = [jnp.maximum(ms[h], jnp.max(x[h], axis=0, keepdims=True)) for h in heads]
        acc_new = [jnp.exp2(ms[h] - m_new[h]) * accs[h] + _dot(v_c, jnp.exp2(x[h] - m_new[h])) for h in heads]
        return tuple(m_new), tuple(acc_new)

    init = (tuple(jnp.full((1, tq), NEG, F32) for _ in heads), tuple(jnp.zeros((2 * dh, tq), F32) for _ in heads))
    _, acc_sel = lax.fori_loop(0, (t0 + tq + tk - 1) // tk, sel_step, init)

    w0 = pl.multiple_of(jnp.maximum(t0 - WINDOW, 0), tq)
    wk = WINDOW + tq
    k_w = kw_ref[0, 0, 0, pl.ds(w0, wk), :]
    v_w = values_t(vw_ref[0, 0, 0, pl.ds(w0, wk), :])
    xw = [jnp.dot(k_w, qh[h], preferred_element_type=F32) + bwin_ref[0, h, 0] for h in heads]
    ow = [_dot(v_w, jnp.exp2(x - jnp.max(x, axis=0, keepdims=True))) for x in xw]

    gates = jax.nn.sigmoid(gate_ref[0]).T
    outs = []
    for h in heads:
        g = [gates[N_BRANCH * h + br:N_BRANCH * h + br + 1, :] for br in range(N_BRANCH)]
        outs.append(g[0] * oc[h] + g[1] * normalised(acc_sel[h]) + g[2] * normalised(ow[h]))
    o_ref[0] = jnp.concatenate(outs, axis=0).T.astype(o_ref.dtype)


def _nsa_attention(q, gate_logits, kv, k_cmp, v_cmp, rel_bias):
    b, t, d = q.shape
    tq, tk, dh = NSA_TQ, NSA_TK, NSA_DH
    bias_cmp, bias_win, bias_sel = _nsa_bias_tiles(rel_bias, t)
    ovl = _nsa_constants(t)
    n_cmp = k_cmp.shape[2]
    n_win = bias_win.shape[2]
    part = lambda idx: pl.BlockSpec((1, 1, 1, t, dh), lambda g, bi, i: (idx, bi, g, 0, 0))
    cmp_spec = pl.BlockSpec((1, 1, n_cmp, dh), lambda g, bi, i: (bi, g, 0, 0))
    return pl.pallas_call(
        _nsa_attn_kernel,
        grid=(NSA_GROUPS, b, t // tq),
        in_specs=[pl.BlockSpec((1, tq, NSA_HPG * dh), lambda g, bi, i: (bi, i, g)),
                  pl.BlockSpec((1, tq, GATE_LANES), lambda g, bi, i: (bi, i, g)),
                  cmp_spec, cmp_spec, part(0), part(1), part(2), part(3),
                  pl.BlockSpec((1, NSA_HPG, n_cmp, tq), lambda g, bi, i: (g, 0, 0, i)),
                  pl.BlockSpec((1, NSA_HPG, 1, WINDOW + tq, tq),
                               lambda g, bi, i: (g, 0, jnp.minimum(i, n_win - 1), 0, 0)),
                  pl.BlockSpec((1, NSA_HPG, bias_sel.shape[2], tk, tq), lambda g, bi, i: (g, 0, 0, 0, 0)),
                  _const_spec(ovl.shape)],
        out_specs=pl.BlockSpec((1, tq, NSA_HPG * dh), lambda g, bi, i: (bi, i, g)),
        out_shape=jax.ShapeDtypeStruct((b, t, d), ACT_DTYPE),
        scratch_shapes=[pltpu.VMEM((t // SEL_BLOCK, tq), F32)],
        compiler_params=_params(("arbitrary", "arbitrary", "arbitrary")),
        name="nsa_attention",
    )(q, gate_logits, k_cmp, v_cmp, kv, kv, kv, kv, bias_cmp, bias_win, bias_sel, ovl)


def _nsa_block(h, moe_out, kv_ln, kv_w, cmp_pe_k, cmp_pe_v, cmp_k_w1, cmp_k_w2, cmp_v_w1, cmp_v_w2,
               rel_bias, b_ln, b_wqg, b_wo, b, t):
    n, d = b * t, kv_ln.shape[0]
    h1, q, gate_logits, kvc, kv = _nsa_proj(h, moe_out, kv_ln, b_ln, kv_w, b_wqg, b, t)
    k_cmp, v_cmp = _compress(kvc, cmp_pe_k, cmp_pe_v, cmp_k_w1, cmp_k_w2, cmp_v_w1, cmp_v_w2)
    o = _nsa_attention(q, gate_logits, kv, k_cmp, v_cmp, rel_bias)
    return _matmul_res(o.reshape(n, d), _mx(b_wo), h1.reshape(n, d))


def kernel(x, a_ln, a_mu, a_wr, a_wk, a_wv, a_wo, a_w0, a_w1, a_w2, a_a0, a_a1, a_a2, a_g1, a_g2, a_kk, a_ka, a_rk, a_lnx_w, a_lnx_b, kv_ln, kv_w, cmp_pe_k, cmp_pe_v, cmp_k_w1, cmp_k_w2, cmp_v_w1, cmp_v_w2, rel_bias, b_ln, b_wqg, b_wo, m_ln, m_wg, m_bg, m_we, m_be, m_w1, m_w3, m_w2, final_ln):
    b, t, d = x.shape
    h = _rwkv_block(x, a_ln[0], a_mu[0], a_wr[0], a_wk[0], a_wv[0], a_wo[0], a_w0[0], a_w1[0], a_w2[0],
                    a_a0[0], a_a1[0], a_a2[0], a_g1[0], a_g2[0], a_kk[0], a_ka[0], a_rk[0],
                    a_lnx_w[0], a_lnx_b[0])
    moe = lambda hh, l: _moe_layer(hh, m_ln[l], m_wg[l], m_bg[l], m_we[l], m_be[l], m_w1[l], m_w3[l], m_w2[l])
    h = _nsa_block(h, moe(h, 0), kv_ln, kv_w, cmp_pe_k, cmp_pe_v, cmp_k_w1, cmp_k_w2, cmp_v_w1, cmp_v_w2,
                   rel_bias, b_ln[0], b_wqg[0], b_wo[0], b, t)
    return _final(h, moe(h, 1), final_ln).reshape(b, t, d)
```

```python
import functools
import math

import numpy as np
import jax
import jax.numpy as jnp
from jax import lax
from jax.experimental import pallas as pl
from jax.experimental.pallas import tpu as pltpu

F32 = jnp.float32
MXU_DTYPE = jnp.bfloat16
ACT_DTYPE = jnp.bfloat16

D_MODEL = 1024
NORM_EPS = 1e-6
RWKV_HEAD = 64
RWKV_HEADS = D_MODEL // RWKV_HEAD
RWKV_GN_EPS = 64e-5
RWKV_CHUNK = 64
LANES = 128
MXU_TILE = 256
VMEM_LIMIT = 56 * 1024 * 1024


def _mx(x):
    return x.astype(MXU_DTYPE)


def _dot(a, b):
    return jnp.dot(_mx(a), _mx(b), preferred_element_type=F32)


def _dot_nt(a, b):
    return lax.dot_general(_mx(a), _mx(b), (((1,), (1,)), ((), ())), preferred_element_type=F32)


def _split2(x):
    hi = x.astype(MXU_DTYPE)
    lo = (x - hi.astype(F32)).astype(MXU_DTYPE)
    return hi, lo


def _dot_hl(x, w):
    hi, lo = _split2(x)
    return jnp.dot(hi, w, preferred_element_type=F32) + jnp.dot(lo, w, preferred_element_type=F32)


def _rms(x, g):
    return x * lax.rsqrt(jnp.mean(x * x, axis=-1, keepdims=True) + NORM_EPS) * g


TOK_ROWS = 8


def _tok_load(ref):
    tm = ref.shape[0] // TOK_ROWS
    return jnp.concatenate([ref[pl.ds(s, tm, stride=TOK_ROWS), :] for s in range(TOK_ROWS)], axis=1)


def _tok_store(ref, val):
    tm = val.shape[0]
    for s in range(TOK_ROWS):
        ref[pl.ds(s, tm, stride=TOK_ROWS), :] = val[:, LANES * s:LANES * (s + 1)]


def _tok_spec(tm, index_map):
    return pl.BlockSpec((TOK_ROWS * tm, LANES), index_map)


def _const_spec(shape):
    nd = len(shape)
    return pl.BlockSpec(shape, lambda *_: (0,) * nd)


def _params(sem):
    return pltpu.CompilerParams(dimension_semantics=sem, vmem_limit_bytes=VMEM_LIMIT)


def _rwkv_pre_kernel(x_ref, ln_ref, mu_ref, wr_ref, wk_ref, wv_ref, w1_ref, a1_ref, g1_ref,
                     w2_ref, a2_ref, g2_ref, vec_ref,
                     r_out, k_out, v_out, kk_out, g_out, ld_out, a_out, prev_ref):
    tm = x_ref.shape[1]
    hn = _rms(x_ref[0], ln_ref[...])

    @pl.when(pl.program_id(1) == 0)
    def _():
        prev_ref[...] = jnp.zeros_like(prev_ref)

    rows = lax.broadcasted_iota(jnp.int32, (tm, 1), 0)
    shifted = jnp.where(rows == 0, prev_ref[0:1, :], pltpu.roll(hn, 1, axis=0))
    prev_ref[0:1, :] = hn[tm - 1:tm, :]
    xx = shifted - hn
    xr, xw, xk, xv, xa, xg = (_mx(hn + xx * mu_ref[i:i + 1, :]) for i in range(6))
    w0, a0, kkp, kap = (vec_ref[i:i + 1, :] for i in range(4))
    r = jnp.dot(xr, wr_ref[...], preferred_element_type=F32)
    k = jnp.dot(xk, wk_ref[...], preferred_element_type=F32)
    v = jnp.dot(xv, wv_ref[...], preferred_element_type=F32)
    z = w0 + _dot(jnp.tanh(jnp.dot(xw, w1_ref[...], preferred_element_type=F32)), w2_ref[...])
    a = jax.nn.sigmoid(a0 + _dot(jnp.dot(xa, a1_ref[...], preferred_element_type=F32), a2_ref[...]))
    g = _dot(jax.nn.sigmoid(jnp.dot(xg, g1_ref[...], preferred_element_type=F32)), g2_ref[...])
    ld_out[0] = (-math.exp(-0.5)) * jax.nn.sigmoid(z)
    a_out[0] = a
    r_out[0] = r.astype(r_out.dtype)
    v_out[0] = v.astype(v_out.dtype)
    kk_out[0] = (k * kkp).astype(kk_out.dtype)
    k_out[0] = (k * (1.0 + (a - 1.0) * kap)).astype(k_out.dtype)
    g_out[0] = g.astype(g_out.dtype)


def _rwkv_pre(x, ln, mu, wr, wk, wv, w1, a1, g1, w2, a2, g2, vecs, tm=256):
    b, t, d = x.shape
    act = jax.ShapeDtypeStruct((b, t, d), ACT_DTYPE)
    f32o = jax.ShapeDtypeStruct((b, t, d), F32)
    blk = pl.BlockSpec((1, tm, d), lambda i, j: (i, j, 0))
    consts = [ln, mu, wr, wk, wv, w1, a1, g1, w2, a2, g2, vecs]
    return pl.pallas_call(
        _rwkv_pre_kernel,
        grid=(b, t // tm),
        in_specs=[blk] + [_const_spec(c.shape) for c in consts],
        out_specs=[blk] * 7,
        out_shape=[act, act, act, act, act, f32o, f32o],
        scratch_shapes=[pltpu.VMEM((8, d), F32)],
        compiler_params=_params(("arbitrary", "arbitrary")),
        name="rwkv_pre",
    )(x, *consts)


def _stack4(x):
    return jnp.concatenate([x[:, MXU_TILE * q:MXU_TILE * (q + 1)] for q in range(4)], axis=0)


def _unstack4(y):
    c = y.shape[0] // 4
    return jnp.concatenate([y[c * q:c * (q + 1), :] for q in range(4)], axis=1)


def _rwkv_scan_kernel(r_ref, k_ref, v_ref, kk_ref, g_ref, ld_ref, a_ref,
                      ltri_ref, smask_ref, eye_ref, j_ref, rk_ref, lnw_ref, lnb_ref,
                      y_out, h_ref):
    c = RWKV_CHUNK

    @pl.when(pl.program_id(1) == 0)
    def _():
        h_ref[...] = jnp.zeros_like(h_ref)

    ld = ld_ref[0]
    r = r_ref[0].astype(F32)
    k = k_ref[0].astype(F32)
    v = v_ref[0].astype(F32)
    kk = kk_ref[0].astype(F32)
    jones = j_ref[...]
    ld_hi, ld_lo = _split2(ld)
    cum = (jnp.dot(ltri_ref[...], ld_hi, preferred_element_type=F32)
           + jnp.dot(ltri_ref[...], ld_lo, preferred_element_type=F32))
    cum_last = cum[c - 1:c, :]
    e_l = jnp.exp(cum)
    e_lm = jnp.exp(cum - ld)
    e_nl = jnp.exp(-cum)
    e_cl = jnp.exp(cum_last - cum)
    e_c = jnp.exp(cum_last)
    ss = _unstack4(_dot_hl(_stack4(kk * kk), jones))
    kkn = kk * lax.rsqrt(jnp.maximum(ss, 1e-24))
    a_vec = -kkn
    b_vec = kkn * a_ref[0]
    rt = r * e_l
    at = a_vec * e_lm
    kt = k * e_nl
    bt = b_vec * e_nl
    bh = b_vec * e_cl
    kh = k * e_cl

    lane = lax.broadcasted_iota(jnp.int32, (1, LANES), 1)
    m_l = (lane < RWKV_HEAD).astype(F32)
    m_r = 1.0 - m_l

    def st(xp):
        return jnp.concatenate([xp * m_l, xp * m_r], axis=0)

    def dup(xp):
        return jnp.concatenate([xp, xp], axis=0)

    smask = smask_ref[...] > 0.0
    eye = eye_ref[...]
    pairs = range(D_MODEL // LANES)
    sls = [slice(LANES * p, LANES * (p + 1)) for p in pairs]
    at_s = [st(at[:, sl]) for sl in sls]
    rt_s = [st(rt[:, sl]) for sl in sls]
    v_s = [st(v[:, sl]) for sl in sls]
    s = [jnp.where(smask,
                   _dot_nt(jnp.concatenate([at_s[p], rt_s[p]], axis=0),
                           jnp.concatenate([dup(bt[:, sls[p]]), dup(kt[:, sls[p]])], axis=0)),
                   0.0) for p in pairs]
    a_ab = [x[:2 * c, :2 * c] for x in s]
    a_r = [x[2 * c:, :] for x in s]
    g1 = [_dot(s[p][:2 * c, 2 * c:], v_s[p]) for p in pairs]
    t_m = [eye + x for x in a_ab]
    a_pow = [_dot(x, x) for x in a_ab]
    n = 2
    while n < c:
        if 2 * n < c:
            both = [_dot(a_pow[p], jnp.concatenate([a_pow[p], t_m[p]], axis=1)) for p in pairs]
            a_pow = [x[:, :2 * c] for x in both]
            t_m = [t_m[p] + both[p][:, 2 * c:] for p in pairs]
        else:
            t_m = [t_m[p] + _dot(a_pow[p], t_m[p]) for p in pairs]
        n *= 2
    tw = [_dot(t_m[p], jnp.concatenate([at_s[p], g1[p]], axis=1)) for p in pairs]
    zmat = [jnp.concatenate([tw[p], jnp.concatenate([jnp.zeros_like(v_s[p]), v_s[p]], axis=1)], axis=0)
            for p in pairs]
    upper = [_dot(a_r[p], zmat[p]) for p in pairs]
    bk_t = [jnp.concatenate([st(bh[:, sl]), st(kh[:, sl])], axis=0).T for sl in sls]
    lower = [_dot(bk_t[p], zmat[p]) for p in pairs]
    seq = [_dot(jnp.concatenate([upper[p][:, :LANES] + rt_s[p],
                                 lower[p][:, :LANES] + eye * e_c[:, sls[p]]], axis=0), h_ref[p])
           for p in pairs]
    ys = []
    for p in pairs:
        y_st = seq[p][:2 * c, :] + upper[p][:, LANES:]
        h_ref[p] = seq[p][2 * c:, :] + lower[p][:, LANES:]
        ys.append(y_st[:c, :] + y_st[c:, :])
    y = jnp.concatenate(ys, axis=1)

    inv_n = 1.0 / RWKV_HEAD
    y4 = _stack4(y)
    dlt = y4 - _dot_hl(y4, jones) * inv_n
    var = _dot_hl(dlt * dlt, jones) * inv_n
    yn = dlt * lax.rsqrt(var + RWKV_GN_EPS)
    bonus = _dot_hl(_stack4(r * k) * rk_ref[...], jones) * _stack4(v)
    out = (yn * lnw_ref[...] + lnb_ref[...] + bonus) * _stack4(g_ref[0].astype(F32))
    y_out[0] = _unstack4(out).astype(y_out.dtype)


def _scan_constants():
    c = RWKV_CHUNK
    ltri = np.tril(np.ones((c, c), np.float32))
    rho = np.arange(4 * c)
    r_type, r_head, r_t = rho // (2 * c), (rho % (2 * c)) // c, rho % c
    same = r_head[:, None] == r_head[None, :]
    strict = r_t[None, :] < r_t[:, None]
    incl = r_t[None, :] <= r_t[:, None]
    smask = same & np.where(r_type[:, None] == 0, strict, incl)
    eye = np.eye(LANES, dtype=np.float32)
    head = np.arange(MXU_TILE) // RWKV_HEAD
    jones = (head[:, None] == head[None, :]).astype(np.float32)
    return (jnp.asarray(ltri, MXU_DTYPE), jnp.asarray(smask, F32), jnp.asarray(eye, F32),
            jnp.asarray(jones, MXU_DTYPE))


def _stack_param(p):
    return jnp.repeat(p.reshape(4, MXU_TILE).astype(F32), RWKV_CHUNK, axis=0)


def _rwkv_scan(r, k, v, kk, g, ld, a, rk, lnw, lnb):
    b, t, d = r.shape
    c = RWKV_CHUNK
    blk = pl.BlockSpec((1, c, d), lambda i, j: (i, j, 0))
    consts = list(_scan_constants()) + [_stack_param(rk.reshape(-1)), _stack_param(lnw), _stack_param(lnb)]
    return pl.pallas_call(
        _rwkv_scan_kernel,
        grid=(b, t // c),
        in_specs=[blk] * 7 + [_const_spec(x.shape) for x in consts],
        out_specs=blk,
        out_shape=jax.ShapeDtypeStruct((b, t, d), ACT_DTYPE),
        scratch_shapes=[pltpu.VMEM((d // LANES, LANES, LANES), F32)],
        compiler_params=_params(("arbitrary", "arbitrary")),
        name="rwkv_scan",
    )(r, k, v, kk, g, ld, a, *consts)


def _matmul_res_kernel(a_ref, w_ref, res_ref, o_ref):
    _tok_store(o_ref, res_ref[...] + jnp.dot(a_ref[...], w_ref[...], preferred_element_type=F32))


def _matmul_res(a, w, res, tm=512):
    n, kdim = a.shape
    d = w.shape[1]
    return pl.pallas_call(
        _matmul_res_kernel,
        grid=(n // tm,),
        in_specs=[pl.BlockSpec((tm, kdim), lambda i: (i, 0)), _const_spec(w.shape),
                  pl.BlockSpec((tm, d), lambda i: (i, 0))],
        out_specs=_tok_spec(tm, lambda i: (i, 0)),
        out_shape=jax.ShapeDtypeStruct((TOK_ROWS * n, LANES), F32),
        compiler_params=_params(("arbitrary",)),
        name="matmul_res",
    )(a, w, res)


def _rwkv_block(x, a_ln, a_mu, a_wr, a_wk, a_wv, a_wo, a_w0, a_w1, a_w2, a_a0, a_a1, a_a2,
                a_g1, a_g2, a_kk, a_ka, a_rk, a_lnx_w, a_lnx_b):
    b, t, d = x.shape
    vecs = jnp.stack([a_w0, a_a0, a_kk, a_ka]).astype(F32)
    r, k, v, kk, g, ld, a = _rwkv_pre(
        x, a_ln.reshape(1, d), a_mu, _mx(a_wr), _mx(a_wk), _mx(a_wv), _mx(a_w1), _mx(a_a1), _mx(a_g1),
        _mx(a_w2), _mx(a_a2), _mx(a_g2), vecs)
    y = _rwkv_scan(r, k, v, kk, g, ld, a, a_rk, a_lnx_w, a_lnx_b)
    return _matmul_res(y.reshape(b * t, d), _mx(a_wo), x.reshape(b * t, d))


MOE_GROUPS = 4
MOE_EPG = 8
MOE_EXPERTS = MOE_GROUPS * MOE_EPG
MOE_TOPK = 2
MOE_BLOCK = 256
NEG = -1e30


def _route(logits):
    lane = lax.broadcasted_iota(jnp.int32, logits.shape, 1)
    is_grp = lane < MOE_GROUPS
    gl = jnp.where(is_grp, logits, NEG)
    ge = jnp.exp(gl - jnp.max(gl, axis=-1, keepdims=True))
    gp = ge / jnp.sum(ge, axis=-1, keepdims=True)
    grp_w = jnp.max(gp, axis=-1, keepdims=True)
    grp = jnp.min(jnp.where(is_grp & (gp == grp_w), lane, LANES), axis=-1, keepdims=True)
    lo = MOE_GROUPS + MOE_EPG * grp
    in_grp = (lane >= lo) & (lane < lo + MOE_EPG)
    el = jnp.where(in_grp, logits, NEG)
    ee = jnp.exp(el - jnp.max(el, axis=-1, keepdims=True))
    ep = ee / jnp.sum(ee, axis=-1, keepdims=True)
    p1 = jnp.max(jnp.where(in_grp, ep, -1.0), axis=-1, keepdims=True)
    i1 = jnp.min(jnp.where(in_grp & (ep == p1), lane, LANES), axis=-1, keepdims=True)
    rest = in_grp & (lane != i1)
    p2 = jnp.max(jnp.where(rest, ep, -1.0), axis=-1, keepdims=True)
    i2 = jnp.min(jnp.where(rest & (ep == p2), lane, LANES), axis=-1, keepdims=True)
    scale = grp_w / (p1 + p2)
    return i1 - MOE_GROUPS, i2 - MOE_GROUPS, p1 * scale, p2 * scale


def _router_logits(xf, w_hi_ref, w_lo_ref, bias_ref):
    hi, lo = _split2(xf)
    logits = (jnp.dot(hi, w_hi_ref[...], preferred_element_type=F32)
              + jnp.dot(lo, w_hi_ref[...], preferred_element_type=F32)
              + jnp.dot(hi, w_lo_ref[...], preferred_element_type=F32))
    return logits + bias_ref[...]


def _moe_router_kernel(h_ref, ln_ref, w_hi_ref, w_lo_ref, bias_ref, tri_ref, route_out, count_out, cnt_ref):
    @pl.when(pl.program_id(0) == 0)
    def _():
        cnt_ref[...] = jnp.zeros_like(cnt_ref)

    xf = _rms(_tok_load(h_ref), ln_ref[...])
    e1, e2, w1, w2 = _route(_router_logits(xf, w_hi_ref, w_lo_ref, bias_ref))
    lane = lax.broadcasted_iota(jnp.int32, (xf.shape[0], LANES), 1)
    hit1, hit2 = lane == e1, lane == e2
    onehot = jnp.where(hit1, 1.0, jnp.where(hit2, 1.0, 0.0))
    before = jnp.dot(tri_ref[...], _mx(onehot), preferred_element_type=F32) + cnt_ref[...]
    r1 = jnp.sum(jnp.where(hit1, before, 0.0), axis=-1, keepdims=True)
    r2 = jnp.sum(jnp.where(hit2, before, 0.0), axis=-1, keepdims=True)
    cnt_ref[...] = cnt_ref[...] + jnp.sum(onehot, axis=0, keepdims=True)
    count_out[...] = cnt_ref[...]
    fields = (e1.astype(F32), e2.astype(F32), w1, w2, r1, r2)
    rec = jnp.zeros(lane.shape, F32)
    for idx, val in enumerate(fields):
        rec = jnp.where(lane == idx, val, rec)
    route_out[...] = rec


def _router_weights(w_gr, b_gr, w_er, b_er):
    d = w_gr.shape[0]
    w = jnp.zeros((d, LANES), F32).at[:, :MOE_GROUPS].set(w_gr).at[:, MOE_GROUPS:MOE_GROUPS + MOE_EXPERTS].set(w_er)
    bias = jnp.zeros((1, LANES), F32).at[0, :MOE_GROUPS].set(b_gr).at[0, MOE_GROUPS:MOE_GROUPS + MOE_EXPERTS].set(b_er)
    w_hi = w.astype(MXU_DTYPE)
    w_lo = (w - w_hi.astype(F32)).astype(MXU_DTYPE)
    return w_hi, w_lo, bias


def _moe_router(h, ln, w_hi, w_lo, bias, tm=512):
    n, d = h.shape[0] // TOK_ROWS, ln.shape[0]
    tri = jnp.asarray(np.tril(np.ones((tm, tm), np.float32), -1), MXU_DTYPE)
    return pl.pallas_call(
        _moe_router_kernel,
        grid=(n // tm,),
        in_specs=[_tok_spec(tm, lambda i: (i, 0)), _const_spec((1, d)), _const_spec(w_hi.shape),
                  _const_spec(w_lo.shape),
                  _const_spec(bias.shape), _const_spec(tri.shape)],
        out_specs=[pl.BlockSpec((tm, LANES), lambda i: (i, 0)), _const_spec((1, LANES))],
        out_shape=[jax.ShapeDtypeStruct((n, LANES), F32), jax.ShapeDtypeStruct((1, LANES), F32)],
        scratch_shapes=[pltpu.VMEM((1, LANES), F32)],
        compiler_params=_params(("arbitrary",)),
        name="moe_router",
    )(h, ln.reshape(1, d), w_hi, w_lo, bias, tri)


def _moe_plan(route, counts, blk):
    n = route.shape[0]
    counts = counts[0, :MOE_EXPERTS].astype(jnp.int32)
    pcounts = (counts + blk - 1) // blk * blk
    pend = jnp.cumsum(pcounts)
    pstart = (pend - pcounts).astype(jnp.int32)
    n_blk = n * MOE_TOPK // blk + MOE_EXPERTS
    blk_start = jnp.arange(n_blk, dtype=jnp.int32) * blk
    blk_e = jnp.sum((blk_start[:, None] >= pend[None, :]).astype(jnp.int32), axis=1)
    blk_e = jnp.minimum(blk_e, MOE_EXPERTS - 1)
    n_valid = jnp.clip((pstart + counts)[blk_e] - blk_start, 0, blk).astype(jnp.int32)
    n_act = (pend[-1] // blk).astype(jnp.int32).reshape(1)
    expert = route[:, 0:MOE_TOPK].astype(jnp.int32)
    onehot = expert[:, :, None] == jnp.arange(MOE_EXPERTS, dtype=jnp.int32)
    seg_start = jnp.sum(jnp.where(onehot, pstart, 0), axis=-1)
    sorted_pos = seg_start + route[:, 4:4 + MOE_TOPK].astype(jnp.int32)
    return blk_e, n_valid, n_act, sorted_pos


def _token_copies(pos_ref, tm, make_copy):
    group = 8

    def body(g, carry):
        t0 = g * group
        pos = [[pos_ref[0, 0, MOE_TOPK * (t0 + j) + k] for k in range(MOE_TOPK)] for j in range(group)]
        for j in range(group):
            for k in range(MOE_TOPK):
                make_copy(t0 + j, k, pos[j][k]).start()
        return carry

    lax.fori_loop(0, tm // group, body, 0)


def _tile(ref, tok):
    return ref.at[pl.ds(pl.multiple_of(tok * TOK_ROWS, TOK_ROWS), TOK_ROWS)]


def _moe_dispatch_kernel(n_valid_ref, pos_ref, src_ref, dst_hbm, stage, sem, zbuf, zsem):
    tm = src_ref.shape[0] // TOK_ROWS
    step = pl.program_id(0)
    slot = lax.rem(step, 2)

    def drain(s):
        for k in range(MOE_TOPK):
            pltpu.make_async_copy(stage.at[s], dst_hbm.at[pl.ds(0, stage.shape[1])], sem.at[s]).wait()

    @pl.when(pl.program_id(0) == 0)
    def _():
        rows = zbuf.shape[0]
        zbuf[...] = jnp.zeros_like(zbuf)

        def fill(j, wait):
            @pl.when(n_valid_ref[j] < rows // TOK_ROWS)
            def _():
                start = 0 if wait else pl.multiple_of(j * rows, rows)
                cp = pltpu.make_async_copy(zbuf, dst_hbm.at[pl.ds(start, rows)], zsem)
                cp.wait() if wait else cp.start()

        lax.fori_loop(0, n_valid_ref.shape[0], lambda j, c: (fill(j, False), c)[1], 0)
        lax.fori_loop(0, n_valid_ref.shape[0], lambda j, c: (fill(j, True), c)[1], 0)

    @pl.when(step >= 2)
    def _():
        drain(slot)

    stage[slot] = src_ref[...]
    _token_copies(pos_ref, tm, lambda t, k, pos: pltpu.make_async_copy(
        _tile(stage.at[slot], t), _tile(dst_hbm, pos), sem.at[slot]))

    @pl.when(step == pl.num_programs(0) - 1)
    def _():
        @pl.when(step >= 1)
        def _():
            drain(1 - slot)
        drain(slot)


def _moe_gather_combine(step, n_steps, pos_ref, pos_next_ref, ys_hbm, h_ref, route, gbuf, gsem):
    tm = gbuf.shape[2] // TOK_ROWS
    slot = lax.rem(step, 2)

    def start(p_ref, s):
        _token_copies(p_ref, tm, lambda t, k, pos: pltpu.make_async_copy(
            _tile(ys_hbm, pos), _tile(gbuf.at[s, k], t), gsem.at[s]))

    @pl.when(step == 0)
    def _():
        start(pos_ref, 0)

    @pl.when(step + 1 < n_steps)
    def _():
        start(pos_next_ref, 1 - slot)

    for k in range(MOE_TOPK):
        pltpu.make_async_copy(ys_hbm.at[pl.ds(0, gbuf.shape[2])], gbuf.at[slot, k], gsem.at[slot]).wait()
    return (_tok_load(h_ref) + route[:, 2:3] * _tok_load(gbuf.at[slot, 0])
            + route[:, 3:4] * _tok_load(gbuf.at[slot, 1]))


def _gather_specs(sorted_pos, tm, step_index):
    n = sorted_pos.shape[0]
    pos = sorted_pos.reshape(n // tm, 1, MOE_TOPK * tm)
    last = n // tm - 1
    cur = pl.BlockSpec((1, 1, MOE_TOPK * tm), lambda *g: (step_index(*g), 0, 0), memory_space=pltpu.SMEM)
    nxt = pl.BlockSpec((1, 1, MOE_TOPK * tm), lambda *g: (jnp.minimum(step_index(*g) + 1, last), 0, 0),
                       memory_space=pltpu.SMEM)
    scratch = [pltpu.VMEM((2, MOE_TOPK, TOK_ROWS * tm, LANES), F32), pltpu.SemaphoreType.DMA((2,))]
    return pos, [cur, nxt, pl.BlockSpec(memory_space=pl.ANY)], scratch


def _moe_dispatch(h, n_valid, sorted_pos, p, blk, tm=512):
    n = sorted_pos.shape[0]
    pos = sorted_pos.reshape(n // tm, 1, MOE_TOPK * tm)
    grid_spec = pltpu.PrefetchScalarGridSpec(
        num_scalar_prefetch=1,
        grid=(n // tm,),
        in_specs=[pl.BlockSpec((1, 1, MOE_TOPK * tm), lambda i, nv: (i, 0, 0), memory_space=pltpu.SMEM),
                  _tok_spec(tm, lambda i, nv: (i, 0))],
        out_specs=pl.BlockSpec(memory_space=pl.ANY),
        scratch_shapes=[pltpu.VMEM((2, TOK_ROWS * tm, LANES), h.dtype), pltpu.SemaphoreType.DMA((2,)),
                        pltpu.VMEM((TOK_ROWS * blk, LANES), h.dtype), pltpu.SemaphoreType.DMA],
    )
    return pl.pallas_call(
        _moe_dispatch_kernel,
        grid_spec=grid_spec,
        out_shape=jax.ShapeDtypeStruct((TOK_ROWS * p, LANES), h.dtype),
        compiler_params=_params(("arbitrary",)),
        name="moe_dispatch",
    )(n_valid, pos, h)


def _moe_expert_kernel(blk_e_ref, n_act_ref, x_ref, ln_ref, w1_ref, w3_ref, w2_ref, y_ref):
    del blk_e_ref
    i = pl.program_id(0)

    @pl.when(i < n_act_ref[0])
    def _():
        xn = _mx(_rms(_tok_load(x_ref), ln_ref[...]))
        h1 = jnp.dot(xn, w1_ref[0], preferred_element_type=F32)
        h3 = jnp.dot(xn, w3_ref[0], preferred_element_type=F32)
        _tok_store(y_ref, _dot(h1 * jax.nn.sigmoid(h1) * h3, w2_ref[0]))

    @pl.when(i >= n_act_ref[0])
    def _():
        y_ref[...] = jnp.zeros_like(y_ref)


def _moe_experts(xs, ln, w1, w3, w2, blk_e, n_act, blk):
    p = xs.shape[0] // TOK_ROWS
    d, hdim = w1.shape[1], w1.shape[2]
    grid_spec = pltpu.PrefetchScalarGridSpec(
        num_scalar_prefetch=2,
        grid=(p // blk,),
        in_specs=[_tok_spec(blk, lambda i, be, na: (i, 0)),
                  pl.BlockSpec((1, d), lambda i, be, na: (0, 0)),
                  pl.BlockSpec((1, d, hdim), lambda i, be, na: (be[i], 0, 0)),
                  pl.BlockSpec((1, d, hdim), lambda i, be, na: (be[i], 0, 0)),
                  pl.BlockSpec((1, hdim, d), lambda i, be, na: (be[i], 0, 0))],
        out_specs=_tok_spec(blk, lambda i, be, na: (i, 0)),
    )
    return pl.pallas_call(
        _moe_expert_kernel,
        grid_spec=grid_spec,
        out_shape=jax.ShapeDtypeStruct(xs.shape, F32),
        compiler_params=_params(("arbitrary",)),
        name="moe_experts",
    )(blk_e, n_act, xs, ln.reshape(1, d), w1, w3, w2)


def _moe_layer(h, ln, w_gr, b_gr, w_er, b_er, w1, w3, w2):
    n = h.shape[0] // TOK_ROWS
    blk = MOE_BLOCK
    route, counts = _moe_router(h, ln, *_router_weights(w_gr, b_gr, w_er, b_er))
    blk_e, n_valid, n_act, sorted_pos = _moe_plan(route, counts, blk)
    p = n * MOE_TOPK + MOE_EXPERTS * blk
    xs = _moe_dispatch(h, n_valid, sorted_pos, p, blk)
    ys = _moe_experts(xs, ln, _mx(w1), _mx(w3), _mx(w2), blk_e, n_act, blk)
    return ys, sorted_pos, route


def _final_kernel(pos_ref, pos_next_ref, ys_hbm, h_ref, route_ref, ln_ref, o_ref, gbuf, gsem):
    h = _moe_gather_combine(pl.program_id(0), pl.num_programs(0), pos_ref, pos_next_ref, ys_hbm, h_ref,
                            route_ref[...], gbuf, gsem)
    o_ref[...] = _rms(h, ln_ref[...])


def _final(h, moe_out, ln, tm=512):
    ys, sorted_pos, route = moe_out
    n, d = route.shape[0], ln.shape[0]
    pos, gather_specs, scratch = _gather_specs(sorted_pos, tm, lambda i: i)
    return pl.pallas_call(
        _final_kernel,
        grid=(n // tm,),
        in_specs=gather_specs + [_tok_spec(tm, lambda i: (i, 0)), pl.BlockSpec((tm, LANES), lambda i: (i, 0)),
                                 _const_spec((1, d))],
        out_specs=pl.BlockSpec((tm, d), lambda i: (i, 0)),
        out_shape=jax.ShapeDtypeStruct((n, d), F32),
        scratch_shapes=scratch,
        compiler_params=_params(("arbitrary",)),
        name="final_norm",
    )(pos, pos, ys, h, route, ln.reshape(1, d))


NSA_HEADS = 16
NSA_GROUPS = 4
NSA_HPG = NSA_HEADS // NSA_GROUPS
NSA_DH = D_MODEL // NSA_HEADS
NSA_PARTS = 6
CMP_BLOCK = 32
CMP_STRIDE = 16
SEL_BLOCK = 64
SEL_TOPK = 8
WINDOW = 512
N_BRANCH = 3
IMP_FORCE = 1e4
REL_BUCKETS = 32
REL_MAX_DIST = 128
LOG2E = math.log2(math.e)
NSA_TQ = 256
NSA_TK = 512
NSA_ROW_BLOCK = 32
GATE_LANES = LANES
CMP_PARTS = 2


def _nsa_proj_kernel(pos_ref, pos_next_ref, ys_hbm, h_ref, route_ref, kvln_ref, qln_ref, kvw_ref, qw_ref,
                     h_out, q_out, gate_out, kvc_out, kv_out, gbuf, gsem):
    step = pl.program_id(0) * pl.num_programs(1) + pl.program_id(1)
    h1 = _moe_gather_combine(step, pl.num_programs(0) * pl.num_programs(1), pos_ref, pos_next_ref, ys_hbm, h_ref,
                             route_ref[0], gbuf, gsem)
    h_out[0] = h1
    kv = _dot(_rms(h1, kvln_ref[...]), kvw_ref[...])
    for part in range(NSA_PARTS):
        for g in range(NSA_GROUPS):
            col = (part * NSA_GROUPS + g) * NSA_DH
            piece = kv[:, col:col + NSA_DH]
            if part < CMP_PARTS:
                kvc_out[part, 0, g] = piece
            else:
                kv_out[part - CMP_PARTS, 0, g] = piece.astype(kv_out.dtype)
    qg = _dot(_rms(h1, qln_ref[...]), qw_ref[...])
    q_out[0] = (qg[:, :D_MODEL] * (NSA_DH ** -0.5 * LOG2E)).astype(q_out.dtype)
    gate_out[0] = qg[:, D_MODEL:]


def _gate_padded_weight(q_w):
    d = q_w.shape[0]
    n_gate = NSA_HPG * N_BRANCH
    gates = q_w[:, d:].reshape(d, NSA_GROUPS, n_gate)
    gates = jnp.pad(gates, ((0, 0), (0, 0), (0, GATE_LANES - n_gate))).reshape(d, NSA_GROUPS * GATE_LANES)
    return jnp.concatenate([q_w[:, :d], gates], axis=1)


def _nsa_proj(h, moe_out, kv_ln, q_ln, kv_w, q_w, b, t, tm=256):
    ys, sorted_pos, route = moe_out
    d = kv_ln.shape[0]
    blk = lambda w: pl.BlockSpec((1, tm, w), lambda i, j: (i, j, 0))
    per_b = t // tm
    tok_block = lambda i, j: i * per_b + j
    pos, gather_specs, scratch = _gather_specs(sorted_pos, tm, tok_block)
    gw = NSA_GROUPS * GATE_LANES
    qw = _gate_padded_weight(q_w)
    kv_spec = lambda parts: pl.BlockSpec((parts, 1, NSA_GROUPS, tm, NSA_DH), lambda i, j: (0, i, 0, j, 0))
    kv_shape = lambda parts: (parts, b, NSA_GROUPS, t, NSA_DH)
    return pl.pallas_call(
        _nsa_proj_kernel,
        grid=(b, per_b),
        in_specs=gather_specs + [_tok_spec(tm, lambda i, j: (tok_block(i, j), 0)), blk(LANES),
                                 _const_spec((1, d)), _const_spec((1, d)),
                                 _const_spec(kv_w.shape), _const_spec(qw.shape)],
        out_specs=[blk(d), blk(d), blk(gw), kv_spec(CMP_PARTS), kv_spec(NSA_PARTS - CMP_PARTS)],
        out_shape=[jax.ShapeDtypeStruct((b, t, d), F32), jax.ShapeDtypeStruct((b, t, d), ACT_DTYPE),
                   jax.ShapeDtypeStruct((b, t, gw), F32), jax.ShapeDtypeStruct(kv_shape(CMP_PARTS), F32),
                   jax.ShapeDtypeStruct(kv_shape(NSA_PARTS - CMP_PARTS), ACT_DTYPE)],
        scratch_shapes=scratch,
        compiler_params=_params(("arbitrary", "arbitrary")),
        name="nsa_proj",
    )(pos, pos, ys, h, route.reshape(b, t, LANES),
      kv_ln.reshape(1, d), q_ln.reshape(1, d), _mx(kv_w), _mx(qw))


def _gelu_tanh(x):
    return 0.5 * x * (1.0 + jnp.tanh(math.sqrt(2.0 / math.pi) * (x + 0.044715 * (x * x * x))))


def _compress_kernel(k_ref, v_ref, pek_ref, pev_ref, kw1_ref, kw2_ref, vw1_ref, vw2_ref, kc_out, vc_out):
    dh = k_ref.shape[-1]
    n_chunk = k_ref.shape[3] // CMP_STRIDE

    def one(z_ref, pe_ref, w1_ref, w2_ref, out):
        first = jnp.zeros((n_chunk, w1_ref.shape[1]), F32)
        second = first
        for l in range(CMP_STRIDE):
            z = z_ref[0, 0, 0, pl.ds(l, n_chunk, stride=CMP_STRIDE), :]
            lo, hi = l, CMP_STRIDE + l
            first = first + _dot(z + pe_ref[0, lo:lo + 1, :], w1_ref[dh * lo:dh * (lo + 1), :])
            second = second + _dot(z + pe_ref[0, hi:hi + 1, :], w1_ref[dh * hi:dh * (hi + 1), :])
        hid = first + pltpu.roll(second, n_chunk - 1, axis=0)
        out[0, 0] = _dot(_gelu_tanh(hid), w2_ref[...]).astype(out.dtype)

    one(k_ref, pek_ref, kw1_ref, kw2_ref, kc_out)
    one(v_ref, pev_ref, vw1_ref, vw2_ref, vc_out)


def _compress(kvc, pe_k, pe_v, k_w1, k_w2, v_w1, v_w2):
    _, b, g, t, dh = kvc.shape
    n_chunk = t // CMP_STRIDE
    part = lambda idx: pl.BlockSpec((1, 1, 1, t, dh), lambda i, j: (idx, i, j, 0, 0))
    pe_spec = pl.BlockSpec((1, CMP_BLOCK, dh), lambda i, j: (j, 0, 0))
    out_spec = pl.BlockSpec((1, 1, n_chunk, dh), lambda i, j: (i, j, 0, 0))
    out = jax.ShapeDtypeStruct((b, g, n_chunk, dh), ACT_DTYPE)
    return pl.pallas_call(
        _compress_kernel,
        grid=(b, g),
        in_specs=[part(0), part(1), pe_spec, pe_spec, _const_spec(k_w1.shape), _const_spec(k_w2.shape),
                  _const_spec(v_w1.shape), _const_spec(v_w2.shape)],
        out_specs=[out_spec, out_spec],
        out_shape=[out, out],
        compiler_params=_params(("arbitrary", "arbitrary")),
        name="nsa_compress",
    )(kvc, kvc, pe_k.astype(F32), pe_v.astype(F32), _mx(k_w1), _mx(k_w2), _mx(v_w1), _mx(v_w2))


def _rel_bucket(dist):
    n = jnp.maximum(dist, 0)
    max_exact = REL_BUCKETS // 2
    nf = jnp.maximum(n, 1).astype(F32)
    large = max_exact + (jnp.log(nf / max_exact) / math.log(REL_MAX_DIST / max_exact)
                         * (REL_BUCKETS - max_exact)).astype(jnp.int32)
    return jnp.where(n < max_exact, n, jnp.minimum(large, REL_BUCKETS - 1))


def _toeplitz(vals, rows, cols):
    span = rows + cols
    z = jnp.concatenate([vals, vals[..., :1]], axis=-1)
    flat = jnp.tile(z, rows)[..., :rows * (span - 1)]
    skew = flat.reshape(vals.shape[:-1] + (rows, span - 1))
    return skew[..., rows - 1:rows - 1 + cols]


def _nsa_bias_tiles(rel_bias, t):
    tq, tk = NSA_TQ, NSA_TK
    table = rel_bias.astype(F32).reshape(REL_BUCKETS, NSA_GROUPS, NSA_HPG).transpose(1, 2, 0)
    by_dist = table[:, :, _rel_bucket(jnp.arange(REL_MAX_DIST + 1))]
    far = by_dist[:, :, REL_MAX_DIST:]

    def band(dist, valid, shift=None):
        b = by_dist[:, :, np.clip(dist, 0, REL_MAX_DIST)]
        if shift is not None:
            b = b - shift
        return jnp.where(jnp.asarray(valid), b * LOG2E, NEG)

    u = lambda rows, cols: np.arange(rows + cols - 1) - (rows - 1)
    wk = WINDOW + tq
    dist = np.stack([v * tq - max(v * tq - WINDOW, 0) - u(tq, wk) for v in range(WINDOW // tq + 1)])
    bias_win = _toeplitz(band(dist, (dist >= 0) & (dist < WINDOW)), tq, wk).swapaxes(-1, -2)
    n_near = (REL_MAX_DIST + tk - 1) // tq + 1
    dist = np.stack([e * tq - u(tq, tk) for e in range(n_near)])
    near = _toeplitz(band(dist, dist >= 0, far[:, :, None]), tq, tk).swapaxes(-1, -2)
    bias_sel = jnp.concatenate([near, jnp.zeros_like(near[:, :, :1])], axis=2)
    n_cmp = t // CMP_STRIDE
    dist = np.stack([b - (CMP_BLOCK - 1) + CMP_STRIDE * u(n_cmp, n_cmp) for b in range(CMP_STRIDE)])
    per_b = _toeplitz(band(dist, dist >= 0), n_cmp, n_cmp)
    bias_cmp = jnp.moveaxis(per_b, 2, 4).reshape(NSA_GROUPS, NSA_HPG, n_cmp, t)
    real = np.arange(n_cmp) < (t - CMP_BLOCK) // CMP_STRIDE + 1
    bias_cmp = jnp.where(jnp.asarray(real)[:, None], bias_cmp, NEG)
    return bias_cmp, bias_win, bias_sel


def _nsa_constants(t):
    n_sel = t // SEL_BLOCK
    n_cmp = t // CMP_STRIDE
    cmp_start = np.arange(n_cmp) * CMP_STRIDE
    sel_start = np.arange(n_sel) * SEL_BLOCK
    overlap = ((cmp_start[:, None] < sel_start[None, :] + SEL_BLOCK)
               & (cmp_start[:, None] + CMP_BLOCK > sel_start[None, :]))
    overlap[(t - CMP_BLOCK) // CMP_STRIDE + 1:, :] = False
    return jnp.asarray(overlap.T, MXU_DTYPE)


def _nsa_attn_kernel(q_ref, gate_ref, kc_ref, vc_ref, ks_ref, vs_ref, kw_ref, vw_ref,
                     bcmp_ref, bwin_ref, bsel_ref, ovl_ref, o_ref, neg_ref):
    tq, tk, dh = NSA_TQ, NSA_TK, NSA_DH
    i = pl.program_id(2)
    t0 = i * tq
    heads = range(NSA_HPG)
    q_t = q_ref[0].astype(F32).T
    qh = [_mx(q_t[dh * h:dh * (h + 1), :]) for h in heads]

    def values_t(v):
        ones_col = (lax.broadcasted_iota(jnp.int32, v.shape, 1) == 0).astype(F32)
        return _mx(jnp.concatenate([v.astype(F32), ones_col], axis=1).T)

    k_cmp = kc_ref[0, 0]
    lc = [jnp.dot(k_cmp, qh[h], preferred_element_type=F32) + bcmp_ref[0, h] for h in heads]
    mc = [jnp.max(x, axis=0, keepdims=True) for x in lc]
    pcu = [jnp.where(mc[h] > 0.5 * NEG, jnp.exp2(lc[h] - mc[h]), 0.0) for h in heads]
    pc = [x / jnp.maximum(jnp.sum(x, axis=0, keepdims=True), 1e-30) for x in pcu]
    v_cmp_t = values_t(vc_ref[0, 0])[:dh]
    oc = [_dot(v_cmp_t, x) for x in pc]

    pcsum = (pc[0] + pc[1]) + (pc[2] + pc[3])
    hi = pcsum.astype(MXU_DTYPE)
    mid = (pcsum - hi.astype(F32))
    lo = (mid - mid.astype(MXU_DTYPE).astype(F32)).astype(MXU_DTYPE)
    ovl = ovl_ref[...]
    imp = (jnp.dot(ovl, hi, preferred_element_type=F32) + jnp.dot(ovl, mid.astype(MXU_DTYPE), preferred_element_type=F32)
           + jnp.dot(ovl, lo, preferred_element_type=F32))
    n_sel = imp.shape[0]
    jj = lax.broadcasted_iota(jnp.int32, (n_sel, 1), 0)
    blk_q = lax.shift_right_logical(t0 + lax.broadcasted_iota(jnp.int32, (1, tq), 1),
                                    int(math.log2(SEL_BLOCK)))
    forced = (jj == 0) | (jj == blk_q) | (jj == blk_q - 1)
    score = jnp.where(forced, IMP_FORCE, jnp.where(jj <= blk_q, imp, -1.0))
    rank = jnp.zeros_like(score)
    for j in range(n_sel):
        row = score[j:j + 1, :]
        tie = (jj > j).astype(F32)
        rank = rank + jnp.where(row > score, 1.0, jnp.where(row == score, tie, 0.0))
    neg_ref[...] = jnp.where(rank < SEL_TOPK, 0.0, NEG)

    def normalised(acc):
        return acc[:dh] / acc[dh:dh + 1]

    per_step = tk // SEL_BLOCK

    def sel_step(c, carry):
        ms, accs = carry
        start = pl.multiple_of(c * tk, tk)
        k_c = ks_ref[0, 0, 0, pl.ds(start, tk), :]
        v_c = values_t(vs_ref[0, 0, 0, pl.ds(start, tk), :])
        neg = jnp.concatenate([jnp.broadcast_to(neg_ref[pl.ds(per_step * c + r, 1), :], (SEL_BLOCK, tq))
                               for r in range(per_step)], axis=0)
        near = jnp.minimum((t0 - start) // tq, bsel_ref.shape[2] - 1)
        x = [jnp.dot(k_c, qh[h], preferred_element_type=F32) + bsel_ref[0, h, near] + neg for h in heads]
        m_new = [jnp.maximum(ms[h], jnp.max(x[h], axis=0, keepdims=True)) for h in heads]
        acc_new = [jnp.exp2(ms[h] - m_new[h]) * accs[h] + _dot(v_c, jnp.exp2(x[h] - m_new[h])) for h in heads]
        return tuple(m_new), tuple(acc_new)

    init = (tuple(jnp.full((1, tq), NEG, F32) for _ in heads), tuple(jnp.zeros((2 * dh, tq), F32) for _ in heads))
    _, acc_sel = lax.fori_loop(0, (t0 + tq + tk - 1) // tk, sel_step, init)

    w0 = pl.multiple_of(jnp.maximum(t0 - WINDOW, 0), tq)
    wk = WINDOW + tq
    k_w = kw_ref[0, 0, 0, pl.ds(w0, wk), :]
    v_w = values_t(vw_ref[0, 0, 0, pl.ds(w0, wk), :])
    xw = [jnp.dot(k_w, qh[h], preferred_element_type=F32) + bwin_ref[0, h, 0] for h in heads]
    ow = [_dot(v_w, jnp.exp2(x - jnp.max(x, axis=0, keepdims=True))) for x in xw]

    gates = jax.nn.sigmoid(gate_ref[0]).T
    outs = []
    for h in heads:
        g = [gates[N_BRANCH * h + br:N_BRANCH * h + br + 1, :] for br in range(N_BRANCH)]
        outs.append(g[0] * oc[h] + g[1] * normalised(acc_sel[h]) + g[2] * normalised(ow[h]))
    o_ref[0] = jnp.concatenate(outs, axis=0).T.astype(o_ref.dtype)


def _nsa_attention(q, gate_logits, kv, k_cmp, v_cmp, rel_bias):
    b, t, d = q.shape
    tq, tk, dh = NSA_TQ, NSA_TK, NSA_DH
    bias_cmp, bias_win, bias_sel = _nsa_bias_tiles(rel_bias, t)
    ovl = _nsa_constants(t)
    n_cmp = k_cmp.shape[2]
    n_win = bias_win.shape[2]
    part = lambda idx: pl.BlockSpec((1, 1, 1, t, dh), lambda g, bi, i: (idx, bi, g, 0, 0))
    cmp_spec = pl.BlockSpec((1, 1, n_cmp, dh), lambda g, bi, i: (bi, g, 0, 0))
    return pl.pallas_call(
        _nsa_attn_kernel,
        grid=(NSA_GROUPS, b, t // tq),
        in_specs=[pl.BlockSpec((1, tq, NSA_HPG * dh), lambda g, bi, i: (bi, i, g)),
                  pl.BlockSpec((1, tq, GATE_LANES), lambda g, bi, i: (bi, i, g)),
                  cmp_spec, cmp_spec, part(0), part(1), part(2), part(3),
                  pl.BlockSpec((1, NSA_HPG, n_cmp, tq), lambda g, bi, i: (g, 0, 0, i)),
                  pl.BlockSpec((1, NSA_HPG, 1, WINDOW + tq, tq),
                               lambda g, bi, i: (g, 0, jnp.minimum(i, n_win - 1), 0, 0)),
                  pl.BlockSpec((1, NSA_HPG, bias_sel.shape[2], tk, tq), lambda g, bi, i: (g, 0, 0, 0, 0)),
                  _const_spec(ovl.shape)],
        out_specs=pl.BlockSpec((1, tq, NSA_HPG * dh), lambda g, bi, i: (bi, i, g)),
        out_shape=jax.ShapeDtypeStruct((b, t, d), ACT_DTYPE),
        scratch_shapes=[pltpu.VMEM((t // SEL_BLOCK, tq), F32)],
        compiler_params=_params(("arbitrary", "arbitrary", "arbitrary")),
        name="nsa_attention",
    )(q, gate_logits, k_cmp, v_cmp, kv, kv, kv, kv, bias_cmp, bias_win, bias_sel, ovl)


def _nsa_block(h, moe_out, kv_ln, kv_w, cmp_pe_k, cmp_pe_v, cmp_k_w1, cmp_k_w2, cmp_v_w1, cmp_v_w2,
               rel_bias, b_ln, b_wqg, b_wo, b, t):
    n, d = b * t, kv_ln.shape[0]
    h1, q, gate_logits, kvc, kv = _nsa_proj(h, moe_out, kv_ln, b_ln, kv_w, b_wqg, b, t)
    k_cmp, v_cmp = _compress(kvc, cmp_pe_k, cmp_pe_v, cmp_k_w1, cmp_k_w2, cmp_v_w1, cmp_v_w2)
    o = _nsa_attention(q, gate_logits, kv, k_cmp, v_cmp, rel_bias)
    return _matmul_res(o.reshape(n, d), _mx(b_wo), h1.reshape(n, d))


def kernel(x, a_ln, a_mu, a_wr, a_wk, a_wv, a_wo, a_w0, a_w1, a_w2, a_a0, a_a1, a_a2, a_g1, a_g2, a_kk, a_ka, a_rk, a_lnx_w, a_lnx_b, kv_ln, kv_w, cmp_pe_k, cmp_pe_v, cmp_k_w1, cmp_k_w2, cmp_v_w1, cmp_v_w2, rel_bias, b_ln, b_wqg, b_wo, m_ln, m_wg, m_bg, m_we, m_be, m_w1, m_w3, m_w2, final_ln):
    b, t, d = x.shape
    h = _rwkv_block(x, a_ln[0], a_mu[0], a_wr[0], a_wk[0], a_wv[0], a_wo[0], a_w0[0], a_w1[0], a_w2[0],
                    a_a0[0], a_a1[0], a_a2[0], a_g1[0], a_g2[0], a_kk[0], a_ka[0], a_rk[0],
                    a_lnx_w[0], a_lnx_b[0])
    moe = lambda hh, l: _moe_layer(hh, m_ln[l], m_wg[l], m_bg[l], m_we[l], m_be[l], m_w1[l], m_w3[l], m_w2[l])
    h = _nsa_block(h, moe(h, 0), kv_ln, kv_w, cmp_pe_k, cmp_pe_v, cmp_k_w1, cmp_k_w2, cmp_v_w1, cmp_v_w2,
                   rel_bias, b_ln[0], b_wqg[0], b_wo[0], b, t)
    return _final(h, moe(h, 1), final_ln).reshape(b, t, d)
```

```python
import functools
import math

import numpy as np
import jax
import jax.numpy as jnp
from jax import lax
from jax.experimental import pallas as pl
from jax.experimental.pallas import tpu as pltpu

F32 = jnp.float32
MXU_DTYPE = jnp.bfloat16
ACT_DTYPE = jnp.bfloat16

D_MODEL = 1024
NORM_EPS = 1e-6
RWKV_HEAD = 64
RWKV_HEADS = D_MODEL // RWKV_HEAD
RWKV_GN_EPS = 64e-5
RWKV_CHUNK = 64
RWKV_SEQS = 2
LANES = 128
MXU_TILE = 256
VMEM_LIMIT = 56 * 1024 * 1024


def _mx(x):
    return x.astype(MXU_DTYPE)


def _dot(a, b):
    return jnp.dot(_mx(a), _mx(b), preferred_element_type=F32)


def _dot_nt(a, b):
    return lax.dot_general(_mx(a), _mx(b), (((1,), (1,)), ((), ())), preferred_element_type=F32)


def _split2(x):
    hi = x.astype(MXU_DTYPE)
    lo = (x - hi.astype(F32)).astype(MXU_DTYPE)
    return hi, lo


def _dot_hl(x, w):
    hi, lo = _split2(x)
    return jnp.dot(hi, w, preferred_element_type=F32) + jnp.dot(lo, w, preferred_element_type=F32)


def _rms(x, g):
    return x * lax.rsqrt(jnp.mean(x * x, axis=-1, keepdims=True) + NORM_EPS) * g


TOK_ROWS = 8


def _tok_load(ref):
    tm = ref.shape[0] // TOK_ROWS
    return jnp.concatenate([ref[pl.ds(s, tm, stride=TOK_ROWS), :] for s in range(TOK_ROWS)], axis=1)


def _tok_store(ref, val):
    tm = val.shape[0]
    for s in range(TOK_ROWS):
        ref[pl.ds(s, tm, stride=TOK_ROWS), :] = val[:, LANES * s:LANES * (s + 1)]


def _tok_spec(tm, index_map):
    return pl.BlockSpec((TOK_ROWS * tm, LANES), index_map)


def _const_spec(shape):
    nd = len(shape)
    return pl.BlockSpec(shape, lambda *_: (0,) * nd)


def _params(sem):
    return pltpu.CompilerParams(dimension_semantics=sem, vmem_limit_bytes=VMEM_LIMIT)


def _rwkv_pre_kernel(x_ref, ln_ref, mu_ref, wr_ref, wk_ref, wv_ref, w1_ref, a1_ref, g1_ref,
                     w2_ref, a2_ref, g2_ref, vec_ref,
                     r_out, k_out, v_out, kk_out, g_out, ld_out, a_out, prev_ref):
    tm = x_ref.shape[1]
    hn = _rms(x_ref[0], ln_ref[...])

    @pl.when(pl.program_id(1) == 0)
    def _():
        prev_ref[...] = jnp.zeros_like(prev_ref)

    rows = lax.broadcasted_iota(jnp.int32, (tm, 1), 0)
    shifted = jnp.where(rows == 0, prev_ref[0:1, :], pltpu.roll(hn, 1, axis=0))
    prev_ref[0:1, :] = hn[tm - 1:tm, :]
    xx = shifted - hn
    xr, xw, xk, xv, xa, xg = (_mx(hn + xx * mu_ref[i:i + 1, :]) for i in range(6))
    w0, a0, kkp, kap = (vec_ref[i:i + 1, :] for i in range(4))
    r = jnp.dot(xr, wr_ref[...], preferred_element_type=F32)
    k = jnp.dot(xk, wk_ref[...], preferred_element_type=F32)
    v = jnp.dot(xv, wv_ref[...], preferred_element_type=F32)
    z = w0 + _dot(jnp.tanh(jnp.dot(xw, w1_ref[...], preferred_element_type=F32)), w2_ref[...])
    a = jax.nn.sigmoid(a0 + _dot(jnp.dot(xa, a1_ref[...], preferred_element_type=F32), a2_ref[...]))
    g = _dot(jax.nn.sigmoid(jnp.dot(xg, g1_ref[...], preferred_element_type=F32)), g2_ref[...])
    ld_out[0] = (-math.exp(-0.5)) * jax.nn.sigmoid(z)
    a_out[0] = a
    r_out[0] = r.astype(r_out.dtype)
    v_out[0] = v.astype(v_out.dtype)
    kk_out[0] = (k * kkp).astype(kk_out.dtype)
    k_out[0] = (k * (1.0 + (a - 1.0) * kap)).astype(k_out.dtype)
    g_out[0] = g.astype(g_out.dtype)


def _rwkv_pre(x, ln, mu, wr, wk, wv, w1, a1, g1, w2, a2, g2, vecs, tm=256):
    b, t, d = x.shape
    act = jax.ShapeDtypeStruct((b, t, d), ACT_DTYPE)
    f32o = jax.ShapeDtypeStruct((b, t, d), F32)
    blk = pl.BlockSpec((1, tm, d), lambda i, j: (i, j, 0))
    consts = [ln, mu, wr, wk, wv, w1, a1, g1, w2, a2, g2, vecs]
    return pl.pallas_call(
        _rwkv_pre_kernel,
        grid=(b, t // tm),
        in_specs=[blk] + [_const_spec(c.shape) for c in consts],
        out_specs=[blk] * 7,
        out_shape=[act, act, act, act, act, f32o, f32o],
        scratch_shapes=[pltpu.VMEM((8, d), F32)],
        compiler_params=_params(("arbitrary", "arbitrary")),
        name="rwkv_pre",
    )(x, *consts)


def _stack4(x):
    return jnp.concatenate([x[:, MXU_TILE * q:MXU_TILE * (q + 1)] for q in range(4)], axis=0)


def _unstack4(y):
    c = y.shape[0] // 4
    return jnp.concatenate([y[c * q:c * (q + 1), :] for q in range(4)], axis=1)


def _rwkv_scan_kernel(r_ref, k_ref, v_ref, kk_ref, g_ref, ld_ref, a_ref,
                      ltri_ref, smask_ref, eye_ref, j_ref, rk_ref, lnw_ref, lnb_ref,
                      y_out, h_ref):
    c = RWKV_CHUNK
    nb = r_ref.shape[0]

    @pl.when(pl.program_id(1) == 0)
    def _():
        h_ref[...] = jnp.zeros_like(h_ref)

    def rows_of(ref):
        return jnp.concatenate([ref[bi].astype(F32) for bi in range(nb)], axis=0)

    ld, r, k, v, kk = (rows_of(x) for x in (ld_ref, r_ref, k_ref, v_ref, kk_ref))
    jones = j_ref[...]
    ld_hi, ld_lo = _split2(ld)
    cum = (jnp.dot(ltri_ref[...], ld_hi, preferred_element_type=F32)
           + jnp.dot(ltri_ref[...], ld_lo, preferred_element_type=F32))
    last = [cum[(bi + 1) * c - 1:(bi + 1) * c, :] for bi in range(nb)]
    cum_last = jnp.concatenate([jnp.broadcast_to(x, (c, x.shape[1])) for x in last], axis=0)
    e_l = jnp.exp(cum)
    e_lm = jnp.exp(cum - ld)
    e_nl = jnp.exp(-cum)
    e_cl = jnp.exp(cum_last - cum)
    e_c = [jnp.exp(x) for x in last]
    ss = _unstack4(_dot_hl(_stack4(kk * kk), jones))
    kkn = kk * lax.rsqrt(jnp.maximum(ss, 1e-24))
    a_vec = -kkn
    b_vec = kkn * rows_of(a_ref)
    rt = r * e_l
    at = a_vec * e_lm
    kt = k * e_nl
    bt = b_vec * e_nl
    bh = b_vec * e_cl
    kh = k * e_cl

    lane = lax.broadcasted_iota(jnp.int32, (1, LANES), 1)
    m_l = (lane < RWKV_HEAD).astype(F32)
    m_r = 1.0 - m_l

    def st(xp):
        return jnp.concatenate([xp * m_l, xp * m_r], axis=0)

    def dup(xp):
        return jnp.concatenate([xp, xp], axis=0)

    smask = smask_ref[...] > 0.0
    eye = eye_ref[...]
    n_pair = D_MODEL // LANES
    pairs = range(nb * n_pair)
    seq_of = [p // n_pair for p in pairs]

    def part(x, p):
        return x[seq_of[p] * c:(seq_of[p] + 1) * c, LANES * (p % n_pair):LANES * (p % n_pair + 1)]

    at_s = [st(part(at, p)) for p in pairs]
    rt_s = [st(part(rt, p)) for p in pairs]
    v_s = [st(part(v, p)) for p in pairs]
    s = [jnp.where(smask,
                   _dot_nt(jnp.concatenate([at_s[p], rt_s[p]], axis=0),
                           jnp.concatenate([dup(part(bt, p)), dup(part(kt, p))], axis=0)),
                   0.0) for p in pairs]
    a_ab = [x[:2 * c, :2 * c] for x in s]
    a_r = [x[2 * c:, :] for x in s]
    g1 = [_dot(s[p][:2 * c, 2 * c:], v_s[p]) for p in pairs]
    t_m = [eye + x for x in a_ab]
    a_pow = [_dot(x, x) for x in a_ab]
    n = 2
    while n < c:
        if 2 * n < c:
            both = [_dot(a_pow[p], jnp.concatenate([a_pow[p], t_m[p]], axis=1)) for p in pairs]
            a_pow = [x[:, :2 * c] for x in both]
            t_m = [t_m[p] + both[p][:, 2 * c:] for p in pairs]
        else:
            t_m = [t_m[p] + _dot(a_pow[p], t_m[p]) for p in pairs]
        n *= 2
    tw = [_dot(t_m[p], jnp.concatenate([at_s[p], g1[p]], axis=1)) for p in pairs]
    zmat = [jnp.concatenate([tw[p], jnp.concatenate([jnp.zeros_like(v_s[p]), v_s[p]], axis=1)], axis=0)
            for p in pairs]
    upper = [_dot(a_r[p], zmat[p]) for p in pairs]
    bk_t = [jnp.concatenate([st(part(bh, p)), st(part(kh, p))], axis=0).T for p in pairs]
    lower = [_dot(bk_t[p], zmat[p]) for p in pairs]
    decay = [eye * e_c[seq_of[p]][:, LANES * (p % n_pair):LANES * (p % n_pair + 1)] for p in pairs]
    seq = [_dot(jnp.concatenate([upper[p][:, :LANES] + rt_s[p], lower[p][:, :LANES] + decay[p]], axis=0), h_ref[p])
           for p in pairs]
    ys = []
    for p in pairs:
        y_st = seq[p][:2 * c, :] + upper[p][:, LANES:]
        h_ref[p] = seq[p][2 * c:, :] + lower[p][:, LANES:]
        ys.append(y_st[:c, :] + y_st[c:, :])
    y = jnp.concatenate([jnp.concatenate(ys[bi * n_pair:(bi + 1) * n_pair], axis=1) for bi in range(nb)],
                        axis=0)

    inv_n = 1.0 / RWKV_HEAD
    y4 = _stack4(y)
    dlt = y4 - _dot_hl(y4, jones) * inv_n
    var = _dot_hl(dlt * dlt, jones) * inv_n
    yn = dlt * lax.rsqrt(var + RWKV_GN_EPS)
    bonus = _dot_hl(_stack4(r * k) * rk_ref[...], jones) * _stack4(v)
    out = _unstack4((yn * lnw_ref[...] + lnb_ref[...] + bonus) * _stack4(rows_of(g_ref)))
    for bi in range(nb):
        y_out[bi] = out[bi * c:(bi + 1) * c, :].astype(y_out.dtype)


def _scan_constants():
    c = RWKV_CHUNK
    ltri = np.kron(np.eye(RWKV_SEQS), np.tril(np.ones((c, c), np.float32)))
    rho = np.arange(4 * c)
    r_type, r_head, r_t = rho // (2 * c), (rho % (2 * c)) // c, rho % c
    same = r_head[:, None] == r_head[None, :]
    strict = r_t[None, :] < r_t[:, None]
    incl = r_t[None, :] <= r_t[:, None]
    smask = same & np.where(r_type[:, None] == 0, strict, incl)
    eye = np.eye(LANES, dtype=np.float32)
    head = np.arange(MXU_TILE) // RWKV_HEAD
    jones = (head[:, None] == head[None, :]).astype(np.float32)
    return (jnp.asarray(ltri, MXU_DTYPE), jnp.asarray(smask, F32), jnp.asarray(eye, F32),
            jnp.asarray(jones, MXU_DTYPE))


def _stack_param(p):
    return jnp.repeat(p.reshape(4, MXU_TILE).astype(F32), RWKV_SEQS * RWKV_CHUNK, axis=0)


def _rwkv_scan(r, k, v, kk, g, ld, a, rk, lnw, lnb):
    b, t, d = r.shape
    c, nb = RWKV_CHUNK, RWKV_SEQS
    blk = pl.BlockSpec((nb, c, d), lambda i, j: (i, j, 0))
    consts = list(_scan_constants()) + [_stack_param(rk.reshape(-1)), _stack_param(lnw), _stack_param(lnb)]
    return pl.pallas_call(
        _rwkv_scan_kernel,
        grid=(b // nb, t // c),
        in_specs=[blk] * 7 + [_const_spec(x.shape) for x in consts],
        out_specs=blk,
        out_shape=jax.ShapeDtypeStruct((b, t, d), ACT_DTYPE),
        scratch_shapes=[pltpu.VMEM((nb * (d // LANES), LANES, LANES), F32)],
        compiler_params=_params(("arbitrary", "arbitrary")),
        name="rwkv_scan",
    )(r, k, v, kk, g, ld, a, *consts)


def _matmul_res_kernel(a_ref, w_ref, res_ref, o_ref):
    _tok_store(o_ref, res_ref[...] + jnp.dot(a_ref[...], w_ref[...], preferred_element_type=F32))


def _matmul_res(a, w, res, tm=512):
    n, kdim = a.shape
    d = w.shape[1]
    return pl.pallas_call(
        _matmul_res_kernel,
        grid=(n // tm,),
        in_specs=[pl.BlockSpec((tm, kdim), lambda i: (i, 0)), _const_spec(w.shape),
                  pl.BlockSpec((tm, d), lambda i: (i, 0))],
        out_specs=_tok_spec(tm, lambda i: (i, 0)),
        out_shape=jax.ShapeDtypeStruct((TOK_ROWS * n, LANES), F32),
        compiler_params=_params(("arbitrary",)),
        name="matmul_res",
    )(a, w, res)


def _rwkv_block(x, a_ln, a_mu, a_wr, a_wk, a_wv, a_wo, a_w0, a_w1, a_w2, a_a0, a_a1, a_a2,
                a_g1, a_g2, a_kk, a_ka, a_rk, a_lnx_w, a_lnx_b):
    b, t, d = x.shape
    vecs = jnp.stack([a_w0, a_a0, a_kk, a_ka]).astype(F32)
    r, k, v, kk, g, ld, a = _rwkv_pre(
        x, a_ln.reshape(1, d), a_mu, _mx(a_wr), _mx(a_wk), _mx(a_wv), _mx(a_w1), _mx(a_a1), _mx(a_g1),
        _mx(a_w2), _mx(a_a2), _mx(a_g2), vecs)
    y = _rwkv_scan(r, k, v, kk, g, ld, a, a_rk, a_lnx_w, a_lnx_b)
    return _matmul_res(y.reshape(b * t, d), _mx(a_wo), x.reshape(b * t, d))


MOE_GROUPS = 4
MOE_EPG = 8
MOE_EXPERTS = MOE_GROUPS * MOE_EPG
MOE_TOPK = 2
MOE_BLOCK = 256
NEG = -1e30


def _route(logits):
    lane = lax.broadcasted_iota(jnp.int32, logits.shape, 1)
    is_grp = lane < MOE_GROUPS
    gl = jnp.where(is_grp, logits, NEG)
    ge = jnp.exp(gl - jnp.max(gl, axis=-1, keepdims=True))
    gp = ge / jnp.sum(ge, axis=-1, keepdims=True)
    grp_w = jnp.max(gp, axis=-1, keepdims=True)
    grp = jnp.min(jnp.where(is_grp & (gp == grp_w), lane, LANES), axis=-1, keepdims=True)
    lo = MOE_GROUPS + MOE_EPG * grp
    in_grp = (lane >= lo) & (lane < lo + MOE_EPG)
    el = jnp.where(in_grp, logits, NEG)
    ee = jnp.exp(el - jnp.max(el, axis=-1, keepdims=True))
    ep = ee / jnp.sum(ee, axis=-1, keepdims=True)
    p1 = jnp.max(jnp.where(in_grp, ep, -1.0), axis=-1, keepdims=True)
    i1 = jnp.min(jnp.where(in_grp & (ep == p1), lane, LANES), axis=-1, keepdims=True)
    rest = in_grp & (lane != i1)
    p2 = jnp.max(jnp.where(rest, ep, -1.0), axis=-1, keepdims=True)
    i2 = jnp.min(jnp.where(rest & (ep == p2), lane, LANES), axis=-1, keepdims=True)
    scale = grp_w / (p1 + p2)
    return i1 - MOE_GROUPS, i2 - MOE_GROUPS, p1 * scale, p2 * scale


def _router_logits(xf, w_hi_ref, w_lo_ref, bias_ref):
    hi, lo = _split2(xf)
    logits = (jnp.dot(hi, w_hi_ref[...], preferred_element_type=F32)
              + jnp.dot(lo, w_hi_ref[...], preferred_element_type=F32)
              + jnp.dot(hi, w_lo_ref[...], preferred_element_type=F32))
    return logits + bias_ref[...]


def _moe_router_kernel(h_ref, ln_ref, w_hi_ref, w_lo_ref, bias_ref, tri_ref, route_out, count_out, cnt_ref):
    @pl.when(pl.program_id(0) == 0)
    def _():
        cnt_ref[...] = jnp.zeros_like(cnt_ref)

    xf = _rms(_tok_load(h_ref), ln_ref[...])
    e1, e2, w1, w2 = _route(_router_logits(xf, w_hi_ref, w_lo_ref, bias_ref))
    lane = lax.broadcasted_iota(jnp.int32, (xf.shape[0], LANES), 1)
    hit1, hit2 = lane == e1, lane == e2
    onehot = jnp.where(hit1, 1.0, jnp.where(hit2, 1.0, 0.0))
    before = jnp.dot(tri_ref[...], _mx(onehot), preferred_element_type=F32) + cnt_ref[...]
    r1 = jnp.sum(jnp.where(hit1, before, 0.0), axis=-1, keepdims=True)
    r2 = jnp.sum(jnp.where(hit2, before, 0.0), axis=-1, keepdims=True)
    cnt_ref[...] = cnt_ref[...] + jnp.sum(onehot, axis=0, keepdims=True)
    count_out[...] = cnt_ref[...]
    fields = (e1.astype(F32), e2.astype(F32), w1, w2, r1, r2)
    rec = jnp.zeros(lane.shape, F32)
    for idx, val in enumerate(fields):
        rec = jnp.where(lane == idx, val, rec)
    route_out[...] = rec


def _router_weights(w_gr, b_gr, w_er, b_er):
    d = w_gr.shape[0]
    w = jnp.zeros((d, LANES), F32).at[:, :MOE_GROUPS].set(w_gr).at[:, MOE_GROUPS:MOE_GROUPS + MOE_EXPERTS].set(w_er)
    bias = jnp.zeros((1, LANES), F32).at[0, :MOE_GROUPS].set(b_gr).at[0, MOE_GROUPS:MOE_GROUPS + MOE_EXPERTS].set(b_er)
    w_hi = w.astype(MXU_DTYPE)
    w_lo = (w - w_hi.astype(F32)).astype(MXU_DTYPE)
    return w_hi, w_lo, bias


def _moe_router(h, ln, w_hi, w_lo, bias, tm=512):
    n, d = h.shape[0] // TOK_ROWS, ln.shape[0]
    tri = jnp.asarray(np.tril(np.ones((tm, tm), np.float32), -1), MXU_DTYPE)
    return pl.pallas_call(
        _moe_router_kernel,
        grid=(n // tm,),
        in_specs=[_tok_spec(tm, lambda i: (i, 0)), _const_spec((1, d)), _const_spec(w_hi.shape),
                  _const_spec(w_lo.shape),
                  _const_spec(bias.shape), _const_spec(tri.shape)],
        out_specs=[pl.BlockSpec((tm, LANES), lambda i: (i, 0)), _const_spec((1, LANES))],
        out_shape=[jax.ShapeDtypeStruct((n, LANES), F32), jax.ShapeDtypeStruct((1, LANES), F32)],
        scratch_shapes=[pltpu.VMEM((1, LANES), F32)],
        compiler_params=_params(("arbitrary",)),
        name="moe_router",
    )(h, ln.reshape(1, d), w_hi, w_lo, bias, tri)


def _moe_plan(route, counts, blk):
    n = route.shape[0]
    counts = counts[0, :MOE_EXPERTS].astype(jnp.int32)
    pcounts = (counts + blk - 1) // blk * blk
    pend = jnp.cumsum(pcounts)
    pstart = (pend - pcounts).astype(jnp.int32)
    n_blk = n * MOE_TOPK // blk + MOE_EXPERTS
    blk_start = jnp.arange(n_blk, dtype=jnp.int32) * blk
    blk_e = jnp.sum((blk_start[:, None] >= pend[None, :]).astype(jnp.int32), axis=1)
    blk_e = jnp.minimum(blk_e, MOE_EXPERTS - 1)
    n_valid = jnp.clip((pstart + counts)[blk_e] - blk_start, 0, blk).astype(jnp.int32)
    n_act = (pend[-1] // blk).astype(jnp.int32).reshape(1)
    expert = route[:, 0:MOE_TOPK].astype(jnp.int32)
    onehot = expert[:, :, None] == jnp.arange(MOE_EXPERTS, dtype=jnp.int32)
    seg_start = jnp.sum(jnp.where(onehot, pstart, 0), axis=-1)
    sorted_pos = seg_start + route[:, 4:4 + MOE_TOPK].astype(jnp.int32)
    return blk_e, n_valid, n_act, sorted_pos


def _token_copies(pos_ref, tm, make_copy):
    group = 8

    def body(g, carry):
        t0 = g * group
        pos = [[pos_ref[0, 0, MOE_TOPK * (t0 + j) + k] for k in range(MOE_TOPK)] for j in range(group)]
        for j in range(group):
            for k in range(MOE_TOPK):
                make_copy(t0 + j, k, pos[j][k]).start()
        return carry

    lax.fori_loop(0, tm // group, body, 0)


def _tile(ref, tok):
    return ref.at[pl.ds(pl.multiple_of(tok * TOK_ROWS, TOK_ROWS), TOK_ROWS)]


def _moe_dispatch_kernel(n_valid_ref, pos_ref, src_ref, dst_hbm, stage, sem, zbuf, zsem):
    tm = src_ref.shape[0] // TOK_ROWS
    step = pl.program_id(0)
    slot = lax.rem(step, 2)

    def drain(s):
        for k in range(MOE_TOPK):
            pltpu.make_async_copy(stage.at[s], dst_hbm.at[pl.ds(0, stage.shape[1])], sem.at[s]).wait()

    @pl.when(pl.program_id(0) == 0)
    def _():
        rows = zbuf.shape[0]
        zbuf[...] = jnp.zeros_like(zbuf)

        def fill(j, wait):
            @pl.when(n_valid_ref[j] < rows // TOK_ROWS)
            def _():
                start = 0 if wait else pl.multiple_of(j * rows, rows)
                cp = pltpu.make_async_copy(zbuf, dst_hbm.at[pl.ds(start, rows)], zsem)
                cp.wait() if wait else cp.start()

        lax.fori_loop(0, n_valid_ref.shape[0], lambda j, c: (fill(j, False), c)[1], 0)
        lax.fori_loop(0, n_valid_ref.shape[0], lambda j, c: (fill(j, True), c)[1], 0)

    @pl.when(step >= 2)
    def _():
        drain(slot)

    stage[slot] = src_ref[...]
    _token_copies(pos_ref, tm, lambda t, k, pos: pltpu.make_async_copy(
        _tile(stage.at[slot], t), _tile(dst_hbm, pos), sem.at[slot]))

    @pl.when(step == pl.num_programs(0) - 1)
    def _():
        @pl.when(step >= 1)
        def _():
            drain(1 - slot)
        drain(slot)


def _moe_gather_combine(step, n_steps, pos_ref, pos_next_ref, ys_hbm, h_ref, route, gbuf, gsem):
    tm = gbuf.shape[2] // TOK_ROWS
    slot = lax.rem(step, 2)

    def start(p_ref, s):
        _token_copies(p_ref, tm, lambda t, k, pos: pltpu.make_async_copy(
            _tile(ys_hbm, pos), _tile(gbuf.at[s, k], t), gsem.at[s]))

    @pl.when(step == 0)
    def _():
        start(pos_ref, 0)

    @pl.when(step + 1 < n_steps)
    def _():
        start(pos_next_ref, 1 - slot)

    for k in range(MOE_TOPK):
        pltpu.make_async_copy(ys_hbm.at[pl.ds(0, gbuf.shape[2])], gbuf.at[slot, k], gsem.at[slot]).wait()
    return (_tok_load(h_ref) + route[:, 2:3] * _tok_load(gbuf.at[slot, 0])
            + route[:, 3:4] * _tok_load(gbuf.at[slot, 1]))


def _gather_specs(sorted_pos, tm, step_index):
    n = sorted_pos.shape[0]
    pos = sorted_pos.reshape(n // tm, 1, MOE_TOPK * tm)
    last = n // tm - 1
    cur = pl.BlockSpec((1, 1, MOE_TOPK * tm), lambda *g: (step_index(*g), 0, 0), memory_space=pltpu.SMEM)
    nxt = pl.BlockSpec((1, 1, MOE_TOPK * tm), lambda *g: (jnp.minimum(step_index(*g) + 1, last), 0, 0),
                       memory_space=pltpu.SMEM)
    scratch = [pltpu.VMEM((2, MOE_TOPK, TOK_ROWS * tm, LANES), F32), pltpu.SemaphoreType.DMA((2,))]
    return pos, [cur, nxt, pl.BlockSpec(memory_space=pl.ANY)], scratch


def _moe_dispatch(h, n_valid, sorted_pos, p, blk, tm=512):
    n = sorted_pos.shape[0]
    pos = sorted_pos.reshape(n // tm, 1, MOE_TOPK * tm)
    grid_spec = pltpu.PrefetchScalarGridSpec(
        num_scalar_prefetch=1,
        grid=(n // tm,),
        in_specs=[pl.BlockSpec((1, 1, MOE_TOPK * tm), lambda i, nv: (i, 0, 0), memory_space=pltpu.SMEM),
                  _tok_spec(tm, lambda i, nv: (i, 0))],
        out_specs=pl.BlockSpec(memory_space=pl.ANY),
        scratch_shapes=[pltpu.VMEM((2, TOK_ROWS * tm, LANES), h.dtype), pltpu.SemaphoreType.DMA((2,)),
                        pltpu.VMEM((TOK_ROWS * blk, LANES), h.dtype), pltpu.SemaphoreType.DMA],
    )
    return pl.pallas_call(
        _moe_dispatch_kernel,
        grid_spec=grid_spec,
        out_shape=jax.ShapeDtypeStruct((TOK_ROWS * p, LANES), h.dtype),
        compiler_params=_params(("arbitrary",)),
        name="moe_dispatch",
    )(n_valid, pos, h)


def _moe_expert_kernel(blk_e_ref, n_act_ref, x_ref, ln_ref, w1_ref, w3_ref, w2_ref, y_ref):
    del blk_e_ref
    i = pl.program_id(0)

    @pl.when(i < n_act_ref[0])
    def _():
        xn = _mx(_rms(_tok_load(x_ref), ln_ref[...]))
        h1 = jnp.dot(xn, w1_ref[0], preferred_element_type=F32)
        h3 = jnp.dot(xn, w3_ref[0], preferred_element_type=F32)
        _tok_store(y_ref, _dot(h1 * jax.nn.sigmoid(h1) * h3, w2_ref[0]))

    @pl.when(i >= n_act_ref[0])
    def _():
        y_ref[...] = jnp.zeros_like(y_ref)


def _moe_experts(xs, ln, w1, w3, w2, blk_e, n_act, blk):
    p = xs.shape[0] // TOK_ROWS
    d, hdim = w1.shape[1], w1.shape[2]
    grid_spec = pltpu.PrefetchScalarGridSpec(
        num_scalar_prefetch=2,
        grid=(p // blk,),
        in_specs=[_tok_spec(blk, lambda i, be, na: (i, 0)),
                  pl.BlockSpec((1, d), lambda i, be, na: (0, 0)),
                  pl.BlockSpec((1, d, hdim), lambda i, be, na: (be[i], 0, 0)),
                  pl.BlockSpec((1, d, hdim), lambda i, be, na: (be[i], 0, 0)),
                  pl.BlockSpec((1, hdim, d), lambda i, be, na: (be[i], 0, 0))],
        out_specs=_tok_spec(blk, lambda i, be, na: (i, 0)),
    )
    return pl.pallas_call(
        _moe_expert_kernel,
        grid_spec=grid_spec,
        out_shape=jax.ShapeDtypeStruct(xs.shape, F32),
        compiler_params=_params(("arbitrary",)),
        name="moe_experts",
    )(blk_e, n_act, xs, ln.reshape(1, d), w1, w3, w2)


def _moe_layer(h, ln, w_gr, b_gr, w_er, b_er, w1, w3, w2):
    n = h.shape[0] // TOK_ROWS
    blk = MOE_BLOCK
    route, counts = _moe_router(h, ln, *_router_weights(w_gr, b_gr, w_er, b_er))
    blk_e, n_valid, n_act, sorted_pos = _moe_plan(route, counts, blk)
    p = n * MOE_TOPK + MOE_EXPERTS * blk
    xs = _moe_dispatch(h, n_valid, sorted_pos, p, blk)
    ys = _moe_experts(xs, ln, _mx(w1), _mx(w3), _mx(w2), blk_e, n_act, blk)
    return ys, sorted_pos, route


def _final_kernel(pos_ref, pos_next_ref, ys_hbm, h_ref, route_ref, ln_ref, o_ref, gbuf, gsem):
    h = _moe_gather_combine(pl.program_id(0), pl.num_programs(0), pos_ref, pos_next_ref, ys_hbm, h_ref,
                            route_ref[...], gbuf, gsem)
    o_ref[...] = _rms(h, ln_ref[...])


def _final(h, moe_out, ln, tm=512):
    ys, sorted_pos, route = moe_out
    n, d = route.shape[0], ln.shape[0]
    pos, gather_specs, scratch = _gather_specs(sorted_pos, tm, lambda i: i)
    return pl.pallas_call(
        _final_kernel,
        grid=(n // tm,),
        in_specs=gather_specs + [_tok_spec(tm, lambda i: (i, 0)), pl.BlockSpec((tm, LANES), lambda i: (i, 0)),
                                 _const_spec((1, d))],
        out_specs=pl.BlockSpec((tm, d), lambda i: (i, 0)),
        out_shape=jax.ShapeDtypeStruct((n, d), F32),
        scratch_shapes=scratch,
        compiler_params=_params(("arbitrary",)),
        name="final_norm",
    )(pos, pos, ys, h, route, ln.reshape(1, d))


NSA_HEADS = 16
NSA_GROUPS = 4
NSA_HPG = NSA_HEADS // NSA_GROUPS
NSA_DH = D_MODEL // NSA_HEADS
NSA_PARTS = 6
CMP_BLOCK = 32
CMP_STRIDE = 16
SEL_BLOCK = 64
SEL_TOPK = 8
WINDOW = 512
N_BRANCH = 3
IMP_FORCE = 1e4
REL_BUCKETS = 32
REL_MAX_DIST = 128
LOG2E = math.log2(math.e)
NSA_TQ = 256
NSA_TK = 512
NSA_ROW_BLOCK = 32
GATE_LANES = LANES
CMP_PARTS = 2


def _nsa_proj_kernel(pos_ref, pos_next_ref, ys_hbm, h_ref, route_ref, kvln_ref, qln_ref, kvw_ref, qw_ref,
                     h_out, q_out, gate_out, kvc_out, kv_out, gbuf, gsem):
    step = pl.program_id(0) * pl.num_programs(1) + pl.program_id(1)
    h1 = _moe_gather_combine(step, pl.num_programs(0) * pl.num_programs(1), pos_ref, pos_next_ref, ys_hbm, h_ref,
                             route_ref[0], gbuf, gsem)
    h_out[0] = h1
    kv = _dot(_rms(h1, kvln_ref[...]), kvw_ref[...])
    for part in range(NSA_PARTS):
        for g in range(NSA_GROUPS):
            col = (part * NSA_GROUPS + g) * NSA_DH
            piece = kv[:, col:col + NSA_DH]
            if part < CMP_PARTS:
                kvc_out[part, 0, g] = piece
            else:
                kv_out[part - CMP_PARTS, 0, g] = piece.astype(kv_out.dtype)
    qg = _dot(_rms(h1, qln_ref[...]), qw_ref[...])
    q_out[0] = (qg[:, :D_MODEL] * (NSA_DH ** -0.5 * LOG2E)).astype(q_out.dtype)
    gate_out[0] = qg[:, D_MODEL:]


def _gate_padded_weight(q_w):
    d = q_w.shape[0]
    n_gate = NSA_HPG * N_BRANCH
    gates = q_w[:, d:].reshape(d, NSA_GROUPS, n_gate)
    gates = jnp.pad(gates, ((0, 0), (0, 0), (0, GATE_LANES - n_gate))).reshape(d, NSA_GROUPS * GATE_LANES)
    return jnp.concatenate([q_w[:, :d], gates], axis=1)


def _nsa_proj(h, moe_out, kv_ln, q_ln, kv_w, q_w, b, t, tm=256):
    ys, sorted_pos, route = moe_out
    d = kv_ln.shape[0]
    blk = lambda w: pl.BlockSpec((1, tm, w), lambda i, j: (i, j, 0))
    per_b = t // tm
    tok_block = lambda i, j: i * per_b + j
    pos, gather_specs, scratch = _gather_specs(sorted_pos, tm, tok_block)
    gw = NSA_GROUPS * GATE_LANES
    qw = _gate_padded_weight(q_w)
    kv_spec = lambda parts: pl.BlockSpec((parts, 1, NSA_GROUPS, tm, NSA_DH), lambda i, j: (0, i, 0, j, 0))
    kv_shape = lambda parts: (parts, b, NSA_GROUPS, t, NSA_DH)
    return pl.pallas_call(
        _nsa_proj_kernel,
        grid=(b, per_b),
        in_specs=gather_specs + [_tok_spec(tm, lambda i, j: (tok_block(i, j), 0)), blk(LANES),
                                 _const_spec((1, d)), _const_spec((1, d)),
                                 _const_spec(kv_w.shape), _const_spec(qw.shape)],
        out_specs=[blk(d), blk(d), blk(gw), kv_spec(CMP_PARTS), kv_spec(NSA_PARTS - CMP_PARTS)],
        out_shape=[jax.ShapeDtypeStruct((b, t, d), F32), jax.ShapeDtypeStruct((b, t, d), ACT_DTYPE),
                   jax.ShapeDtypeStruct((b, t, gw), F32), jax.ShapeDtypeStruct(kv_shape(CMP_PARTS), F32),
                   jax.ShapeDtypeStruct(kv_shape(NSA_PARTS - CMP_PARTS), ACT_DTYPE)],
        scratch_shapes=scratch,
        compiler_params=_params(("arbitrary", "arbitrary")),
        name="nsa_proj",
    )(pos, pos, ys, h, route.reshape(b, t, LANES),
      kv_ln.reshape(1, d), q_ln.reshape(1, d), _mx(kv_w), _mx(qw))


def _gelu_tanh(x):
    return 0.5 * x * (1.0 + jnp.tanh(math.sqrt(2.0 / math.pi) * (x + 0.044715 * (x * x * x))))


def _compress_kernel(k_ref, v_ref, pek_ref, pev_ref, kw1_ref, kw2_ref, vw1_ref, vw2_ref, kc_out, vc_out):
    dh = k_ref.shape[-1]
    n_chunk = k_ref.shape[3] // CMP_STRIDE

    def one(z_ref, pe_ref, w1_ref, w2_ref, out):
        first = jnp.zeros((n_chunk, w1_ref.shape[1]), F32)
        second = first
        for l in range(CMP_STRIDE):
            z = z_ref[0, 0, 0, pl.ds(l, n_chunk, stride=CMP_STRIDE), :]
            lo, hi = l, CMP_STRIDE + l
            first = first + _dot(z + pe_ref[0, lo:lo + 1, :], w1_ref[dh * lo:dh * (lo + 1), :])
            second = second + _dot(z + pe_ref[0, hi:hi + 1, :], w1_ref[dh * hi:dh * (hi + 1), :])
        hid = first + pltpu.roll(second, n_chunk - 1, axis=0)
        out[0, 0] = _dot(_gelu_tanh(hid), w2_ref[...]).astype(out.dtype)

    one(k_ref, pek_ref, kw1_ref, kw2_ref, kc_out)
    one(v_ref, pev_ref, vw1_ref, vw2_ref, vc_out)


def _compress(kvc, pe_k, pe_v, k_w1, k_w2, v_w1, v_w2):
    _, b, g, t, dh = kvc.shape
    n_chunk = t // CMP_STRIDE
    part = lambda idx: pl.BlockSpec((1, 1, 1, t, dh), lambda i, j: (idx, i, j, 0, 0))
    pe_spec = pl.BlockSpec((1, CMP_BLOCK, dh), lambda i, j: (j, 0, 0))
    out_spec = pl.BlockSpec((1, 1, n_chunk, dh), lambda i, j: (i, j, 0, 0))
    out = jax.ShapeDtypeStruct((b, g, n_chunk, dh), ACT_DTYPE)
    return pl.pallas_call(
        _compress_kernel,
        grid=(b, g),
        in_specs=[part(0), part(1), pe_spec, pe_spec, _const_spec(k_w1.shape), _const_spec(k_w2.shape),
                  _const_spec(v_w1.shape), _const_spec(v_w2.shape)],
        out_specs=[out_spec, out_spec],
        out_shape=[out, out],
        compiler_params=_params(("arbitrary", "arbitrary")),
        name="nsa_compress",
    )(kvc, kvc, pe_k.astype(F32), pe_v.astype(F32), _mx(k_w1), _mx(k_w2), _mx(v_w1), _mx(v_w2))


def _rel_bucket(dist):
    n = jnp.maximum(dist, 0)
    max_exact = REL_BUCKETS // 2
    nf = jnp.maximum(n, 1).astype(F32)
    large = max_exact + (jnp.log(nf / max_exact) / math.log(REL_MAX_DIST / max_exact)
                         * (REL_BUCKETS - max_exact)).astype(jnp.int32)
    return jnp.where(n < max_exact, n, jnp.minimum(large, REL_BUCKETS - 1))


def _toeplitz(vals, rows, cols):
    span = rows + cols
    z = jnp.concatenate([vals, vals[..., :1]], axis=-1)
    flat = jnp.tile(z, rows)[..., :rows * (span - 1)]
    skew = flat.reshape(vals.shape[:-1] + (rows, span - 1))
    return skew[..., rows - 1:rows - 1 + cols]


def _nsa_bias_tiles(rel_bias, t):
    tq, tk = NSA_TQ, NSA_TK
    table = rel_bias.astype(F32).reshape(REL_BUCKETS, NSA_GROUPS, NSA_HPG).transpose(1, 2, 0)
    by_dist = table[:, :, _rel_bucket(jnp.arange(REL_MAX_DIST + 1))]
    far = by_dist[:, :, REL_MAX_DIST:]

    def band(dist, valid, shift=None):
        b = by_dist[:, :, np.clip(dist, 0, REL_MAX_DIST)]
        if shift is not None:
            b = b - shift
        return jnp.where(jnp.asarray(valid), b * LOG2E, NEG)

    u = lambda rows, cols: np.arange(rows + cols - 1) - (rows - 1)
    wk = WINDOW + tq
    dist = np.stack([v * tq - max(v * tq - WINDOW, 0) - u(tq, wk) for v in range(WINDOW // tq + 1)])
    bias_win = _toeplitz(band(dist, (dist >= 0) & (dist < WINDOW)), tq, wk).swapaxes(-1, -2)
    n_near = (REL_MAX_DIST + tk - 1) // tq + 1
    dist = np.stack([e * tq - u(tq, tk) for e in range(n_near)])
    near = _toeplitz(band(dist, dist >= 0, far[:, :, None]), tq, tk).swapaxes(-1, -2)
    bias_sel = jnp.concatenate([near, jnp.zeros_like(near[:, :, :1])], axis=2)
    n_cmp = t // CMP_STRIDE
    dist = np.stack([b - (CMP_BLOCK - 1) + CMP_STRIDE * u(n_cmp, n_cmp) for b in range(CMP_STRIDE)])
    per_b = _toeplitz(band(dist, dist >= 0), n_cmp, n_cmp)
    bias_cmp = jnp.moveaxis(per_b, 2, 4).reshape(NSA_GROUPS, NSA_HPG, n_cmp, t)
    real = np.arange(n_cmp) < (t - CMP_BLOCK) // CMP_STRIDE + 1
    bias_cmp = jnp.where(jnp.asarray(real)[:, None], bias_cmp, NEG)
    return bias_cmp, bias_win, bias_sel


def _nsa_constants(t):
    n_sel = t // SEL_BLOCK
    n_cmp = t // CMP_STRIDE
    cmp_start = np.arange(n_cmp) * CMP_STRIDE
    sel_start = np.arange(n_sel) * SEL_BLOCK
    overlap = ((cmp_start[:, None] < sel_start[None, :] + SEL_BLOCK)
               & (cmp_start[:, None] + CMP_BLOCK > sel_start[None, :]))
    overlap[(t - CMP_BLOCK) // CMP_STRIDE + 1:, :] = False
    return jnp.asarray(overlap.T, MXU_DTYPE)


def _nsa_attn_kernel(q_ref, gate_ref, kc_ref, vc_ref, ks_ref, vs_ref, kw_ref, vw_ref,
                     bcmp_ref, bwin_ref, bsel_ref, ovl_ref, o_ref, neg_ref):
    tq, tk, dh = NSA_TQ, NSA_TK, NSA_DH
    i = pl.program_id(2)
    t0 = i * tq
    heads = range(NSA_HPG)
    q_t = q_ref[0].astype(F32).T
    qh = [_mx(q_t[dh * h:dh * (h + 1), :]) for h in heads]

    def values_t(v):
        ones_col = (lax.broadcasted_iota(jnp.int32, v.shape, 1) == 0).astype(F32)
        return _mx(jnp.concatenate([v.astype(F32), ones_col], axis=1).T)

    k_cmp = kc_ref[0, 0]
    lc = [jnp.dot(k_cmp, qh[h], preferred_element_type=F32) + bcmp_ref[0, h] for h in heads]
    mc = [jnp.max(x, axis=0, keepdims=True) for x in lc]
    pcu = [jnp.where(mc[h] > 0.5 * NEG, jnp.exp2(lc[h] - mc[h]), 0.0) for h in heads]
    pc = [x / jnp.maximum(jnp.sum(x, axis=0, keepdims=True), 1e-30) for x in pcu]
    v_cmp_t = values_t(vc_ref[0, 0])[:dh]
    oc = [_dot(v_cmp_t, x) for x in pc]

    pcsum = (pc[0] + pc[1]) + (pc[2] + pc[3])
    hi = pcsum.astype(MXU_DTYPE)
    mid = (pcsum - hi.astype(F32))
    lo = (mid - mid.astype(MXU_DTYPE).astype(F32)).astype(MXU_DTYPE)
    ovl = ovl_ref[...]
    imp = (jnp.dot(ovl, hi, preferred_element_type=F32) + jnp.dot(ovl, mid.astype(MXU_DTYPE), preferred_element_type=F32)
           + jnp.dot(ovl, lo, preferred_element_type=F32))
    n_sel = imp.shape[0]
    jj = lax.broadcasted_iota(jnp.int32, (n_sel, 1), 0)
    blk_q = lax.shift_right_logical(t0 + lax.broadcasted_iota(jnp.int32, (1, tq), 1),
                                    int(math.log2(SEL_BLOCK)))
    forced = (jj == 0) | (jj == blk_q) | (jj == blk_q - 1)
    score = jnp.where(forced, IMP_FORCE, jnp.where(jj <= blk_q, imp, -1.0))
    rank = jnp.zeros_like(score)
    for j in range(n_sel):
        row = score[j:j + 1, :]
        tie = (jj > j).astype(F32)
        rank = rank + jnp.where(row > score, 1.0, jnp.where(row == score, tie, 0.0))
    neg_ref[...] = jnp.where(rank < SEL_TOPK, 0.0, NEG)

    def normalised(acc):
        return acc[:dh] / acc[dh:dh + 1]

    per_step = tk // SEL_BLOCK

    def sel_step(c, carry):
        ms, accs = carry
        start = pl.multiple_of(c * tk, tk)
        k_c = ks_ref[0, 0, 0, pl.ds(start, tk), :]
        v_c = values_t(vs_ref[0, 0, 0, pl.ds(start, tk), :])
        neg = jnp.concatenate([jnp.broadcast_to(neg_ref[pl.ds(per_step * c + r, 1), :], (SEL_BLOCK, tq))
                               for r in range(per_step)], axis=0)
        near = jnp.minimum((t0 - start) // tq, bsel_ref.shape[2] - 1)
        x = [jnp.dot(k_c, qh[h], preferred_element_type=F32) + bsel_ref[0, h, near] + neg for h in heads]
        m_new = [jnp.maximum(ms[h], jnp.max(x[h], axis=0, keepdims=True)) for h in heads]
        acc_new = [jnp.exp2(ms[h] - m_new[h]) * accs[h] + _dot(v_c, jnp.exp2(x[h] - m_new[h])) for h in heads]
        return tuple(m_new), tuple(acc_new)

    init = (tuple(jnp.full((1, tq), NEG, F32) for _ in heads), tuple(jnp.zeros((2 * dh, tq), F32) for _ in heads))
    _, acc_sel = lax.fori_loop(0, (t0 + tq + tk - 1) // tk, sel_step, init)

    w0 = pl.multiple_of(jnp.maximum(t0 - WINDOW, 0), tq)
    wk = WINDOW + tq
    k_w = kw_ref[0, 0, 0, pl.ds(w0, wk), :]
    v_w = values_t(vw_ref[0, 0, 0, pl.ds(w0, wk), :])
    xw = [jnp.dot(k_w, qh[h], preferred_element_type=F32) + bwin_ref[0, h, 0] for h in heads]
    ow = [_dot(v_w, jnp.exp2(x - jnp.max(x, axis=0, keepdims=True))) for x in xw]

    gates = jax.nn.sigmoid(gate_ref[0]).T
    outs = []
    for h in heads:
        g = [gates[N_BRANCH * h + br:N_BRANCH * h + br + 1, :] for br in range(N_BRANCH)]
        outs.append(g[0] * oc[h] + g[1] * normalised(acc_sel[h]) + g[2] * normalised(ow[h]))
    o_ref[0] = jnp.concatenate(outs, axis=0).T.astype(o_ref.dtype)


def _nsa_attention(q, gate_logits, kv, k_cmp, v_cmp, rel_bias):
    b, t, d = q.shape
    tq, tk, dh = NSA_TQ, NSA_TK, NSA_DH
    bias_cmp, bias_win, bias_sel = _nsa_bias_tiles(rel_bias, t)
    ovl = _nsa_constants(t)
    n_cmp = k_cmp.shape[2]
    n_win = bias_win.shape[2]
    part = lambda idx: pl.BlockSpec((1, 1, 1, t, dh), lambda g, bi, i: (idx, bi, g, 0, 0))
    cmp_spec = pl.BlockSpec((1, 1, n_cmp, dh), lambda g, bi, i: (bi, g, 0, 0))
    return pl.pallas_call(
        _nsa_attn_kernel,
        grid=(NSA_GROUPS, b, t // tq),
        in_specs=[pl.BlockSpec((1, tq, NSA_HPG * dh), lambda g, bi, i: (bi, i, g)),
                  pl.BlockSpec((1, tq, GATE_LANES), lambda g, bi, i: (bi, i, g)),
                  cmp_spec, cmp_spec, part(0), part(1), part(2), part(3),
                  pl.BlockSpec((1, NSA_HPG, n_cmp, tq), lambda g, bi, i: (g, 0, 0, i)),
                  pl.BlockSpec((1, NSA_HPG, 1, WINDOW + tq, tq),
                               lambda g, bi, i: (g, 0, jnp.minimum(i, n_win - 1), 0, 0)),
                  pl.BlockSpec((1, NSA_HPG, bias_sel.shape[2], tk, tq), lambda g, bi, i: (g, 0, 0, 0, 0)),
                  _const_spec(ovl.shape)],
        out_specs=pl.BlockSpec((1, tq, NSA_HPG * dh), lambda g, bi, i: (bi, i, g)),
        out_shape=jax.ShapeDtypeStruct((b, t, d), ACT_DTYPE),
        scratch_shapes=[pltpu.VMEM((t // SEL_BLOCK, tq), F32)],
        compiler_params=_params(("arbitrary", "arbitrary", "arbitrary")),
        name="nsa_attention",
    )(q, gate_logits, k_cmp, v_cmp, kv, kv, kv, kv, bias_cmp, bias_win, bias_sel, ovl)


def _nsa_block(h, moe_out, kv_ln, kv_w, cmp_pe_k, cmp_pe_v, cmp_k_w1, cmp_k_w2, cmp_v_w1, cmp_v_w2,
               rel_bias, b_ln, b_wqg, b_wo, b, t):
    n, d = b * t, kv_ln.shape[0]
    h1, q, gate_logits, kvc, kv = _nsa_proj(h, moe_out, kv_ln, b_ln, kv_w, b_wqg, b, t)
    k_cmp, v_cmp = _compress(kvc, cmp_pe_k, cmp_pe_v, cmp_k_w1, cmp_k_w2, cmp_v_w1, cmp_v_w2)
    o = _nsa_attention(q, gate_logits, kv, k_cmp, v_cmp, rel_bias)
    return _matmul_res(o.reshape(n, d), _mx(b_wo), h1.reshape(n, d))


def kernel(x, a_ln, a_mu, a_wr, a_wk, a_wv, a_wo, a_w0, a_w1, a_w2, a_a0, a_a1, a_a2, a_g1, a_g2, a_kk, a_ka, a_rk, a_lnx_w, a_lnx_b, kv_ln, kv_w, cmp_pe_k, cmp_pe_v, cmp_k_w1, cmp_k_w2, cmp_v_w1, cmp_v_w2, rel_bias, b_ln, b_wqg, b_wo, m_ln, m_wg, m_bg, m_we, m_be, m_w1, m_w3, m_w2, final_ln):
    b, t, d = x.shape
    h = _rwkv_block(x, a_ln[0], a_mu[0], a_wr[0], a_wk[0], a_wv[0], a_wo[0], a_w0[0], a_w1[0], a_w2[0],
                    a_a0[0], a_a1[0], a_a2[0], a_g1[0], a_g2[0], a_kk[0], a_ka[0], a_rk[0],
                    a_lnx_w[0], a_lnx_b[0])
    moe = lambda hh, l: _moe_layer(hh, m_ln[l], m_wg[l], m_bg[l], m_we[l], m_be[l], m_w1[l], m_w3[l], m_w2[l])
    h = _nsa_block(h, moe(h, 0), kv_ln, kv_w, cmp_pe_k, cmp_pe_v, cmp_k_w1, cmp_k_w2, cmp_v_w1, cmp_v_w2,
                   rel_bias, b_ln[0], b_wqg[0], b_wo[0], b, t)
    return _final(h, moe(h, 1), final_ln).reshape(b, t, d)
```

```python
import math

import numpy as np
import jax
import jax.numpy as jnp
from jax import lax
from jax.experimental import pallas as pl
from jax.experimental.pallas import tpu as pltpu

F32 = jnp.float32
MXU_DTYPE = jnp.bfloat16
ACT_DTYPE = jnp.bfloat16

D_MODEL = 1024
NORM_EPS = 1e-6
RWKV_HEAD = 64
RWKV_GN_EPS = 64e-5
RWKV_CHUNK = 64
RWKV_SEQS = 2
LANES = 128
MXU_TILE = 256
VMEM_LIMIT = 56 * 1024 * 1024


def _mx(x):
    return x.astype(MXU_DTYPE)


def _dot(a, b):
    return jnp.dot(_mx(a), _mx(b), preferred_element_type=F32)


def _dot_nt(a, b):
    return lax.dot_general(_mx(a), _mx(b), (((1,), (1,)), ((), ())), preferred_element_type=F32)


def _split2(x):
    hi = x.astype(MXU_DTYPE)
    lo = (x - hi.astype(F32)).astype(MXU_DTYPE)
    return hi, lo


def _dot_hl(x, w):
    hi, lo = _split2(x)
    return jnp.dot(hi, w, preferred_element_type=F32) + jnp.dot(lo, w, preferred_element_type=F32)


def _rms(x, g):
    return x * lax.rsqrt(jnp.mean(x * x, axis=-1, keepdims=True) + NORM_EPS) * g


TOK_ROWS = 8


def _tok_load(ref):
    tm = ref.shape[0] // TOK_ROWS
    return jnp.concatenate([ref[pl.ds(s, tm, stride=TOK_ROWS), :] for s in range(TOK_ROWS)], axis=1)


def _tok_store(ref, val):
    tm = val.shape[0]
    for s in range(TOK_ROWS):
        ref[pl.ds(s, tm, stride=TOK_ROWS), :] = val[:, LANES * s:LANES * (s + 1)]


def _tok_spec(tm, index_map):
    return pl.BlockSpec((TOK_ROWS * tm, LANES), index_map)


def _const_spec(shape):
    nd = len(shape)
    return pl.BlockSpec(shape, lambda *_: (0,) * nd)


def _params(sem):
    return pltpu.CompilerParams(dimension_semantics=sem, vmem_limit_bytes=VMEM_LIMIT)


def _rwkv_pre_kernel(x_ref, ln_ref, mu_ref, wr_ref, wk_ref, wv_ref, w1_ref, a1_ref, g1_ref,
                     w2_ref, a2_ref, g2_ref, vec_ref,
                     r_out, k_out, v_out, kk_out, g_out, ld_out, a_out, prev_ref):
    tm = x_ref.shape[1]
    hn = _rms(x_ref[0], ln_ref[...])

    @pl.when(pl.program_id(1) == 0)
    def _():
        prev_ref[...] = jnp.zeros_like(prev_ref)

    rows = lax.broadcasted_iota(jnp.int32, (tm, 1), 0)
    shifted = jnp.where(rows == 0, prev_ref[0:1, :], pltpu.roll(hn, 1, axis=0))
    prev_ref[0:1, :] = hn[tm - 1:tm, :]
    xx = shifted - hn
    xr, xw, xk, xv, xa, xg = (_mx(hn + xx * mu_ref[i:i + 1, :]) for i in range(6))
    w0, a0, kkp, kap = (vec_ref[i:i + 1, :] for i in range(4))
    r = jnp.dot(xr, wr_ref[...], preferred_element_type=F32)
    k = jnp.dot(xk, wk_ref[...], preferred_element_type=F32)
    v = jnp.dot(xv, wv_ref[...], preferred_element_type=F32)
    z = w0 + _dot(jnp.tanh(jnp.dot(xw, w1_ref[...], preferred_element_type=F32)), w2_ref[...])
    a = jax.nn.sigmoid(a0 + _dot(jnp.dot(xa, a1_ref[...], preferred_element_type=F32), a2_ref[...]))
    g = _dot(jax.nn.sigmoid(jnp.dot(xg, g1_ref[...], preferred_element_type=F32)), g2_ref[...])
    ld_out[0] = (-math.exp(-0.5)) * jax.nn.sigmoid(z)
    a_out[0] = a
    r_out[0] = r.astype(r_out.dtype)
    v_out[0] = v.astype(v_out.dtype)
    kk_out[0] = (k * kkp).astype(kk_out.dtype)
    k_out[0] = (k * (1.0 + (a - 1.0) * kap)).astype(k_out.dtype)
    g_out[0] = g.astype(g_out.dtype)


def _rwkv_pre(x, ln, mu, wr, wk, wv, w1, a1, g1, w2, a2, g2, vecs, tm=256):
    b, t, d = x.shape
    act = jax.ShapeDtypeStruct((b, t, d), ACT_DTYPE)
    f32o = jax.ShapeDtypeStruct((b, t, d), F32)
    blk = pl.BlockSpec((1, tm, d), lambda i, j: (i, j, 0))
    consts = [ln, mu, wr, wk, wv, w1, a1, g1, w2, a2, g2, vecs]
    return pl.pallas_call(
        _rwkv_pre_kernel,
        grid=(b, t // tm),
        in_specs=[blk] + [_const_spec(c.shape) for c in consts],
        out_specs=[blk] * 7,
        out_shape=[act, act, act, act, act, f32o, f32o],
        scratch_shapes=[pltpu.VMEM((8, d), F32)],
        compiler_params=_params(("arbitrary", "arbitrary")),
        name="rwkv_pre",
    )(x, *consts)


def _stack4(x):
    return jnp.concatenate([x[:, MXU_TILE * q:MXU_TILE * (q + 1)] for q in range(4)], axis=0)


def _unstack4(y):
    c = y.shape[0] // 4
    return jnp.concatenate([y[c * q:c * (q + 1), :] for q in range(4)], axis=1)


def _rwkv_scan_kernel(r_ref, k_ref, v_ref, kk_ref, g_ref, ld_ref, a_ref,
                      ltri_ref, smask_ref, eye_ref, j_ref, rk_ref, lnw_ref, lnb_ref,
                      y_out, h_ref):
    c = RWKV_CHUNK
    nb = r_ref.shape[0]

    @pl.when(pl.program_id(1) == 0)
    def _():
        h_ref[...] = jnp.zeros_like(h_ref)

    def rows_of(ref):
        return jnp.concatenate([ref[bi].astype(F32) for bi in range(nb)], axis=0)

    ld, r, k, v, kk = (rows_of(x) for x in (ld_ref, r_ref, k_ref, v_ref, kk_ref))
    jones = j_ref[...]
    ld_hi, ld_lo = _split2(ld)
    cum = (jnp.dot(ltri_ref[...], ld_hi, preferred_element_type=F32)
           + jnp.dot(ltri_ref[...], ld_lo, preferred_element_type=F32))
    last = [cum[(bi + 1) * c - 1:(bi + 1) * c, :] for bi in range(nb)]
    cum_last = jnp.concatenate([jnp.broadcast_to(x, (c, x.shape[1])) for x in last], axis=0)
    e_l = jnp.exp(cum)
    e_lm = jnp.exp(cum - ld)
    e_nl = jnp.exp(-cum)
    e_cl = jnp.exp(cum_last - cum)
    e_c = [jnp.exp(x) for x in last]
    ss = _unstack4(_dot_hl(_stack4(kk * kk), jones))
    kkn = kk * lax.rsqrt(jnp.maximum(ss, 1e-24))
    a_vec = -kkn
    b_vec = kkn * rows_of(a_ref)
    rt = r * e_l
    at = a_vec * e_lm
    kt = k * e_nl
    bt = b_vec * e_nl
    bh = b_vec * e_cl
    kh = k * e_cl

    lane = lax.broadcasted_iota(jnp.int32, (1, LANES), 1)
    m_l = (lane < RWKV_HEAD).astype(F32)
    m_r = 1.0 - m_l

    def st(xp):
        return jnp.concatenate([xp * m_l, xp * m_r], axis=0)

    def dup(xp):
        return jnp.concatenate([xp, xp], axis=0)

    smask = smask_ref[...] > 0.0
    eye = eye_ref[...]
    n_pair = D_MODEL // LANES
    pairs = range(nb * n_pair)
    seq_of = [p // n_pair for p in pairs]

    def part(x, p):
        return x[seq_of[p] * c:(seq_of[p] + 1) * c, LANES * (p % n_pair):LANES * (p % n_pair + 1)]

    at_s = [st(part(at, p)) for p in pairs]
    rt_s = [st(part(rt, p)) for p in pairs]
    v_s = [st(part(v, p)) for p in pairs]
    s = [jnp.where(smask,
                   _dot_nt(jnp.concatenate([at_s[p], rt_s[p]], axis=0),
                           jnp.concatenate([dup(part(bt, p)), dup(part(kt, p))], axis=0)),
                   0.0) for p in pairs]
    a_ab = [x[:2 * c, :2 * c] for x in s]
    a_r = [x[2 * c:, :] for x in s]
    g1 = [_dot(s[p][:2 * c, 2 * c:], v_s[p]) for p in pairs]
    t_m = [eye + x for x in a_ab]
    a_pow = [_dot(x, x) for x in a_ab]
    n = 2
    while n < c:
        if 2 * n < c:
            both = [_dot(a_pow[p], jnp.concatenate([a_pow[p], t_m[p]], axis=1)) for p in pairs]
            a_pow = [x[:, :2 * c] for x in both]
            t_m = [t_m[p] + both[p][:, 2 * c:] for p in pairs]
        else:
            t_m = [t_m[p] + _dot(a_pow[p], t_m[p]) for p in pairs]
        n *= 2
    tw = [_dot(t_m[p], jnp.concatenate([at_s[p], g1[p]], axis=1)) for p in pairs]
    zmat = [jnp.concatenate([tw[p], jnp.concatenate([jnp.zeros_like(v_s[p]), v_s[p]], axis=1)], axis=0)
            for p in pairs]
    upper = [_dot(a_r[p], zmat[p]) for p in pairs]
    bk_t = [jnp.concatenate([st(part(bh, p)), st(part(kh, p))], axis=0).T for p in pairs]
    lower = [_dot(bk_t[p], zmat[p]) for p in pairs]
    decay = [eye * e_c[seq_of[p]][:, LANES * (p % n_pair):LANES * (p % n_pair + 1)] for p in pairs]
    seq = [_dot(jnp.concatenate([upper[p][:, :LANES] + rt_s[p], lower[p][:, :LANES] + decay[p]], axis=0), h_ref[p])
           for p in pairs]
    ys = []
    for p in pairs:
        y_st = seq[p][:2 * c, :] + upper[p][:, LANES:]
        h_ref[p] = seq[p][2 * c:, :] + lower[p][:, LANES:]
        ys.append(y_st[:c, :] + y_st[c:, :])
    y = jnp.concatenate([jnp.concatenate(ys[bi * n_pair:(bi + 1) * n_pair], axis=1) for bi in range(nb)],
                        axis=0)

    inv_n = 1.0 / RWKV_HEAD
    y4 = _stack4(y)
    dlt = y4 - _dot_hl(y4, jones) * inv_n
    var = _dot_hl(dlt * dlt, jones) * inv_n
    yn = dlt * lax.rsqrt(var + RWKV_GN_EPS)
    bonus = _dot_hl(_stack4(r * k) * rk_ref[...], jones) * _stack4(v)
    out = _unstack4((yn * lnw_ref[...] + lnb_ref[...] + bonus) * _stack4(rows_of(g_ref)))
    for bi in range(nb):
        y_out[bi] = out[bi * c:(bi + 1) * c, :].astype(y_out.dtype)


def _scan_constants():
    c = RWKV_CHUNK
    ltri = np.kron(np.eye(RWKV_SEQS), np.tril(np.ones((c, c), np.float32)))
    rho = np.arange(4 * c)
    r_type, r_head, r_t = rho // (2 * c), (rho % (2 * c)) // c, rho % c
    same = r_head[:, None] == r_head[None, :]
    strict = r_t[None, :] < r_t[:, None]
    incl = r_t[None, :] <= r_t[:, None]
    smask = same & np.where(r_type[:, None] == 0, strict, incl)
    eye = np.eye(LANES, dtype=np.float32)
    head = np.arange(MXU_TILE) // RWKV_HEAD
    jones = (head[:, None] == head[None, :]).astype(np.float32)
    return (jnp.asarray(ltri, MXU_DTYPE), jnp.asarray(smask, F32), jnp.asarray(eye, F32),
            jnp.asarray(jones, MXU_DTYPE))


def _stack_param(p):
    return jnp.repeat(p.reshape(4, MXU_TILE).astype(F32), RWKV_SEQS * RWKV_CHUNK, axis=0)


def _rwkv_scan(r, k, v, kk, g, ld, a, rk, lnw, lnb):
    b, t, d = r.shape
    c, nb = RWKV_CHUNK, RWKV_SEQS
    blk = pl.BlockSpec((nb, c, d), lambda i, j: (i, j, 0))
    consts = list(_scan_constants()) + [_stack_param(rk.reshape(-1)), _stack_param(lnw), _stack_param(lnb)]
    return pl.pallas_call(
        _rwkv_scan_kernel,
        grid=(b // nb, t // c),
        in_specs=[blk] * 7 + [_const_spec(x.shape) for x in consts],
        out_specs=blk,
        out_shape=jax.ShapeDtypeStruct((b, t, d), ACT_DTYPE),
        scratch_shapes=[pltpu.VMEM((nb * (d // LANES), LANES, LANES), F32)],
        compiler_params=_params(("arbitrary", "arbitrary")),
        name="rwkv_scan",
    )(r, k, v, kk, g, ld, a, *consts)


def _matmul_res_kernel(a_ref, w_ref, res_ref, o_ref):
    _tok_store(o_ref, res_ref[...] + jnp.dot(a_ref[...], w_ref[...], preferred_element_type=F32))


def _matmul_res(a, w, res, tm=512):
    n, kdim = a.shape
    d = w.shape[1]
    return pl.pallas_call(
        _matmul_res_kernel,
        grid=(n // tm,),
        in_specs=[pl.BlockSpec((tm, kdim), lambda i: (i, 0)), _const_spec(w.shape),
                  pl.BlockSpec((tm, d), lambda i: (i, 0))],
        out_specs=_tok_spec(tm, lambda i: (i, 0)),
        out_shape=jax.ShapeDtypeStruct((TOK_ROWS * n, LANES), F32),
        compiler_params=_params(("arbitrary",)),
        name="matmul_res",
    )(a, w, res)


def _rwkv_block(x, a_ln, a_mu, a_wr, a_wk, a_wv, a_wo, a_w0, a_w1, a_w2, a_a0, a_a1, a_a2,
                a_g1, a_g2, a_kk, a_ka, a_rk, a_lnx_w, a_lnx_b):
    b, t, d = x.shape
    vecs = jnp.stack([a_w0, a_a0, a_kk, a_ka]).astype(F32)
    r, k, v, kk, g, ld, a = _rwkv_pre(
        x, a_ln.reshape(1, d), a_mu, _mx(a_wr), _mx(a_wk), _mx(a_wv), _mx(a_w1), _mx(a_a1), _mx(a_g1),
        _mx(a_w2), _mx(a_a2), _mx(a_g2), vecs)
    y = _rwkv_scan(r, k, v, kk, g, ld, a, a_rk, a_lnx_w, a_lnx_b)
    return _matmul_res(y.reshape(b * t, d), _mx(a_wo), x.reshape(b * t, d))


MOE_GROUPS = 4
MOE_EPG = 8
MOE_EXPERTS = MOE_GROUPS * MOE_EPG
MOE_TOPK = 2
MOE_BLOCK = 512
NEG = -1e30


def _route(logits):
    lane = lax.broadcasted_iota(jnp.int32, logits.shape, 1)
    is_grp = lane < MOE_GROUPS
    gl = jnp.where(is_grp, logits, NEG)
    ge = jnp.exp(gl - jnp.max(gl, axis=-1, keepdims=True))
    gp = ge / jnp.sum(ge, axis=-1, keepdims=True)
    grp_w = jnp.max(gp, axis=-1, keepdims=True)
    grp = jnp.min(jnp.where(is_grp & (gp == grp_w), lane, LANES), axis=-1, keepdims=True)
    lo = MOE_GROUPS + MOE_EPG * grp
    in_grp = (lane >= lo) & (lane < lo + MOE_EPG)
    el = jnp.where(in_grp, logits, NEG)
    ee = jnp.exp(el - jnp.max(el, axis=-1, keepdims=True))
    ep = ee / jnp.sum(ee, axis=-1, keepdims=True)
    p1 = jnp.max(jnp.where(in_grp, ep, -1.0), axis=-1, keepdims=True)
    i1 = jnp.min(jnp.where(in_grp & (ep == p1), lane, LANES), axis=-1, keepdims=True)
    rest = in_grp & (lane != i1)
    p2 = jnp.max(jnp.where(rest, ep, -1.0), axis=-1, keepdims=True)
    i2 = jnp.min(jnp.where(rest & (ep == p2), lane, LANES), axis=-1, keepdims=True)
    scale = grp_w / (p1 + p2)
    return i1 - MOE_GROUPS, i2 - MOE_GROUPS, p1 * scale, p2 * scale


def _router_logits(xf, w_hi_ref, w_lo_ref, bias_ref):
    hi, lo = _split2(xf)
    logits = (jnp.dot(hi, w_hi_ref[...], preferred_element_type=F32)
              + jnp.dot(lo, w_hi_ref[...], preferred_element_type=F32)
              + jnp.dot(hi, w_lo_ref[...], preferred_element_type=F32))
    return logits + bias_ref[...]


def _moe_router_kernel(h_ref, ln_ref, w_hi_ref, w_lo_ref, bias_ref, tri_ref, route_out, count_out, cnt_ref):
    @pl.when(pl.program_id(0) == 0)
    def _():
        cnt_ref[...] = jnp.zeros_like(cnt_ref)

    xf = _rms(_tok_load(h_ref), ln_ref[...])
    e1, e2, w1, w2 = _route(_router_logits(xf, w_hi_ref, w_lo_ref, bias_ref))
    lane = lax.broadcasted_iota(jnp.int32, (xf.shape[0], LANES), 1)
    hit1, hit2 = lane == e1, lane == e2
    onehot = jnp.where(hit1, 1.0, jnp.where(hit2, 1.0, 0.0))
    before = jnp.dot(tri_ref[...], _mx(onehot), preferred_element_type=F32) + cnt_ref[...]
    r1 = jnp.sum(jnp.where(hit1, before, 0.0), axis=-1, keepdims=True)
    r2 = jnp.sum(jnp.where(hit2, before, 0.0), axis=-1, keepdims=True)
    cnt_ref[...] = cnt_ref[...] + jnp.sum(onehot, axis=0, keepdims=True)
    count_out[...] = cnt_ref[...]
    fields = (e1.astype(F32), e2.astype(F32), w1, w2, r1, r2)
    rec = jnp.zeros(lane.shape, F32)
    for idx, val in enumerate(fields):
        rec = jnp.where(lane == idx, val, rec)
    route_out[...] = rec


def _router_weights(w_gr, b_gr, w_er, b_er):
    d = w_gr.shape[0]
    w = jnp.zeros((d, LANES), F32).at[:, :MOE_GROUPS].set(w_gr).at[:, MOE_GROUPS:MOE_GROUPS + MOE_EXPERTS].set(w_er)
    bias = jnp.zeros((1, LANES), F32).at[0, :MOE_GROUPS].set(b_gr).at[0, MOE_GROUPS:MOE_GROUPS + MOE_EXPERTS].set(b_er)
    w_hi = w.astype(MXU_DTYPE)
    w_lo = (w - w_hi.astype(F32)).astype(MXU_DTYPE)
    return w_hi, w_lo, bias


def _moe_router(h, ln, w_hi, w_lo, bias, tm=512):
    n, d = h.shape[0] // TOK_ROWS, ln.shape[0]
    tri = jnp.asarray(np.tril(np.ones((tm, tm), np.float32), -1), MXU_DTYPE)
    return pl.pallas_call(
        _moe_router_kernel,
        grid=(n // tm,),
        in_specs=[_tok_spec(tm, lambda i: (i, 0)), _const_spec((1, d)), _const_spec(w_hi.shape),
                  _const_spec(w_lo.shape),
                  _const_spec(bias.shape), _const_spec(tri.shape)],
        out_specs=[pl.BlockSpec((tm, LANES), lambda i: (i, 0)), _const_spec((1, LANES))],
        out_shape=[jax.ShapeDtypeStruct((n, LANES), F32), jax.ShapeDtypeStruct((1, LANES), F32)],
        scratch_shapes=[pltpu.VMEM((1, LANES), F32)],
        compiler_params=_params(("arbitrary",)),
        name="moe_router",
    )(h, ln.reshape(1, d), w_hi, w_lo, bias, tri)


def _moe_plan(route, counts, blk):
    n = route.shape[0]
    counts = counts[0, :MOE_EXPERTS].astype(jnp.int32)
    pcounts = (counts + blk - 1) // blk * blk
    pend = jnp.cumsum(pcounts)
    pstart = (pend - pcounts).astype(jnp.int32)
    n_blk = n * MOE_TOPK // blk + MOE_EXPERTS
    blk_start = jnp.arange(n_blk, dtype=jnp.int32) * blk
    blk_e = jnp.sum((blk_start[:, None] >= pend[None, :]).astype(jnp.int32), axis=1)
    blk_e = jnp.minimum(blk_e, MOE_EXPERTS - 1)
    n_valid = jnp.clip((pstart + counts)[blk_e] - blk_start, 0, blk).astype(jnp.int32)
    n_act = (pend[-1] // blk).astype(jnp.int32).reshape(1)
    expert = route[:, 0:MOE_TOPK].astype(jnp.int32)
    onehot = expert[:, :, None] == jnp.arange(MOE_EXPERTS, dtype=jnp.int32)
    seg_start = jnp.sum(jnp.where(onehot, pstart, 0), axis=-1)
    sorted_pos = seg_start + route[:, 4:4 + MOE_TOPK].astype(jnp.int32)
    return blk_e, n_valid, n_act, sorted_pos


def _token_copies(pos_ref, tm, make_copy):
    group = 8

    def body(g, carry):
        t0 = g * group
        pos = [[pos_ref[0, 0, MOE_TOPK * (t0 + j) + k] for k in range(MOE_TOPK)] for j in range(group)]
        for j in range(group):
            for k in range(MOE_TOPK):
                make_copy(t0 + j, k, pos[j][k]).start()
        return carry

    lax.fori_loop(0, tm // group, body, 0)


def _tile(ref, tok):
    return ref.at[pl.ds(pl.multiple_of(tok * TOK_ROWS, TOK_ROWS), TOK_ROWS)]


def _moe_dispatch_kernel(n_valid_ref, pos_ref, src_ref, dst_hbm, stage, sem, zbuf, zsem):
    tm = src_ref.shape[0] // TOK_ROWS
    step = pl.program_id(0)
    slot = lax.rem(step, 2)

    def drain(s):
        for k in range(MOE_TOPK):
            pltpu.make_async_copy(stage.at[s], dst_hbm.at[pl.ds(0, stage.shape[1])], sem.at[s]).wait()

    @pl.when(pl.program_id(0) == 0)
    def _():
        rows = zbuf.shape[0]
        zbuf[...] = jnp.zeros_like(zbuf)

        def fill(j, wait):
            @pl.when(n_valid_ref[j] < rows // TOK_ROWS)
            def _():
                start = 0 if wait else pl.multiple_of(j * rows, rows)
                cp = pltpu.make_async_copy(zbuf, dst_hbm.at[pl.ds(start, rows)], zsem)
                cp.wait() if wait else cp.start()

        lax.fori_loop(0, n_valid_ref.shape[0], lambda j, c: (fill(j, False), c)[1], 0)
        lax.fori_loop(0, n_valid_ref.shape[0], lambda j, c: (fill(j, True), c)[1], 0)

    @pl.when(step >= 2)
    def _():
        drain(slot)

    stage[slot] = src_ref[...]
    _token_copies(pos_ref, tm, lambda t, k, pos: pltpu.make_async_copy(
        _tile(stage.at[slot], t), _tile(dst_hbm, pos), sem.at[slot]))

    @pl.when(step == pl.num_programs(0) - 1)
    def _():
        @pl.when(step >= 1)
        def _():
            drain(1 - slot)
        drain(slot)


def _moe_gather_combine(step, n_steps, pos_ref, pos_next_ref, ys_hbm, h_ref, route, gbuf, gsem):
    tm = gbuf.shape[2] // TOK_ROWS
    slot = lax.rem(step, 2)

    def start(p_ref, s):
        _token_copies(p_ref, tm, lambda t, k, pos: pltpu.make_async_copy(
            _tile(ys_hbm, pos), _tile(gbuf.at[s, k], t), gsem.at[s]))

    @pl.when(step == 0)
    def _():
        start(pos_ref, 0)

    @pl.when(step + 1 < n_steps)
    def _():
        start(pos_next_ref, 1 - slot)

    for k in range(MOE_TOPK):
        pltpu.make_async_copy(ys_hbm.at[pl.ds(0, gbuf.shape[2])], gbuf.at[slot, k], gsem.at[slot]).wait()
    return (_tok_load(h_ref) + route[:, 2:3] * _tok_load(gbuf.at[slot, 0])
            + route[:, 3:4] * _tok_load(gbuf.at[slot, 1]))


def _gather_specs(sorted_pos, tm, step_index):
    n = sorted_pos.shape[0]
    pos = sorted_pos.reshape(n // tm, 1, MOE_TOPK * tm)
    last = n // tm - 1
    cur = pl.BlockSpec((1, 1, MOE_TOPK * tm), lambda *g: (step_index(*g), 0, 0), memory_space=pltpu.SMEM)
    nxt = pl.BlockSpec((1, 1, MOE_TOPK * tm), lambda *g: (jnp.minimum(step_index(*g) + 1, last), 0, 0),
                       memory_space=pltpu.SMEM)
    scratch = [pltpu.VMEM((2, MOE_TOPK, TOK_ROWS * tm, LANES), F32), pltpu.SemaphoreType.DMA((2,))]
    return pos, [cur, nxt, pl.BlockSpec(memory_space=pl.ANY)], scratch


def _moe_dispatch(h, n_valid, sorted_pos, p, blk, tm=512):
    n = sorted_pos.shape[0]
    pos = sorted_pos.reshape(n // tm, 1, MOE_TOPK * tm)
    grid_spec = pltpu.PrefetchScalarGridSpec(
        num_scalar_prefetch=1,
        grid=(n // tm,),
        in_specs=[pl.BlockSpec((1, 1, MOE_TOPK * tm), lambda i, nv: (i, 0, 0), memory_space=pltpu.SMEM),
                  _tok_spec(tm, lambda i, nv: (i, 0))],
        out_specs=pl.BlockSpec(memory_space=pl.ANY),
        scratch_shapes=[pltpu.VMEM((2, TOK_ROWS * tm, LANES), h.dtype), pltpu.SemaphoreType.DMA((2,)),
                        pltpu.VMEM((TOK_ROWS * blk, LANES), h.dtype), pltpu.SemaphoreType.DMA],
    )
    return pl.pallas_call(
        _moe_dispatch_kernel,
        grid_spec=grid_spec,
        out_shape=jax.ShapeDtypeStruct((TOK_ROWS * p, LANES), h.dtype),
        compiler_params=_params(("arbitrary",)),
        name="moe_dispatch",
    )(n_valid, pos, h)


def _moe_expert_kernel(blk_e_ref, n_act_ref, x_ref, ln_ref, w1_ref, w3_ref, w2_ref, y_ref):
    del blk_e_ref
    i = pl.program_id(0)

    @pl.when(i < n_act_ref[0])
    def _():
        xn = _mx(_rms(_tok_load(x_ref), ln_ref[...]))
        h1 = jnp.dot(xn, w1_ref[0], preferred_element_type=F32)
        h3 = jnp.dot(xn, w3_ref[0], preferred_element_type=F32)
        _tok_store(y_ref, _dot(h1 * jax.nn.sigmoid(h1) * h3, w2_ref[0]))

    @pl.when(i >= n_act_ref[0])
    def _():
        y_ref[...] = jnp.zeros_like(y_ref)


def _moe_experts(xs, ln, w1, w3, w2, blk_e, n_act, blk):
    p = xs.shape[0] // TOK_ROWS
    d, hdim = w1.shape[1], w1.shape[2]
    grid_spec = pltpu.PrefetchScalarGridSpec(
        num_scalar_prefetch=2,
        grid=(p // blk,),
        in_specs=[_tok_spec(blk, lambda i, be, na: (i, 0)),
                  pl.BlockSpec((1, d), lambda i, be, na: (0, 0)),
                  pl.BlockSpec((1, d, hdim), lambda i, be, na: (be[i], 0, 0)),
                  pl.BlockSpec((1, d, hdim), lambda i, be, na: (be[i], 0, 0)),
                  pl.BlockSpec((1, hdim, d), lambda i, be, na: (be[i], 0, 0))],
        out_specs=_tok_spec(blk, lambda i, be, na: (i, 0)),
    )
    return pl.pallas_call(
        _moe_expert_kernel,
        grid_spec=grid_spec,
        out_shape=jax.ShapeDtypeStruct(xs.shape, F32),
        compiler_params=_params(("arbitrary",)),
        name="moe_experts",
    )(blk_e, n_act, xs, ln.reshape(1, d), w1, w3, w2)


def _moe_layer(h, ln, w_gr, b_gr, w_er, b_er, w1, w3, w2):
    n = h.shape[0] // TOK_ROWS
    blk = MOE_BLOCK
    route, counts = _moe_router(h, ln, *_router_weights(w_gr, b_gr, w_er, b_er))
    blk_e, n_valid, n_act, sorted_pos = _moe_plan(route, counts, blk)
    p = n * MOE_TOPK + MOE_EXPERTS * blk
    xs = _moe_dispatch(h, n_valid, sorted_pos, p, blk)
    ys = _moe_experts(xs, ln, _mx(w1), _mx(w3), _mx(w2), blk_e, n_act, blk)
    return ys, sorted_pos, route


def _final_kernel(pos_ref, pos_next_ref, ys_hbm, h_ref, route_ref, ln_ref, o_ref, gbuf, gsem):
    h = _moe_gather_combine(pl.program_id(0), pl.num_programs(0), pos_ref, pos_next_ref, ys_hbm, h_ref,
                            route_ref[...], gbuf, gsem)
    o_ref[...] = _rms(h, ln_ref[...])


def _final(h, moe_out, ln, tm=512):
    ys, sorted_pos, route = moe_out
    n, d = route.shape[0], ln.shape[0]
    pos, gather_specs, scratch = _gather_specs(sorted_pos, tm, lambda i: i)
    return pl.pallas_call(
        _final_kernel,
        grid=(n // tm,),
        in_specs=gather_specs + [_tok_spec(tm, lambda i: (i, 0)), pl.BlockSpec((tm, LANES), lambda i: (i, 0)),
                                 _const_spec((1, d))],
        out_specs=pl.BlockSpec((tm, d), lambda i: (i, 0)),
        out_shape=jax.ShapeDtypeStruct((n, d), F32),
        scratch_shapes=scratch,
        compiler_params=_params(("arbitrary",)),
        name="final_norm",
    )(pos, pos, ys, h, route, ln.reshape(1, d))


NSA_HEADS = 16
NSA_GROUPS = 4
NSA_HPG = NSA_HEADS // NSA_GROUPS
NSA_DH = D_MODEL // NSA_HEADS
NSA_PARTS = 6
CMP_BLOCK = 32
CMP_STRIDE = 16
SEL_BLOCK = 64
SEL_TOPK = 8
WINDOW = 512
N_BRANCH = 3
IMP_FORCE = 1e4
REL_BUCKETS = 32
REL_MAX_DIST = 128
LOG2E = math.log2(math.e)
NSA_TQ = 256
NSA_TK = 512
GATE_LANES = LANES
CMP_PARTS = 2


def _nsa_proj_kernel(pos_ref, pos_next_ref, ys_hbm, h_ref, route_ref, kvln_ref, qln_ref, kvw_ref, qw_ref,
                     h_out, q_out, gate_out, kvc_out, kv_out, gbuf, gsem):
    step = pl.program_id(0) * pl.num_programs(1) + pl.program_id(1)
    h1 = _moe_gather_combine(step, pl.num_programs(0) * pl.num_programs(1), pos_ref, pos_next_ref, ys_hbm, h_ref,
                             route_ref[0], gbuf, gsem)
    h_out[0] = h1
    kv = _dot(_rms(h1, kvln_ref[...]), kvw_ref[...])
    for part in range(NSA_PARTS):
        for g in range(NSA_GROUPS):
            col = (part * NSA_GROUPS + g) * NSA_DH
            piece = kv[:, col:col + NSA_DH]
            if part < CMP_PARTS:
                kvc_out[part, 0, g] = piece
            else:
                kv_out[part - CMP_PARTS, 0, g] = piece.astype(kv_out.dtype)
    qg = _dot(_rms(h1, qln_ref[...]), qw_ref[...])
    q_out[0] = (qg[:, :D_MODEL] * (NSA_DH ** -0.5 * LOG2E)).astype(q_out.dtype)
    gate_out[0] = qg[:, D_MODEL:]


def _gate_padded_weight(q_w):
    d = q_w.shape[0]
    n_gate = NSA_HPG * N_BRANCH
    gates = q_w[:, d:].reshape(d, NSA_GROUPS, n_gate)
    gates = jnp.pad(gates, ((0, 0), (0, 0), (0, GATE_LANES - n_gate))).reshape(d, NSA_GROUPS * GATE_LANES)
    return jnp.concatenate([q_w[:, :d], gates], axis=1)


def _nsa_proj(h, moe_out, kv_ln, q_ln, kv_w, q_w, b, t, tm=256):
    ys, sorted_pos, route = moe_out
    d = kv_ln.shape[0]
    blk = lambda w: pl.BlockSpec((1, tm, w), lambda i, j: (i, j, 0))
    per_b = t // tm
    tok_block = lambda i, j: i * per_b + j
    pos, gather_specs, scratch = _gather_specs(sorted_pos, tm, tok_block)
    gw = NSA_GROUPS * GATE_LANES
    qw = _gate_padded_weight(q_w)
    kv_spec = lambda parts: pl.BlockSpec((parts, 1, NSA_GROUPS, tm, NSA_DH), lambda i, j: (0, i, 0, j, 0))
    kv_shape = lambda parts: (parts, b, NSA_GROUPS, t, NSA_DH)
    return pl.pallas_call(
        _nsa_proj_kernel,
        grid=(b, per_b),
        in_specs=gather_specs + [_tok_spec(tm, lambda i, j: (tok_block(i, j), 0)), blk(LANES),
                                 _const_spec((1, d)), _const_spec((1, d)),
                                 _const_spec(kv_w.shape), _const_spec(qw.shape)],
        out_specs=[blk(d), blk(d), blk(gw), kv_spec(CMP_PARTS), kv_spec(NSA_PARTS - CMP_PARTS)],
        out_shape=[jax.ShapeDtypeStruct((b, t, d), F32), jax.ShapeDtypeStruct((b, t, d), ACT_DTYPE),
                   jax.ShapeDtypeStruct((b, t, gw), F32), jax.ShapeDtypeStruct(kv_shape(CMP_PARTS), F32),
                   jax.ShapeDtypeStruct(kv_shape(NSA_PARTS - CMP_PARTS), ACT_DTYPE)],
        scratch_shapes=scratch,
        compiler_params=_params(("arbitrary", "arbitrary")),
        name="nsa_proj",
    )(pos, pos, ys, h, route.reshape(b, t, LANES),
      kv_ln.reshape(1, d), q_ln.reshape(1, d), _mx(kv_w), _mx(qw))


def _gelu_tanh(x):
    return 0.5 * x * (1.0 + jnp.tanh(math.sqrt(2.0 / math.pi) * (x + 0.044715 * (x * x * x))))


def _compress_kernel(k_ref, v_ref, pek_ref, pev_ref, kw1_ref, kw2_ref, vw1_ref, vw2_ref, kc_out, vc_out):
    dh = k_ref.shape[-1]
    n_chunk = k_ref.shape[3] // CMP_STRIDE

    def one(z_ref, pe_ref, w1_ref, w2_ref, out):
        first = jnp.zeros((n_chunk, w1_ref.shape[1]), F32)
        second = first
        for l in range(CMP_STRIDE):
            z = z_ref[0, 0, 0, pl.ds(l, n_chunk, stride=CMP_STRIDE), :]
            lo, hi = l, CMP_STRIDE + l
            first = first + _dot(z + pe_ref[0, lo:lo + 1, :], w1_ref[dh * lo:dh * (lo + 1), :])
            second = second + _dot(z + pe_ref[0, hi:hi + 1, :], w1_ref[dh * hi:dh * (hi + 1), :])
        hid = first + pltpu.roll(second, n_chunk - 1, axis=0)
        out[0, 0] = _dot(_gelu_tanh(hid), w2_ref[...]).astype(out.dtype)

    one(k_ref, pek_ref, kw1_ref, kw2_ref, kc_out)
    one(v_ref, pev_ref, vw1_ref, vw2_ref, vc_out)


def _compress(kvc, pe_k, pe_v, k_w1, k_w2, v_w1, v_w2):
    _, b, g, t, dh = kvc.shape
    n_chunk = t // CMP_STRIDE
    part = lambda idx: pl.BlockSpec((1, 1, 1, t, dh), lambda i, j: (idx, i, j, 0, 0))
    pe_spec = pl.BlockSpec((1, CMP_BLOCK, dh), lambda i, j: (j, 0, 0))
    out_spec = pl.BlockSpec((1, 1, n_chunk, dh), lambda i, j: (i, j, 0, 0))
    out = jax.ShapeDtypeStruct((b, g, n_chunk, dh), ACT_DTYPE)
    return pl.pallas_call(
        _compress_kernel,
        grid=(b, g),
        in_specs=[part(0), part(1), pe_spec, pe_spec, _const_spec(k_w1.shape), _const_spec(k_w2.shape),
                  _const_spec(v_w1.shape), _const_spec(v_w2.shape)],
        out_specs=[out_spec, out_spec],
        out_shape=[out, out],
        compiler_params=_params(("arbitrary", "arbitrary")),
        name="nsa_compress",
    )(kvc, kvc, pe_k.astype(F32), pe_v.astype(F32), _mx(k_w1), _mx(k_w2), _mx(v_w1), _mx(v_w2))


def _rel_bucket(dist):
    n = jnp.maximum(dist, 0)
    max_exact = REL_BUCKETS // 2
    nf = jnp.maximum(n, 1).astype(F32)
    large = max_exact + (jnp.log(nf / max_exact) / math.log(REL_MAX_DIST / max_exact)
                         * (REL_BUCKETS - max_exact)).astype(jnp.int32)
    return jnp.where(n < max_exact, n, jnp.minimum(large, REL_BUCKETS - 1))


def _toeplitz(vals, rows, cols):
    span = rows + cols
    z = jnp.concatenate([vals, vals[..., :1]], axis=-1)
    flat = jnp.tile(z, rows)[..., :rows * (span - 1)]
    skew = flat.reshape(vals.shape[:-1] + (rows, span - 1))
    return skew[..., rows - 1:rows - 1 + cols]


def _nsa_bias_tiles(rel_bias, t):
    tq, tk = NSA_TQ, NSA_TK
    table = rel_bias.astype(F32).reshape(REL_BUCKETS, NSA_GROUPS, NSA_HPG).transpose(1, 2, 0)
    by_dist = table[:, :, _rel_bucket(jnp.arange(REL_MAX_DIST + 1))]
    far = by_dist[:, :, REL_MAX_DIST:]

    def band(dist, valid, shift=None):
        b = by_dist[:, :, np.clip(dist, 0, REL_MAX_DIST)]
        if shift is not None:
            b = b - shift
        return jnp.where(jnp.asarray(valid), b * LOG2E, NEG)

    u = lambda rows, cols: np.arange(rows + cols - 1) - (rows - 1)
    wk = WINDOW + tq
    dist = np.stack([v * tq - max(v * tq - WINDOW, 0) - u(tq, wk) for v in range(WINDOW // tq + 1)])
    bias_win = _toeplitz(band(dist, (dist >= 0) & (dist < WINDOW)), tq, wk).swapaxes(-1, -2)
    n_near = (REL_MAX_DIST + tk - 1) // tq + 1
    dist = np.stack([e * tq - u(tq, tk) for e in range(n_near)])
    near = _toeplitz(band(dist, dist >= 0, far[:, :, None]), tq, tk).swapaxes(-1, -2)
    bias_sel = jnp.concatenate([near, jnp.zeros_like(near[:, :, :1])], axis=2)
    n_cmp = t // CMP_STRIDE
    dist = np.stack([b - (CMP_BLOCK - 1) + CMP_STRIDE * u(n_cmp, n_cmp) for b in range(CMP_STRIDE)])
    per_b = _toeplitz(band(dist, dist >= 0), n_cmp, n_cmp)
    bias_cmp = jnp.moveaxis(per_b, 2, 4).reshape(NSA_GROUPS, NSA_HPG, n_cmp, t)
    real = np.arange(n_cmp) < (t - CMP_BLOCK) // CMP_STRIDE + 1
    bias_cmp = jnp.where(jnp.asarray(real)[:, None], bias_cmp, NEG)
    return bias_cmp, bias_win, bias_sel


def _nsa_constants(t):
    n_sel = t // SEL_BLOCK
    n_cmp = t // CMP_STRIDE
    cmp_start = np.arange(n_cmp) * CMP_STRIDE
    sel_start = np.arange(n_sel) * SEL_BLOCK
    overlap = ((cmp_start[:, None] < sel_start[None, :] + SEL_BLOCK)
               & (cmp_start[:, None] + CMP_BLOCK > sel_start[None, :]))
    overlap[(t - CMP_BLOCK) // CMP_STRIDE + 1:, :] = False
    return jnp.asarray(overlap.T, MXU_DTYPE)


def _nsa_attn_kernel(q_ref, gate_ref, kc_ref, vc_ref, ks_ref, vs_ref, kw_ref, vw_ref,
                     bcmp_ref, bwin_ref, bsel_ref, ovl_ref, o_ref, neg_ref):
    tq, tk, dh = NSA_TQ, NSA_TK, NSA_DH
    i = pl.program_id(2)
    t0 = i * tq
    heads = range(NSA_HPG)
    q_t = q_ref[0].astype(F32).T
    qh = [_mx(q_t[dh * h:dh * (h + 1), :]) for h in heads]

    def values_t(v):
        ones_col = (lax.broadcasted_iota(jnp.int32, v.shape, 1) == 0).astype(F32)
        return _mx(jnp.concatenate([v.astype(F32), ones_col], axis=1).T)

    k_cmp = kc_ref[0, 0]
    lc = [jnp.dot(k_cmp, qh[h], preferred_element_type=F32) + bcmp_ref[0, h] for h in heads]
    mc = [jnp.max(x, axis=0, keepdims=True) for x in lc]
    pcu = [jnp.where(mc[h] > 0.5 * NEG, jnp.exp2(lc[h] - mc[h]), 0.0) for h in heads]
    pc = [x / jnp.maximum(jnp.sum(x, axis=0, keepdims=True), 1e-30) for x in pcu]
    v_cmp_t = values_t(vc_ref[0, 0])[:dh]
    oc = [_dot(v_cmp_t, x) for x in pc]

    pcsum = (pc[0] + pc[1]) + (pc[2] + pc[3])
    hi = pcsum.astype(MXU_DTYPE)
    mid = (pcsum - hi.astype(F32))
    lo = (mid - mid.astype(MXU_DTYPE).astype(F32)).astype(MXU_DTYPE)
    ovl = ovl_ref[...]
    imp = (jnp.dot(ovl, hi, preferred_element_type=F32) + jnp.dot(ovl, mid.astype(MXU_DTYPE), preferred_element_type=F32)
           + jnp.dot(ovl, lo, preferred_element_type=F32))
    n_sel = imp.shape[0]
    jj = lax.broadcasted_iota(jnp.int32, (n_sel, 1), 0)
    blk_q = lax.shift_right_logical(t0 + lax.broadcasted_iota(jnp.int32, (1, tq), 1),
                                    int(math.log2(SEL_BLOCK)))
    forced = (jj == 0) | (jj == blk_q) | (jj == blk_q - 1)
    score = jnp.where(forced, IMP_FORCE, jnp.where(jj <= blk_q, imp, -1.0))
    rank = jnp.zeros_like(score)
    for j in range(n_sel):
        row = score[j:j + 1, :]
        tie = (jj > j).astype(F32)
        rank = rank + jnp.where(row > score, 1.0, jnp.where(row == score, tie, 0.0))
    neg_ref[...] = jnp.where(rank < SEL_TOPK, 0.0, NEG)

    def normalised(acc):
        return acc[:dh] / acc[dh:dh + 1]

    per_step = tk // SEL_BLOCK

    def sel_step(c, carry):
        ms, accs = carry
        start = pl.multiple_of(c * tk, tk)
        k_c = ks_ref[0, 0, 0, pl.ds(start, tk), :]
        v_c = values_t(vs_ref[0, 0, 0, pl.ds(start, tk), :])
        neg = jnp.concatenate([jnp.broadcast_to(neg_ref[pl.ds(per_step * c + r, 1), :], (SEL_BLOCK, tq))
                               for r in range(per_step)], axis=0)
        near = jnp.minimum((t0 - start) // tq, bsel_ref.shape[2] - 1)
        x = [jnp.dot(k_c, qh[h], preferred_element_type=F32) + bsel_ref[0, h, near] + neg for h in heads]
        m_new = [jnp.maximum(ms[h], jnp.max(x[h], axis=0, keepdims=True)) for h in heads]
        acc_new = [jnp.exp2(ms[h] - m_new[h]) * accs[h] + _dot(v_c, jnp.exp2(x[h] - m_new[h])) for h in heads]
        return tuple(m_new), tuple(acc_new)

    init = (tuple(jnp.full((1, tq), NEG, F32) for _ in heads), tuple(jnp.zeros((2 * dh, tq), F32) for _ in heads))
    _, acc_sel = lax.fori_loop(0, (t0 + tq + tk - 1) // tk, sel_step, init)

    w0 = pl.multiple_of(jnp.maximum(t0 - WINDOW, 0), tq)
    wk = WINDOW + tq
    k_w = kw_ref[0, 0, 0, pl.ds(w0, wk), :]
    v_w = values_t(vw_ref[0, 0, 0, pl.ds(w0, wk), :])
    xw = [jnp.dot(k_w, qh[h], preferred_element_type=F32) + bwin_ref[0, h, 0] for h in heads]
    ow = [_dot(v_w, jnp.exp2(x - jnp.max(x, axis=0, keepdims=True))) for x in xw]

    gates = jax.nn.sigmoid(gate_ref[0]).T
    outs = []
    for h in heads:
        g = [gates[N_BRANCH * h + br:N_BRANCH * h + br + 1, :] for br in range(N_BRANCH)]
        outs.append(g[0] * oc[h] + g[1] * normalised(acc_sel[h]) + g[2] * normalised(ow[h]))
    o_ref[0] = jnp.concatenate(outs, axis=0).T.astype(o_ref.dtype)


def _nsa_attention(q, gate_logits, kv, k_cmp, v_cmp, rel_bias):
    b, t, d = q.shape
    tq, tk, dh = NSA_TQ, NSA_TK, NSA_DH
    bias_cmp, bias_win, bias_sel = _nsa_bias_tiles(rel_bias, t)
    ovl = _nsa_constants(t)
    n_cmp = k_cmp.shape[2]
    n_win = bias_win.shape[2]
    part = lambda idx: pl.BlockSpec((1, 1, 1, t, dh), lambda g, bi, i: (idx, bi, g, 0, 0))
    cmp_spec = pl.BlockSpec((1, 1, n_cmp, dh), lambda g, bi, i: (bi, g, 0, 0))
    return pl.pallas_call(
        _nsa_attn_kernel,
        grid=(NSA_GROUPS, b, t // tq),
        in_specs=[pl.BlockSpec((1, tq, NSA_HPG * dh), lambda g, bi, i: (bi, i, g)),
                  pl.BlockSpec((1, tq, GATE_LANES), lambda g, bi, i: (bi, i, g)),
                  cmp_spec, cmp_spec, part(0), part(1), part(2), part(3),
                  pl.BlockSpec((1, NSA_HPG, n_cmp, tq), lambda g, bi, i: (g, 0, 0, i)),
                  pl.BlockSpec((1, NSA_HPG, 1, WINDOW + tq, tq),
                               lambda g, bi, i: (g, 0, jnp.minimum(i, n_win - 1), 0, 0)),
                  pl.BlockSpec((1, NSA_HPG, bias_sel.shape[2], tk, tq), lambda g, bi, i: (g, 0, 0, 0, 0)),
                  _const_spec(ovl.shape)],
        out_specs=pl.BlockSpec((1, tq, NSA_HPG * dh), lambda g, bi, i: (bi, i, g)),
        out_shape=jax.ShapeDtypeStruct((b, t, d), ACT_DTYPE),
        scratch_shapes=[pltpu.VMEM((t // SEL_BLOCK, tq), F32)],
        compiler_params=_params(("arbitrary", "arbitrary", "arbitrary")),
        name="nsa_attention",
    )(q, gate_logits, k_cmp, v_cmp, kv, kv, kv, kv, bias_cmp, bias_win, bias_sel, ovl)


def _nsa_block(h, moe_out, kv_ln, kv_w, cmp_pe_k, cmp_pe_v, cmp_k_w1, cmp_k_w2, cmp_v_w1, cmp_v_w2,
               rel_bias, b_ln, b_wqg, b_wo, b, t):
    n, d = b * t, kv_ln.shape[0]
    h1, q, gate_logits, kvc, kv = _nsa_proj(h, moe_out, kv_ln, b_ln, kv_w, b_wqg, b, t)
    k_cmp, v_cmp = _compress(kvc, cmp_pe_k, cmp_pe_v, cmp_k_w1, cmp_k_w2, cmp_v_w1, cmp_v_w2)
    o = _nsa_attention(q, gate_logits, kv, k_cmp, v_cmp, rel_bias)
    return _matmul_res(o.reshape(n, d), _mx(b_wo), h1.reshape(n, d))


def kernel(x, a_ln, a_mu, a_wr, a_wk, a_wv, a_wo, a_w0, a_w1, a_w2, a_a0, a_a1, a_a2, a_g1, a_g2, a_kk, a_ka, a_rk, a_lnx_w, a_lnx_b, kv_ln, kv_w, cmp_pe_k, cmp_pe_v, cmp_k_w1, cmp_k_w2, cmp_v_w1, cmp_v_w2, rel_bias, b_ln, b_wqg, b_wo, m_ln, m_wg, m_bg, m_we, m_be, m_w1, m_w3, m_w2, final_ln):
    b, t, d = x.shape
    h = _rwkv_block(x, a_ln[0], a_mu[0], a_wr[0], a_wk[0], a_wv[0], a_wo[0], a_w0[0], a_w1[0], a_w2[0],
                    a_a0[0], a_a1[0], a_a2[0], a_g1[0], a_g2[0], a_kk[0], a_ka[0], a_rk[0],
                    a_lnx_w[0], a_lnx_b[0])
    moe = lambda hh, l: _moe_layer(hh, m_ln[l], m_wg[l], m_bg[l], m_we[l], m_be[l], m_w1[l], m_w3[l], m_w2[l])
    h = _nsa_block(h, moe(h, 0), kv_ln, kv_w, cmp_pe_k, cmp_pe_v, cmp_k_w1, cmp_k_w2, cmp_v_w1, cmp_v_w2,
                   rel_bias, b_ln[0], b_wqg[0], b_wo[0], b, t)
    return _final(h, moe(h, 1), final_ln).reshape(b, t, d)
```

```python
import math

import numpy as np
import jax
import jax.numpy as jnp
from jax import lax
from jax.experimental import pallas as pl
from jax.experimental.pallas import tpu as pltpu

F32 = jnp.float32
MXU_DTYPE = jnp.bfloat16
ACT_DTYPE = jnp.bfloat16

D_MODEL = 1024
NORM_EPS = 1e-6
RWKV_HEAD = 64
RWKV_GN_EPS = 64e-5
RWKV_CHUNK = 64
RWKV_SEQS = 2
LANES = 128
MXU_TILE = 256
VMEM_LIMIT = 56 * 1024 * 1024


def _mx(x):
    return x.astype(MXU_DTYPE)


def _dot(a, b):
    return jnp.dot(_mx(a), _mx(b), preferred_element_type=F32)


def _dot_nt(a, b):
    return lax.dot_general(_mx(a), _mx(b), (((1,), (1,)), ((), ())), preferred_element_type=F32)


def _split2(x):
    hi = x.astype(MXU_DTYPE)
    lo = (x - hi.astype(F32)).astype(MXU_DTYPE)
    return hi, lo


def _dot_hl(x, w):
    hi, lo = _split2(x)
    return jnp.dot(hi, w, preferred_element_type=F32) + jnp.dot(lo, w, preferred_element_type=F32)


def _rms(x, g):
    return x * lax.rsqrt(jnp.mean(x * x, axis=-1, keepdims=True) + NORM_EPS) * g


TOK_ROWS = 8


def _tok_load(ref):
    tm = ref.shape[0] // TOK_ROWS
    return jnp.concatenate([ref[pl.ds(s, tm, stride=TOK_ROWS), :] for s in range(TOK_ROWS)], axis=1)


def _tok_store(ref, val):
    tm = val.shape[0]
    for s in range(TOK_ROWS):
        ref[pl.ds(s, tm, stride=TOK_ROWS), :] = val[:, LANES * s:LANES * (s + 1)]


def _tok_spec(tm, index_map):
    return pl.BlockSpec((TOK_ROWS * tm, LANES), index_map)


def _const_spec(shape):
    nd = len(shape)
    return pl.BlockSpec(shape, lambda *_: (0,) * nd)


def _params(sem):
    return pltpu.CompilerParams(dimension_semantics=sem, vmem_limit_bytes=VMEM_LIMIT)


def _rwkv_pre_kernel(x_ref, ln_ref, mu_ref, wr_ref, wk_ref, wv_ref, w1_ref, a1_ref, g1_ref,
                     w2_ref, a2_ref, g2_ref, vec_ref,
                     r_out, k_out, v_out, kk_out, g_out, ld_out, a_out, prev_ref):
    tm = x_ref.shape[1]
    hn = _rms(x_ref[0], ln_ref[...])

    @pl.when(pl.program_id(1) == 0)
    def _():
        prev_ref[...] = jnp.zeros_like(prev_ref)

    rows = lax.broadcasted_iota(jnp.int32, (tm, 1), 0)
    shifted = jnp.where(rows == 0, prev_ref[0:1, :], pltpu.roll(hn, 1, axis=0))
    prev_ref[0:1, :] = hn[tm - 1:tm, :]
    xx = shifted - hn
    xr, xw, xk, xv, xa, xg = (_mx(hn + xx * mu_ref[i:i + 1, :]) for i in range(6))
    w0, a0, kkp, kap = (vec_ref[i:i + 1, :] for i in range(4))
    r = jnp.dot(xr, wr_ref[...], preferred_element_type=F32)
    k = jnp.dot(xk, wk_ref[...], preferred_element_type=F32)
    v = jnp.dot(xv, wv_ref[...], preferred_element_type=F32)
    z = w0 + _dot(jnp.tanh(jnp.dot(xw, w1_ref[...], preferred_element_type=F32)), w2_ref[...])
    a = jax.nn.sigmoid(a0 + _dot(jnp.dot(xa, a1_ref[...], preferred_element_type=F32), a2_ref[...]))
    g = _dot(jax.nn.sigmoid(jnp.dot(xg, g1_ref[...], preferred_element_type=F32)), g2_ref[...])
    ld_out[0] = (-math.exp(-0.5)) * jax.nn.sigmoid(z)
    a_out[0] = a
    r_out[0] = r.astype(r_out.dtype)
    v_out[0] = v.astype(v_out.dtype)
    kk_out[0] = (k * kkp).astype(kk_out.dtype)
    k_out[0] = (k * (1.0 + (a - 1.0) * kap)).astype(k_out.dtype)
    g_out[0] = g.astype(g_out.dtype)


def _rwkv_pre(x, ln, mu, wr, wk, wv, w1, a1, g1, w2, a2, g2, vecs, tm=256):
    b, t, d = x.shape
    act = jax.ShapeDtypeStruct((b, t, d), ACT_DTYPE)
    f32o = jax.ShapeDtypeStruct((b, t, d), F32)
    blk = pl.BlockSpec((1, tm, d), lambda i, j: (i, j, 0))
    consts = [ln, mu, wr, wk, wv, w1, a1, g1, w2, a2, g2, vecs]
    return pl.pallas_call(
        _rwkv_pre_kernel,
        grid=(b, t // tm),
        in_specs=[blk] + [_const_spec(c.shape) for c in consts],
        out_specs=[blk] * 7,
        out_shape=[act, act, act, act, act, f32o, f32o],
        scratch_shapes=[pltpu.VMEM((8, d), F32)],
        compiler_params=_params(("arbitrary", "arbitrary")),
        name="rwkv_pre",
    )(x, *consts)


def _stack4(x):
    return jnp.concatenate([x[:, MXU_TILE * q:MXU_TILE * (q + 1)] for q in range(4)], axis=0)


def _unstack4(y):
    c = y.shape[0] // 4
    return jnp.concatenate([y[c * q:c * (q + 1), :] for q in range(4)], axis=1)


def _rwkv_scan_kernel(r_ref, k_ref, v_ref, kk_ref, g_ref, ld_ref, a_ref,
                      ltri_ref, smask_ref, eye_ref, j_ref, rk_ref, lnw_ref, lnb_ref,
                      y_out, h_ref):
    c = RWKV_CHUNK
    nb = r_ref.shape[0]

    @pl.when(pl.program_id(1) == 0)
    def _():
        h_ref[...] = jnp.zeros_like(h_ref)

    def rows_of(ref):
        return jnp.concatenate([ref[bi].astype(F32) for bi in range(nb)], axis=0)

    ld, r, k, v, kk = (rows_of(x) for x in (ld_ref, r_ref, k_ref, v_ref, kk_ref))
    jones = j_ref[...]
    ld_hi, ld_lo = _split2(ld)
    cum = (jnp.dot(ltri_ref[...], ld_hi, preferred_element_type=F32)
           + jnp.dot(ltri_ref[...], ld_lo, preferred_element_type=F32))
    last = [cum[(bi + 1) * c - 1:(bi + 1) * c, :] for bi in range(nb)]
    cum_last = jnp.concatenate([jnp.broadcast_to(x, (c, x.shape[1])) for x in last], axis=0)
    e_l = jnp.exp(cum)
    e_lm = jnp.exp(cum - ld)
    e_nl = jnp.exp(-cum)
    e_cl = jnp.exp(cum_last - cum)
    e_c = [jnp.exp(x) for x in last]
    ss = _unstack4(_dot_hl(_stack4(kk * kk), jones))
    kkn = kk * lax.rsqrt(jnp.maximum(ss, 1e-24))
    a_vec = -kkn
    b_vec = kkn * rows_of(a_ref)
    rt = r * e_l
    at = a_vec * e_lm
    kt = k * e_nl
    bt = b_vec * e_nl
    bh = b_vec * e_cl
    kh = k * e_cl

    lane = lax.broadcasted_iota(jnp.int32, (1, LANES), 1)
    m_l = (lane < RWKV_HEAD).astype(F32)
    m_r = 1.0 - m_l

    def st(xp):
        return jnp.concatenate([xp * m_l, xp * m_r], axis=0)

    def dup(xp):
        return jnp.concatenate([xp, xp], axis=0)

    smask = smask_ref[...] > 0.0
    eye = eye_ref[...]
    n_pair = D_MODEL // LANES
    pairs = range(nb * n_pair)
    seq_of = [p // n_pair for p in pairs]

    def part(x, p):
        return x[seq_of[p] * c:(seq_of[p] + 1) * c, LANES * (p % n_pair):LANES * (p % n_pair + 1)]

    at_s = [st(part(at, p)) for p in pairs]
    rt_s = [st(part(rt, p)) for p in pairs]
    v_s = [st(part(v, p)) for p in pairs]
    s = [jnp.where(smask,
                   _dot_nt(jnp.concatenate([at_s[p], rt_s[p]], axis=0),
                           jnp.concatenate([dup(part(bt, p)), dup(part(kt, p))], axis=0)),
                   0.0) for p in pairs]
    a_ab = [x[:2 * c, :2 * c] for x in s]
    a_r = [x[2 * c:, :] for x in s]
    g1 = [_dot(s[p][:2 * c, 2 * c:], v_s[p]) for p in pairs]
    t_m = [eye + x for x in a_ab]
    a_pow = [_dot(x, x) for x in a_ab]
    n = 2
    while n < c:
        if 2 * n < c:
            both = [_dot(a_pow[p], jnp.concatenate([a_pow[p], t_m[p]], axis=1)) for p in pairs]
            a_pow = [x[:, :2 * c] for x in both]
            t_m = [t_m[p] + both[p][:, 2 * c:] for p in pairs]
        else:
            t_m = [t_m[p] + _dot(a_pow[p], t_m[p]) for p in pairs]
        n *= 2
    tw = [_dot(t_m[p], jnp.concatenate([at_s[p], g1[p]], axis=1)) for p in pairs]
    zmat = [jnp.concatenate([tw[p], jnp.concatenate([jnp.zeros_like(v_s[p]), v_s[p]], axis=1)], axis=0)
            for p in pairs]
    upper = [_dot(a_r[p], zmat[p]) for p in pairs]
    bk_t = [jnp.concatenate([st(part(bh, p)), st(part(kh, p))], axis=0).T for p in pairs]
    lower = [_dot(bk_t[p], zmat[p]) for p in pairs]
    decay = [eye * e_c[seq_of[p]][:, LANES * (p % n_pair):LANES * (p % n_pair + 1)] for p in pairs]
    seq = [_dot(jnp.concatenate([upper[p][:, :LANES] + rt_s[p], lower[p][:, :LANES] + decay[p]], axis=0), h_ref[p])
           for p in pairs]
    ys = []
    for p in pairs:
        y_st = seq[p][:2 * c, :] + upper[p][:, LANES:]
        h_ref[p] = seq[p][2 * c:, :] + lower[p][:, LANES:]
        ys.append(y_st[:c, :] + y_st[c:, :])
    y = jnp.concatenate([jnp.concatenate(ys[bi * n_pair:(bi + 1) * n_pair], axis=1) for bi in range(nb)],
                        axis=0)

    inv_n = 1.0 / RWKV_HEAD
    y4 = _stack4(y)
    dlt = y4 - _dot_hl(y4, jones) * inv_n
    var = _dot_hl(dlt * dlt, jones) * inv_n
    yn = dlt * lax.rsqrt(var + RWKV_GN_EPS)
    bonus = _dot_hl(_stack4(r * k) * rk_ref[...], jones) * _stack4(v)
    out = _unstack4((yn * lnw_ref[...] + lnb_ref[...] + bonus) * _stack4(rows_of(g_ref)))
    for bi in range(nb):
        y_out[bi] = out[bi * c:(bi + 1) * c, :].astype(y_out.dtype)


def _scan_constants():
    c = RWKV_CHUNK
    ltri = np.kron(np.eye(RWKV_SEQS), np.tril(np.ones((c, c), np.float32)))
    rho = np.arange(4 * c)
    r_type, r_head, r_t = rho // (2 * c), (rho % (2 * c)) // c, rho % c
    same = r_head[:, None] == r_head[None, :]
    strict = r_t[None, :] < r_t[:, None]
    incl = r_t[None, :] <= r_t[:, None]
    smask = same & np.where(r_type[:, None] == 0, strict, incl)
    eye = np.eye(LANES, dtype=np.float32)
    head = np.arange(MXU_TILE) // RWKV_HEAD
    jones = (head[:, None] == head[None, :]).astype(np.float32)
    return (jnp.asarray(ltri, MXU_DTYPE), jnp.asarray(smask, F32), jnp.asarray(eye, F32),
            jnp.asarray(jones, MXU_DTYPE))


def _stack_param(p):
    return jnp.repeat(p.reshape(4, MXU_TILE).astype(F32), RWKV_SEQS * RWKV_CHUNK, axis=0)


def _rwkv_scan(r, k, v, kk, g, ld, a, rk, lnw, lnb):
    b, t, d = r.shape
    c, nb = RWKV_CHUNK, RWKV_SEQS
    assert b % nb == 0 and t % c == 0 and d == D_MODEL
    blk = pl.BlockSpec((nb, c, d), lambda i, j: (i, j, 0))
    consts = list(_scan_constants()) + [_stack_param(rk.reshape(-1)), _stack_param(lnw), _stack_param(lnb)]
    return pl.pallas_call(
        _rwkv_scan_kernel,
        grid=(b // nb, t // c),
        in_specs=[blk] * 7 + [_const_spec(x.shape) for x in consts],
        out_specs=blk,
        out_shape=jax.ShapeDtypeStruct((b, t, d), ACT_DTYPE),
        scratch_shapes=[pltpu.VMEM((nb * (d // LANES), LANES, LANES), F32)],
        compiler_params=_params(("arbitrary", "arbitrary")),
        name="rwkv_scan",
    )(r, k, v, kk, g, ld, a, *consts)


def _matmul_res_kernel(a_ref, w_ref, res_ref, ln_ref, w_hi_ref, w_lo_ref, bias_ref, tri_ref,
                       o_ref, route_out, count_out, cnt_ref):
    h = res_ref[...] + jnp.dot(a_ref[...], w_ref[...], preferred_element_type=F32)
    _tok_store(o_ref, h)
    _route_tokens(h, ln_ref, w_hi_ref, w_lo_ref, bias_ref, tri_ref, route_out, count_out, cnt_ref)


def _matmul_res(a, w, res, ln, router, tm=512):
    n, kdim = a.shape
    d = w.shape[1]
    tri = jnp.asarray(np.tril(np.ones((tm, tm), np.float32), -1), MXU_DTYPE)
    consts = [ln.reshape(1, d), *router, tri]
    return pl.pallas_call(
        _matmul_res_kernel,
        grid=(n // tm,),
        in_specs=[pl.BlockSpec((tm, kdim), lambda i: (i, 0)), _const_spec(w.shape),
                  pl.BlockSpec((tm, d), lambda i: (i, 0))] + [_const_spec(c.shape) for c in consts],
        out_specs=[_tok_spec(tm, lambda i: (i, 0)), pl.BlockSpec((tm, LANES), lambda i: (i, 0)),
                   _const_spec((1, LANES))],
        out_shape=[jax.ShapeDtypeStruct((TOK_ROWS * n, LANES), F32), jax.ShapeDtypeStruct((n, LANES), F32),
                   jax.ShapeDtypeStruct((1, LANES), F32)],
        scratch_shapes=[pltpu.VMEM((1, LANES), F32)],
        compiler_params=_params(("arbitrary",)),
        name="matmul_res_route",
    )(a, w, res, *consts)


def _rwkv_block(x, a_ln, a_mu, a_wr, a_wk, a_wv, a_w0, a_w1, a_w2, a_a0, a_a1, a_a2,
                a_g1, a_g2, a_kk, a_ka, a_rk, a_lnx_w, a_lnx_b):
    b, t, d = x.shape
    vecs = jnp.stack([a_w0, a_a0, a_kk, a_ka]).astype(F32)
    r, k, v, kk, g, ld, a = _rwkv_pre(
        x, a_ln.reshape(1, d), a_mu, _mx(a_wr), _mx(a_wk), _mx(a_wv), _mx(a_w1), _mx(a_a1), _mx(a_g1),
        _mx(a_w2), _mx(a_a2), _mx(a_g2), vecs)
    return _rwkv_scan(r, k, v, kk, g, ld, a, a_rk, a_lnx_w, a_lnx_b)


MOE_GROUPS = 4
MOE_EPG = 8
MOE_EXPERTS = MOE_GROUPS * MOE_EPG
MOE_TOPK = 2
MOE_BLOCK = 512
NEG = -1e30


def _route(logits):
    lane = lax.broadcasted_iota(jnp.int32, logits.shape, 1)
    is_grp = lane < MOE_GROUPS
    gl = jnp.where(is_grp, logits, NEG)
    ge = jnp.exp(gl - jnp.max(gl, axis=-1, keepdims=True))
    gp = ge / jnp.sum(ge, axis=-1, keepdims=True)
    grp_w = jnp.max(gp, axis=-1, keepdims=True)
    grp = jnp.min(jnp.where(is_grp & (gp == grp_w), lane, LANES), axis=-1, keepdims=True)
    lo = MOE_GROUPS + MOE_EPG * grp
    in_grp = (lane >= lo) & (lane < lo + MOE_EPG)
    el = jnp.where(in_grp, logits, NEG)
    ee = jnp.exp(el - jnp.max(el, axis=-1, keepdims=True))
    ep = ee / jnp.sum(ee, axis=-1, keepdims=True)
    p1 = jnp.max(jnp.where(in_grp, ep, -1.0), axis=-1, keepdims=True)
    i1 = jnp.min(jnp.where(in_grp & (ep == p1), lane, LANES), axis=-1, keepdims=True)
    rest = in_grp & (lane != i1)
    p2 = jnp.max(jnp.where(rest, ep, -1.0), axis=-1, keepdims=True)
    i2 = jnp.min(jnp.where(rest & (ep == p2), lane, LANES), axis=-1, keepdims=True)
    scale = grp_w / (p1 + p2)
    return i1 - MOE_GROUPS, i2 - MOE_GROUPS, p1 * scale, p2 * scale


def _router_logits(xf, w_hi_ref, w_lo_ref, bias_ref):
    hi, lo = _split2(xf)
    logits = (jnp.dot(hi, w_hi_ref[...], preferred_element_type=F32)
              + jnp.dot(lo, w_hi_ref[...], preferred_element_type=F32)
              + jnp.dot(hi, w_lo_ref[...], preferred_element_type=F32))
    return logits + bias_ref[...]


def _route_tokens(h, ln_ref, w_hi_ref, w_lo_ref, bias_ref, tri_ref, route_out, count_out, cnt_ref):
    @pl.when(pl.program_id(0) == 0)
    def _():
        cnt_ref[...] = jnp.zeros_like(cnt_ref)

    xf = _rms(h, ln_ref[...])
    e1, e2, w1, w2 = _route(_router_logits(xf, w_hi_ref, w_lo_ref, bias_ref))
    lane = lax.broadcasted_iota(jnp.int32, (xf.shape[0], LANES), 1)
    hit1, hit2 = lane == e1, lane == e2
    onehot = jnp.where(hit1, 1.0, jnp.where(hit2, 1.0, 0.0))
    before = jnp.dot(tri_ref[...], _mx(onehot), preferred_element_type=F32) + cnt_ref[...]
    r1 = jnp.sum(jnp.where(hit1, before, 0.0), axis=-1, keepdims=True)
    r2 = jnp.sum(jnp.where(hit2, before, 0.0), axis=-1, keepdims=True)
    cnt_ref[...] = cnt_ref[...] + jnp.sum(onehot, axis=0, keepdims=True)
    count_out[...] = cnt_ref[...]
    fields = (e1.astype(F32), e2.astype(F32), w1, w2, r1, r2)
    rec = jnp.zeros(lane.shape, F32)
    for idx, val in enumerate(fields):
        rec = jnp.where(lane == idx, val, rec)
    route_out[...] = rec


def _router_weights(w_gr, b_gr, w_er, b_er):
    d = w_gr.shape[0]
    w = jnp.zeros((d, LANES), F32).at[:, :MOE_GROUPS].set(w_gr).at[:, MOE_GROUPS:MOE_GROUPS + MOE_EXPERTS].set(w_er)
    bias = jnp.zeros((1, LANES), F32).at[0, :MOE_GROUPS].set(b_gr).at[0, MOE_GROUPS:MOE_GROUPS + MOE_EXPERTS].set(b_er)
    w_hi = w.astype(MXU_DTYPE)
    w_lo = (w - w_hi.astype(F32)).astype(MXU_DTYPE)
    return w_hi, w_lo, bias


def _moe_plan(route, counts, blk):
    n = route.shape[0]
    counts = counts[0, :MOE_EXPERTS].astype(jnp.int32)
    pcounts = (counts + blk - 1) // blk * blk
    pend = jnp.cumsum(pcounts)
    pstart = (pend - pcounts).astype(jnp.int32)
    n_blk = n * MOE_TOPK // blk + MOE_EXPERTS
    blk_start = jnp.arange(n_blk, dtype=jnp.int32) * blk
    blk_e = jnp.sum((blk_start[:, None] >= pend[None, :]).astype(jnp.int32), axis=1)
    blk_e = jnp.minimum(blk_e, MOE_EXPERTS - 1)
    n_valid = jnp.clip((pstart + counts)[blk_e] - blk_start, 0, blk).astype(jnp.int32)
    n_act = (pend[-1] // blk).astype(jnp.int32).reshape(1)
    expert = route[:, 0:MOE_TOPK].astype(jnp.int32)
    onehot = expert[:, :, None] == jnp.arange(MOE_EXPERTS, dtype=jnp.int32)
    seg_start = jnp.sum(jnp.where(onehot, pstart, 0), axis=-1)
    sorted_pos = seg_start + route[:, 4:4 + MOE_TOPK].astype(jnp.int32)
    return blk_e, n_valid, n_act, sorted_pos


def _token_copies(pos_ref, tm, make_copy):
    group = 8

    def body(g, carry):
        t0 = g * group
        pos = [[pos_ref[0, 0, MOE_TOPK * (t0 + j) + k] for k in range(MOE_TOPK)] for j in range(group)]
        for j in range(group):
            for k in range(MOE_TOPK):
                make_copy(t0 + j, k, pos[j][k]).start()
        return carry

    lax.fori_loop(0, tm // group, body, 0)


def _tile(ref, tok):
    return ref.at[pl.ds(pl.multiple_of(tok * TOK_ROWS, TOK_ROWS), TOK_ROWS)]


def _moe_dispatch_kernel(n_valid_ref, pos_ref, src_ref, dst_hbm, stage, sem, zbuf, zsem):
    tm = src_ref.shape[0] // TOK_ROWS
    step = pl.program_id(0)
    slot = lax.rem(step, 2)

    def drain(s):
        for k in range(MOE_TOPK):
            pltpu.make_async_copy(stage.at[s], dst_hbm.at[pl.ds(0, stage.shape[1])], sem.at[s]).wait()

    @pl.when(pl.program_id(0) == 0)
    def _():
        rows = zbuf.shape[0]
        zbuf[...] = jnp.zeros_like(zbuf)

        def fill(j, wait):
            @pl.when(n_valid_ref[j] < rows // TOK_ROWS)
            def _():
                start = 0 if wait else pl.multiple_of(j * rows, rows)
                cp = pltpu.make_async_copy(zbuf, dst_hbm.at[pl.ds(start, rows)], zsem)
                cp.wait() if wait else cp.start()

        lax.fori_loop(0, n_valid_ref.shape[0], lambda j, c: (fill(j, False), c)[1], 0)
        lax.fori_loop(0, n_valid_ref.shape[0], lambda j, c: (fill(j, True), c)[1], 0)

    @pl.when(step >= 2)
    def _():
        drain(slot)

    stage[slot] = src_ref[...]
    _token_copies(pos_ref, tm, lambda t, k, pos: pltpu.make_async_copy(
        _tile(stage.at[slot], t), _tile(dst_hbm, pos), sem.at[slot]))

    @pl.when(step == pl.num_programs(0) - 1)
    def _():
        @pl.when(step >= 1)
        def _():
            drain(1 - slot)
        drain(slot)


def _moe_gather_combine(step, n_steps, pos_ref, pos_next_ref, ys_hbm, h_ref, route, gbuf, gsem):
    tm = gbuf.shape[2] // TOK_ROWS
    slot = lax.rem(step, 2)

    def start(p_ref, s):
        _token_copies(p_ref, tm, lambda t, k, pos: pltpu.make_async_copy(
            _tile(ys_hbm, pos), _tile(gbuf.at[s, k], t), gsem.at[s]))

    @pl.when(step == 0)
    def _():
        start(pos_ref, 0)

    @pl.when(step + 1 < n_steps)
    def _():
        start(pos_next_ref, 1 - slot)

    for k in range(MOE_TOPK):
        pltpu.make_async_copy(ys_hbm.at[pl.ds(0, gbuf.shape[2])], gbuf.at[slot, k], gsem.at[slot]).wait()
    return (_tok_load(h_ref) + route[:, 2:3] * _tok_load(gbuf.at[slot, 0])
            + route[:, 3:4] * _tok_load(gbuf.at[slot, 1]))


def _gather_specs(sorted_pos, tm, step_index):
    n = sorted_pos.shape[0]
    pos = sorted_pos.reshape(n // tm, 1, MOE_TOPK * tm)
    last = n // tm - 1
    cur = pl.BlockSpec((1, 1, MOE_TOPK * tm), lambda *g: (step_index(*g), 0, 0), memory_space=pltpu.SMEM)
    nxt = pl.BlockSpec((1, 1, MOE_TOPK * tm), lambda *g: (jnp.minimum(step_index(*g) + 1, last), 0, 0),
                       memory_space=pltpu.SMEM)
    scratch = [pltpu.VMEM((2, MOE_TOPK, TOK_ROWS * tm, LANES), F32), pltpu.SemaphoreType.DMA((2,))]
    return pos, [cur, nxt, pl.BlockSpec(memory_space=pl.ANY)], scratch


def _moe_dispatch(h, n_valid, sorted_pos, p, blk, tm=512):
    n = sorted_pos.shape[0]
    pos = sorted_pos.reshape(n // tm, 1, MOE_TOPK * tm)
    grid_spec = pltpu.PrefetchScalarGridSpec(
        num_scalar_prefetch=1,
        grid=(n // tm,),
        in_specs=[pl.BlockSpec((1, 1, MOE_TOPK * tm), lambda i, nv: (i, 0, 0), memory_space=pltpu.SMEM),
                  _tok_spec(tm, lambda i, nv: (i, 0))],
        out_specs=pl.BlockSpec(memory_space=pl.ANY),
        scratch_shapes=[pltpu.VMEM((2, TOK_ROWS * tm, LANES), h.dtype), pltpu.SemaphoreType.DMA((2,)),
                        pltpu.VMEM((TOK_ROWS * blk, LANES), h.dtype), pltpu.SemaphoreType.DMA],
    )
    return pl.pallas_call(
        _moe_dispatch_kernel,
        grid_spec=grid_spec,
        out_shape=jax.ShapeDtypeStruct((TOK_ROWS * p, LANES), h.dtype),
        compiler_params=_params(("arbitrary",)),
        name="moe_dispatch",
    )(n_valid, pos, h)


def _moe_expert_kernel(blk_e_ref, n_act_ref, x_ref, ln_ref, w1_ref, w3_ref, w2_ref, y_ref):
    del blk_e_ref
    i = pl.program_id(0)

    @pl.when(i < n_act_ref[0])
    def _():
        xn = _mx(_rms(_tok_load(x_ref), ln_ref[...]))
        h1 = jnp.dot(xn, w1_ref[0], preferred_element_type=F32)
        h3 = jnp.dot(xn, w3_ref[0], preferred_element_type=F32)
        _tok_store(y_ref, _dot(h1 * jax.nn.sigmoid(h1) * h3, w2_ref[0]))

    @pl.when(i >= n_act_ref[0])
    def _():
        y_ref[...] = jnp.zeros_like(y_ref)


def _moe_experts(xs, ln, w1, w3, w2, blk_e, n_act, blk):
    p = xs.shape[0] // TOK_ROWS
    d, hdim = w1.shape[1], w1.shape[2]
    grid_spec = pltpu.PrefetchScalarGridSpec(
        num_scalar_prefetch=2,
        grid=(p // blk,),
        in_specs=[_tok_spec(blk, lambda i, be, na: (i, 0)),
                  pl.BlockSpec((1, d), lambda i, be, na: (0, 0)),
                  pl.BlockSpec((1, d, hdim), lambda i, be, na: (be[i], 0, 0)),
                  pl.BlockSpec((1, d, hdim), lambda i, be, na: (be[i], 0, 0)),
                  pl.BlockSpec((1, hdim, d), lambda i, be, na: (be[i], 0, 0))],
        out_specs=_tok_spec(blk, lambda i, be, na: (i, 0)),
    )
    return pl.pallas_call(
        _moe_expert_kernel,
        grid_spec=grid_spec,
        out_shape=jax.ShapeDtypeStruct(xs.shape, F32),
        compiler_params=_params(("arbitrary",)),
        name="moe_experts",
    )(blk_e, n_act, xs, ln.reshape(1, d), w1, w3, w2)


def _moe_layer(h, route, counts, ln, w1, w3, w2):
    n = h.shape[0] // TOK_ROWS
    blk = MOE_BLOCK
    blk_e, n_valid, n_act, sorted_pos = _moe_plan(route, counts, blk)
    p = n * MOE_TOPK + MOE_EXPERTS * blk
    xs = _moe_dispatch(h, n_valid, sorted_pos, p, blk)
    ys = _moe_experts(xs, ln, _mx(w1), _mx(w3), _mx(w2), blk_e, n_act, blk)
    return ys, sorted_pos, route


def _final_kernel(pos_ref, pos_next_ref, ys_hbm, h_ref, route_ref, ln_ref, o_ref, gbuf, gsem):
    h = _moe_gather_combine(pl.program_id(0), pl.num_programs(0), pos_ref, pos_next_ref, ys_hbm, h_ref,
                            route_ref[...], gbuf, gsem)
    o_ref[...] = _rms(h, ln_ref[...])


def _final(h, moe_out, ln, tm=512):
    ys, sorted_pos, route = moe_out
    n, d = route.shape[0], ln.shape[0]
    pos, gather_specs, scratch = _gather_specs(sorted_pos, tm, lambda i: i)
    return pl.pallas_call(
        _final_kernel,
        grid=(n // tm,),
        in_specs=gather_specs + [_tok_spec(tm, lambda i: (i, 0)), pl.BlockSpec((tm, LANES), lambda i: (i, 0)),
                                 _const_spec((1, d))],
        out_specs=pl.BlockSpec((tm, d), lambda i: (i, 0)),
        out_shape=jax.ShapeDtypeStruct((n, d), F32),
        scratch_shapes=scratch,
        compiler_params=_params(("arbitrary",)),
        name="final_norm",
    )(pos, pos, ys, h, route, ln.reshape(1, d))


NSA_HEADS = 16
NSA_GROUPS = 4
NSA_HPG = NSA_HEADS // NSA_GROUPS
NSA_DH = D_MODEL // NSA_HEADS
NSA_PARTS = 6
CMP_BLOCK = 32
CMP_STRIDE = 16
SEL_BLOCK = 64
SEL_TOPK = 8
WINDOW = 512
N_BRANCH = 3
IMP_FORCE = 1e4
REL_BUCKETS = 32
REL_MAX_DIST = 128
LOG2E = math.log2(math.e)
NSA_TQ = 256
NSA_TK = 512
GATE_LANES = LANES
CMP_PARTS = 2


def _nsa_proj_kernel(pos_ref, pos_next_ref, ys_hbm, h_ref, route_ref, kvln_ref, qln_ref, kvw_ref, qw_ref,
                     h_out, q_out, gate_out, kvc_out, kv_out, gbuf, gsem):
    step = pl.program_id(0) * pl.num_programs(1) + pl.program_id(1)
    h1 = _moe_gather_combine(step, pl.num_programs(0) * pl.num_programs(1), pos_ref, pos_next_ref, ys_hbm, h_ref,
                             route_ref[0], gbuf, gsem)
    h_out[0] = h1
    kv = _dot(_rms(h1, kvln_ref[...]), kvw_ref[...])
    for part in range(NSA_PARTS):
        for g in range(NSA_GROUPS):
            col = (part * NSA_GROUPS + g) * NSA_DH
            piece = kv[:, col:col + NSA_DH]
            if part < CMP_PARTS:
                kvc_out[part, 0, g] = piece
            else:
                kv_out[part - CMP_PARTS, 0, g] = piece.astype(kv_out.dtype)
    qg = _dot(_rms(h1, qln_ref[...]), qw_ref[...])
    q_out[0] = (qg[:, :D_MODEL] * (NSA_DH ** -0.5 * LOG2E)).astype(q_out.dtype)
    gate_out[0] = qg[:, D_MODEL:]


def _gate_padded_weight(q_w):
    d = q_w.shape[0]
    n_gate = NSA_HPG * N_BRANCH
    gates = q_w[:, d:].reshape(d, NSA_GROUPS, n_gate)
    gates = jnp.pad(gates, ((0, 0), (0, 0), (0, GATE_LANES - n_gate))).reshape(d, NSA_GROUPS * GATE_LANES)
    return jnp.concatenate([q_w[:, :d], gates], axis=1)


def _nsa_proj(h, moe_out, kv_ln, q_ln, kv_w, q_w, b, t, tm=256):
    ys, sorted_pos, route = moe_out
    d = kv_ln.shape[0]
    blk = lambda w: pl.BlockSpec((1, tm, w), lambda i, j: (i, j, 0))
    per_b = t // tm
    tok_block = lambda i, j: i * per_b + j
    pos, gather_specs, scratch = _gather_specs(sorted_pos, tm, tok_block)
    gw = NSA_GROUPS * GATE_LANES
    qw = _gate_padded_weight(q_w)
    kv_spec = lambda parts: pl.BlockSpec((parts, 1, NSA_GROUPS, tm, NSA_DH), lambda i, j: (0, i, 0, j, 0))
    kv_shape = lambda parts: (parts, b, NSA_GROUPS, t, NSA_DH)
    return pl.pallas_call(
        _nsa_proj_kernel,
        grid=(b, per_b),
        in_specs=gather_specs + [_tok_spec(tm, lambda i, j: (tok_block(i, j), 0)), blk(LANES),
                                 _const_spec((1, d)), _const_spec((1, d)),
                                 _const_spec(kv_w.shape), _const_spec(qw.shape)],
        out_specs=[blk(d), blk(d), blk(gw), kv_spec(CMP_PARTS), kv_spec(NSA_PARTS - CMP_PARTS)],
        out_shape=[jax.ShapeDtypeStruct((b, t, d), F32), jax.ShapeDtypeStruct((b, t, d), ACT_DTYPE),
                   jax.ShapeDtypeStruct((b, t, gw), F32), jax.ShapeDtypeStruct(kv_shape(CMP_PARTS), F32),
                   jax.ShapeDtypeStruct(kv_shape(NSA_PARTS - CMP_PARTS), ACT_DTYPE)],
        scratch_shapes=scratch,
        compiler_params=_params(("arbitrary", "arbitrary")),
        name="nsa_proj",
    )(pos, pos, ys, h, route.reshape(b, t, LANES),
      kv_ln.reshape(1, d), q_ln.reshape(1, d), _mx(kv_w), _mx(qw))


def _gelu_tanh(x):
    return 0.5 * x * (1.0 + jnp.tanh(math.sqrt(2.0 / math.pi) * (x + 0.044715 * (x * x * x))))


def _compress_kernel(k_ref, v_ref, pek_ref, pev_ref, kw1_ref, kw2_ref, vw1_ref, vw2_ref, kc_out, vc_out):
    dh = k_ref.shape[-1]
    n_chunk = k_ref.shape[3] // CMP_STRIDE

    def one(z_ref, pe_ref, w1_ref, w2_ref, out):
        first = jnp.zeros((n_chunk, w1_ref.shape[1]), F32)
        second = first
        for l in range(CMP_STRIDE):
            z = z_ref[0, 0, 0, pl.ds(l, n_chunk, stride=CMP_STRIDE), :]
            lo, hi = l, CMP_STRIDE + l
            first = first + _dot(z + pe_ref[0, lo:lo + 1, :], w1_ref[dh * lo:dh * (lo + 1), :])
            second = second + _dot(z + pe_ref[0, hi:hi + 1, :], w1_ref[dh * hi:dh * (hi + 1), :])
        hid = first + pltpu.roll(second, n_chunk - 1, axis=0)
        out[0, 0] = _dot(_gelu_tanh(hid), w2_ref[...]).astype(out.dtype)

    one(k_ref, pek_ref, kw1_ref, kw2_ref, kc_out)
    one(v_ref, pev_ref, vw1_ref, vw2_ref, vc_out)


def _compress(kvc, pe_k, pe_v, k_w1, k_w2, v_w1, v_w2):
    _, b, g, t, dh = kvc.shape
    n_chunk = t // CMP_STRIDE
    part = lambda idx: pl.BlockSpec((1, 1, 1, t, dh), lambda i, j: (idx, i, j, 0, 0))
    pe_spec = pl.BlockSpec((1, CMP_BLOCK, dh), lambda i, j: (j, 0, 0))
    out_spec = pl.BlockSpec((1, 1, n_chunk, dh), lambda i, j: (i, j, 0, 0))
    out = jax.ShapeDtypeStruct((b, g, n_chunk, dh), ACT_DTYPE)
    return pl.pallas_call(
        _compress_kernel,
        grid=(b, g),
        in_specs=[part(0), part(1), pe_spec, pe_spec, _const_spec(k_w1.shape), _const_spec(k_w2.shape),
                  _const_spec(v_w1.shape), _const_spec(v_w2.shape)],
        out_specs=[out_spec, out_spec],
        out_shape=[out, out],
        compiler_params=_params(("arbitrary", "arbitrary")),
        name="nsa_compress",
    )(kvc, kvc, pe_k.astype(F32), pe_v.astype(F32), _mx(k_w1), _mx(k_w2), _mx(v_w1), _mx(v_w2))


def _rel_bucket(dist):
    n = jnp.maximum(dist, 0)
    max_exact = REL_BUCKETS // 2
    nf = jnp.maximum(n, 1).astype(F32)
    large = max_exact + (jnp.log(nf / max_exact) / math.log(REL_MAX_DIST / max_exact)
                         * (REL_BUCKETS - max_exact)).astype(jnp.int32)
    return jnp.where(n < max_exact, n, jnp.minimum(large, REL_BUCKETS - 1))


def _toeplitz(vals, rows, cols):
    span = rows + cols
    z = jnp.concatenate([vals, vals[..., :1]], axis=-1)
    flat = jnp.tile(z, rows)[..., :rows * (span - 1)]
    skew = flat.reshape(vals.shape[:-1] + (rows, span - 1))
    return skew[..., rows - 1:rows - 1 + cols]


def _nsa_bias_tiles(rel_bias, t):
    tq, tk = NSA_TQ, NSA_TK
    table = rel_bias.astype(F32).reshape(REL_BUCKETS, NSA_GROUPS, NSA_HPG).transpose(1, 2, 0)
    by_dist = table[:, :, _rel_bucket(jnp.arange(REL_MAX_DIST + 1))]
    far = by_dist[:, :, REL_MAX_DIST:]

    def band(dist, valid, shift=None):
        b = by_dist[:, :, np.clip(dist, 0, REL_MAX_DIST)]
        if shift is not None:
            b = b - shift
        return jnp.where(jnp.asarray(valid), b * LOG2E, NEG)

    u = lambda rows, cols: np.arange(rows + cols - 1) - (rows - 1)
    wk = WINDOW + tq
    dist = np.stack([v * tq - max(v * tq - WINDOW, 0) - u(tq, wk) for v in range(WINDOW // tq + 1)])
    bias_win = _toeplitz(band(dist, (dist >= 0) & (dist < WINDOW)), tq, wk).swapaxes(-1, -2)
    n_near = (REL_MAX_DIST + tk - 1) // tq + 1
    dist = np.stack([e * tq - u(tq, tk) for e in range(n_near)])
    near = _toeplitz(band(dist, dist >= 0, far[:, :, None]), tq, tk).swapaxes(-1, -2)
    bias_sel = jnp.concatenate([near, jnp.zeros_like(near[:, :, :1])], axis=2)
    n_cmp = t // CMP_STRIDE
    dist = np.stack([b - (CMP_BLOCK - 1) + CMP_STRIDE * u(n_cmp, n_cmp) for b in range(CMP_STRIDE)])
    per_b = _toeplitz(band(dist, dist >= 0), n_cmp, n_cmp)
    bias_cmp = jnp.moveaxis(per_b, 2, 4).reshape(NSA_GROUPS, NSA_HPG, n_cmp, t)
    real = np.arange(n_cmp) < (t - CMP_BLOCK) // CMP_STRIDE + 1
    bias_cmp = jnp.where(jnp.asarray(real)[:, None], bias_cmp, NEG)
    return bias_cmp, bias_win, bias_sel


def _nsa_constants(t):
    n_sel = t // SEL_BLOCK
    n_cmp = t // CMP_STRIDE
    cmp_start = np.arange(n_cmp) * CMP_STRIDE
    sel_start = np.arange(n_sel) * SEL_BLOCK
    overlap = ((cmp_start[:, None] < sel_start[None, :] + SEL_BLOCK)
               & (cmp_start[:, None] + CMP_BLOCK > sel_start[None, :]))
    overlap[(t - CMP_BLOCK) // CMP_STRIDE + 1:, :] = False
    return jnp.asarray(overlap.T, MXU_DTYPE)


def _nsa_attn_kernel(q_ref, gate_ref, kc_ref, vc_ref, ks_ref, vs_ref, kw_ref, vw_ref,
                     bcmp_ref, bwin_ref, bsel_ref, ovl_ref, o_ref, neg_ref):
    tq, tk, dh = NSA_TQ, NSA_TK, NSA_DH
    i = pl.program_id(2)
    t0 = i * tq
    heads = range(NSA_HPG)
    q_t = q_ref[0].astype(F32).T
    qh = [_mx(q_t[dh * h:dh * (h + 1), :]) for h in heads]

    def values_t(v):
        ones_col = (lax.broadcasted_iota(jnp.int32, v.shape, 1) == 0).astype(F32)
        return _mx(jnp.concatenate([v.astype(F32), ones_col], axis=1).T)

    k_cmp = kc_ref[0, 0]
    lc = [jnp.dot(k_cmp, qh[h], preferred_element_type=F32) + bcmp_ref[0, h] for h in heads]
    mc = [jnp.max(x, axis=0, keepdims=True) for x in lc]
    pcu = [jnp.where(mc[h] > 0.5 * NEG, jnp.exp2(lc[h] - mc[h]), 0.0) for h in heads]
    pc = [x / jnp.maximum(jnp.sum(x, axis=0, keepdims=True), 1e-30) for x in pcu]
    v_cmp_t = values_t(vc_ref[0, 0])[:dh]
    oc = [_dot(v_cmp_t, x) for x in pc]

    pcsum = (pc[0] + pc[1]) + (pc[2] + pc[3])
    hi = pcsum.astype(MXU_DTYPE)
    mid = (pcsum - hi.astype(F32))
    lo = (mid - mid.astype(MXU_DTYPE).astype(F32)).astype(MXU_DTYPE)
    ovl = ovl_ref[...]
    imp = (jnp.dot(ovl, hi, preferred_element_type=F32) + jnp.dot(ovl, mid.astype(MXU_DTYPE), preferred_element_type=F32)
           + jnp.dot(ovl, lo, preferred_element_type=F32))
    n_sel = imp.shape[0]
    jj = lax.broadcasted_iota(jnp.int32, (n_sel, 1), 0)
    blk_q = lax.shift_right_logical(t0 + lax.broadcasted_iota(jnp.int32, (1, tq), 1),
                                    int(math.log2(SEL_BLOCK)))
    forced = (jj == 0) | (jj == blk_q) | (jj == blk_q - 1)
    score = jnp.where(forced, IMP_FORCE, jnp.where(jj <= blk_q, imp, -1.0))
    rank = jnp.zeros_like(score)
    for j in range(n_sel):
        row = score[j:j + 1, :]
        tie = (jj > j).astype(F32)
        rank = rank + jnp.where(row > score, 1.0, jnp.where(row == score, tie, 0.0))
    neg_ref[...] = jnp.where(rank < SEL_TOPK, 0.0, NEG)

    def normalised(acc):
        return acc[:dh] / acc[dh:dh + 1]

    per_step = tk // SEL_BLOCK

    def sel_step(c, carry):
        ms, accs = carry
        start = pl.multiple_of(c * tk, tk)
        k_c = ks_ref[0, 0, 0, pl.ds(start, tk), :]
        v_c = values_t(vs_ref[0, 0, 0, pl.ds(start, tk), :])
        neg = jnp.concatenate([jnp.broadcast_to(neg_ref[pl.ds(per_step * c + r, 1), :], (SEL_BLOCK, tq))
                               for r in range(per_step)], axis=0)
        near = jnp.minimum((t0 - start) // tq, bsel_ref.shape[2] - 1)
        x = [jnp.dot(k_c, qh[h], preferred_element_type=F32) + bsel_ref[0, h, near] + neg for h in heads]
        m_new = [jnp.maximum(ms[h], jnp.max(x[h], axis=0, keepdims=True)) for h in heads]
        acc_new = [jnp.exp2(ms[h] - m_new[h]) * accs[h] + _dot(v_c, jnp.exp2(x[h] - m_new[h])) for h in heads]
        return tuple(m_new), tuple(acc_new)

    init = (tuple(jnp.full((1, tq), NEG, F32) for _ in heads), tuple(jnp.zeros((2 * dh, tq), F32) for _ in heads))
    _, acc_sel = lax.fori_loop(0, (t0 + tq + tk - 1) // tk, sel_step, init)

    w0 = pl.multiple_of(jnp.maximum(t0 - WINDOW, 0), tq)
    wk = WINDOW + tq
    k_w = kw_ref[0, 0, 0, pl.ds(w0, wk), :]
    v_w = values_t(vw_ref[0, 0, 0, pl.ds(w0, wk), :])
    xw = [jnp.dot(k_w, qh[h], preferred_element_type=F32) + bwin_ref[0, h, 0] for h in heads]
    ow = [_dot(v_w, jnp.exp2(x - jnp.max(x, axis=0, keepdims=True))) for x in xw]

    gates = jax.nn.sigmoid(gate_ref[0]).T
    outs = []
    for h in heads:
        g = [gates[N_BRANCH * h + br:N_BRANCH * h + br + 1, :] for br in range(N_BRANCH)]
        outs.append(g[0] * oc[h] + g[1] * normalised(acc_sel[h]) + g[2] * normalised(ow[h]))
    o_ref[0] = jnp.concatenate(outs, axis=0).T.astype(o_ref.dtype)


def _nsa_attention(q, gate_logits, kv, k_cmp, v_cmp, rel_bias):
    b, t, d = q.shape
    tq, tk, dh = NSA_TQ, NSA_TK, NSA_DH
    bias_cmp, bias_win, bias_sel = _nsa_bias_tiles(rel_bias, t)
    ovl = _nsa_constants(t)
    n_cmp = k_cmp.shape[2]
    n_win = bias_win.shape[2]
    part = lambda idx: pl.BlockSpec((1, 1, 1, t, dh), lambda g, bi, i: (idx, bi, g, 0, 0))
    cmp_spec = pl.BlockSpec((1, 1, n_cmp, dh), lambda g, bi, i: (bi, g, 0, 0))
    return pl.pallas_call(
        _nsa_attn_kernel,
        grid=(NSA_GROUPS, b, t // tq),
        in_specs=[pl.BlockSpec((1, tq, NSA_HPG * dh), lambda g, bi, i: (bi, i, g)),
                  pl.BlockSpec((1, tq, GATE_LANES), lambda g, bi, i: (bi, i, g)),
                  cmp_spec, cmp_spec, part(0), part(1), part(2), part(3),
                  pl.BlockSpec((1, NSA_HPG, n_cmp, tq), lambda g, bi, i: (g, 0, 0, i)),
                  pl.BlockSpec((1, NSA_HPG, 1, WINDOW + tq, tq),
                               lambda g, bi, i: (g, 0, jnp.minimum(i, n_win - 1), 0, 0)),
                  pl.BlockSpec((1, NSA_HPG, bias_sel.shape[2], tk, tq), lambda g, bi, i: (g, 0, 0, 0, 0)),
                  _const_spec(ovl.shape)],
        out_specs=pl.BlockSpec((1, tq, NSA_HPG * dh), lambda g, bi, i: (bi, i, g)),
        out_shape=jax.ShapeDtypeStruct((b, t, d), ACT_DTYPE),
        scratch_shapes=[pltpu.VMEM((t // SEL_BLOCK, tq), F32)],
        compiler_params=_params(("arbitrary", "arbitrary", "arbitrary")),
        name="nsa_attention",
    )(q, gate_logits, k_cmp, v_cmp, kv, kv, kv, kv, bias_cmp, bias_win, bias_sel, ovl)


def _nsa_block(h, moe_out, kv_ln, kv_w, cmp_pe_k, cmp_pe_v, cmp_k_w1, cmp_k_w2, cmp_v_w1, cmp_v_w2,
               rel_bias, b_ln, b_wqg, b, t):
    n, d = b * t, kv_ln.shape[0]
    h1, q, gate_logits, kvc, kv = _nsa_proj(h, moe_out, kv_ln, b_ln, kv_w, b_wqg, b, t)
    k_cmp, v_cmp = _compress(kvc, cmp_pe_k, cmp_pe_v, cmp_k_w1, cmp_k_w2, cmp_v_w1, cmp_v_w2)
    o = _nsa_attention(q, gate_logits, kv, k_cmp, v_cmp, rel_bias)
    return o.reshape(n, d), h1.reshape(n, d)


def kernel(x, a_ln, a_mu, a_wr, a_wk, a_wv, a_wo, a_w0, a_w1, a_w2, a_a0, a_a1, a_a2, a_g1, a_g2, a_kk, a_ka, a_rk, a_lnx_w, a_lnx_b, kv_ln, kv_w, cmp_pe_k, cmp_pe_v, cmp_k_w1, cmp_k_w2, cmp_v_w1, cmp_v_w2, rel_bias, b_ln, b_wqg, b_wo, m_ln, m_wg, m_bg, m_we, m_be, m_w1, m_w3, m_w2, final_ln):
    b, t, d = x.shape
    n = b * t
    router = lambda l: _router_weights(m_wg[l], m_bg[l], m_we[l], m_be[l])
    moe = lambda hh, route, counts, l: _moe_layer(hh, route, counts, m_ln[l], m_w1[l], m_w3[l], m_w2[l])
    y = _rwkv_block(x, a_ln[0], a_mu[0], a_wr[0], a_wk[0], a_wv[0], a_w0[0], a_w1[0], a_w2[0],
                    a_a0[0], a_a1[0], a_a2[0], a_g1[0], a_g2[0], a_kk[0], a_ka[0], a_rk[0],
                    a_lnx_w[0], a_lnx_b[0])
    h, route, counts = _matmul_res(y.reshape(n, d), _mx(a_wo[0]), x.reshape(n, d), m_ln[0], router(0))
    o, h1 = _nsa_block(h, moe(h, route, counts, 0), kv_ln, kv_w, cmp_pe_k, cmp_pe_v, cmp_k_w1, cmp_k_w2,
                       cmp_v_w1, cmp_v_w2, rel_bias, b_ln[0], b_wqg[0], b, t)
    h, route, counts = _matmul_res(o, _mx(b_wo[0]), h1, m_ln[1], router(1))
    return _final(h, moe(h, route, counts, 1), final_ln).reshape(b, t, d)
```

```python
import math

import numpy as np
import jax
import jax.numpy as jnp
from jax import lax
from jax.experimental import pallas as pl
from jax.experimental.pallas import tpu as pltpu

F32 = jnp.float32
MXU_DTYPE = jnp.bfloat16
ACT_DTYPE = jnp.bfloat16

D_MODEL = 1024
NORM_EPS = 1e-6
RWKV_HEAD = 64
RWKV_GN_EPS = 64e-5
RWKV_CHUNK = 64
RWKV_SEQS = 4
LANES = 128
MXU_TILE = 256
VMEM_LIMIT = 56 * 1024 * 1024


def _mx(x):
    return x.astype(MXU_DTYPE)


def _dot(a, b):
    return jnp.dot(_mx(a), _mx(b), preferred_element_type=F32)


def _dot_nt(a, b):
    return lax.dot_general(_mx(a), _mx(b), (((1,), (1,)), ((), ())), preferred_element_type=F32)


def _split2(x):
    hi = x.astype(MXU_DTYPE)
    lo = (x - hi.astype(F32)).astype(MXU_DTYPE)
    return hi, lo


def _dot_hl(x, w):
    hi, lo = _split2(x)
    return jnp.dot(hi, w, preferred_element_type=F32) + jnp.dot(lo, w, preferred_element_type=F32)


def _rms(x, g):
    return x * lax.rsqrt(jnp.mean(x * x, axis=-1, keepdims=True) + NORM_EPS) * g


TOK_ROWS = 8


def _tok_load(ref):
    tm = ref.shape[0] // TOK_ROWS
    return jnp.concatenate([ref[pl.ds(s, tm, stride=TOK_ROWS), :] for s in range(TOK_ROWS)], axis=1)


def _tok_store(ref, val):
    tm = val.shape[0]
    for s in range(TOK_ROWS):
        ref[pl.ds(s, tm, stride=TOK_ROWS), :] = val[:, LANES * s:LANES * (s + 1)]


def _tok_spec(tm, index_map):
    return pl.BlockSpec((TOK_ROWS * tm, LANES), index_map)


def _const_spec(shape):
    nd = len(shape)
    return pl.BlockSpec(shape, lambda *_: (0,) * nd)


def _params(sem):
    return pltpu.CompilerParams(dimension_semantics=sem, vmem_limit_bytes=VMEM_LIMIT)


def _rwkv_pre_kernel(x_ref, ln_ref, mu_ref, wr_ref, wk_ref, wv_ref, w1_ref, a1_ref, g1_ref,
                     w2_ref, a2_ref, g2_ref, vec_ref,
                     r_out, k_out, v_out, kk_out, g_out, ld_out, a_out, prev_ref):
    tm = x_ref.shape[1]
    hn = _rms(x_ref[0], ln_ref[...])

    @pl.when(pl.program_id(1) == 0)
    def _():
        prev_ref[...] = jnp.zeros_like(prev_ref)

    rows = lax.broadcasted_iota(jnp.int32, (tm, 1), 0)
    shifted = jnp.where(rows == 0, prev_ref[0:1, :], pltpu.roll(hn, 1, axis=0))
    prev_ref[0:1, :] = hn[tm - 1:tm, :]
    xx = shifted - hn
    xr, xw, xk, xv, xa, xg = (_mx(hn + xx * mu_ref[i:i + 1, :]) for i in range(6))
    w0, a0, kkp, kap = (vec_ref[i:i + 1, :] for i in range(4))
    r = jnp.dot(xr, wr_ref[...], preferred_element_type=F32)
    k = jnp.dot(xk, wk_ref[...], preferred_element_type=F32)
    v = jnp.dot(xv, wv_ref[...], preferred_element_type=F32)
    z = w0 + _dot(jnp.tanh(jnp.dot(xw, w1_ref[...], preferred_element_type=F32)), w2_ref[...])
    a = jax.nn.sigmoid(a0 + _dot(jnp.dot(xa, a1_ref[...], preferred_element_type=F32), a2_ref[...]))
    g = _dot(jax.nn.sigmoid(jnp.dot(xg, g1_ref[...], preferred_element_type=F32)), g2_ref[...])
    ld_out[0] = (-math.exp(-0.5)) * jax.nn.sigmoid(z)
    a_out[0] = a
    r_out[0] = r.astype(r_out.dtype)
    v_out[0] = v.astype(v_out.dtype)
    kk_out[0] = (k * kkp).astype(kk_out.dtype)
    k_out[0] = (k * (1.0 + (a - 1.0) * kap)).astype(k_out.dtype)
    g_out[0] = g.astype(g_out.dtype)


def _rwkv_pre(x, ln, mu, wr, wk, wv, w1, a1, g1, w2, a2, g2, vecs, tm=256):
    b, t, d = x.shape
    act = jax.ShapeDtypeStruct((b, t, d), ACT_DTYPE)
    f32o = jax.ShapeDtypeStruct((b, t, d), F32)
    blk = pl.BlockSpec((1, tm, d), lambda i, j: (i, j, 0))
    consts = [ln, mu, wr, wk, wv, w1, a1, g1, w2, a2, g2, vecs]
    return pl.pallas_call(
        _rwkv_pre_kernel,
        grid=(b, t // tm),
        in_specs=[blk] + [_const_spec(c.shape) for c in consts],
        out_specs=[blk] * 7,
        out_shape=[act, act, act, act, act, f32o, f32o],
        scratch_shapes=[pltpu.VMEM((8, d), F32)],
        compiler_params=_params(("arbitrary", "arbitrary")),
        name="rwkv_pre",
    )(x, *consts)


def _stack4(x):
    return jnp.concatenate([x[:, MXU_TILE * q:MXU_TILE * (q + 1)] for q in range(4)], axis=0)


def _unstack4(y):
    c = y.shape[0] // 4
    return jnp.concatenate([y[c * q:c * (q + 1), :] for q in range(4)], axis=1)


def _rwkv_scan_kernel(r_ref, k_ref, v_ref, kk_ref, g_ref, ld_ref, a_ref,
                      ltri_ref, smask_ref, eye_ref, j_ref, rk_ref, lnw_ref, lnb_ref,
                      y_out, h_ref):
    c = RWKV_CHUNK
    nb = r_ref.shape[0]

    @pl.when(pl.program_id(1) == 0)
    def _():
        h_ref[...] = jnp.zeros_like(h_ref)

    def rows_of(ref):
        return jnp.concatenate([ref[bi].astype(F32) for bi in range(nb)], axis=0)

    ld, r, k, v, kk = (rows_of(x) for x in (ld_ref, r_ref, k_ref, v_ref, kk_ref))
    jones = j_ref[...]
    ld_hi, ld_lo = _split2(ld)
    cum = (jnp.dot(ltri_ref[...], ld_hi, preferred_element_type=F32)
           + jnp.dot(ltri_ref[...], ld_lo, preferred_element_type=F32))
    last = [cum[(bi + 1) * c - 1:(bi + 1) * c, :] for bi in range(nb)]
    cum_last = jnp.concatenate([jnp.broadcast_to(x, (c, x.shape[1])) for x in last], axis=0)
    e_l = jnp.exp(cum)
    e_lm = jnp.exp(cum - ld)
    e_nl = jnp.exp(-cum)
    e_cl = jnp.exp(cum_last - cum)
    e_c = [jnp.exp(x) for x in last]
    ss = _unstack4(_dot_hl(_stack4(kk * kk), jones))
    kkn = kk * lax.rsqrt(jnp.maximum(ss, 1e-24))
    a_vec = -kkn
    b_vec = kkn * rows_of(a_ref)
    rt = r * e_l
    at = a_vec * e_lm
    kt = k * e_nl
    bt = b_vec * e_nl
    bh = b_vec * e_cl
    kh = k * e_cl

    lane = lax.broadcasted_iota(jnp.int32, (1, LANES), 1)
    m_l = (lane < RWKV_HEAD).astype(F32)
    m_r = 1.0 - m_l

    def st(xp):
        return jnp.concatenate([xp * m_l, xp * m_r], axis=0)

    def dup(xp):
        return jnp.concatenate([xp, xp], axis=0)

    smask = smask_ref[...] > 0.0
    eye = eye_ref[...]
    n_pair = D_MODEL // LANES
    pairs = range(nb * n_pair)
    seq_of = [p // n_pair for p in pairs]

    def part(x, p):
        return x[seq_of[p] * c:(seq_of[p] + 1) * c, LANES * (p % n_pair):LANES * (p % n_pair + 1)]

    at_s = [st(part(at, p)) for p in pairs]
    rt_s = [st(part(rt, p)) for p in pairs]
    v_s = [st(part(v, p)) for p in pairs]
    s = [jnp.where(smask,
                   _dot_nt(jnp.concatenate([at_s[p], rt_s[p]], axis=0),
                           jnp.concatenate([dup(part(bt, p)), dup(part(kt, p))], axis=0)),
                   0.0) for p in pairs]
    a_ab = [x[:2 * c, :2 * c] for x in s]
    a_r = [x[2 * c:, :] for x in s]
    g1 = [_dot(s[p][:2 * c, 2 * c:], v_s[p]) for p in pairs]
    t_m = [eye + x for x in a_ab]
    a_pow = [_dot(x, x) for x in a_ab]
    n = 2
    while n < c:
        if 2 * n < c:
            both = [_dot(a_pow[p], jnp.concatenate([a_pow[p], t_m[p]], axis=1)) for p in pairs]
            a_pow = [x[:, :2 * c] for x in both]
            t_m = [t_m[p] + both[p][:, 2 * c:] for p in pairs]
        else:
            t_m = [t_m[p] + _dot(a_pow[p], t_m[p]) for p in pairs]
        n *= 2
    tw = [_dot(t_m[p], jnp.concatenate([at_s[p], g1[p]], axis=1)) for p in pairs]
    zmat = [jnp.concatenate([tw[p], jnp.concatenate([jnp.zeros_like(v_s[p]), v_s[p]], axis=1)], axis=0)
            for p in pairs]
    upper = [_dot(a_r[p], zmat[p]) for p in pairs]
    bk_t = [jnp.concatenate([st(part(bh, p)), st(part(kh, p))], axis=0).T for p in pairs]
    lower = [_dot(bk_t[p], zmat[p]) for p in pairs]
    decay = [eye * e_c[seq_of[p]][:, LANES * (p % n_pair):LANES * (p % n_pair + 1)] for p in pairs]
    seq = [_dot(jnp.concatenate([upper[p][:, :LANES] + rt_s[p], lower[p][:, :LANES] + decay[p]], axis=0), h_ref[p])
           for p in pairs]
    ys = []
    for p in pairs:
        y_st = seq[p][:2 * c, :] + upper[p][:, LANES:]
        h_ref[p] = seq[p][2 * c:, :] + lower[p][:, LANES:]
        ys.append(y_st[:c, :] + y_st[c:, :])
    y = jnp.concatenate([jnp.concatenate(ys[bi * n_pair:(bi + 1) * n_pair], axis=1) for bi in range(nb)],
                        axis=0)

    inv_n = 1.0 / RWKV_HEAD
    y4 = _stack4(y)
    dlt = y4 - _dot_hl(y4, jones) * inv_n
    var = _dot_hl(dlt * dlt, jones) * inv_n
    yn = dlt * lax.rsqrt(var + RWKV_GN_EPS)
    bonus = _dot_hl(_stack4(r * k) * rk_ref[...], jones) * _stack4(v)
    out = _unstack4((yn * lnw_ref[...] + lnb_ref[...] + bonus) * _stack4(rows_of(g_ref)))
    for bi in range(nb):
        y_out[bi] = out[bi * c:(bi + 1) * c, :].astype(y_out.dtype)


def _scan_constants():
    c = RWKV_CHUNK
    ltri = np.kron(np.eye(RWKV_SEQS), np.tril(np.ones((c, c), np.float32)))
    rho = np.arange(4 * c)
    r_type, r_head, r_t = rho // (2 * c), (rho % (2 * c)) // c, rho % c
    same = r_head[:, None] == r_head[None, :]
    strict = r_t[None, :] < r_t[:, None]
    incl = r_t[None, :] <= r_t[:, None]
    smask = same & np.where(r_type[:, None] == 0, strict, incl)
    eye = np.eye(LANES, dtype=np.float32)
    head = np.arange(MXU_TILE) // RWKV_HEAD
    jones = (head[:, None] == head[None, :]).astype(np.float32)
    return (jnp.asarray(ltri, MXU_DTYPE), jnp.asarray(smask, F32), jnp.asarray(eye, F32),
            jnp.asarray(jones, MXU_DTYPE))


def _stack_param(p):
    return jnp.repeat(p.reshape(4, MXU_TILE).astype(F32), RWKV_SEQS * RWKV_CHUNK, axis=0)


def _rwkv_scan(r, k, v, kk, g, ld, a, rk, lnw, lnb):
    b, t, d = r.shape
    c, nb = RWKV_CHUNK, RWKV_SEQS
    assert b % nb == 0 and t % c == 0 and d == D_MODEL
    blk = pl.BlockSpec((nb, c, d), lambda i, j: (i, j, 0))
    consts = list(_scan_constants()) + [_stack_param(rk.reshape(-1)), _stack_param(lnw), _stack_param(lnb)]
    return pl.pallas_call(
        _rwkv_scan_kernel,
        grid=(b // nb, t // c),
        in_specs=[blk] * 7 + [_const_spec(x.shape) for x in consts],
        out_specs=blk,
        out_shape=jax.ShapeDtypeStruct((b, t, d), ACT_DTYPE),
        scratch_shapes=[pltpu.VMEM((nb * (d // LANES), LANES, LANES), F32)],
        compiler_params=_params(("arbitrary", "arbitrary")),
        name="rwkv_scan",
    )(r, k, v, kk, g, ld, a, *consts)


def _matmul_res_kernel(a_ref, w_ref, res_ref, ln_ref, w_hi_ref, w_lo_ref, bias_ref, tri_ref,
                       o_ref, route_out, count_out, cnt_ref):
    h = res_ref[...] + jnp.dot(a_ref[...], w_ref[...], preferred_element_type=F32)
    _tok_store(o_ref, h)
    _route_tokens(h, ln_ref, w_hi_ref, w_lo_ref, bias_ref, tri_ref, route_out, count_out, cnt_ref)


def _matmul_res(a, w, res, ln, router, tm=512):
    n, kdim = a.shape
    d = w.shape[1]
    tri = jnp.asarray(np.tril(np.ones((tm, tm), np.float32), -1), MXU_DTYPE)
    consts = [ln.reshape(1, d), *router, tri]
    return pl.pallas_call(
        _matmul_res_kernel,
        grid=(n // tm,),
        in_specs=[pl.BlockSpec((tm, kdim), lambda i: (i, 0)), _const_spec(w.shape),
                  pl.BlockSpec((tm, d), lambda i: (i, 0))] + [_const_spec(c.shape) for c in consts],
        out_specs=[_tok_spec(tm, lambda i: (i, 0)), pl.BlockSpec((tm, LANES), lambda i: (i, 0)),
                   _const_spec((1, LANES))],
        out_shape=[jax.ShapeDtypeStruct((TOK_ROWS * n, LANES), F32), jax.ShapeDtypeStruct((n, LANES), F32),
                   jax.ShapeDtypeStruct((1, LANES), F32)],
        scratch_shapes=[pltpu.VMEM((1, LANES), F32)],
        compiler_params=_params(("arbitrary",)),
        name="matmul_res_route",
    )(a, w, res, *consts)


def _rwkv_block(x, a_ln, a_mu, a_wr, a_wk, a_wv, a_w0, a_w1, a_w2, a_a0, a_a1, a_a2,
                a_g1, a_g2, a_kk, a_ka, a_rk, a_lnx_w, a_lnx_b):
    b, t, d = x.shape
    vecs = jnp.stack([a_w0, a_a0, a_kk, a_ka]).astype(F32)
    r, k, v, kk, g, ld, a = _rwkv_pre(
        x, a_ln.reshape(1, d), a_mu, _mx(a_wr), _mx(a_wk), _mx(a_wv), _mx(a_w1), _mx(a_a1), _mx(a_g1),
        _mx(a_w2), _mx(a_a2), _mx(a_g2), vecs)
    return _rwkv_scan(r, k, v, kk, g, ld, a, a_rk, a_lnx_w, a_lnx_b)


MOE_GROUPS = 4
MOE_EPG = 8
MOE_EXPERTS = MOE_GROUPS * MOE_EPG
MOE_TOPK = 2
MOE_BLOCK = 512
NEG = -1e30


def _route(logits):
    lane = lax.broadcasted_iota(jnp.int32, logits.shape, 1)
    is_grp = lane < MOE_GROUPS
    gl = jnp.where(is_grp, logits, NEG)
    ge = jnp.exp(gl - jnp.max(gl, axis=-1, keepdims=True))
    gp = ge / jnp.sum(ge, axis=-1, keepdims=True)
    grp_w = jnp.max(gp, axis=-1, keepdims=True)
    grp = jnp.min(jnp.where(is_grp & (gp == grp_w), lane, LANES), axis=-1, keepdims=True)
    lo = MOE_GROUPS + MOE_EPG * grp
    in_grp = (lane >= lo) & (lane < lo + MOE_EPG)
    el = jnp.where(in_grp, logits, NEG)
    ee = jnp.exp(el - jnp.max(el, axis=-1, keepdims=True))
    ep = ee / jnp.sum(ee, axis=-1, keepdims=True)
    p1 = jnp.max(jnp.where(in_grp, ep, -1.0), axis=-1, keepdims=True)
    i1 = jnp.min(jnp.where(in_grp & (ep == p1), lane, LANES), axis=-1, keepdims=True)
    rest = in_grp & (lane != i1)
    p2 = jnp.max(jnp.where(rest, ep, -1.0), axis=-1, keepdims=True)
    i2 = jnp.min(jnp.where(rest & (ep == p2), lane, LANES), axis=-1, keepdims=True)
    scale = grp_w / (p1 + p2)
    return i1 - MOE_GROUPS, i2 - MOE_GROUPS, p1 * scale, p2 * scale


def _router_logits(xf, w_hi_ref, w_lo_ref, bias_ref):
    hi, lo = _split2(xf)
    logits = (jnp.dot(hi, w_hi_ref[...], preferred_element_type=F32)
              + jnp.dot(lo, w_hi_ref[...], preferred_element_type=F32)
              + jnp.dot(hi, w_lo_ref[...], preferred_element_type=F32))
    return logits + bias_ref[...]


def _route_tokens(h, ln_ref, w_hi_ref, w_lo_ref, bias_ref, tri_ref, route_out, count_out, cnt_ref):
    @pl.when(pl.program_id(0) == 0)
    def _():
        cnt_ref[...] = jnp.zeros_like(cnt_ref)

    xf = _rms(h, ln_ref[...])
    e1, e2, w1, w2 = _route(_router_logits(xf, w_hi_ref, w_lo_ref, bias_ref))
    lane = lax.broadcasted_iota(jnp.int32, (xf.shape[0], LANES), 1)
    hit1, hit2 = lane == e1, lane == e2
    onehot = jnp.where(hit1, 1.0, jnp.where(hit2, 1.0, 0.0))
    before = jnp.dot(tri_ref[...], _mx(onehot), preferred_element_type=F32) + cnt_ref[...]
    r1 = jnp.sum(jnp.where(hit1, before, 0.0), axis=-1, keepdims=True)
    r2 = jnp.sum(jnp.where(hit2, before, 0.0), axis=-1, keepdims=True)
    cnt_ref[...] = cnt_ref[...] + jnp.sum(onehot, axis=0, keepdims=True)
    count_out[...] = cnt_ref[...]
    fields = (e1.astype(F32), e2.astype(F32), w1, w2, r1, r2)
    rec = jnp.zeros(lane.shape, F32)
    for idx, val in enumerate(fields):
        rec = jnp.where(lane == idx, val, rec)
    route_out[...] = rec


def _router_weights(w_gr, b_gr, w_er, b_er):
    d = w_gr.shape[0]
    w = jnp.zeros((d, LANES), F32).at[:, :MOE_GROUPS].set(w_gr).at[:, MOE_GROUPS:MOE_GROUPS + MOE_EXPERTS].set(w_er)
    bias = jnp.zeros((1, LANES), F32).at[0, :MOE_GROUPS].set(b_gr).at[0, MOE_GROUPS:MOE_GROUPS + MOE_EXPERTS].set(b_er)
    w_hi = w.astype(MXU_DTYPE)
    w_lo = (w - w_hi.astype(F32)).astype(MXU_DTYPE)
    return w_hi, w_lo, bias


def _moe_plan(route, counts, blk):
    n = route.shape[0]
    counts = counts[0, :MOE_EXPERTS].astype(jnp.int32)
    pcounts = (counts + blk - 1) // blk * blk
    pend = jnp.cumsum(pcounts)
    pstart = (pend - pcounts).astype(jnp.int32)
    n_blk = n * MOE_TOPK // blk + MOE_EXPERTS
    blk_start = jnp.arange(n_blk, dtype=jnp.int32) * blk
    blk_e = jnp.sum((blk_start[:, None] >= pend[None, :]).astype(jnp.int32), axis=1)
    blk_e = jnp.minimum(blk_e, MOE_EXPERTS - 1)
    n_valid = jnp.clip((pstart + counts)[blk_e] - blk_start, 0, blk).astype(jnp.int32)
    n_act = (pend[-1] // blk).astype(jnp.int32).reshape(1)
    expert = route[:, 0:MOE_TOPK].astype(jnp.int32)
    onehot = expert[:, :, None] == jnp.arange(MOE_EXPERTS, dtype=jnp.int32)
    seg_start = jnp.sum(jnp.where(onehot, pstart, 0), axis=-1)
    sorted_pos = seg_start + route[:, 4:4 + MOE_TOPK].astype(jnp.int32)
    return blk_e, n_valid, n_act, sorted_pos


def _token_copies(pos_ref, tm, make_copy):
    group = 8

    def body(g, carry):
        t0 = g * group
        pos = [[pos_ref[0, 0, MOE_TOPK * (t0 + j) + k] for k in range(MOE_TOPK)] for j in range(group)]
        for j in range(group):
            for k in range(MOE_TOPK):
                make_copy(t0 + j, k, pos[j][k]).start()
        return carry

    lax.fori_loop(0, tm // group, body, 0)


def _tile(ref, tok):
    return ref.at[pl.ds(pl.multiple_of(tok * TOK_ROWS, TOK_ROWS), TOK_ROWS)]


def _moe_dispatch_kernel(n_valid_ref, pos_ref, src_ref, dst_hbm, stage, sem, zbuf, zsem):
    tm = src_ref.shape[0] // TOK_ROWS
    step = pl.program_id(0)
    slot = lax.rem(step, 2)

    def drain(s):
        for k in range(MOE_TOPK):
            pltpu.make_async_copy(stage.at[s], dst_hbm.at[pl.ds(0, stage.shape[1])], sem.at[s]).wait()

    @pl.when(pl.program_id(0) == 0)
    def _():
        rows = zbuf.shape[0]
        zbuf[...] = jnp.zeros_like(zbuf)

        def fill(j, wait):
            @pl.when(n_valid_ref[j] < rows // TOK_ROWS)
            def _():
                start = 0 if wait else pl.multiple_of(j * rows, rows)
                cp = pltpu.make_async_copy(zbuf, dst_hbm.at[pl.ds(start, rows)], zsem)
                cp.wait() if wait else cp.start()

        lax.fori_loop(0, n_valid_ref.shape[0], lambda j, c: (fill(j, False), c)[1], 0)
        lax.fori_loop(0, n_valid_ref.shape[0], lambda j, c: (fill(j, True), c)[1], 0)

    @pl.when(step >= 2)
    def _():
        drain(slot)

    stage[slot] = src_ref[...]
    _token_copies(pos_ref, tm, lambda t, k, pos: pltpu.make_async_copy(
        _tile(stage.at[slot], t), _tile(dst_hbm, pos), sem.at[slot]))

    @pl.when(step == pl.num_programs(0) - 1)
    def _():
        @pl.when(step >= 1)
        def _():
            drain(1 - slot)
        drain(slot)


def _moe_gather_combine(step, n_steps, pos_ref, pos_next_ref, ys_hbm, h_ref, route, gbuf, gsem):
    tm = gbuf.shape[2] // TOK_ROWS
    slot = lax.rem(step, 2)

    def start(p_ref, s):
        _token_copies(p_ref, tm, lambda t, k, pos: pltpu.make_async_copy(
            _tile(ys_hbm, pos), _tile(gbuf.at[s, k], t), gsem.at[s]))

    @pl.when(step == 0)
    def _():
        start(pos_ref, 0)

    @pl.when(step + 1 < n_steps)
    def _():
        start(pos_next_ref, 1 - slot)

    for k in range(MOE_TOPK):
        pltpu.make_async_copy(ys_hbm.at[pl.ds(0, gbuf.shape[2])], gbuf.at[slot, k], gsem.at[slot]).wait()
    return (_tok_load(h_ref) + route[:, 2:3] * _tok_load(gbuf.at[slot, 0])
            + route[:, 3:4] * _tok_load(gbuf.at[slot, 1]))


def _gather_specs(sorted_pos, tm, step_index):
    n = sorted_pos.shape[0]
    pos = sorted_pos.reshape(n // tm, 1, MOE_TOPK * tm)
    last = n // tm - 1
    cur = pl.BlockSpec((1, 1, MOE_TOPK * tm), lambda *g: (step_index(*g), 0, 0), memory_space=pltpu.SMEM)
    nxt = pl.BlockSpec((1, 1, MOE_TOPK * tm), lambda *g: (jnp.minimum(step_index(*g) + 1, last), 0, 0),
                       memory_space=pltpu.SMEM)
    scratch = [pltpu.VMEM((2, MOE_TOPK, TOK_ROWS * tm, LANES), F32), pltpu.SemaphoreType.DMA((2,))]
    return pos, [cur, nxt, pl.BlockSpec(memory_space=pl.ANY)], scratch


def _moe_dispatch(h, n_valid, sorted_pos, p, blk, tm=512):
    n = sorted_pos.shape[0]
    pos = sorted_pos.reshape(n // tm, 1, MOE_TOPK * tm)
    grid_spec = pltpu.PrefetchScalarGridSpec(
        num_scalar_prefetch=1,
        grid=(n // tm,),
        in_specs=[pl.BlockSpec((1, 1, MOE_TOPK * tm), lambda i, nv: (i, 0, 0), memory_space=pltpu.SMEM),
                  _tok_spec(tm, lambda i, nv: (i, 0))],
        out_specs=pl.BlockSpec(memory_space=pl.ANY),
        scratch_shapes=[pltpu.VMEM((2, TOK_ROWS * tm, LANES), h.dtype), pltpu.SemaphoreType.DMA((2,)),
                        pltpu.VMEM((TOK_ROWS * blk, LANES), h.dtype), pltpu.SemaphoreType.DMA],
    )
    return pl.pallas_call(
        _moe_dispatch_kernel,
        grid_spec=grid_spec,
        out_shape=jax.ShapeDtypeStruct((TOK_ROWS * p, LANES), h.dtype),
        compiler_params=_params(("arbitrary",)),
        name="moe_dispatch",
    )(n_valid, pos, h)


def _moe_expert_kernel(blk_e_ref, n_act_ref, x_ref, ln_ref, w1_ref, w3_ref, w2_ref, y_ref):
    del blk_e_ref
    i = pl.program_id(0)

    @pl.when(i < n_act_ref[0])
    def _():
        xn = _mx(_rms(_tok_load(x_ref), ln_ref[...]))
        h1 = jnp.dot(xn, w1_ref[0], preferred_element_type=F32)
        h3 = jnp.dot(xn, w3_ref[0], preferred_element_type=F32)
        _tok_store(y_ref, _dot(h1 * jax.nn.sigmoid(h1) * h3, w2_ref[0]))

    @pl.when(i >= n_act_ref[0])
    def _():
        y_ref[...] = jnp.zeros_like(y_ref)


def _moe_experts(xs, ln, w1, w3, w2, blk_e, n_act, blk):
    p = xs.shape[0] // TOK_ROWS
    d, hdim = w1.shape[1], w1.shape[2]
    grid_spec = pltpu.PrefetchScalarGridSpec(
        num_scalar_prefetch=2,
        grid=(p // blk,),
        in_specs=[_tok_spec(blk, lambda i, be, na: (i, 0)),
                  pl.BlockSpec((1, d), lambda i, be, na: (0, 0)),
                  pl.BlockSpec((1, d, hdim), lambda i, be, na: (be[i], 0, 0)),
                  pl.BlockSpec((1, d, hdim), lambda i, be, na: (be[i], 0, 0)),
                  pl.BlockSpec((1, hdim, d), lambda i, be, na: (be[i], 0, 0))],
        out_specs=_tok_spec(blk, lambda i, be, na: (i, 0)),
    )
    return pl.pallas_call(
        _moe_expert_kernel,
        grid_spec=grid_spec,
        out_shape=jax.ShapeDtypeStruct(xs.shape, F32),
        compiler_params=_params(("arbitrary",)),
        name="moe_experts",
    )(blk_e, n_act, xs, ln.reshape(1, d), w1, w3, w2)


def _moe_layer(h, route, counts, ln, w1, w3, w2):
    n = h.shape[0] // TOK_ROWS
    blk = MOE_BLOCK
    blk_e, n_valid, n_act, sorted_pos = _moe_plan(route, counts, blk)
    p = n * MOE_TOPK + MOE_EXPERTS * blk
    xs = _moe_dispatch(h, n_valid, sorted_pos, p, blk)
    ys = _moe_experts(xs, ln, _mx(w1), _mx(w3), _mx(w2), blk_e, n_act, blk)
    return ys, sorted_pos, route


def _final_kernel(pos_ref, pos_next_ref, ys_hbm, h_ref, route_ref, ln_ref, o_ref, gbuf, gsem):
    h = _moe_gather_combine(pl.program_id(0), pl.num_programs(0), pos_ref, pos_next_ref, ys_hbm, h_ref,
                            route_ref[...], gbuf, gsem)
    o_ref[...] = _rms(h, ln_ref[...])


def _final(h, moe_out, ln, tm=512):
    ys, sorted_pos, route = moe_out
    n, d = route.shape[0], ln.shape[0]
    pos, gather_specs, scratch = _gather_specs(sorted_pos, tm, lambda i: i)
    return pl.pallas_call(
        _final_kernel,
        grid=(n // tm,),
        in_specs=gather_specs + [_tok_spec(tm, lambda i: (i, 0)), pl.BlockSpec((tm, LANES), lambda i: (i, 0)),
                                 _const_spec((1, d))],
        out_specs=pl.BlockSpec((tm, d), lambda i: (i, 0)),
        out_shape=jax.ShapeDtypeStruct((n, d), F32),
        scratch_shapes=scratch,
        compiler_params=_params(("arbitrary",)),
        name="final_norm",
    )(pos, pos, ys, h, route, ln.reshape(1, d))


NSA_HEADS = 16
NSA_GROUPS = 4
NSA_HPG = NSA_HEADS // NSA_GROUPS
NSA_DH = D_MODEL // NSA_HEADS
NSA_PARTS = 6
CMP_BLOCK = 32
CMP_STRIDE = 16
SEL_BLOCK = 64
SEL_TOPK = 8
WINDOW = 512
N_BRANCH = 3
IMP_FORCE = 1e4
REL_BUCKETS = 32
REL_MAX_DIST = 128
LOG2E = math.log2(math.e)
NSA_TQ = 256
NSA_TK = 512
GATE_LANES = LANES
CMP_PARTS = 2


def _nsa_proj_kernel(pos_ref, pos_next_ref, ys_hbm, h_ref, route_ref, kvln_ref, qln_ref, kvw_ref, qw_ref,
                     h_out, q_out, gate_out, kvc_out, kv_out, gbuf, gsem):
    step = pl.program_id(0) * pl.num_programs(1) + pl.program_id(1)
    h1 = _moe_gather_combine(step, pl.num_programs(0) * pl.num_programs(1), pos_ref, pos_next_ref, ys_hbm, h_ref,
                             route_ref[0], gbuf, gsem)
    h_out[0] = h1
    kv = _dot(_rms(h1, kvln_ref[...]), kvw_ref[...])
    for part in range(NSA_PARTS):
        for g in range(NSA_GROUPS):
            col = (part * NSA_GROUPS + g) * NSA_DH
            piece = kv[:, col:col + NSA_DH]
            if part < CMP_PARTS:
                kvc_out[part, 0, g] = piece
            else:
                kv_out[part - CMP_PARTS, 0, g] = piece.astype(kv_out.dtype)
    qg = _dot(_rms(h1, qln_ref[...]), qw_ref[...])
    q_out[0] = (qg[:, :D_MODEL] * (NSA_DH ** -0.5 * LOG2E)).astype(q_out.dtype)
    gate_out[0] = qg[:, D_MODEL:]


def _gate_padded_weight(q_w):
    d = q_w.shape[0]
    n_gate = NSA_HPG * N_BRANCH
    gates = q_w[:, d:].reshape(d, NSA_GROUPS, n_gate)
    gates = jnp.pad(gates, ((0, 0), (0, 0), (0, GATE_LANES - n_gate))).reshape(d, NSA_GROUPS * GATE_LANES)
    return jnp.concatenate([q_w[:, :d], gates], axis=1)


def _nsa_proj(h, moe_out, kv_ln, q_ln, kv_w, q_w, b, t, tm=256):
    ys, sorted_pos, route = moe_out
    d = kv_ln.shape[0]
    blk = lambda w: pl.BlockSpec((1, tm, w), lambda i, j: (i, j, 0))
    per_b = t // tm
    tok_block = lambda i, j: i * per_b + j
    pos, gather_specs, scratch = _gather_specs(sorted_pos, tm, tok_block)
    gw = NSA_GROUPS * GATE_LANES
    qw = _gate_padded_weight(q_w)
    kv_spec = lambda parts: pl.BlockSpec((parts, 1, NSA_GROUPS, tm, NSA_DH), lambda i, j: (0, i, 0, j, 0))
    kv_shape = lambda parts: (parts, b, NSA_GROUPS, t, NSA_DH)
    return pl.pallas_call(
        _nsa_proj_kernel,
        grid=(b, per_b),
        in_specs=gather_specs + [_tok_spec(tm, lambda i, j: (tok_block(i, j), 0)), blk(LANES),
                                 _const_spec((1, d)), _const_spec((1, d)),
                                 _const_spec(kv_w.shape), _const_spec(qw.shape)],
        out_specs=[blk(d), blk(d), blk(gw), kv_spec(CMP_PARTS), kv_spec(NSA_PARTS - CMP_PARTS)],
        out_shape=[jax.ShapeDtypeStruct((b, t, d), F32), jax.ShapeDtypeStruct((b, t, d), ACT_DTYPE),
                   jax.ShapeDtypeStruct((b, t, gw), F32), jax.ShapeDtypeStruct(kv_shape(CMP_PARTS), F32),
                   jax.ShapeDtypeStruct(kv_shape(NSA_PARTS - CMP_PARTS), ACT_DTYPE)],
        scratch_shapes=scratch,
        compiler_params=_params(("arbitrary", "arbitrary")),
        name="nsa_proj",
    )(pos, pos, ys, h, route.reshape(b, t, LANES),
      kv_ln.reshape(1, d), q_ln.reshape(1, d), _mx(kv_w), _mx(qw))


def _gelu_tanh(x):
    return 0.5 * x * (1.0 + jnp.tanh(math.sqrt(2.0 / math.pi) * (x + 0.044715 * (x * x * x))))


def _compress_kernel(k_ref, v_ref, pek_ref, pev_ref, kw1_ref, kw2_ref, vw1_ref, vw2_ref, kc_out, vc_out):
    dh = k_ref.shape[-1]
    n_chunk = k_ref.shape[3] // CMP_STRIDE

    def one(z_ref, pe_ref, w1_ref, w2_ref, out):
        first = jnp.zeros((n_chunk, w1_ref.shape[1]), F32)
        second = first
        for l in range(CMP_STRIDE):
            z = z_ref[0, 0, 0, pl.ds(l, n_chunk, stride=CMP_STRIDE), :]
            lo, hi = l, CMP_STRIDE + l
            first = first + _dot(z + pe_ref[0, lo:lo + 1, :], w1_ref[dh * lo:dh * (lo + 1), :])
            second = second + _dot(z + pe_ref[0, hi:hi + 1, :], w1_ref[dh * hi:dh * (hi + 1), :])
        hid = first + pltpu.roll(second, n_chunk - 1, axis=0)
        out[0, 0] = _dot(_gelu_tanh(hid), w2_ref[...]).astype(out.dtype)

    one(k_ref, pek_ref, kw1_ref, kw2_ref, kc_out)
    one(v_ref, pev_ref, vw1_ref, vw2_ref, vc_out)


def _compress(kvc, pe_k, pe_v, k_w1, k_w2, v_w1, v_w2):
    _, b, g, t, dh = kvc.shape
    n_chunk = t // CMP_STRIDE
    part = lambda idx: pl.BlockSpec((1, 1, 1, t, dh), lambda i, j: (idx, i, j, 0, 0))
    pe_spec = pl.BlockSpec((1, CMP_BLOCK, dh), lambda i, j: (j, 0, 0))
    out_spec = pl.BlockSpec((1, 1, n_chunk, dh), lambda i, j: (i, j, 0, 0))
    out = jax.ShapeDtypeStruct((b, g, n_chunk, dh), ACT_DTYPE)
    return pl.pallas_call(
        _compress_kernel,
        grid=(b, g),
        in_specs=[part(0), part(1), pe_spec, pe_spec, _const_spec(k_w1.shape), _const_spec(k_w2.shape),
                  _const_spec(v_w1.shape), _const_spec(v_w2.shape)],
        out_specs=[out_spec, out_spec],
        out_shape=[out, out],
        compiler_params=_params(("arbitrary", "arbitrary")),
        name="nsa_compress",
    )(kvc, kvc, pe_k.astype(F32), pe_v.astype(F32), _mx(k_w1), _mx(k_w2), _mx(v_w1), _mx(v_w2))


def _rel_bucket(dist):
    n = jnp.maximum(dist, 0)
    max_exact = REL_BUCKETS // 2
    nf = jnp.maximum(n, 1).astype(F32)
    large = max_exact + (jnp.log(nf / max_exact) / math.log(REL_MAX_DIST / max_exact)
                         * (REL_BUCKETS - max_exact)).astype(jnp.int32)
    return jnp.where(n < max_exact, n, jnp.minimum(large, REL_BUCKETS - 1))


def _toeplitz(vals, rows, cols):
    span = rows + cols
    z = jnp.concatenate([vals, vals[..., :1]], axis=-1)
    flat = jnp.tile(z, rows)[..., :rows * (span - 1)]
    skew = flat.reshape(vals.shape[:-1] + (rows, span - 1))
    return skew[..., rows - 1:rows - 1 + cols]


def _nsa_bias_tiles(rel_bias, t):
    tq, tk = NSA_TQ, NSA_TK
    table = rel_bias.astype(F32).reshape(REL_BUCKETS, NSA_GROUPS, NSA_HPG).transpose(1, 2, 0)
    by_dist = table[:, :, _rel_bucket(jnp.arange(REL_MAX_DIST + 1))]
    far = by_dist[:, :, REL_MAX_DIST:]

    def band(dist, valid, shift=None):
        b = by_dist[:, :, np.clip(dist, 0, REL_MAX_DIST)]
        if shift is not None:
            b = b - shift
        return jnp.where(jnp.asarray(valid), b * LOG2E, NEG)

    u = lambda rows, cols: np.arange(rows + cols - 1) - (rows - 1)
    wk = WINDOW + tq
    dist = np.stack([v * tq - max(v * tq - WINDOW, 0) - u(tq, wk) for v in range(WINDOW // tq + 1)])
    bias_win = _toeplitz(band(dist, (dist >= 0) & (dist < WINDOW)), tq, wk).swapaxes(-1, -2)
    n_near = (REL_MAX_DIST + tk - 1) // tq + 1
    dist = np.stack([e * tq - u(tq, tk) for e in range(n_near)])
    near = _toeplitz(band(dist, dist >= 0, far[:, :, None]), tq, tk).swapaxes(-1, -2)
    bias_sel = jnp.concatenate([near, jnp.zeros_like(near[:, :, :1])], axis=2)
    n_cmp = t // CMP_STRIDE
    dist = np.stack([b - (CMP_BLOCK - 1) + CMP_STRIDE * u(n_cmp, n_cmp) for b in range(CMP_STRIDE)])
    per_b = _toeplitz(band(dist, dist >= 0), n_cmp, n_cmp)
    bias_cmp = jnp.moveaxis(per_b, 2, 4).reshape(NSA_GROUPS, NSA_HPG, n_cmp, t)
    real = np.arange(n_cmp) < (t - CMP_BLOCK) // CMP_STRIDE + 1
    bias_cmp = jnp.where(jnp.asarray(real)[:, None], bias_cmp, NEG)
    return bias_cmp, bias_win, bias_sel


def _nsa_constants(t):
    n_sel = t // SEL_BLOCK
    n_cmp = t // CMP_STRIDE
    cmp_start = np.arange(n_cmp) * CMP_STRIDE
    sel_start = np.arange(n_sel) * SEL_BLOCK
    overlap = ((cmp_start[:, None] < sel_start[None, :] + SEL_BLOCK)
               & (cmp_start[:, None] + CMP_BLOCK > sel_start[None, :]))
    overlap[(t - CMP_BLOCK) // CMP_STRIDE + 1:, :] = False
    return jnp.asarray(overlap.T, MXU_DTYPE)


def _nsa_attn_kernel(q_ref, gate_ref, kc_ref, vc_ref, ks_ref, vs_ref, kw_ref, vw_ref,
                     bcmp_ref, bwin_ref, bsel_ref, ovl_ref, o_ref, neg_ref):
    tq, tk, dh = NSA_TQ, NSA_TK, NSA_DH
    i = pl.program_id(2)
    t0 = i * tq
    heads = range(NSA_HPG)
    q_t = q_ref[0].astype(F32).T
    qh = [_mx(q_t[dh * h:dh * (h + 1), :]) for h in heads]

    def values_t(v):
        ones_col = (lax.broadcasted_iota(jnp.int32, v.shape, 1) == 0).astype(F32)
        return _mx(jnp.concatenate([v.astype(F32), ones_col], axis=1).T)

    k_cmp = kc_ref[0, 0]
    lc = [jnp.dot(k_cmp, qh[h], preferred_element_type=F32) + bcmp_ref[0, h] for h in heads]
    mc = [jnp.max(x, axis=0, keepdims=True) for x in lc]
    pcu = [jnp.where(mc[h] > 0.5 * NEG, jnp.exp2(lc[h] - mc[h]), 0.0) for h in heads]
    pc = [x / jnp.maximum(jnp.sum(x, axis=0, keepdims=True), 1e-30) for x in pcu]
    v_cmp_t = values_t(vc_ref[0, 0])[:dh]
    oc = [_dot(v_cmp_t, x) for x in pc]

    pcsum = (pc[0] + pc[1]) + (pc[2] + pc[3])
    hi = pcsum.astype(MXU_DTYPE)
    mid = (pcsum - hi.astype(F32))
    lo = (mid - mid.astype(MXU_DTYPE).astype(F32)).astype(MXU_DTYPE)
    ovl = ovl_ref[...]
    imp = (jnp.dot(ovl, hi, preferred_element_type=F32) + jnp.dot(ovl, mid.astype(MXU_DTYPE), preferred_element_type=F32)
           + jnp.dot(ovl, lo, preferred_element_type=F32))
    n_sel = imp.shape[0]
    jj = lax.broadcasted_iota(jnp.int32, (n_sel, 1), 0)
    blk_q = lax.shift_right_logical(t0 + lax.broadcasted_iota(jnp.int32, (1, tq), 1),
                                    int(math.log2(SEL_BLOCK)))
    forced = (jj == 0) | (jj == blk_q) | (jj == blk_q - 1)
    score = jnp.where(forced, IMP_FORCE, jnp.where(jj <= blk_q, imp, -1.0))
    rank = jnp.zeros_like(score)
    for j in range(n_sel):
        row = score[j:j + 1, :]
        tie = (jj > j).astype(F32)
        rank = rank + jnp.where(row > score, 1.0, jnp.where(row == score, tie, 0.0))
    neg_ref[...] = jnp.where(rank < SEL_TOPK, 0.0, NEG)

    def normalised(acc):
        return acc[:dh] / acc[dh:dh + 1]

    per_step = tk // SEL_BLOCK

    def sel_step(c, carry):
        ms, accs = carry
        start = pl.multiple_of(c * tk, tk)
        k_c = ks_ref[0, 0, 0, pl.ds(start, tk), :]
        v_c = values_t(vs_ref[0, 0, 0, pl.ds(start, tk), :])
        neg = jnp.concatenate([jnp.broadcast_to(neg_ref[pl.ds(per_step * c + r, 1), :], (SEL_BLOCK, tq))
                               for r in range(per_step)], axis=0)
        near = jnp.minimum((t0 - start) // tq, bsel_ref.shape[2] - 1)
        x = [jnp.dot(k_c, qh[h], preferred_element_type=F32) + bsel_ref[0, h, near] + neg for h in heads]
        m_new = [jnp.maximum(ms[h], jnp.max(x[h], axis=0, keepdims=True)) for h in heads]
        acc_new = [jnp.exp2(ms[h] - m_new[h]) * accs[h] + _dot(v_c, jnp.exp2(x[h] - m_new[h])) for h in heads]
        return tuple(m_new), tuple(acc_new)

    init = (tuple(jnp.full((1, tq), NEG, F32) for _ in heads), tuple(jnp.zeros((2 * dh, tq), F32) for _ in heads))
    _, acc_sel = lax.fori_loop(0, (t0 + tq + tk - 1) // tk, sel_step, init)

    w0 = pl.multiple_of(jnp.maximum(t0 - WINDOW, 0), tq)
    wk = WINDOW + tq
    k_w = kw_ref[0, 0, 0, pl.ds(w0, wk), :]
    v_w = values_t(vw_ref[0, 0, 0, pl.ds(w0, wk), :])
    xw = [jnp.dot(k_w, qh[h], preferred_element_type=F32) + bwin_ref[0, h, 0] for h in heads]
    ow = [_dot(v_w, jnp.exp2(x - jnp.max(x, axis=0, keepdims=True))) for x in xw]

    gates = jax.nn.sigmoid(gate_ref[0]).T
    outs = []
    for h in heads:
        g = [gates[N_BRANCH * h + br:N_BRANCH * h + br + 1, :] for br in range(N_BRANCH)]
        outs.append(g[0] * oc[h] + g[1] * normalised(acc_sel[h]) + g[2] * normalised(ow[h]))
    o_ref[0] = jnp.concatenate(outs, axis=0).T.astype(o_ref.dtype)


def _nsa_attention(q, gate_logits, kv, k_cmp, v_cmp, rel_bias):
    b, t, d = q.shape
    tq, tk, dh = NSA_TQ, NSA_TK, NSA_DH
    bias_cmp, bias_win, bias_sel = _nsa_bias_tiles(rel_bias, t)
    ovl = _nsa_constants(t)
    n_cmp = k_cmp.shape[2]
    n_win = bias_win.shape[2]
    part = lambda idx: pl.BlockSpec((1, 1, 1, t, dh), lambda g, bi, i: (idx, bi, g, 0, 0))
    cmp_spec = pl.BlockSpec((1, 1, n_cmp, dh), lambda g, bi, i: (bi, g, 0, 0))
    return pl.pallas_call(
        _nsa_attn_kernel,
        grid=(NSA_GROUPS, b, t // tq),
        in_specs=[pl.BlockSpec((1, tq, NSA_HPG * dh), lambda g, bi, i: (bi, i, g)),
                  pl.BlockSpec((1, tq, GATE_LANES), lambda g, bi, i: (bi, i, g)),
                  cmp_spec, cmp_spec, part(0), part(1), part(2), part(3),
                  pl.BlockSpec((1, NSA_HPG, n_cmp, tq), lambda g, bi, i: (g, 0, 0, i)),
                  pl.BlockSpec((1, NSA_HPG, 1, WINDOW + tq, tq),
                               lambda g, bi, i: (g, 0, jnp.minimum(i, n_win - 1), 0, 0)),
                  pl.BlockSpec((1, NSA_HPG, bias_sel.shape[2], tk, tq), lambda g, bi, i: (g, 0, 0, 0, 0)),
                  _const_spec(ovl.shape)],
        out_specs=pl.BlockSpec((1, tq, NSA_HPG * dh), lambda g, bi, i: (bi, i, g)),
        out_shape=jax.ShapeDtypeStruct((b, t, d), ACT_DTYPE),
        scratch_shapes=[pltpu.VMEM((t // SEL_BLOCK, tq), F32)],
        compiler_params=_params(("arbitrary", "arbitrary", "arbitrary")),
        name="nsa_attention",
    )(q, gate_logits, k_cmp, v_cmp, kv, kv, kv, kv, bias_cmp, bias_win, bias_sel, ovl)


def _nsa_block(h, moe_out, kv_ln, kv_w, cmp_pe_k, cmp_pe_v, cmp_k_w1, cmp_k_w2, cmp_v_w1, cmp_v_w2,
               rel_bias, b_ln, b_wqg, b, t):
    n, d = b * t, kv_ln.shape[0]
    h1, q, gate_logits, kvc, kv = _nsa_proj(h, moe_out, kv_ln, b_ln, kv_w, b_wqg, b, t)
    k_cmp, v_cmp = _compress(kvc, cmp_pe_k, cmp_pe_v, cmp_k_w1, cmp_k_w2, cmp_v_w1, cmp_v_w2)
    o = _nsa_attention(q, gate_logits, kv, k_cmp, v_cmp, rel_bias)
    return o.reshape(n, d), h1.reshape(n, d)


def kernel(x, a_ln, a_mu, a_wr, a_wk, a_wv, a_wo, a_w0, a_w1, a_w2, a_a0, a_a1, a_a2, a_g1, a_g2, a_kk, a_ka, a_rk, a_lnx_w, a_lnx_b, kv_ln, kv_w, cmp_pe_k, cmp_pe_v, cmp_k_w1, cmp_k_w2, cmp_v_w1, cmp_v_w2, rel_bias, b_ln, b_wqg, b_wo, m_ln, m_wg, m_bg, m_we, m_be, m_w1, m_w3, m_w2, final_ln):
    b, t, d = x.shape
    n = b * t
    router = lambda l: _router_weights(m_wg[l], m_bg[l], m_we[l], m_be[l])
    moe = lambda hh, route, counts, l: _moe_layer(hh, route, counts, m_ln[l], m_w1[l], m_w3[l], m_w2[l])
    y = _rwkv_block(x, a_ln[0], a_mu[0], a_wr[0], a_wk[0], a_wv[0], a_w0[0], a_w1[0], a_w2[0],
                    a_a0[0], a_a1[0], a_a2[0], a_g1[0], a_g2[0], a_kk[0], a_ka[0], a_rk[0],
                    a_lnx_w[0], a_lnx_b[0])
    h, route, counts = _matmul_res(y.reshape(n, d), _mx(a_wo[0]), x.reshape(n, d), m_ln[0], router(0))
    o, h1 = _nsa_block(h, moe(h, route, counts, 0), kv_ln, kv_w, cmp_pe_k, cmp_pe_v, cmp_k_w1, cmp_k_w2,
                       cmp_v_w1, cmp_v_w2, rel_bias, b_ln[0], b_wqg[0], b, t)
    h, route, counts = _matmul_res(o, _mx(b_wo[0]), h1, m_ln[1], router(1))
    return _final(h, moe(h, route, counts, 1), final_ln).reshape(b, t, d)
```

```python
import math

import numpy as np
import jax
import jax.numpy as jnp
from jax import lax
from jax.experimental import pallas as pl
from jax.experimental.pallas import tpu as pltpu

F32 = jnp.float32
MXU_DTYPE = jnp.bfloat16
ACT_DTYPE = jnp.bfloat16

D_MODEL = 1024
NORM_EPS = 1e-6
RWKV_HEAD = 64
RWKV_GN_EPS = 64e-5
RWKV_CHUNK = 64
RWKV_SEQS = 4
LANES = 128
MXU_TILE = 256
VMEM_LIMIT = 56 * 1024 * 1024


def _mx(x):
    return x.astype(MXU_DTYPE)


def _dot(a, b):
    return jnp.dot(_mx(a), _mx(b), preferred_element_type=F32)


def _dot_nt(a, b):
    return lax.dot_general(_mx(a), _mx(b), (((1,), (1,)), ((), ())), preferred_element_type=F32)


def _split2(x):
    hi = x.astype(MXU_DTYPE)
    lo = (x - hi.astype(F32)).astype(MXU_DTYPE)
    return hi, lo


def _dot_hl(x, w):
    hi, lo = _split2(x)
    return jnp.dot(hi, w, preferred_element_type=F32) + jnp.dot(lo, w, preferred_element_type=F32)


def _rms(x, g):
    return x * lax.rsqrt(jnp.mean(x * x, axis=-1, keepdims=True) + NORM_EPS) * g


TOK_ROWS = 8


def _tok_load(ref):
    tm = ref.shape[0] // TOK_ROWS
    return jnp.concatenate([ref[pl.ds(s, tm, stride=TOK_ROWS), :] for s in range(TOK_ROWS)], axis=1)


def _tok_store(ref, val):
    tm = val.shape[0]
    for s in range(TOK_ROWS):
        ref[pl.ds(s, tm, stride=TOK_ROWS), :] = val[:, LANES * s:LANES * (s + 1)]


def _tok_spec(tm, index_map):
    return pl.BlockSpec((TOK_ROWS * tm, LANES), index_map)


def _const_spec(shape):
    nd = len(shape)
    return pl.BlockSpec(shape, lambda *_: (0,) * nd)


def _params(sem):
    return pltpu.CompilerParams(dimension_semantics=sem, vmem_limit_bytes=VMEM_LIMIT)


def _rwkv_pre_kernel(x_ref, ln_ref, mu_ref, wr_ref, wk_ref, wv_ref, w1_ref, a1_ref, g1_ref,
                     w2_ref, a2_ref, g2_ref, vec_ref,
                     r_out, k_out, v_out, kk_out, g_out, ld_out, a_out, prev_ref):
    tm = x_ref.shape[1]
    hn = _rms(x_ref[0], ln_ref[...])

    @pl.when(pl.program_id(1) == 0)
    def _():
        prev_ref[...] = jnp.zeros_like(prev_ref)

    rows = lax.broadcasted_iota(jnp.int32, (tm, 1), 0)
    shifted = jnp.where(rows == 0, prev_ref[0:1, :], pltpu.roll(hn, 1, axis=0))
    prev_ref[0:1, :] = hn[tm - 1:tm, :]
    xx = shifted - hn
    xr, xw, xk, xv, xa, xg = (_mx(hn + xx * mu_ref[i:i + 1, :]) for i in range(6))
    w0, a0, kkp, kap = (vec_ref[i:i + 1, :] for i in range(4))
    r = jnp.dot(xr, wr_ref[...], preferred_element_type=F32)
    k = jnp.dot(xk, wk_ref[...], preferred_element_type=F32)
    v = jnp.dot(xv, wv_ref[...], preferred_element_type=F32)
    z = w0 + _dot(jnp.tanh(jnp.dot(xw, w1_ref[...], preferred_element_type=F32)), w2_ref[...])
    a = jax.nn.sigmoid(a0 + _dot(jnp.dot(xa, a1_ref[...], preferred_element_type=F32), a2_ref[...]))
    g = _dot(jax.nn.sigmoid(jnp.dot(xg, g1_ref[...], preferred_element_type=F32)), g2_ref[...])
    ld_out[0] = (-math.exp(-0.5)) * jax.nn.sigmoid(z)
    a_out[0] = a
    r_out[0] = r.astype(r_out.dtype)
    v_out[0] = v.astype(v_out.dtype)
    kk_out[0] = (k * kkp).astype(kk_out.dtype)
    k_out[0] = (k * (1.0 + (a - 1.0) * kap)).astype(k_out.dtype)
    g_out[0] = g.astype(g_out.dtype)


def _rwkv_pre(x, ln, mu, wr, wk, wv, w1, a1, g1, w2, a2, g2, vecs, tm=256):
    b, t, d = x.shape
    act = jax.ShapeDtypeStruct((b, t, d), ACT_DTYPE)
    f32o = jax.ShapeDtypeStruct((b, t, d), F32)
    blk = pl.BlockSpec((1, tm, d), lambda i, j: (i, j, 0))
    consts = [ln, mu, wr, wk, wv, w1, a1, g1, w2, a2, g2, vecs]
    return pl.pallas_call(
        _rwkv_pre_kernel,
        grid=(b, t // tm),
        in_specs=[blk] + [_const_spec(c.shape) for c in consts],
        out_specs=[blk] * 7,
        out_shape=[act, act, act, act, act, f32o, f32o],
        scratch_shapes=[pltpu.VMEM((8, d), F32)],
        compiler_params=_params(("arbitrary", "arbitrary")),
        name="rwkv_pre",
    )(x, *consts)


def _stack4(x):
    return jnp.concatenate([x[:, MXU_TILE * q:MXU_TILE * (q + 1)] for q in range(4)], axis=0)


def _unstack4(y):
    c = y.shape[0] // 4
    return jnp.concatenate([y[c * q:c * (q + 1), :] for q in range(4)], axis=1)


def _rwkv_scan_kernel(r_ref, k_ref, v_ref, kk_ref, g_ref, ld_ref, a_ref,
                      ltri_ref, smask_ref, eye_ref, j_ref, rk_ref, lnw_ref, lnb_ref,
                      y_out, h_ref):
    c = RWKV_CHUNK
    nb = r_ref.shape[0]

    @pl.when(pl.program_id(1) == 0)
    def _():
        h_ref[...] = jnp.zeros_like(h_ref)

    def rows_of(ref):
        return jnp.concatenate([ref[bi].astype(F32) for bi in range(nb)], axis=0)

    ld, r, k, v, kk = (rows_of(x) for x in (ld_ref, r_ref, k_ref, v_ref, kk_ref))
    jones = j_ref[...]
    ld_hi, ld_lo = _split2(ld)
    cum = (jnp.dot(ltri_ref[...], ld_hi, preferred_element_type=F32)
           + jnp.dot(ltri_ref[...], ld_lo, preferred_element_type=F32))
    last = [cum[(bi + 1) * c - 1:(bi + 1) * c, :] for bi in range(nb)]
    cum_last = jnp.concatenate([jnp.broadcast_to(x, (c, x.shape[1])) for x in last], axis=0)
    e_l = jnp.exp(cum)
    e_lm = jnp.exp(cum - ld)
    e_nl = jnp.exp(-cum)
    e_cl = jnp.exp(cum_last - cum)
    e_c = [jnp.exp(x) for x in last]
    ss = _unstack4(_dot_hl(_stack4(kk * kk), jones))
    kkn = kk * lax.rsqrt(jnp.maximum(ss, 1e-24))
    a_vec = -kkn
    b_vec = kkn * rows_of(a_ref)
    rt = r * e_l
    at = a_vec * e_lm
    kt = k * e_nl
    bt = b_vec * e_nl
    bh = b_vec * e_cl
    kh = k * e_cl

    lane = lax.broadcasted_iota(jnp.int32, (1, LANES), 1)
    m_l = (lane < RWKV_HEAD).astype(F32)
    m_r = 1.0 - m_l

    def st(xp):
        return jnp.concatenate([xp * m_l, xp * m_r], axis=0)

    def dup(xp):
        return jnp.concatenate([xp, xp], axis=0)

    smask = smask_ref[...] > 0.0
    eye = eye_ref[...]
    n_pair = D_MODEL // LANES
    pairs = range(nb * n_pair)
    seq_of = [p // n_pair for p in pairs]

    def part(x, p):
        return x[seq_of[p] * c:(seq_of[p] + 1) * c, LANES * (p % n_pair):LANES * (p % n_pair + 1)]

    at_s = [st(part(at, p)) for p in pairs]
    rt_s = [st(part(rt, p)) for p in pairs]
    v_s = [st(part(v, p)) for p in pairs]
    s = [jnp.where(smask,
                   _dot_nt(jnp.concatenate([at_s[p], rt_s[p]], axis=0),
                           jnp.concatenate([dup(part(bt, p)), dup(part(kt, p))], axis=0)),
                   0.0) for p in pairs]
    a_ab = [x[:2 * c, :2 * c] for x in s]
    a_r = [x[2 * c:, :] for x in s]
    g1 = [_dot(s[p][:2 * c, 2 * c:], v_s[p]) for p in pairs]
    t_m = [eye + x for x in a_ab]
    a_pow = [_dot(x, x) for x in a_ab]
    n = 2
    while n < c:
        if 2 * n < c:
            both = [_dot(a_pow[p], jnp.concatenate([a_pow[p], t_m[p]], axis=1)) for p in pairs]
            a_pow = [x[:, :2 * c] for x in both]
            t_m = [t_m[p] + both[p][:, 2 * c:] for p in pairs]
        else:
            t_m = [t_m[p] + _dot(a_pow[p], t_m[p]) for p in pairs]
        n *= 2
    tw = [_dot(t_m[p], jnp.concatenate([at_s[p], g1[p]], axis=1)) for p in pairs]
    zmat = [jnp.concatenate([tw[p], jnp.concatenate([jnp.zeros_like(v_s[p]), v_s[p]], axis=1)], axis=0)
            for p in pairs]
    upper = [_dot(a_r[p], zmat[p]) for p in pairs]
    bk_t = [jnp.concatenate([st(part(bh, p)), st(part(kh, p))], axis=0).T for p in pairs]
    lower = [_dot(bk_t[p], zmat[p]) for p in pairs]
    decay = [eye * e_c[seq_of[p]][:, LANES * (p % n_pair):LANES * (p % n_pair + 1)] for p in pairs]
    seq = [_dot(jnp.concatenate([upper[p][:, :LANES] + rt_s[p], lower[p][:, :LANES] + decay[p]], axis=0), h_ref[p])
           for p in pairs]
    ys = []
    for p in pairs:
        y_st = seq[p][:2 * c, :] + upper[p][:, LANES:]
        h_ref[p] = seq[p][2 * c:, :] + lower[p][:, LANES:]
        ys.append(y_st[:c, :] + y_st[c:, :])
    y = jnp.concatenate([jnp.concatenate(ys[bi * n_pair:(bi + 1) * n_pair], axis=1) for bi in range(nb)],
                        axis=0)

    inv_n = 1.0 / RWKV_HEAD
    y4 = _stack4(y)
    dlt = y4 - _dot_hl(y4, jones) * inv_n
    var = _dot_hl(dlt * dlt, jones) * inv_n
    yn = dlt * lax.rsqrt(var + RWKV_GN_EPS)
    bonus = _dot_hl(_stack4(r * k) * rk_ref[...], jones) * _stack4(v)
    out = _unstack4((yn * lnw_ref[...] + lnb_ref[...] + bonus) * _stack4(rows_of(g_ref)))
    for bi in range(nb):
        y_out[bi] = out[bi * c:(bi + 1) * c, :].astype(y_out.dtype)


def _scan_constants():
    c = RWKV_CHUNK
    ltri = np.kron(np.eye(RWKV_SEQS), np.tril(np.ones((c, c), np.float32)))
    rho = np.arange(4 * c)
    r_type, r_head, r_t = rho // (2 * c), (rho % (2 * c)) // c, rho % c
    same = r_head[:, None] == r_head[None, :]
    strict = r_t[None, :] < r_t[:, None]
    incl = r_t[None, :] <= r_t[:, None]
    smask = same & np.where(r_type[:, None] == 0, strict, incl)
    eye = np.eye(LANES, dtype=np.float32)
    head = np.arange(MXU_TILE) // RWKV_HEAD
    jones = (head[:, None] == head[None, :]).astype(np.float32)
    return (jnp.asarray(ltri, MXU_DTYPE), jnp.asarray(smask, F32), jnp.asarray(eye, F32),
            jnp.asarray(jones, MXU_DTYPE))


def _stack_param(p):
    return jnp.repeat(p.reshape(4, MXU_TILE).astype(F32), RWKV_SEQS * RWKV_CHUNK, axis=0)


def _rwkv_scan(r, k, v, kk, g, ld, a, rk, lnw, lnb):
    b, t, d = r.shape
    c, nb = RWKV_CHUNK, RWKV_SEQS
    assert b % nb == 0 and t % c == 0 and d == D_MODEL
    blk = pl.BlockSpec((nb, c, d), lambda i, j: (i, j, 0))
    consts = list(_scan_constants()) + [_stack_param(rk.reshape(-1)), _stack_param(lnw), _stack_param(lnb)]
    return pl.pallas_call(
        _rwkv_scan_kernel,
        grid=(b // nb, t // c),
        in_specs=[blk] * 7 + [_const_spec(x.shape) for x in consts],
        out_specs=blk,
        out_shape=jax.ShapeDtypeStruct((b, t, d), ACT_DTYPE),
        scratch_shapes=[pltpu.VMEM((nb * (d // LANES), LANES, LANES), F32)],
        compiler_params=_params(("arbitrary", "arbitrary")),
        name="rwkv_scan",
    )(r, k, v, kk, g, ld, a, *consts)


def _matmul_res_kernel(a_ref, w_ref, res_ref, ln_ref, w_hi_ref, w_lo_ref, bias_ref, tri_ref,
                       o_ref, route_out, count_out, cnt_ref):
    h = res_ref[...] + jnp.dot(a_ref[...], w_ref[...], preferred_element_type=F32)
    _tok_store(o_ref, h)
    _route_tokens(h, ln_ref, w_hi_ref, w_lo_ref, bias_ref, tri_ref, route_out, count_out, cnt_ref)


def _matmul_res(a, w, res, ln, router, tm=512):
    n, kdim = a.shape
    d = w.shape[1]
    tri = jnp.asarray(np.tril(np.ones((tm, tm), np.float32), -1), MXU_DTYPE)
    consts = [ln.reshape(1, d), *router, tri]
    return pl.pallas_call(
        _matmul_res_kernel,
        grid=(n // tm,),
        in_specs=[pl.BlockSpec((tm, kdim), lambda i: (i, 0)), _const_spec(w.shape),
                  pl.BlockSpec((tm, d), lambda i: (i, 0))] + [_const_spec(c.shape) for c in consts],
        out_specs=[_tok_spec(tm, lambda i: (i, 0)), pl.BlockSpec((tm, LANES), lambda i: (i, 0)),
                   _const_spec((1, LANES))],
        out_shape=[jax.ShapeDtypeStruct((TOK_ROWS * n, LANES), F32), jax.ShapeDtypeStruct((n, LANES), F32),
                   jax.ShapeDtypeStruct((1, LANES), F32)],
        scratch_shapes=[pltpu.VMEM((1, LANES), F32)],
        compiler_params=_params(("arbitrary",)),
        name="matmul_res_route",
    )(a, w, res, *consts)


def _rwkv_block(x, a_ln, a_mu, a_wr, a_wk, a_wv, a_w0, a_w1, a_w2, a_a0, a_a1, a_a2,
                a_g1, a_g2, a_kk, a_ka, a_rk, a_lnx_w, a_lnx_b):
    b, t, d = x.shape
    vecs = jnp.stack([a_w0, a_a0, a_kk, a_ka]).astype(F32)
    r, k, v, kk, g, ld, a = _rwkv_pre(
        x, a_ln.reshape(1, d), a_mu, _mx(a_wr), _mx(a_wk), _mx(a_wv), _mx(a_w1), _mx(a_a1), _mx(a_g1),
        _mx(a_w2), _mx(a_a2), _mx(a_g2), vecs)
    return _rwkv_scan(r, k, v, kk, g, ld, a, a_rk, a_lnx_w, a_lnx_b)


MOE_GROUPS = 4
MOE_EPG = 8
MOE_EXPERTS = MOE_GROUPS * MOE_EPG
MOE_TOPK = 2
MOE_BLOCK = 512
NEG = -1e30


def _route(logits):
    lane = lax.broadcasted_iota(jnp.int32, logits.shape, 1)
    is_grp = lane < MOE_GROUPS
    gl = jnp.where(is_grp, logits, NEG)
    ge = jnp.exp(gl - jnp.max(gl, axis=-1, keepdims=True))
    gp = ge / jnp.sum(ge, axis=-1, keepdims=True)
    grp_w = jnp.max(gp, axis=-1, keepdims=True)
    grp = jnp.min(jnp.where(is_grp & (gp == grp_w), lane, LANES), axis=-1, keepdims=True)
    lo = MOE_GROUPS + MOE_EPG * grp
    in_grp = (lane >= lo) & (lane < lo + MOE_EPG)
    el = jnp.where(in_grp, logits, NEG)
    ee = jnp.exp(el - jnp.max(el, axis=-1, keepdims=True))
    ep = ee / jnp.sum(ee, axis=-1, keepdims=True)
    p1 = jnp.max(jnp.where(in_grp, ep, -1.0), axis=-1, keepdims=True)
    i1 = jnp.min(jnp.where(in_grp & (ep == p1), lane, LANES), axis=-1, keepdims=True)
    rest = in_grp & (lane != i1)
    p2 = jnp.max(jnp.where(rest, ep, -1.0), axis=-1, keepdims=True)
    i2 = jnp.min(jnp.where(rest & (ep == p2), lane, LANES), axis=-1, keepdims=True)
    scale = grp_w / (p1 + p2)
    return i1 - MOE_GROUPS, i2 - MOE_GROUPS, p1 * scale, p2 * scale


def _router_logits(xf, w_hi_ref, w_lo_ref, bias_ref):
    hi, lo = _split2(xf)
    logits = (jnp.dot(hi, w_hi_ref[...], preferred_element_type=F32)
              + jnp.dot(lo, w_hi_ref[...], preferred_element_type=F32)
              + jnp.dot(hi, w_lo_ref[...], preferred_element_type=F32))
    return logits + bias_ref[...]


def _route_tokens(h, ln_ref, w_hi_ref, w_lo_ref, bias_ref, tri_ref, route_out, count_out, cnt_ref):
    @pl.when(pl.program_id(0) == 0)
    def _():
        cnt_ref[...] = jnp.zeros_like(cnt_ref)

    xf = _rms(h, ln_ref[...])
    e1, e2, w1, w2 = _route(_router_logits(xf, w_hi_ref, w_lo_ref, bias_ref))
    lane = lax.broadcasted_iota(jnp.int32, (xf.shape[0], LANES), 1)
    hit1, hit2 = lane == e1, lane == e2
    onehot = jnp.where(hit1, 1.0, jnp.where(hit2, 1.0, 0.0))
    before = jnp.dot(tri_ref[...], _mx(onehot), preferred_element_type=F32) + cnt_ref[...]
    r1 = jnp.sum(jnp.where(hit1, before, 0.0), axis=-1, keepdims=True)
    r2 = jnp.sum(jnp.where(hit2, before, 0.0), axis=-1, keepdims=True)
    cnt_ref[...] = cnt_ref[...] + jnp.sum(onehot, axis=0, keepdims=True)
    count_out[...] = cnt_ref[...]
    fields = (e1.astype(F32), e2.astype(F32), w1, w2, r1, r2)
    rec = jnp.zeros(lane.shape, F32)
    for idx, val in enumerate(fields):
        rec = jnp.where(lane == idx, val, rec)
    route_out[...] = rec


def _router_weights(w_gr, b_gr, w_er, b_er):
    d = w_gr.shape[0]
    w = jnp.zeros((d, LANES), F32).at[:, :MOE_GROUPS].set(w_gr).at[:, MOE_GROUPS:MOE_GROUPS + MOE_EXPERTS].set(w_er)
    bias = jnp.zeros((1, LANES), F32).at[0, :MOE_GROUPS].set(b_gr).at[0, MOE_GROUPS:MOE_GROUPS + MOE_EXPERTS].set(b_er)
    w_hi = w.astype(MXU_DTYPE)
    w_lo = (w - w_hi.astype(F32)).astype(MXU_DTYPE)
    return w_hi, w_lo, bias


def _moe_plan(route, counts, blk):
    n = route.shape[0]
    counts = counts[0, :MOE_EXPERTS].astype(jnp.int32)
    pcounts = (counts + blk - 1) // blk * blk
    pend = jnp.cumsum(pcounts)
    pstart = (pend - pcounts).astype(jnp.int32)
    n_blk = n * MOE_TOPK // blk + MOE_EXPERTS
    blk_start = jnp.arange(n_blk, dtype=jnp.int32) * blk
    blk_e = jnp.sum((blk_start[:, None] >= pend[None, :]).astype(jnp.int32), axis=1)
    blk_e = jnp.minimum(blk_e, MOE_EXPERTS - 1)
    n_valid = jnp.clip((pstart + counts)[blk_e] - blk_start, 0, blk).astype(jnp.int32)
    n_act = (pend[-1] // blk).astype(jnp.int32).reshape(1)
    expert = route[:, 0:MOE_TOPK].astype(jnp.int32)
    onehot = expert[:, :, None] == jnp.arange(MOE_EXPERTS, dtype=jnp.int32)
    seg_start = jnp.sum(jnp.where(onehot, pstart, 0), axis=-1)
    sorted_pos = seg_start + route[:, 4:4 + MOE_TOPK].astype(jnp.int32)
    return blk_e, n_valid, n_act, sorted_pos


def _token_copies(pos_ref, tm, make_copy):
    group = 8

    def body(g, carry):
        t0 = g * group
        pos = [[pos_ref[0, 0, MOE_TOPK * (t0 + j) + k] for k in range(MOE_TOPK)] for j in range(group)]
        for j in range(group):
            for k in range(MOE_TOPK):
                make_copy(t0 + j, k, pos[j][k]).start(priority=(MOE_TOPK * j + k) % 2)
        return carry

    lax.fori_loop(0, tm // group, body, 0)


def _tile(ref, tok):
    return ref.at[pl.ds(pl.multiple_of(tok * TOK_ROWS, TOK_ROWS), TOK_ROWS)]


def _moe_dispatch_kernel(n_valid_ref, pos_ref, src_ref, dst_hbm, stage, sem, zbuf, zsem):
    tm = src_ref.shape[0] // TOK_ROWS
    step = pl.program_id(0)
    slot = lax.rem(step, 2)

    def drain(s):
        for k in range(MOE_TOPK):
            pltpu.make_async_copy(stage.at[s], dst_hbm.at[pl.ds(0, stage.shape[1])], sem.at[s]).wait()

    @pl.when(pl.program_id(0) == 0)
    def _():
        rows = zbuf.shape[0]
        zbuf[...] = jnp.zeros_like(zbuf)

        def fill(j, wait):
            @pl.when(n_valid_ref[j] < rows // TOK_ROWS)
            def _():
                start = 0 if wait else pl.multiple_of(j * rows, rows)
                cp = pltpu.make_async_copy(zbuf, dst_hbm.at[pl.ds(start, rows)], zsem)
                cp.wait() if wait else cp.start()

        lax.fori_loop(0, n_valid_ref.shape[0], lambda j, c: (fill(j, False), c)[1], 0)
        lax.fori_loop(0, n_valid_ref.shape[0], lambda j, c: (fill(j, True), c)[1], 0)

    @pl.when(step >= 2)
    def _():
        drain(slot)

    stage[slot] = src_ref[...]
    _token_copies(pos_ref, tm, lambda t, k, pos: pltpu.make_async_copy(
        _tile(stage.at[slot], t), _tile(dst_hbm, pos), sem.at[slot]))

    @pl.when(step == pl.num_programs(0) - 1)
    def _():
        @pl.when(step >= 1)
        def _():
            drain(1 - slot)
        drain(slot)


def _moe_gather_combine(step, n_steps, pos_ref, pos_next_ref, ys_hbm, h_ref, route, gbuf, gsem):
    tm = gbuf.shape[2] // TOK_ROWS
    slot = lax.rem(step, 2)

    def start(p_ref, s):
        _token_copies(p_ref, tm, lambda t, k, pos: pltpu.make_async_copy(
            _tile(ys_hbm, pos), _tile(gbuf.at[s, k], t), gsem.at[s]))

    @pl.when(step == 0)
    def _():
        start(pos_ref, 0)

    @pl.when(step + 1 < n_steps)
    def _():
        start(pos_next_ref, 1 - slot)

    for k in range(MOE_TOPK):
        pltpu.make_async_copy(ys_hbm.at[pl.ds(0, gbuf.shape[2])], gbuf.at[slot, k], gsem.at[slot]).wait()
    return (_tok_load(h_ref) + route[:, 2:3] * _tok_load(gbuf.at[slot, 0])
            + route[:, 3:4] * _tok_load(gbuf.at[slot, 1]))


def _gather_specs(sorted_pos, tm, step_index):
    n = sorted_pos.shape[0]
    pos = sorted_pos.reshape(n // tm, 1, MOE_TOPK * tm)
    last = n // tm - 1
    cur = pl.BlockSpec((1, 1, MOE_TOPK * tm), lambda *g: (step_index(*g), 0, 0), memory_space=pltpu.SMEM)
    nxt = pl.BlockSpec((1, 1, MOE_TOPK * tm), lambda *g: (jnp.minimum(step_index(*g) + 1, last), 0, 0),
                       memory_space=pltpu.SMEM)
    scratch = [pltpu.VMEM((2, MOE_TOPK, TOK_ROWS * tm, LANES), F32), pltpu.SemaphoreType.DMA((2,))]
    return pos, [cur, nxt, pl.BlockSpec(memory_space=pl.ANY)], scratch


def _moe_dispatch(h, n_valid, sorted_pos, p, blk, tm=512):
    n = sorted_pos.shape[0]
    pos = sorted_pos.reshape(n // tm, 1, MOE_TOPK * tm)
    grid_spec = pltpu.PrefetchScalarGridSpec(
        num_scalar_prefetch=1,
        grid=(n // tm,),
        in_specs=[pl.BlockSpec((1, 1, MOE_TOPK * tm), lambda i, nv: (i, 0, 0), memory_space=pltpu.SMEM),
                  _tok_spec(tm, lambda i, nv: (i, 0))],
        out_specs=pl.BlockSpec(memory_space=pl.ANY),
        scratch_shapes=[pltpu.VMEM((2, TOK_ROWS * tm, LANES), h.dtype), pltpu.SemaphoreType.DMA((2,)),
                        pltpu.VMEM((TOK_ROWS * blk, LANES), h.dtype), pltpu.SemaphoreType.DMA],
    )
    return pl.pallas_call(
        _moe_dispatch_kernel,
        grid_spec=grid_spec,
        out_shape=jax.ShapeDtypeStruct((TOK_ROWS * p, LANES), h.dtype),
        compiler_params=_params(("arbitrary",)),
        name="moe_dispatch",
    )(n_valid, pos, h)


def _moe_expert_kernel(blk_e_ref, n_act_ref, x_ref, ln_ref, w1_ref, w3_ref, w2_ref, y_ref):
    del blk_e_ref
    i = pl.program_id(0)

    @pl.when(i < n_act_ref[0])
    def _():
        xn = _mx(_rms(_tok_load(x_ref), ln_ref[...]))
        h1 = jnp.dot(xn, w1_ref[0], preferred_element_type=F32)
        h3 = jnp.dot(xn, w3_ref[0], preferred_element_type=F32)
        _tok_store(y_ref, _dot(h1 * jax.nn.sigmoid(h1) * h3, w2_ref[0]))

    @pl.when(i >= n_act_ref[0])
    def _():
        y_ref[...] = jnp.zeros_like(y_ref)


def _moe_experts(xs, ln, w1, w3, w2, blk_e, n_act, blk):
    p = xs.shape[0] // TOK_ROWS
    d, hdim = w1.shape[1], w1.shape[2]
    grid_spec = pltpu.PrefetchScalarGridSpec(
        num_scalar_prefetch=2,
        grid=(p // blk,),
        in_specs=[_tok_spec(blk, lambda i, be, na: (i, 0)),
                  pl.BlockSpec((1, d), lambda i, be, na: (0, 0)),
                  pl.BlockSpec((1, d, hdim), lambda i, be, na: (be[i], 0, 0)),
                  pl.BlockSpec((1, d, hdim), lambda i, be, na: (be[i], 0, 0)),
                  pl.BlockSpec((1, hdim, d), lambda i, be, na: (be[i], 0, 0))],
        out_specs=_tok_spec(blk, lambda i, be, na: (i, 0)),
    )
    return pl.pallas_call(
        _moe_expert_kernel,
        grid_spec=grid_spec,
        out_shape=jax.ShapeDtypeStruct(xs.shape, F32),
        compiler_params=_params(("arbitrary",)),
        name="moe_experts",
    )(blk_e, n_act, xs, ln.reshape(1, d), w1, w3, w2)


def _moe_layer(h, route, counts, ln, w1, w3, w2):
    n = h.shape[0] // TOK_ROWS
    blk = MOE_BLOCK
    blk_e, n_valid, n_act, sorted_pos = _moe_plan(route, counts, blk)
    p = n * MOE_TOPK + MOE_EXPERTS * blk
    xs = _moe_dispatch(h, n_valid, sorted_pos, p, blk)
    ys = _moe_experts(xs, ln, _mx(w1), _mx(w3), _mx(w2), blk_e, n_act, blk)
    return ys, sorted_pos, route


def _final_kernel(pos_ref, pos_next_ref, ys_hbm, h_ref, route_ref, ln_ref, o_ref, gbuf, gsem):
    h = _moe_gather_combine(pl.program_id(0), pl.num_programs(0), pos_ref, pos_next_ref, ys_hbm, h_ref,
                            route_ref[...], gbuf, gsem)
    o_ref[...] = _rms(h, ln_ref[...])


def _final(h, moe_out, ln, tm=512):
    ys, sorted_pos, route = moe_out
    n, d = route.shape[0], ln.shape[0]
    pos, gather_specs, scratch = _gather_specs(sorted_pos, tm, lambda i: i)
    return pl.pallas_call(
        _final_kernel,
        grid=(n // tm,),
        in_specs=gather_specs + [_tok_spec(tm, lambda i: (i, 0)), pl.BlockSpec((tm, LANES), lambda i: (i, 0)),
                                 _const_spec((1, d))],
        out_specs=pl.BlockSpec((tm, d), lambda i: (i, 0)),
        out_shape=jax.ShapeDtypeStruct((n, d), F32),
        scratch_shapes=scratch,
        compiler_params=_params(("arbitrary",)),
        name="final_norm",
    )(pos, pos, ys, h, route, ln.reshape(1, d))


NSA_HEADS = 16
NSA_GROUPS = 4
NSA_HPG = NSA_HEADS // NSA_GROUPS
NSA_DH = D_MODEL // NSA_HEADS
NSA_PARTS = 6
CMP_BLOCK = 32
CMP_STRIDE = 16
SEL_BLOCK = 64
SEL_TOPK = 8
WINDOW = 512
N_BRANCH = 3
IMP_FORCE = 1e4
REL_BUCKETS = 32
REL_MAX_DIST = 128
LOG2E = math.log2(math.e)
NSA_TQ = 256
NSA_TK = 512
GATE_LANES = LANES
CMP_PARTS = 2


def _nsa_proj_kernel(pos_ref, pos_next_ref, ys_hbm, h_ref, route_ref, kvln_ref, qln_ref, kvw_ref, qw_ref,
                     h_out, q_out, gate_out, kvc_out, kv_out, gbuf, gsem):
    step = pl.program_id(0) * pl.num_programs(1) + pl.program_id(1)
    h1 = _moe_gather_combine(step, pl.num_programs(0) * pl.num_programs(1), pos_ref, pos_next_ref, ys_hbm, h_ref,
                             route_ref[0], gbuf, gsem)
    h_out[0] = h1
    kv = _dot(_rms(h1, kvln_ref[...]), kvw_ref[...])
    for part in range(NSA_PARTS):
        for g in range(NSA_GROUPS):
            col = (part * NSA_GROUPS + g) * NSA_DH
            piece = kv[:, col:col + NSA_DH]
            if part < CMP_PARTS:
                kvc_out[part, 0, g] = piece
            else:
                kv_out[part - CMP_PARTS, 0, g] = piece.astype(kv_out.dtype)
    qg = _dot(_rms(h1, qln_ref[...]), qw_ref[...])
    q_out[0] = (qg[:, :D_MODEL] * (NSA_DH ** -0.5 * LOG2E)).astype(q_out.dtype)
    gate_out[0] = qg[:, D_MODEL:]


def _gate_padded_weight(q_w):
    d = q_w.shape[0]
    n_gate = NSA_HPG * N_BRANCH
    gates = q_w[:, d:].reshape(d, NSA_GROUPS, n_gate)
    gates = jnp.pad(gates, ((0, 0), (0, 0), (0, GATE_LANES - n_gate))).reshape(d, NSA_GROUPS * GATE_LANES)
    return jnp.concatenate([q_w[:, :d], gates], axis=1)


def _nsa_proj(h, moe_out, kv_ln, q_ln, kv_w, q_w, b, t, tm=256):
    ys, sorted_pos, route = moe_out
    d = kv_ln.shape[0]
    blk = lambda w: pl.BlockSpec((1, tm, w), lambda i, j: (i, j, 0))
    per_b = t // tm
    tok_block = lambda i, j: i * per_b + j
    pos, gather_specs, scratch = _gather_specs(sorted_pos, tm, tok_block)
    gw = NSA_GROUPS * GATE_LANES
    qw = _gate_padded_weight(q_w)
    kv_spec = lambda parts: pl.BlockSpec((parts, 1, NSA_GROUPS, tm, NSA_DH), lambda i, j: (0, i, 0, j, 0))
    kv_shape = lambda parts: (parts, b, NSA_GROUPS, t, NSA_DH)
    return pl.pallas_call(
        _nsa_proj_kernel,
        grid=(b, per_b),
        in_specs=gather_specs + [_tok_spec(tm, lambda i, j: (tok_block(i, j), 0)), blk(LANES),
                                 _const_spec((1, d)), _const_spec((1, d)),
                                 _const_spec(kv_w.shape), _const_spec(qw.shape)],
        out_specs=[blk(d), blk(d), blk(gw), kv_spec(CMP_PARTS), kv_spec(NSA_PARTS - CMP_PARTS)],
        out_shape=[jax.ShapeDtypeStruct((b, t, d), F32), jax.ShapeDtypeStruct((b, t, d), ACT_DTYPE),
                   jax.ShapeDtypeStruct((b, t, gw), F32), jax.ShapeDtypeStruct(kv_shape(CMP_PARTS), F32),
                   jax.ShapeDtypeStruct(kv_shape(NSA_PARTS - CMP_PARTS), ACT_DTYPE)],
        scratch_shapes=scratch,
        compiler_params=_params(("arbitrary", "arbitrary")),
        name="nsa_proj",
    )(pos, pos, ys, h, route.reshape(b, t, LANES),
      kv_ln.reshape(1, d), q_ln.reshape(1, d), _mx(kv_w), _mx(qw))


def _gelu_tanh(x):
    return 0.5 * x * (1.0 + jnp.tanh(math.sqrt(2.0 / math.pi) * (x + 0.044715 * (x * x * x))))


def _compress_kernel(k_ref, v_ref, pek_ref, pev_ref, kw1_ref, kw2_ref, vw1_ref, vw2_ref, kc_out, vc_out):
    dh = k_ref.shape[-1]
    n_chunk = k_ref.shape[3] // CMP_STRIDE

    def one(z_ref, pe_ref, w1_ref, w2_ref, out):
        first = jnp.zeros((n_chunk, w1_ref.shape[1]), F32)
        second = first
        for l in range(CMP_STRIDE):
            z = z_ref[0, 0, 0, pl.ds(l, n_chunk, stride=CMP_STRIDE), :]
            lo, hi = l, CMP_STRIDE + l
            first = first + _dot(z + pe_ref[0, lo:lo + 1, :], w1_ref[dh * lo:dh * (lo + 1), :])
            second = second + _dot(z + pe_ref[0, hi:hi + 1, :], w1_ref[dh * hi:dh * (hi + 1), :])
        hid = first + pltpu.roll(second, n_chunk - 1, axis=0)
        out[0, 0] = _dot(_gelu_tanh(hid), w2_ref[...]).astype(out.dtype)

    one(k_ref, pek_ref, kw1_ref, kw2_ref, kc_out)
    one(v_ref, pev_ref, vw1_ref, vw2_ref, vc_out)


def _compress(kvc, pe_k, pe_v, k_w1, k_w2, v_w1, v_w2):
    _, b, g, t, dh = kvc.shape
    n_chunk = t // CMP_STRIDE
    part = lambda idx: pl.BlockSpec((1, 1, 1, t, dh), lambda i, j: (idx, i, j, 0, 0))
    pe_spec = pl.BlockSpec((1, CMP_BLOCK, dh), lambda i, j: (j, 0, 0))
    out_spec = pl.BlockSpec((1, 1, n_chunk, dh), lambda i, j: (i, j, 0, 0))
    out = jax.ShapeDtypeStruct((b, g, n_chunk, dh), ACT_DTYPE)
    return pl.pallas_call(
        _compress_kernel,
        grid=(b, g),
        in_specs=[part(0), part(1), pe_spec, pe_spec, _const_spec(k_w1.shape), _const_spec(k_w2.shape),
                  _const_spec(v_w1.shape), _const_spec(v_w2.shape)],
        out_specs=[out_spec, out_spec],
        out_shape=[out, out],
        compiler_params=_params(("arbitrary", "arbitrary")),
        name="nsa_compress",
    )(kvc, kvc, pe_k.astype(F32), pe_v.astype(F32), _mx(k_w1), _mx(k_w2), _mx(v_w1), _mx(v_w2))


def _rel_bucket(dist):
    n = jnp.maximum(dist, 0)
    max_exact = REL_BUCKETS // 2
    nf = jnp.maximum(n, 1).astype(F32)
    large = max_exact + (jnp.log(nf / max_exact) / math.log(REL_MAX_DIST / max_exact)
                         * (REL_BUCKETS - max_exact)).astype(jnp.int32)
    return jnp.where(n < max_exact, n, jnp.minimum(large, REL_BUCKETS - 1))


def _toeplitz(vals, rows, cols):
    span = rows + cols
    z = jnp.concatenate([vals, vals[..., :1]], axis=-1)
    flat = jnp.tile(z, rows)[..., :rows * (span - 1)]
    skew = flat.reshape(vals.shape[:-1] + (rows, span - 1))
    return skew[..., rows - 1:rows - 1 + cols]


def _nsa_bias_tiles(rel_bias, t):
    tq, tk = NSA_TQ, NSA_TK
    table = rel_bias.astype(F32).reshape(REL_BUCKETS, NSA_GROUPS, NSA_HPG).transpose(1, 2, 0)
    by_dist = table[:, :, _rel_bucket(jnp.arange(REL_MAX_DIST + 1))]
    far = by_dist[:, :, REL_MAX_DIST:]

    def band(dist, valid, shift=None):
        b = by_dist[:, :, np.clip(dist, 0, REL_MAX_DIST)]
        if shift is not None:
            b = b - shift
        return jnp.where(jnp.asarray(valid), b * LOG2E, NEG)

    u = lambda rows, cols: np.arange(rows + cols - 1) - (rows - 1)
    wk = WINDOW + tq
    dist = np.stack([v * tq - max(v * tq - WINDOW, 0) - u(tq, wk) for v in range(WINDOW // tq + 1)])
    bias_win = _toeplitz(band(dist, (dist >= 0) & (dist < WINDOW)), tq, wk).swapaxes(-1, -2)
    n_near = (REL_MAX_DIST + tk - 1) // tq + 1
    dist = np.stack([e * tq - u(tq, tk) for e in range(n_near)])
    near = _toeplitz(band(dist, dist >= 0, far[:, :, None]), tq, tk).swapaxes(-1, -2)
    bias_sel = jnp.concatenate([near, jnp.zeros_like(near[:, :, :1])], axis=2)
    n_cmp = t // CMP_STRIDE
    dist = np.stack([b - (CMP_BLOCK - 1) + CMP_STRIDE * u(n_cmp, n_cmp) for b in range(CMP_STRIDE)])
    per_b = _toeplitz(band(dist, dist >= 0), n_cmp, n_cmp)
    bias_cmp = jnp.moveaxis(per_b, 2, 4).reshape(NSA_GROUPS, NSA_HPG, n_cmp, t)
    real = np.arange(n_cmp) < (t - CMP_BLOCK) // CMP_STRIDE + 1
    bias_cmp = jnp.where(jnp.asarray(real)[:, None], bias_cmp, NEG)
    return bias_cmp, bias_win, bias_sel


def _nsa_constants(t):
    n_sel = t // SEL_BLOCK
    n_cmp = t // CMP_STRIDE
    cmp_start = np.arange(n_cmp) * CMP_STRIDE
    sel_start = np.arange(n_sel) * SEL_BLOCK
    overlap = ((cmp_start[:, None] < sel_start[None, :] + SEL_BLOCK)
               & (cmp_start[:, None] + CMP_BLOCK > sel_start[None, :]))
    overlap[(t - CMP_BLOCK) // CMP_STRIDE + 1:, :] = False
    return jnp.asarray(overlap.T, MXU_DTYPE)


def _nsa_attn_kernel(q_ref, gate_ref, kc_ref, vc_ref, ks_ref, vs_ref, kw_ref, vw_ref,
                     bcmp_ref, bwin_ref, bsel_ref, ovl_ref, o_ref, neg_ref):
    tq, tk, dh = NSA_TQ, NSA_TK, NSA_DH
    i = pl.program_id(2)
    t0 = i * tq
    heads = range(NSA_HPG)
    q_t = q_ref[0].astype(F32).T
    qh = [_mx(q_t[dh * h:dh * (h + 1), :]) for h in heads]

    def values_t(v):
        ones_col = (lax.broadcasted_iota(jnp.int32, v.shape, 1) == 0).astype(F32)
        return _mx(jnp.concatenate([v.astype(F32), ones_col], axis=1).T)

    k_cmp = kc_ref[0, 0]
    lc = [jnp.dot(k_cmp, qh[h], preferred_element_type=F32) + bcmp_ref[0, h] for h in heads]
    mc = [jnp.max(x, axis=0, keepdims=True) for x in lc]
    pcu = [jnp.where(mc[h] > 0.5 * NEG, jnp.exp2(lc[h] - mc[h]), 0.0) for h in heads]
    pc = [x / jnp.maximum(jnp.sum(x, axis=0, keepdims=True), 1e-30) for x in pcu]
    v_cmp_t = values_t(vc_ref[0, 0])[:dh]
    oc = [_dot(v_cmp_t, x) for x in pc]

    pcsum = (pc[0] + pc[1]) + (pc[2] + pc[3])
    hi = pcsum.astype(MXU_DTYPE)
    mid = (pcsum - hi.astype(F32))
    lo = (mid - mid.astype(MXU_DTYPE).astype(F32)).astype(MXU_DTYPE)
    ovl = ovl_ref[...]
    imp = (jnp.dot(ovl, hi, preferred_element_type=F32) + jnp.dot(ovl, mid.astype(MXU_DTYPE), preferred_element_type=F32)
           + jnp.dot(ovl, lo, preferred_element_type=F32))
    n_sel = imp.shape[0]
    jj = lax.broadcasted_iota(jnp.int32, (n_sel, 1), 0)
    blk_q = lax.shift_right_logical(t0 + lax.broadcasted_iota(jnp.int32, (1, tq), 1),
                                    int(math.log2(SEL_BLOCK)))
    forced = (jj == 0) | (jj == blk_q) | (jj == blk_q - 1)
    score = jnp.where(forced, IMP_FORCE, jnp.where(jj <= blk_q, imp, -1.0))
    rank = jnp.zeros_like(score)
    for j in range(n_sel):
        row = score[j:j + 1, :]
        tie = (jj > j).astype(F32)
        rank = rank + jnp.where(row > score, 1.0, jnp.where(row == score, tie, 0.0))
    neg_ref[...] = jnp.where(rank < SEL_TOPK, 0.0, NEG)

    def normalised(acc):
        return acc[:dh] / acc[dh:dh + 1]

    per_step = tk // SEL_BLOCK

    def sel_step(c, carry):
        ms, accs = carry
        start = pl.multiple_of(c * tk, tk)
        k_c = ks_ref[0, 0, 0, pl.ds(start, tk), :]
        v_c = values_t(vs_ref[0, 0, 0, pl.ds(start, tk), :])
        neg = jnp.concatenate([jnp.broadcast_to(neg_ref[pl.ds(per_step * c + r, 1), :], (SEL_BLOCK, tq))
                               for r in range(per_step)], axis=0)
        near = jnp.minimum((t0 - start) // tq, bsel_ref.shape[2] - 1)
        x = [jnp.dot(k_c, qh[h], preferred_element_type=F32) + bsel_ref[0, h, near] + neg for h in heads]
        m_new = [jnp.maximum(ms[h], jnp.max(x[h], axis=0, keepdims=True)) for h in heads]
        acc_new = [jnp.exp2(ms[h] - m_new[h]) * accs[h] + _dot(v_c, jnp.exp2(x[h] - m_new[h])) for h in heads]
        return tuple(m_new), tuple(acc_new)

    init = (tuple(jnp.full((1, tq), NEG, F32) for _ in heads), tuple(jnp.zeros((2 * dh, tq), F32) for _ in heads))
    _, acc_sel = lax.fori_loop(0, (t0 + tq + tk - 1) // tk, sel_step, init)

    w0 = pl.multiple_of(jnp.maximum(t0 - WINDOW, 0), tq)
    wk = WINDOW + tq
    k_w = kw_ref[0, 0, 0, pl.ds(w0, wk), :]
    v_w = values_t(vw_ref[0, 0, 0, pl.ds(w0, wk), :])
    xw = [jnp.dot(k_w, qh[h], preferred_element_type=F32) + bwin_ref[0, h, 0] for h in heads]
    ow = [_dot(v_w, jnp.exp2(x - jnp.max(x, axis=0, keepdims=True))) for x in xw]

    gates = jax.nn.sigmoid(gate_ref[0]).T
    outs = []
    for h in heads:
        g = [gates[N_BRANCH * h + br:N_BRANCH * h + br + 1, :] for br in range(N_BRANCH)]
        outs.append(g[0] * oc[h] + g[1] * normalised(acc_sel[h]) + g[2] * normalised(ow[h]))
    o_ref[0] = jnp.concatenate(outs, axis=0).T.astype(o_ref.dtype)


def _nsa_attention(q, gate_logits, kv, k_cmp, v_cmp, rel_bias):
    b, t, d = q.shape
    tq, tk, dh = NSA_TQ, NSA_TK, NSA_DH
    bias_cmp, bias_win, bias_sel = _nsa_bias_tiles(rel_bias, t)
    ovl = _nsa_constants(t)
    n_cmp = k_cmp.shape[2]
    n_win = bias_win.shape[2]
    part = lambda idx: pl.BlockSpec((1, 1, 1, t, dh), lambda g, bi, i: (idx, bi, g, 0, 0))
    cmp_spec = pl.BlockSpec((1, 1, n_cmp, dh), lambda g, bi, i: (bi, g, 0, 0))
    return pl.pallas_call(
        _nsa_attn_kernel,
        grid=(NSA_GROUPS, b, t // tq),
        in_specs=[pl.BlockSpec((1, tq, NSA_HPG * dh), lambda g, bi, i: (bi, i, g)),
                  pl.BlockSpec((1, tq, GATE_LANES), lambda g, bi, i: (bi, i, g)),
                  cmp_spec, cmp_spec, part(0), part(1), part(2), part(3),
                  pl.BlockSpec((1, NSA_HPG, n_cmp, tq), lambda g, bi, i: (g, 0, 0, i)),
                  pl.BlockSpec((1, NSA_HPG, 1, WINDOW + tq, tq),
                               lambda g, bi, i: (g, 0, jnp.minimum(i, n_win - 1), 0, 0)),
                  pl.BlockSpec((1, NSA_HPG, bias_sel.shape[2], tk, tq), lambda g, bi, i: (g, 0, 0, 0, 0)),
                  _const_spec(ovl.shape)],
        out_specs=pl.BlockSpec((1, tq, NSA_HPG * dh), lambda g, bi, i: (bi, i, g)),
        out_shape=jax.ShapeDtypeStruct((b, t, d), ACT_DTYPE),
        scratch_shapes=[pltpu.VMEM((t // SEL_BLOCK, tq), F32)],
        compiler_params=_params(("arbitrary", "arbitrary", "arbitrary")),
        name="nsa_attention",
    )(q, gate_logits, k_cmp, v_cmp, kv, kv, kv, kv, bias_cmp, bias_win, bias_sel, ovl)


def _nsa_block(h, moe_out, kv_ln, kv_w, cmp_pe_k, cmp_pe_v, cmp_k_w1, cmp_k_w2, cmp_v_w1, cmp_v_w2,
               rel_bias, b_ln, b_wqg, b, t):
    n, d = b * t, kv_ln.shape[0]
    h1, q, gate_logits, kvc, kv = _nsa_proj(h, moe_out, kv_ln, b_ln, kv_w, b_wqg, b, t)
    k_cmp, v_cmp = _compress(kvc, cmp_pe_k, cmp_pe_v, cmp_k_w1, cmp_k_w2, cmp_v_w1, cmp_v_w2)
    o = _nsa_attention(q, gate_logits, kv, k_cmp, v_cmp, rel_bias)
    return o.reshape(n, d), h1.reshape(n, d)


def kernel(x, a_ln, a_mu, a_wr, a_wk, a_wv, a_wo, a_w0, a_w1, a_w2, a_a0, a_a1, a_a2, a_g1, a_g2, a_kk, a_ka, a_rk, a_lnx_w, a_lnx_b, kv_ln, kv_w, cmp_pe_k, cmp_pe_v, cmp_k_w1, cmp_k_w2, cmp_v_w1, cmp_v_w2, rel_bias, b_ln, b_wqg, b_wo, m_ln, m_wg, m_bg, m_we, m_be, m_w1, m_w3, m_w2, final_ln):
    b, t, d = x.shape
    n = b * t
    router = lambda l: _router_weights(m_wg[l], m_bg[l], m_we[l], m_be[l])
    moe = lambda hh, route, counts, l: _moe_layer(hh, route, counts, m_ln[l], m_w1[l], m_w3[l], m_w2[l])
    y = _rwkv_block(x, a_ln[0], a_mu[0], a_wr[0], a_wk[0], a_wv[0], a_w0[0], a_w1[0], a_w2[0],
                    a_a0[0], a_a1[0], a_a2[0], a_g1[0], a_g2[0], a_kk[0], a_ka[0], a_rk[0],
                    a_lnx_w[0], a_lnx_b[0])
    h, route, counts = _matmul_res(y.reshape(n, d), _mx(a_wo[0]), x.reshape(n, d), m_ln[0], router(0))
    o, h1 = _nsa_block(h, moe(h, route, counts, 0), kv_ln, kv_w, cmp_pe_k, cmp_pe_v, cmp_k_w1, cmp_k_w2,
                       cmp_v_w1, cmp_v_w2, rel_bias, b_ln[0], b_wqg[0], b, t)
    h, route, counts = _matmul_res(o, _mx(b_wo[0]), h1, m_ln[1], router(1))
    return _final(h, moe(h, route, counts, 1), final_ln).reshape(b, t, d)
```
